```python
import math
import jax, jax.numpy as jnp
from jax import lax
import numpy as np

D_MODEL = 1024
BATCH = 2
SEQ = 8192
DEPTH = 2

D_INNER = D_MODEL
HEAD_DIM = 64
N_HEADS = D_INNER // HEAD_DIM
N_MIXERS = 2
GRID_W = 64
NA_ROWS = 8
NA_COLS = 16
NA_QBLOCK = 16
NA_KBLOCK = NA_QBLOCK + NA_COLS
DIL_PAIRS = ((128, 1), (512, 4), (2048, 16))
N_DIL_GROUPS = len(DIL_PAIRS)
RMS_EPS = 1e-6
NEG_INF = -1e30

kernel_name = "hybrid_natten_dilated_encoder"


def rmsnorm(x, g):
    xf = x.astype(jnp.float32)
    y = xf * lax.rsqrt(jnp.mean(xf * xf, axis=-1, keepdims=True) + RMS_EPS)
    return (y * g.astype(jnp.float32)).astype(x.dtype)


def alibi_slopes(n_heads):
    return jnp.asarray(2.0 ** (-8.0 * (np.arange(n_heads) + 1) / n_heads), dtype=jnp.float32)


def neighbourhood_attention(q, k, v, rpb):
    B, S, H, hd = q.shape
    rows = S // GRID_W
    kh = min(NA_ROWS, rows)
    kw = NA_COLS
    nb = GRID_W // NA_QBLOCK
    scale = 1.0 / math.sqrt(hd)
    qg = q.reshape(B, rows, GRID_W, H, hd)
    kg = k.reshape(B, rows, GRID_W, H, hd)
    vg = v.reshape(B, rows, GRID_W, H, hd)

    blk = np.arange(nb)
    kb_start = np.clip(blk * NA_QBLOCK - kw // 2, 0, GRID_W - NA_KBLOCK)
    col_idx = kb_start[:, None] + np.arange(NA_KBLOCK)[None, :]
    q_col = blk[:, None] * NA_QBLOCK + np.arange(NA_QBLOCK)[None, :]
    q_cstart = np.clip(q_col - kw // 2, 0, GRID_W - kw)
    kcol = col_idx[:, None, :]
    col_valid = (kcol >= q_cstart[:, :, None]) & (kcol < q_cstart[:, :, None] + kw)
    col_off = np.clip(kcol - q_col[:, :, None] + NA_COLS - 1, 0, 2 * NA_COLS - 2)
    col_bias = rpb.astype(jnp.float32)[:, :, col_off]
    col_valid = jnp.asarray(col_valid)[:, :, None, :]

    def row_fn(r):
        rs = jnp.clip(r - kh // 2, 0, rows - kh)
        qr = lax.dynamic_index_in_dim(qg, r, axis=1, keepdims=False).reshape(B, nb, NA_QBLOCK, H, hd)
        kr = lax.dynamic_slice_in_dim(kg, rs, kh, axis=1)[:, :, col_idx]
        vr = lax.dynamic_slice_in_dim(vg, rs, kh, axis=1)[:, :, col_idx]
        s = jnp.einsum('bnqhd,bknjhd->bhnqkj', qr, kr).astype(jnp.float32) * scale
        row_off = rs + jnp.arange(kh) - r + NA_ROWS - 1
        bias = jnp.take(col_bias, row_off, axis=1).transpose(0, 2, 3, 1, 4)
        s = jnp.where(col_valid[None, None], s + bias[None], NEG_INF)
        p = jax.nn.softmax(s.reshape(B, H, nb, NA_QBLOCK, kh * NA_KBLOCK), axis=-1)
        p = p.reshape(B, H, nb, NA_QBLOCK, kh, NA_KBLOCK).astype(v.dtype)
        o = jnp.einsum('bhnqkj,bknjhd->bnqhd', p, vr)
        return o.reshape(B, GRID_W, H, hd)

    out = lax.map(row_fn, jnp.arange(rows))
    return out.transpose(1, 0, 2, 3, 4).reshape(B, S, H, hd)


def dilated_attention(q, k, v, dil, radius, slopes):
    B, S, H, hd = q.shape
    L = S // dil
    C = radius
    nc = -(-L // C)
    lp = nc * C
    scale = 1.0 / math.sqrt(hd)
    qs = jnp.pad(q.reshape(B, L, dil, H, hd), ((0, 0), (0, lp - L), (0, 0), (0, 0), (0, 0)))
    qc = qs.reshape(B, nc, C, dil, H, hd)

    def band(a):
        a = jnp.pad(a.reshape(B, L, dil, H, hd), ((0, 0), (C, lp - L + C), (0, 0), (0, 0), (0, 0)))
        a = a.reshape(B, nc + 2, C, dil, H, hd)
        return jnp.concatenate([a[:, :-2], a[:, 1:-1], a[:, 2:]], axis=2)

    kc, vc = band(k), band(v)
    s = jnp.einsum('bncrhd,bnjrhd->bhrncj', qc, kc).astype(jnp.float32) * scale
    q_i = np.arange(nc)[:, None] * C + np.arange(C)[None, :]
    k_i = (np.arange(nc)[:, None] - 1) * C + np.arange(3 * C)[None, :]
    delta = k_i[:, None, :] - q_i[:, :, None]
    valid = jnp.asarray((np.abs(delta) <= radius) & (k_i[:, None, :] >= 0) & (k_i[:, None, :] < L))
    dist = jnp.asarray(np.abs(delta) * dil, dtype=jnp.float32)
    bias = -slopes[:, None, None, None] * dist[None]
    s = jnp.where(valid[None, None, None], s + bias[None, :, None], NEG_INF)
    lse = jax.nn.logsumexp(s, axis=-1)
    p = jnp.exp(s - lse[..., None]).astype(v.dtype)
    o = jnp.einsum('bhrncj,bnjrhd->bncrhd', p, vc)
    o = o.reshape(B, lp, dil, H, hd)[:, :L].reshape(B, S, H, hd)
    lse = lse.transpose(0, 3, 4, 2, 1).reshape(B, lp, dil, H)[:, :L].reshape(B, S, H)
    return o, lse


def neighbourhood_mixer(h, w_in, rpb):
    B, S, _ = h.shape
    proj = h @ w_in
    q, k, v, gate = jnp.split(proj, 4, axis=-1)
    heads = lambda a: a.reshape(B, S, N_HEADS, HEAD_DIM)
    o = neighbourhood_attention(heads(q), heads(k), heads(v), rpb)
    return o.reshape(B, S, D_INNER), gate


def dilated_mixer(h, w_in):
    B, S, _ = h.shape
    proj = h @ w_in
    qkv = proj[..., :3 * N_DIL_GROUPS * D_INNER].reshape(B, S, N_DIL_GROUPS, 3, N_HEADS, HEAD_DIM)
    gate = proj[..., 3 * N_DIL_GROUPS * D_INNER:]
    slopes = alibi_slopes(N_HEADS)
    outs, lses = [], []
    for g, (window, dil) in enumerate(DIL_PAIRS):
        o_g, lse_g = dilated_attention(qkv[:, :, g, 0], qkv[:, :, g, 1], qkv[:, :, g, 2],
                                       dil, window // (2 * dil), slopes)
        outs.append(o_g)
        lses.append(lse_g)
    wts = jax.nn.softmax(jnp.stack(lses, axis=0), axis=0)
    o = jnp.einsum('gbsh,gbshd->bshd', wts, jnp.stack(outs, axis=0).astype(jnp.float32))
    return o.astype(h.dtype).reshape(B, S, D_INNER), gate


def setup_inputs(seed: int = 0) -> dict:
    key = jax.random.key(seed)
    ks = jax.random.split(key, 10)
    f32 = jnp.float32
    n_in_a = 4 * D_INNER
    n_in_b = (3 * N_DIL_GROUPS + 1) * D_INNER
    return {
        "x": jax.random.normal(ks[0], (BATCH, SEQ, D_MODEL), f32),
        "norm_0": 1.0 + 0.02 * jax.random.normal(ks[1], (D_MODEL,), f32),
        "w_in_0": jax.random.normal(ks[2], (D_MODEL, n_in_a), f32) * D_MODEL ** -0.5,
        "rpb_0": 0.02 * jax.random.normal(ks[3], (N_HEADS, 2 * NA_ROWS - 1, 2 * NA_COLS - 1), f32),
        "w_out_0": jax.random.normal(ks[4], (D_INNER, D_MODEL), f32) * D_INNER ** -0.5,
        "norm_1": 1.0 + 0.02 * jax.random.normal(ks[5], (D_MODEL,), f32),
        "w_in_1": jax.random.normal(ks[6], (D_MODEL, n_in_b), f32) * D_MODEL ** -0.5,
        "w_out_1": jax.random.normal(ks[7], (D_INNER, D_MODEL), f32) * D_INNER ** -0.5,
        "norm_f": 1.0 + 0.02 * jax.random.normal(ks[8], (D_MODEL,), f32),
    }


def reference(x, norm_0, w_in_0, rpb_0, w_out_0, norm_1, w_in_1, w_out_1, norm_f):
    layers = ((norm_0, w_in_0, rpb_0, w_out_0), (norm_1, w_in_1, None, w_out_1))
    for i in range(DEPTH):
        g, w_in, rpb, w_out = layers[i]
        h = rmsnorm(x, g)
        if i % N_MIXERS == 0:
            o, gate = neighbourhood_mixer(h, w_in, rpb)
        else:
            o, gate = dilated_mixer(h, w_in)
        x = x + (o * jax.nn.silu(gate)) @ w_out
    return rmsnorm(x, norm_f)
```

```python
import functools
import math

import numpy as np
import jax
import jax.numpy as jnp
from jax import lax
from jax.experimental import pallas as pl
from jax.experimental.pallas import tpu as pltpu

D_MODEL = 1024
HEAD_DIM = 64
N_HEADS = 16
GRID_W = 64
NA_ROWS = 8
NA_COLS = 16
DIL_PAIRS = ((128, 1), (512, 4), (2048, 16))
N_DIL_GROUPS = len(DIL_PAIRS)
RMS_EPS = 1e-6
NEG_INF = -1e30

HEADS_PER_STEP = 4
SLAB = HEADS_PER_STEP * HEAD_DIM
N_SLABS = D_MODEL // SLAB
Q_TILE = 128
NA_KEY_ROWS = 10
NA_WIN = NA_KEY_ROWS * GRID_W
DIL_RADIUS = 64
DIL_WIN = Q_TILE + 2 * DIL_RADIUS
LSE_LANES = 128
VMEM_LIMIT = 56 * 1024 * 1024

_NT_DIMS = (((1,), (1,)), ((), ()))


def _proj_kernel(x_ref, g_ref, w_ref, o_ref, h_ref):
    @pl.when(pl.program_id(1) == 0)
    def _():
        x = x_ref[...]
        ms = jnp.mean(x * x, axis=-1, keepdims=True)
        h_ref[...] = (x * lax.rsqrt(ms + RMS_EPS) * g_ref[...]).astype(h_ref.dtype)

    o_ref[...] = jnp.dot(h_ref[...], w_ref[...],
                         preferred_element_type=jnp.float32).astype(o_ref.dtype)


def _norm_proj(x, g, w, *, tm=1024, tn=1024):
    t, d = x.shape
    n = w.shape[1]
    return pl.pallas_call(
        _proj_kernel,
        grid=(t // tm, n // tn),
        in_specs=[
            pl.BlockSpec((tm, d), lambda i, j: (i, 0)),
            pl.BlockSpec((1, d), lambda i, j: (0, 0)),
            pl.BlockSpec((d, tn), lambda i, j: (0, j)),
        ],
        out_specs=pl.BlockSpec((tm, tn), lambda i, j: (i, j)),
        out_shape=jax.ShapeDtypeStruct((t, n), jnp.bfloat16),
        scratch_shapes=[pltpu.VMEM((tm, d), jnp.bfloat16)],
        compiler_params=pltpu.CompilerParams(
            dimension_semantics=("parallel", "arbitrary"),
            vmem_limit_bytes=VMEM_LIMIT),
        name="norm_proj",
    )(x, g.reshape(1, d), w)


def _slab_attention(q4, k4, v4, bias):
    lane = lax.broadcasted_iota(jnp.int32, q4.shape, 1) // HEAD_DIM
    zero = jnp.zeros_like(q4)
    q_stack = jnp.concatenate(
        [jnp.where(lane == h, q4, zero) for h in range(HEADS_PER_STEP)], axis=0)
    s = lax.dot_general(q_stack, k4, _NT_DIMS, preferred_element_type=jnp.float32)
    s = s + bias
    m = jnp.max(s, axis=-1, keepdims=True)
    p = jnp.exp(s - m)
    l = jnp.sum(p, axis=-1, keepdims=True)
    p = p.astype(jnp.bfloat16)
    inv_l = 1.0 / l
    out = None
    for h in range(HEADS_PER_STEP):
        rows = slice(h * Q_TILE, (h + 1) * Q_TILE)
        o_h = jnp.dot(p[rows], v4, preferred_element_type=jnp.float32) * inv_l[rows]
        out = o_h if out is None else jnp.where(lane == h, o_h, out)
    return out, m, l


def _na_key_row_start(i):
    return jnp.clip(2 * i - NA_ROWS // 2, 0, 128 - NA_KEY_ROWS)


def _na_kernel(q_ref, k_ref, v_ref, b_ref, o_ref):
    i = pl.program_id(2)
    start = pl.multiple_of(_na_key_row_start(i) * GRID_W, 128)
    k4 = k_ref[0, pl.ds(start, NA_WIN), :]
    v4 = v_ref[0, pl.ds(start, NA_WIN), :]
    bias = b_ref[0].reshape(HEADS_PER_STEP * Q_TILE, NA_WIN)
    out, _, _ = _slab_attention(q_ref[0], k4, v4, bias)
    o_ref[0] = out.astype(o_ref.dtype)


def _na_pattern(i, n_pairs):
    return jnp.where(i < 2, i + 1, jnp.where(i >= n_pairs - 2, i - (n_pairs - 2) + 3, 0))


def _na_bias_table(rpb, rows):
    n_pairs = rows // 2
    reps = (5, 0, 1, n_pairs - 2, n_pairs - 1)
    kh = min(NA_ROWS, rows)
    qr = np.arange(Q_TILE) // GRID_W
    qc = np.arange(Q_TILE) % GRID_W
    kr = np.arange(NA_WIN) // GRID_W
    kc = np.arange(NA_WIN) % GRID_W
    row_idx, col_idx, valid = [], [], []
    for i in reps:
        ks = int(np.clip(2 * i - NA_ROWS // 2, 0, rows - NA_KEY_ROWS))
        r = 2 * i + qr
        rs = np.clip(r - kh // 2, 0, rows - kh)
        k_abs = ks + kr
        row_ok = (k_abs[None, :] >= rs[:, None]) & (k_abs[None, :] < rs[:, None] + kh)
        ro = np.clip(k_abs[None, :] - r[:, None] + NA_ROWS - 1, 0, 2 * NA_ROWS - 2)
        cs = np.clip(qc - NA_COLS // 2, 0, GRID_W - NA_COLS)
        col_ok = (kc[None, :] >= cs[:, None]) & (kc[None, :] < cs[:, None] + NA_COLS)
        co = np.clip(kc[None, :] - qc[:, None] + NA_COLS - 1, 0, 2 * NA_COLS - 2)
        row_idx.append(ro)
        col_idx.append(co)
        valid.append(row_ok & col_ok)
    row_idx = np.stack(row_idx)
    col_idx = np.stack(col_idx)
    valid = np.stack(valid)
    flat = rpb.astype(jnp.float32).reshape(N_HEADS, -1)
    gathered = jnp.take(flat, jnp.asarray(row_idx * (2 * NA_COLS - 1) + col_idx), axis=1)
    table = jnp.where(jnp.asarray(valid)[None], gathered, NEG_INF)
    return table.transpose(1, 0, 2, 3)


def _na_attention(proj, bias_table):
    b, s, _ = proj.shape
    n_pairs = s // Q_TILE
    return pl.pallas_call(
        _na_kernel,
        grid=(b, N_SLABS, n_pairs),
        in_specs=[
            pl.BlockSpec((1, Q_TILE, SLAB), lambda bi, hg, i: (bi, i, hg)),
            pl.BlockSpec((1, s, SLAB), lambda bi, hg, i: (bi, 0, N_SLABS + hg)),
            pl.BlockSpec((1, s, SLAB), lambda bi, hg, i: (bi, 0, 2 * N_SLABS + hg)),
            pl.BlockSpec((1, HEADS_PER_STEP, Q_TILE, NA_WIN),
                         lambda bi, hg, i: (_na_pattern(i, n_pairs), hg, 0, 0)),
        ],
        out_specs=pl.BlockSpec((1, Q_TILE, SLAB), lambda bi, hg, i: (bi, i, hg)),
        out_shape=jax.ShapeDtypeStruct((b, s, D_MODEL), jnp.bfloat16),
        compiler_params=pltpu.CompilerParams(
            dimension_semantics=("parallel", "parallel", "arbitrary"),
            vmem_limit_bytes=VMEM_LIMIT),
        name="na_attention",
    )(proj, proj, proj, bias_table)


def _dil_kernel(q_ref, ka_ref, kb_ref, kc_ref, va_ref, vb_ref, vc_ref, b_ref,
                o_ref, lse_ref):
    i = pl.program_id(2)
    n_tiles = pl.num_programs(2)
    pattern = jnp.where(i == 0, 1, jnp.where(i == n_tiles - 1, 2, 0))
    k_win = jnp.concatenate([ka_ref[0], kb_ref[0], kc_ref[0]], axis=0)
    v_win = jnp.concatenate([va_ref[0], vb_ref[0], vc_ref[0]], axis=0)
    lane = lax.broadcasted_iota(jnp.int32, (Q_TILE, LSE_LANES), 1)
    lse_tile = jnp.zeros((Q_TILE, LSE_LANES), jnp.float32)
    for hg in range(N_SLABS):
        cols = slice(hg * SLAB, (hg + 1) * SLAB)
        bias = b_ref[pattern, hg * HEADS_PER_STEP:(hg + 1) * HEADS_PER_STEP]
        bias = bias.reshape(HEADS_PER_STEP * Q_TILE, DIL_WIN)
        out, m, l = _slab_attention(q_ref[0, :, cols], k_win[:, cols], v_win[:, cols], bias)
        o_ref[0, :, cols] = out.astype(o_ref.dtype)
        lse = m + jnp.log(l)
        for h in range(HEADS_PER_STEP):
            lse_h = lse[h * Q_TILE:(h + 1) * Q_TILE]
            lse_tile = jnp.where(lane == hg * HEADS_PER_STEP + h, lse_h, lse_tile)
    lse_ref[0] = lse_tile


def _alibi_slopes():
    return np.asarray(2.0 ** (-8.0 * (np.arange(N_HEADS) + 1) / N_HEADS), dtype=np.float32)


def _dil_bias_table(dil):
    qi = np.arange(Q_TILE)[:, None]
    kj = np.arange(DIL_WIN)[None, :]
    delta = kj - DIL_RADIUS - qi
    in_band = np.abs(delta) <= DIL_RADIUS
    valid = np.stack([in_band,
                      in_band & (kj >= DIL_RADIUS),
                      in_band & (kj < DIL_WIN - DIL_RADIUS)])
    dist = jnp.asarray(np.abs(delta) * dil, dtype=jnp.float32)
    bias = -jnp.asarray(_alibi_slopes())[:, None, None] * dist[None]
    return jnp.where(jnp.asarray(valid)[:, None], bias[None], NEG_INF)


def _dil_attention(proj, group, dil):
    b, s, n = proj.shape
    l = s // dil
    n_tiles = l // Q_TILE
    half = Q_TILE // 2
    n_half = l // half
    blocks_per_token = n // D_MODEL
    view = proj.reshape(b, l, dil * n)
    qcol = 3 * group

    def col(r, which):
        return r * blocks_per_token + qcol + which

    def spec_mid(which):
        return pl.BlockSpec((1, Q_TILE, D_MODEL), lambda bi, r, i: (bi, i, col(r, which)))

    def spec_lo(which):
        return pl.BlockSpec((1, half, D_MODEL),
                            lambda bi, r, i: (bi, jnp.maximum(2 * i - 1, 0), col(r, which)))

    def spec_hi(which):
        return pl.BlockSpec((1, half, D_MODEL),
                            lambda bi, r, i: (bi, jnp.minimum(2 * i + 2, n_half - 1), col(r, which)))

    bias = _dil_bias_table(dil)
    o, lse = pl.pallas_call(
        _dil_kernel,
        grid=(b, dil, n_tiles),
        in_specs=[
            spec_mid(0),
            spec_lo(1), spec_mid(1), spec_hi(1),
            spec_lo(2), spec_mid(2), spec_hi(2),
            pl.BlockSpec(bias.shape, lambda bi, r, i: (0, 0, 0, 0)),
        ],
        out_specs=[
            pl.BlockSpec((1, Q_TILE, D_MODEL), lambda bi, r, i: (bi, i, r)),
            pl.BlockSpec((1, Q_TILE, LSE_LANES), lambda bi, r, i: (bi, i, r)),
        ],
        out_shape=[
            jax.ShapeDtypeStruct((b, l, dil * D_MODEL), jnp.bfloat16),
            jax.ShapeDtypeStruct((b, l, dil * LSE_LANES), jnp.float32),
        ],
        compiler_params=pltpu.CompilerParams(
            dimension_semantics=("parallel", "parallel", "arbitrary"),
            vmem_limit_bytes=VMEM_LIMIT),
        name=f"dilated_attention_{dil}",
    )(view, view, view, view, view, view, view, bias)
    return o.reshape(b, s, D_MODEL), lse.reshape(b, s, LSE_LANES)


def _silu(x):
    return x * jax.nn.sigmoid(x)


def _out0_kernel(o_ref, gate_ref, x_ref, w_ref, y_ref):
    o = o_ref[...].astype(jnp.float32)
    gate = gate_ref[...].astype(jnp.float32)
    z = (o * _silu(gate)).astype(jnp.bfloat16)
    y_ref[...] = x_ref[...] + jnp.dot(z, w_ref[...], preferred_element_type=jnp.float32)


def _out_proj0(o, proj, x, w_out, *, tm=512):
    t, d = x.shape
    gate_block = proj.shape[1] // d - 1
    return pl.pallas_call(
        _out0_kernel,
        grid=(t // tm,),
        in_specs=[
            pl.BlockSpec((tm, d), lambda i: (i, 0)),
            pl.BlockSpec((tm, d), lambda i: (i, gate_block)),
            pl.BlockSpec((tm, d), lambda i: (i, 0)),
            pl.BlockSpec((d, d), lambda i: (0, 0)),
        ],
        out_specs=pl.BlockSpec((tm, d), lambda i: (i, 0)),
        out_shape=jax.ShapeDtypeStruct((t, d), jnp.float32),
        compiler_params=pltpu.CompilerParams(
            dimension_semantics=("parallel",), vmem_limit_bytes=VMEM_LIMIT),
        name="out_proj0",
    )(o, proj, x, w_out)


def _out1_kernel(o0_ref, o1_ref, o2_ref, l0_ref, l1_ref, l2_ref, gate_ref, x_ref,
                 w_ref, e_ref, g_ref, y_ref):
    lses = [l0_ref[...], l1_ref[...], l2_ref[...]]
    m = jnp.maximum(jnp.maximum(lses[0], lses[1]), lses[2])
    es = [jnp.exp(v - m) for v in lses]
    denom = es[0] + es[1] + es[2]
    o = None
    for e, o_ref in zip(es, (o0_ref, o1_ref, o2_ref)):
        w = e / denom
        hi = w.astype(jnp.bfloat16)
        lo = (w - hi.astype(jnp.float32)).astype(jnp.bfloat16)
        w_full = jnp.dot(jnp.concatenate([hi, lo], axis=1), e_ref[...],
                         preferred_element_type=jnp.float32)
        term = w_full * o_ref[...].astype(jnp.float32)
        o = term if o is None else o + term
    gate = gate_ref[...].astype(jnp.float32)
    z = (o * _silu(gate)).astype(jnp.bfloat16)
    x = x_ref[...] + jnp.dot(z, w_ref[...], preferred_element_type=jnp.float32)
    ms = jnp.mean(x * x, axis=-1, keepdims=True)
    y_ref[...] = x * lax.rsqrt(ms + RMS_EPS) * g_ref[...]


def _head_expansion():
    e = np.zeros((LSE_LANES, D_MODEL), np.float32)
    for h in range(N_HEADS):
        e[h, h * HEAD_DIM:(h + 1) * HEAD_DIM] = 1.0
    return jnp.asarray(np.concatenate([e, e], axis=0), dtype=jnp.bfloat16)


def _out_proj1(os_, lses, proj, x, w_out, norm_f, *, tm=512):
    t, d = x.shape
    gate_block = proj.shape[1] // d - 1
    tok = pl.BlockSpec((tm, d), lambda i: (i, 0))
    lse_spec = pl.BlockSpec((tm, LSE_LANES), lambda i: (i, 0))
    return pl.pallas_call(
        _out1_kernel,
        grid=(t // tm,),
        in_specs=[
            tok, tok, tok, lse_spec, lse_spec, lse_spec,
            pl.BlockSpec((tm, d), lambda i: (i, gate_block)),
            tok,
            pl.BlockSpec((d, d), lambda i: (0, 0)),
            pl.BlockSpec((2 * LSE_LANES, d), lambda i: (0, 0)),
            pl.BlockSpec((1, d), lambda i: (0, 0)),
        ],
        out_specs=tok,
        out_shape=jax.ShapeDtypeStruct((t, d), jnp.float32),
        compiler_params=pltpu.CompilerParams(
            dimension_semantics=("parallel",), vmem_limit_bytes=VMEM_LIMIT),
        name="out_proj1",
    )(*os_, *lses, proj, x, w_out, _head_expansion(), norm_f.reshape(1, d))


def _prep_w_in(w_in, q_blocks):
    scale = np.ones((w_in.shape[1],), np.float32)
    for blk in q_blocks:
        scale[blk * D_MODEL:(blk + 1) * D_MODEL] = 1.0 / math.sqrt(HEAD_DIM)
    return (w_in * jnp.asarray(scale)[None, :]).astype(jnp.bfloat16)


def kernel(x, norm_0, w_in_0, rpb_0, w_out_0, norm_1, w_in_1, w_out_1, norm_f):
    b, s, d = x.shape
    t = b * s
    x2 = x.reshape(t, d)

    proj0 = _norm_proj(x2, norm_0, _prep_w_in(w_in_0, (0,)))
    o0 = _na_attention(proj0.reshape(b, s, -1), _na_bias_table(rpb_0, s // GRID_W))
    x2 = _out_proj0(o0.reshape(t, d), proj0, x2, w_out_0.astype(jnp.bfloat16))

    proj1 = _norm_proj(x2, norm_1, _prep_w_in(w_in_1, tuple(3 * g for g in range(N_DIL_GROUPS))))
    outs, lses = [], []
    for g, (window, dil) in enumerate(DIL_PAIRS):
        assert window // (2 * dil) == DIL_RADIUS
        o_g, lse_g = _dil_attention(proj1.reshape(b, s, -1), g, dil)
        outs.append(o_g.reshape(t, d))
        lses.append(lse_g.reshape(t, LSE_LANES))
    y = _out_proj1(outs, lses, proj1, x2, w_out_1.astype(jnp.bfloat16), norm_f)
    return y.reshape(b, s, d)
```

```python
import functools
import math

import numpy as np
import jax
import jax.numpy as jnp
from jax import lax
from jax.experimental import pallas as pl
from jax.experimental.pallas import tpu as pltpu

D_MODEL = 1024
HEAD_DIM = 64
N_HEADS = 16
GRID_W = 64
NA_ROWS = 8
NA_COLS = 16
DIL_PAIRS = ((128, 1), (512, 4), (2048, 16))
N_DIL_GROUPS = len(DIL_PAIRS)
RMS_EPS = 1e-6
NEG_INF = -1e30

LANES = 128
N_LANE_SLABS = D_MODEL // LANES
HEADS_PER_STEP = 4
SLAB = HEADS_PER_STEP * HEAD_DIM
N_SLABS = D_MODEL // SLAB
Q_TILE = 128
NA_KEY_ROWS = 10
NA_WIN = NA_KEY_ROWS * GRID_W
DIL_RADIUS = 64
DIL_WIN = Q_TILE + 2 * DIL_RADIUS
LSE_LANES = LANES
VMEM_LIMIT = 56 * 1024 * 1024

_NT_DIMS = (((1,), (1,)), ((), ()))


def _proj_kernel(x_ref, g_ref, w_ref, o_ref, h_ref, *slab_refs, dil):
    tm = x_ref.shape[0]

    @pl.when(pl.program_id(1) == 0)
    def _():
        x = x_ref[...]
        ms = jnp.mean(x * x, axis=-1, keepdims=True)
        hn = x * lax.rsqrt(ms + RMS_EPS) * g_ref[...]
        if dil == 1:
            h_ref[...] = hn.astype(h_ref.dtype)
        else:
            slab_ref, = slab_refs
            n = tm // dil
            for s in range(N_LANE_SLABS):
                slab_ref[s] = hn[:, s * LANES:(s + 1) * LANES]
            for r in range(dil):
                for s in range(N_LANE_SLABS):
                    rows = slab_ref[s, pl.ds(r, n, stride=dil), :]
                    h_ref[r * n:(r + 1) * n, s * LANES:(s + 1) * LANES] = rows.astype(h_ref.dtype)

    res = jnp.dot(h_ref[...], w_ref[...], preferred_element_type=jnp.float32)
    o_ref[0] = res.astype(o_ref.dtype).reshape(o_ref.shape[1:])


def _norm_proj(x, g, w, dil, *, tm=1024, tn=1024):
    b, s, d = x.shape
    n = w.shape[1]
    tiles = s // tm
    scratch = [pltpu.VMEM((tm, d), jnp.bfloat16)]
    if dil > 1:
        scratch.append(pltpu.VMEM((N_LANE_SLABS, tm, LANES), jnp.float32))
    return pl.pallas_call(
        functools.partial(_proj_kernel, dil=dil),
        grid=(b * tiles, n // tn),
        in_specs=[
            pl.BlockSpec((None, tm, d), lambda i, j: (i // tiles, i % tiles, 0)),
            pl.BlockSpec((1, d), lambda i, j: (0, 0)),
            pl.BlockSpec((d, tn), lambda i, j: (0, j)),
        ],
        out_specs=pl.BlockSpec((1, dil, tm // dil, tn),
                               lambda i, j: (i // tiles, 0, i % tiles, j)),
        out_shape=jax.ShapeDtypeStruct((b, dil, s // dil, n), jnp.bfloat16),
        scratch_shapes=scratch,
        compiler_params=pltpu.CompilerParams(
            dimension_semantics=("parallel", "arbitrary"),
            vmem_limit_bytes=VMEM_LIMIT),
        name=f"norm_proj_{dil}",
    )(x, g.reshape(1, d), w)


def _slab_attention(q4, k4, v4, bias):
    lane = lax.broadcasted_iota(jnp.int32, q4.shape, 1) // HEAD_DIM
    zero = jnp.zeros_like(q4)
    q_stack = jnp.concatenate(
        [jnp.where(lane == h, q4, zero) for h in range(HEADS_PER_STEP)], axis=0)
    s = lax.dot_general(q_stack, k4, _NT_DIMS, preferred_element_type=jnp.float32)
    s = s + bias
    m = jnp.max(s, axis=-1, keepdims=True)
    p = jnp.exp(s - m)
    l = jnp.sum(p, axis=-1, keepdims=True)
    p = p.astype(jnp.bfloat16)
    inv_l = 1.0 / l
    out = None
    for h in range(HEADS_PER_STEP):
        rows = slice(h * Q_TILE, (h + 1) * Q_TILE)
        o_h = jnp.dot(p[rows], v4, preferred_element_type=jnp.float32) * inv_l[rows]
        out = o_h if out is None else jnp.where(lane == h, o_h, out)
    return out, m, l


def _na_key_row_start(i, rows):
    return jnp.clip(2 * i - NA_ROWS // 2, 0, rows - NA_KEY_ROWS)


def _na_kernel(q_ref, k_ref, v_ref, b_ref, o_ref):
    i = pl.program_id(2)
    rows = k_ref.shape[1] // GRID_W
    start = pl.multiple_of(_na_key_row_start(i, rows) * GRID_W, Q_TILE)
    k4 = k_ref[0, pl.ds(start, NA_WIN), :]
    v4 = v_ref[0, pl.ds(start, NA_WIN), :]
    bias = b_ref[0].reshape(HEADS_PER_STEP * Q_TILE, NA_WIN)
    out, _, _ = _slab_attention(q_ref[0], k4, v4, bias)
    o_ref[0] = out.astype(o_ref.dtype)


def _na_pattern(i, n_pairs):
    return jnp.where(i < 2, i + 1, jnp.where(i >= n_pairs - 2, i - (n_pairs - 2) + 3, 0))


def _na_bias_table(rpb, rows):
    n_pairs = rows // 2
    reps = (5, 0, 1, n_pairs - 2, n_pairs - 1)
    kh = min(NA_ROWS, rows)
    n_col_off = 2 * NA_COLS - 1
    padded = jnp.pad(rpb.astype(jnp.float32), ((0, 0), (0, 0), (GRID_W, GRID_W)))
    col_part = jnp.stack(
        [padded[:, :, GRID_W + NA_COLS - 1 - c: 2 * GRID_W + NA_COLS - 1 - c] for c in range(GRID_W)],
        axis=2)
    c = np.arange(GRID_W)[:, None]
    kc = np.arange(GRID_W)[None, :]
    cs = np.clip(c - NA_COLS // 2, 0, GRID_W - NA_COLS)
    col_ok = (kc >= cs) & (kc < cs + NA_COLS)
    assert np.all(((kc - c + NA_COLS - 1 >= 0) & (kc - c + NA_COLS - 1 < n_col_off))[col_ok])
    col_part = jnp.where(jnp.asarray(col_ok), col_part, NEG_INF)
    masked = jnp.full((N_HEADS, GRID_W, GRID_W), NEG_INF, jnp.float32)
    patterns = []
    for i in reps:
        ks = int(np.clip(2 * i - NA_ROWS // 2, 0, rows - NA_KEY_ROWS))
        q_rows = []
        for qr in range(Q_TILE // GRID_W):
            r = 2 * i + qr
            rs = int(np.clip(r - kh // 2, 0, rows - kh))
            blocks = []
            for kr in range(NA_KEY_ROWS):
                k_abs = ks + kr
                ok = rs <= k_abs < rs + kh
                blocks.append(col_part[:, k_abs - r + NA_ROWS - 1] if ok else masked)
            q_rows.append(jnp.concatenate(blocks, axis=-1))
        patterns.append(jnp.concatenate(q_rows, axis=1))
    return jnp.stack(patterns)


def _na_attention(proj, bias_table):
    b, s, _ = proj.shape
    n_pairs = s // Q_TILE
    return pl.pallas_call(
        _na_kernel,
        grid=(b, N_SLABS, n_pairs),
        in_specs=[
            pl.BlockSpec((1, Q_TILE, SLAB), lambda bi, hg, i: (bi, i, hg)),
            pl.BlockSpec((1, s, SLAB), lambda bi, hg, i: (bi, 0, N_SLABS + hg)),
            pl.BlockSpec((1, s, SLAB), lambda bi, hg, i: (bi, 0, 2 * N_SLABS + hg)),
            pl.BlockSpec((1, HEADS_PER_STEP, Q_TILE, NA_WIN),
                         lambda bi, hg, i: (_na_pattern(i, n_pairs), hg, 0, 0)),
        ],
        out_specs=pl.BlockSpec((1, Q_TILE, SLAB), lambda bi, hg, i: (bi, i, hg)),
        out_shape=jax.ShapeDtypeStruct((b, s, D_MODEL), jnp.bfloat16),
        compiler_params=pltpu.CompilerParams(
            dimension_semantics=("parallel", "parallel", "arbitrary"),
            vmem_limit_bytes=VMEM_LIMIT),
        name="na_attention",
    )(proj, proj, proj, bias_table)


def _dil_kernel(q_ref, ka_ref, kb_ref, kc_ref, va_ref, vb_ref, vc_ref, b_ref,
                o_ref, lse_ref):
    i = pl.program_id(2)
    n_tiles = pl.num_programs(2)
    pattern = jnp.where(i == 0, 1, jnp.where(i == n_tiles - 1, 2, 0))
    k_win = jnp.concatenate([ka_ref[0, 0], kb_ref[0, 0], kc_ref[0, 0]], axis=0)
    v_win = jnp.concatenate([va_ref[0, 0], vb_ref[0, 0], vc_ref[0, 0]], axis=0)
    lane = lax.broadcasted_iota(jnp.int32, (Q_TILE, LSE_LANES), 1)
    lse_tile = jnp.zeros((Q_TILE, LSE_LANES), jnp.float32)
    for hg in range(N_SLABS):
        cols = slice(hg * SLAB, (hg + 1) * SLAB)
        bias = b_ref[pattern, hg * HEADS_PER_STEP:(hg + 1) * HEADS_PER_STEP]
        bias = bias.reshape(HEADS_PER_STEP * Q_TILE, DIL_WIN)
        out, m, l = _slab_attention(q_ref[0, 0, :, cols], k_win[:, cols], v_win[:, cols], bias)
        o_ref[0, 0, :, cols] = out.astype(o_ref.dtype)
        lse = m + jnp.log(l)
        for h in range(HEADS_PER_STEP):
            lse_h = lse[h * Q_TILE:(h + 1) * Q_TILE]
            lse_tile = jnp.where(lane == hg * HEADS_PER_STEP + h, lse_h, lse_tile)
    lse_ref[0, 0] = lse_tile


def _alibi_slopes():
    return np.asarray(2.0 ** (-8.0 * (np.arange(N_HEADS) + 1) / N_HEADS), dtype=np.float32)


def _dil_bias_table(dil):
    qi = np.arange(Q_TILE)[:, None]
    kj = np.arange(DIL_WIN)[None, :]
    delta = kj - DIL_RADIUS - qi
    in_band = np.abs(delta) <= DIL_RADIUS
    valid = np.stack([in_band,
                      in_band & (kj >= DIL_RADIUS),
                      in_band & (kj < DIL_WIN - DIL_RADIUS)])
    dist = jnp.asarray(np.abs(delta) * dil, dtype=jnp.float32)
    bias = -jnp.asarray(_alibi_slopes())[:, None, None] * dist[None]
    return jnp.where(jnp.asarray(valid)[:, None], bias[None], NEG_INF)


def _dil_attention(qkv, dil):
    b, _, l, _ = qkv.shape
    n_tiles = l // Q_TILE
    half = Q_TILE // 2
    n_half = l // half

    def spec_mid(which):
        return pl.BlockSpec((1, 1, Q_TILE, D_MODEL), lambda bi, r, i: (bi, r, i, which))

    def spec_lo(which):
        return pl.BlockSpec((1, 1, half, D_MODEL),
                            lambda bi, r, i: (bi, r, jnp.maximum(2 * i - 1, 0), which))

    def spec_hi(which):
        return pl.BlockSpec((1, 1, half, D_MODEL),
                            lambda bi, r, i: (bi, r, jnp.minimum(2 * i + 2, n_half - 1), which))

    bias = _dil_bias_table(dil)
    return pl.pallas_call(
        _dil_kernel,
        grid=(b, dil, n_tiles),
        in_specs=[
            spec_mid(0),
            spec_lo(1), spec_mid(1), spec_hi(1),
            spec_lo(2), spec_mid(2), spec_hi(2),
            pl.BlockSpec(bias.shape, lambda bi, r, i: (0, 0, 0, 0)),
        ],
        out_specs=[
            pl.BlockSpec((1, 1, Q_TILE, D_MODEL), lambda bi, r, i: (bi, r, i, 0)),
            pl.BlockSpec((1, 1, Q_TILE, LSE_LANES), lambda bi, r, i: (bi, r, i, 0)),
        ],
        out_shape=[
            jax.ShapeDtypeStruct((b, dil, l, D_MODEL), jnp.bfloat16),
            jax.ShapeDtypeStruct((b, dil, l, LSE_LANES), jnp.float32),
        ],
        compiler_params=pltpu.CompilerParams(
            dimension_semantics=("parallel", "parallel", "arbitrary"),
            vmem_limit_bytes=VMEM_LIMIT),
        name=f"dilated_attention_{dil}",
    )(qkv, qkv, qkv, qkv, qkv, qkv, qkv, bias)


def _silu(x):
    return x * jax.nn.sigmoid(x)


def _out0_kernel(o_ref, gate_ref, x_ref, w_ref, y_ref):
    o = o_ref[...].astype(jnp.float32)
    gate = gate_ref[...].astype(jnp.float32)
    z = (o * _silu(gate)).astype(jnp.bfloat16)
    y_ref[...] = x_ref[...] + jnp.dot(z, w_ref[...], preferred_element_type=jnp.float32)


def _out_proj0(o, proj, x, w_out, *, tm=512):
    t, d = x.shape
    gate_block = proj.shape[1] // d - 1
    return pl.pallas_call(
        _out0_kernel,
        grid=(t // tm,),
        in_specs=[
            pl.BlockSpec((tm, d), lambda i: (i, 0)),
            pl.BlockSpec((tm, d), lambda i: (i, gate_block)),
            pl.BlockSpec((tm, d), lambda i: (i, 0)),
            pl.BlockSpec((d, d), lambda i: (0, 0)),
        ],
        out_specs=pl.BlockSpec((tm, d), lambda i: (i, 0)),
        out_shape=jax.ShapeDtypeStruct((t, d), jnp.float32),
        compiler_params=pltpu.CompilerParams(
            dimension_semantics=("parallel",), vmem_limit_bytes=VMEM_LIMIT),
        name="out_proj0",
    )(o, proj, x, w_out)


def _natural_order(ref, slab_ref, dil):
    _, n, c = ref.shape
    if dil == 1:
        return ref[0].astype(jnp.float32)
    pieces = []
    for s in range(c // LANES):
        for r in range(dil):
            slab_ref[s, pl.ds(r, n, stride=dil), :] = (
                ref[r, :, s * LANES:(s + 1) * LANES].astype(jnp.float32))
        pieces.append(slab_ref[s])
    return pieces[0] if len(pieces) == 1 else jnp.concatenate(pieces, axis=1)


def _out1_kernel(o0_ref, o1_ref, o2_ref, l0_ref, l1_ref, l2_ref, gate_ref, x_ref,
                 w_ref, e_ref, g_ref, y_ref, slab_ref):
    dils = [dil for _, dil in DIL_PAIRS]
    lses = [_natural_order(ref.at[0], slab_ref, dil)
            for ref, dil in zip((l0_ref, l1_ref, l2_ref), dils)]
    m = jnp.maximum(jnp.maximum(lses[0], lses[1]), lses[2])
    es = [jnp.exp(v - m) for v in lses]
    denom = es[0] + es[1] + es[2]
    o = None
    for e, o_ref, dil in zip(es, (o0_ref, o1_ref, o2_ref), dils):
        w = e / denom
        hi = w.astype(jnp.bfloat16)
        lo = (w - hi.astype(jnp.float32)).astype(jnp.bfloat16)
        w_full = jnp.dot(jnp.concatenate([hi, lo], axis=1), e_ref[...],
                         preferred_element_type=jnp.float32)
        term = w_full * _natural_order(o_ref.at[0], slab_ref, dil)
        o = term if o is None else o + term
    gate = gate_ref[0].astype(jnp.float32)
    z = (o * _silu(gate)).astype(jnp.bfloat16)
    x = x_ref[0] + jnp.dot(z, w_ref[...], preferred_element_type=jnp.float32)
    ms = jnp.mean(x * x, axis=-1, keepdims=True)
    y_ref[0] = x * lax.rsqrt(ms + RMS_EPS) * g_ref[...]


def _head_expansion():
    e = np.zeros((LSE_LANES, D_MODEL), np.float32)
    for h in range(N_HEADS):
        e[h, h * HEAD_DIM:(h + 1) * HEAD_DIM] = 1.0
    return jnp.asarray(np.concatenate([e, e], axis=0), dtype=jnp.bfloat16)


def _out_proj1(os_, lses, proj, x, w_out, norm_f, *, tm=512):
    b, s, d = x.shape
    gate_block = proj.shape[2] // d - 1
    dils = [dil for _, dil in DIL_PAIRS]

    def grouped(width, dil):
        return pl.BlockSpec((1, dil, tm // dil, width), lambda bi, i: (bi, 0, i, 0))

    tok = pl.BlockSpec((1, tm, d), lambda bi, i: (bi, i, 0))
    return pl.pallas_call(
        _out1_kernel,
        grid=(b, s // tm),
        in_specs=(
            [grouped(d, dil) for dil in dils]
            + [grouped(LSE_LANES, dil) for dil in dils]
            + [pl.BlockSpec((1, tm, d), lambda bi, i: (bi, i, gate_block)),
               tok,
               pl.BlockSpec((d, d), lambda bi, i: (0, 0)),
               pl.BlockSpec((2 * LSE_LANES, d), lambda bi, i: (0, 0)),
               pl.BlockSpec((1, d), lambda bi, i: (0, 0))]),
        out_specs=tok,
        out_shape=jax.ShapeDtypeStruct((b, s, d), jnp.float32),
        scratch_shapes=[pltpu.VMEM((N_LANE_SLABS, tm, LANES), jnp.float32)],
        compiler_params=pltpu.CompilerParams(
            dimension_semantics=("parallel", "parallel"), vmem_limit_bytes=VMEM_LIMIT),
        name="out_proj1",
    )(*os_, *lses, proj, x, w_out, _head_expansion(), norm_f.reshape(1, d))


def _prep_w_in(w_in, q_blocks):
    scale = np.ones((w_in.shape[1],), np.float32)
    for blk in q_blocks:
        scale[blk * D_MODEL:(blk + 1) * D_MODEL] = 1.0 / math.sqrt(HEAD_DIM)
    return (w_in * jnp.asarray(scale)[None, :]).astype(jnp.bfloat16)


def kernel(x, norm_0, w_in_0, rpb_0, w_out_0, norm_1, w_in_1, w_out_1, norm_f):
    b, s, d = x.shape
    t = b * s

    proj0 = _norm_proj(x, norm_0, _prep_w_in(w_in_0, (0,)), 1).reshape(b, s, -1)
    o0 = _na_attention(proj0, _na_bias_table(rpb_0, s // GRID_W))
    x1 = _out_proj0(o0.reshape(t, d), proj0.reshape(t, -1), x.reshape(t, d),
                    w_out_0.astype(jnp.bfloat16)).reshape(b, s, d)

    w1 = _prep_w_in(w_in_1, tuple(3 * g for g in range(N_DIL_GROUPS)))
    qkv_cols = 3 * d
    gate_w = w1[:, N_DIL_GROUPS * qkv_cols:]
    outs, lses, proj_gate = [], [], None
    for g, (window, dil) in enumerate(DIL_PAIRS):
        assert window // (2 * dil) == DIL_RADIUS
        w_g = w1[:, g * qkv_cols:(g + 1) * qkv_cols]
        if dil == 1:
            w_g = jnp.concatenate([w_g, gate_w], axis=1)
        qkv = _norm_proj(x1, norm_1, w_g, dil)
        if dil == 1:
            proj_gate = qkv.reshape(b, s, -1)
        o_g, lse_g = _dil_attention(qkv, dil)
        outs.append(o_g)
        lses.append(lse_g)
    return _out_proj1(outs, lses, proj_gate, x1, w_out_1.astype(jnp.bfloat16), norm_f)
```

```python
import functools
import math

import numpy as np
import jax
import jax.numpy as jnp
from jax import lax
from jax.experimental import pallas as pl
from jax.experimental.pallas import tpu as pltpu

D_MODEL = 1024
HEAD_DIM = 64
N_HEADS = 16
GRID_W = 64
NA_ROWS = 8
NA_COLS = 16
DIL_PAIRS = ((128, 1), (512, 4), (2048, 16))
N_DIL_GROUPS = len(DIL_PAIRS)
RMS_EPS = 1e-6
NEG_INF = -1e30

LANES = 128
N_LANE_SLABS = D_MODEL // LANES
HEADS_PER_STEP = 4
SLAB = HEADS_PER_STEP * HEAD_DIM
N_SLABS = D_MODEL // SLAB
Q_TILE = 128
NA_KEY_ROWS = 10
NA_WIN = NA_KEY_ROWS * GRID_W
ATTN_TILES_PER_STEP = 2
DIL_RADIUS = 64
DIL_WIN = Q_TILE + 2 * DIL_RADIUS
LSE_LANES = LANES
VMEM_LIMIT = 56 * 1024 * 1024

_NT_DIMS = (((1,), (1,)), ((), ()))


def _proj_kernel(x_ref, g_ref, w_ref, o_ref, h_ref, *slab_refs, dil):
    tm = x_ref.shape[0]

    @pl.when(pl.program_id(1) == 0)
    def _():
        x = x_ref[...]
        ms = jnp.mean(x * x, axis=-1, keepdims=True)
        hn = x * lax.rsqrt(ms + RMS_EPS) * g_ref[...]
        if dil == 1:
            h_ref[...] = hn.astype(h_ref.dtype)
        else:
            slab_ref, = slab_refs
            n = tm // dil
            for s in range(N_LANE_SLABS):
                slab_ref[s] = hn[:, s * LANES:(s + 1) * LANES]
            for r in range(dil):
                for s in range(N_LANE_SLABS):
                    rows = slab_ref[s, pl.ds(r, n, stride=dil), :]
                    h_ref[r * n:(r + 1) * n, s * LANES:(s + 1) * LANES] = rows.astype(h_ref.dtype)

    res = jnp.dot(h_ref[...], w_ref[...], preferred_element_type=jnp.float32)
    o_ref[0] = res.astype(o_ref.dtype).reshape(o_ref.shape[1:])


def _norm_proj(x, g, w, dil, *, tm=1024, tn=1024):
    b, s, d = x.shape
    n = w.shape[1]
    tiles = s // tm
    scratch = [pltpu.VMEM((tm, d), jnp.bfloat16)]
    if dil > 1:
        scratch.append(pltpu.VMEM((N_LANE_SLABS, tm, LANES), jnp.float32))
    return pl.pallas_call(
        functools.partial(_proj_kernel, dil=dil),
        grid=(b * tiles, n // tn),
        in_specs=[
            pl.BlockSpec((None, tm, d), lambda i, j: (i // tiles, i % tiles, 0)),
            pl.BlockSpec((1, d), lambda i, j: (0, 0)),
            pl.BlockSpec((d, tn), lambda i, j: (0, j)),
        ],
        out_specs=pl.BlockSpec((1, dil, tm // dil, tn),
                               lambda i, j: (i // tiles, 0, i % tiles, j)),
        out_shape=jax.ShapeDtypeStruct((b, dil, s // dil, n), jnp.bfloat16),
        scratch_shapes=scratch,
        compiler_params=pltpu.CompilerParams(
            dimension_semantics=("parallel", "arbitrary"),
            vmem_limit_bytes=VMEM_LIMIT),
        name=f"norm_proj_{dil}",
    )(x, g.reshape(1, d), w)


def _attn_kernel(q_ref, k_ref, v_ref, b_ref, *rest, win, tiles_per_seq, key_start,
                 pattern_of, emit_lse):
    if emit_lse:
        o_ref, lse_ref = rest[:2]
        scratch = rest[2:]
    else:
        o_ref, lse_ref = rest[0], None
        scratch = rest[1:]
    s_refs, p_refs, m_refs, l_refs = scratch[0:2], scratch[2:4], scratch[4:6], scratch[6:8]
    step = pl.program_id(2)
    n_tiles = k_ref.shape[1] * tiles_per_seq
    rows = HEADS_PER_STEP * Q_TILE

    @pl.when((pl.program_id(0) == 0) & (pl.program_id(1) == 0) & (step == 0))
    def _():
        for ref in (*s_refs, *p_refs, *m_refs):
            ref[...] = jnp.zeros(ref.shape, ref.dtype)
        for ref in l_refs:
            ref[...] = jnp.ones(ref.shape, ref.dtype)

    lane = lax.broadcasted_iota(jnp.int32, (Q_TILE, SLAB), 1) // HEAD_DIM

    def window(ref, t):
        seq = t // tiles_per_seq
        return ref[0, seq, pl.ds(key_start(t % tiles_per_seq), win), :]

    for u in range(ATTN_TILES_PER_STEP):
        cur, prev = u % 2, 1 - u % 2
        t = step * ATTN_TILES_PER_STEP + u
        tile_rows = slice(u * Q_TILE, (u + 1) * Q_TILE)

        t_s = jnp.minimum(t, n_tiles - 1)
        q4 = q_ref[0, 0, tile_rows, :]
        zero = jnp.zeros_like(q4)
        q_stack = jnp.concatenate(
            [jnp.where(lane == h, q4, zero) for h in range(HEADS_PER_STEP)], axis=0)
        s = lax.dot_general(q_stack, window(k_ref, t_s), _NT_DIMS,
                            preferred_element_type=jnp.float32)
        s_refs[cur][...] = s + b_ref[pattern_of(t_s % tiles_per_seq)].reshape(rows, win)

        s_prev = s_refs[prev][...]
        m = jnp.max(s_prev, axis=-1, keepdims=True)
        e = jnp.exp(s_prev - m)
        p_refs[prev][...] = e.astype(jnp.bfloat16)
        m_refs[prev][...] = jnp.broadcast_to(m, (rows, LANES))
        l_refs[prev][...] = jnp.broadcast_to(jnp.sum(e, axis=-1, keepdims=True), (rows, LANES))

        t_o = jnp.clip(t - 2, 0, n_tiles - 1)
        l = l_refs[cur][...]
        inv_l = 1.0 / l
        o_all = jnp.dot(p_refs[cur][...], window(v_ref, t_o),
                        preferred_element_type=jnp.float32)
        o_all = o_all * jnp.concatenate([inv_l] * (SLAB // LANES), axis=1)
        out = o_all[:Q_TILE]
        for h in range(1, HEADS_PER_STEP):
            out = jnp.where(lane == h, o_all[h * Q_TILE:(h + 1) * Q_TILE], out)
        o_ref[0, 0, tile_rows, :] = out.astype(o_ref.dtype)
        if emit_lse:
            lse = m_refs[cur][...] + jnp.log(l)
            lse_lane = lax.broadcasted_iota(jnp.int32, (Q_TILE, LSE_LANES), 1)
            first_head = pl.program_id(1) * HEADS_PER_STEP
            lse_tile = jnp.zeros((Q_TILE, LSE_LANES), jnp.float32)
            for h in range(HEADS_PER_STEP):
                lse_tile = jnp.where(lse_lane == first_head + h,
                                     lse[h * Q_TILE:(h + 1) * Q_TILE], lse_tile)
            lse_ref[0, 0, tile_rows, :] = lse_tile


def _windowed_attention(qkv, bias_table, *, win, key_start, pattern_of, emit_lse, name):
    b, n_seq, l, _ = qkv.shape
    tile = Q_TILE * ATTN_TILES_PER_STEP
    assert ATTN_TILES_PER_STEP % 2 == 0 and l % tile == 0
    tiles_per_seq = l // Q_TILE
    blocks_per_seq = l // tile
    n_blocks = n_seq * blocks_per_seq
    rows = HEADS_PER_STEP * Q_TILE

    def q_map(bi, g, j):
        jq = jnp.minimum(j, n_blocks - 1)
        return (bi, jq // blocks_per_seq, jq % blocks_per_seq, g)

    def out_map(bi, g, j):
        jo = jnp.maximum(j - 1, 0)
        return (bi, jo // blocks_per_seq, jo % blocks_per_seq, g)

    out_specs = [pl.BlockSpec((1, 1, tile, SLAB), out_map)]
    out_shape = [jax.ShapeDtypeStruct((b, n_seq, l, D_MODEL), jnp.bfloat16)]
    if emit_lse:
        out_specs.append(pl.BlockSpec((1, 1, tile, LSE_LANES), out_map))
        out_shape.append(jax.ShapeDtypeStruct((b, n_seq, l, N_SLABS * LSE_LANES), jnp.float32))
    scratch = ([pltpu.VMEM((rows, win), jnp.float32)] * 2
               + [pltpu.VMEM((rows, win), jnp.bfloat16)] * 2
               + [pltpu.VMEM((rows, LANES), jnp.float32)] * 4)
    kern = functools.partial(_attn_kernel, win=win, tiles_per_seq=tiles_per_seq,
                             key_start=key_start, pattern_of=pattern_of, emit_lse=emit_lse)
    res = pl.pallas_call(
        kern,
        grid=(b, N_SLABS, n_blocks + 1),
        in_specs=[
            pl.BlockSpec((1, 1, tile, SLAB), q_map),
            pl.BlockSpec((1, n_seq, l, SLAB), lambda bi, g, j: (bi, 0, 0, N_SLABS + g)),
            pl.BlockSpec((1, n_seq, l, SLAB), lambda bi, g, j: (bi, 0, 0, 2 * N_SLABS + g)),
            pl.BlockSpec((bias_table.shape[0], HEADS_PER_STEP, Q_TILE, win),
                         lambda bi, g, j: (0, g, 0, 0)),
        ],
        out_specs=out_specs,
        out_shape=out_shape,
        scratch_shapes=scratch,
        compiler_params=pltpu.CompilerParams(
            dimension_semantics=("arbitrary", "arbitrary", "arbitrary"),
            vmem_limit_bytes=VMEM_LIMIT),
        name=name,
    )(qkv, qkv, qkv, bias_table)
    return res if emit_lse else res[0]


def _na_pattern(i, n_pairs):
    return jnp.where(i < 2, i + 1, jnp.where(i >= n_pairs - 2, i - (n_pairs - 2) + 3, 0))


def _na_bias_table(rpb, rows):
    n_pairs = rows // 2
    reps = (5, 0, 1, n_pairs - 2, n_pairs - 1)
    kh = min(NA_ROWS, rows)
    n_col_off = 2 * NA_COLS - 1
    padded = jnp.pad(rpb.astype(jnp.float32), ((0, 0), (0, 0), (GRID_W, GRID_W)))
    col_part = jnp.stack(
        [padded[:, :, GRID_W + NA_COLS - 1 - c: 2 * GRID_W + NA_COLS - 1 - c] for c in range(GRID_W)],
        axis=2)
    c = np.arange(GRID_W)[:, None]
    kc = np.arange(GRID_W)[None, :]
    cs = np.clip(c - NA_COLS // 2, 0, GRID_W - NA_COLS)
    col_ok = (kc >= cs) & (kc < cs + NA_COLS)
    assert np.all(((kc - c + NA_COLS - 1 >= 0) & (kc - c + NA_COLS - 1 < n_col_off))[col_ok])
    col_part = jnp.where(jnp.asarray(col_ok), col_part, NEG_INF)
    masked = jnp.full((N_HEADS, GRID_W, GRID_W), NEG_INF, jnp.float32)
    patterns = []
    for i in reps:
        ks = int(np.clip(2 * i - NA_ROWS // 2, 0, rows - NA_KEY_ROWS))
        q_rows = []
        for qr in range(Q_TILE // GRID_W):
            r = 2 * i + qr
            rs = int(np.clip(r - kh // 2, 0, rows - kh))
            blocks = []
            for kr in range(NA_KEY_ROWS):
                k_abs = ks + kr
                ok = rs <= k_abs < rs + kh
                blocks.append(col_part[:, k_abs - r + NA_ROWS - 1] if ok else masked)
            q_rows.append(jnp.concatenate(blocks, axis=-1))
        patterns.append(jnp.concatenate(q_rows, axis=1))
    return jnp.stack(patterns)


def _na_attention(proj, bias_table):
    rows = proj.shape[2] // GRID_W
    n_pairs = rows // 2

    def key_start(i):
        first_row = jnp.clip(2 * i - NA_ROWS // 2, 0, rows - NA_KEY_ROWS)
        return pl.multiple_of(first_row * GRID_W, Q_TILE)

    return _windowed_attention(
        proj, bias_table, win=NA_WIN, key_start=key_start,
        pattern_of=lambda i: _na_pattern(i, n_pairs), emit_lse=False, name="na_attention")


def _alibi_slopes():
    return np.asarray(2.0 ** (-8.0 * (np.arange(N_HEADS) + 1) / N_HEADS), dtype=np.float32)


_DIL_QUERY_OFFSETS = (DIL_RADIUS, 0, DIL_WIN - Q_TILE)


def _dil_bias_table(dil):
    qi = np.arange(Q_TILE)[:, None]
    kj = np.arange(DIL_WIN)[None, :]
    delta = np.stack([kj - off - qi for off in _DIL_QUERY_OFFSETS])
    valid = np.abs(delta) <= DIL_RADIUS
    dist = jnp.asarray(np.abs(delta) * dil, dtype=jnp.float32)
    bias = -jnp.asarray(_alibi_slopes())[None, :, None, None] * dist[:, None]
    return jnp.where(jnp.asarray(valid)[:, None], bias, NEG_INF)


def _dil_attention(qkv, dil):
    l = qkv.shape[2]
    n_tiles = l // Q_TILE
    assert n_tiles >= 2

    def key_start(i):
        return pl.multiple_of(jnp.clip(i * Q_TILE - DIL_RADIUS, 0, l - DIL_WIN), DIL_RADIUS)

    def pattern_of(i):
        return jnp.where(i == 0, 1, jnp.where(i == n_tiles - 1, 2, 0))

    return _windowed_attention(
        qkv, _dil_bias_table(dil), win=DIL_WIN, key_start=key_start, pattern_of=pattern_of,
        emit_lse=True, name=f"dilated_attention_{dil}")


def _silu(x):
    return x * jax.nn.sigmoid(x)


def _out0_kernel(o_ref, gate_ref, x_ref, w_ref, y_ref):
    o = o_ref[...].astype(jnp.float32)
    gate = gate_ref[...].astype(jnp.float32)
    z = (o * _silu(gate)).astype(jnp.bfloat16)
    y_ref[...] = x_ref[...] + jnp.dot(z, w_ref[...], preferred_element_type=jnp.float32)


def _out_proj0(o, proj, x, w_out, *, tm=512):
    t, d = x.shape
    gate_block = proj.shape[1] // d - 1
    return pl.pallas_call(
        _out0_kernel,
        grid=(t // tm,),
        in_specs=[
            pl.BlockSpec((tm, d), lambda i: (i, 0)),
            pl.BlockSpec((tm, d), lambda i: (i, gate_block)),
            pl.BlockSpec((tm, d), lambda i: (i, 0)),
            pl.BlockSpec((d, d), lambda i: (0, 0)),
        ],
        out_specs=pl.BlockSpec((tm, d), lambda i: (i, 0)),
        out_shape=jax.ShapeDtypeStruct((t, d), jnp.float32),
        compiler_params=pltpu.CompilerParams(
            dimension_semantics=("parallel",), vmem_limit_bytes=VMEM_LIMIT),
        name="out_proj0",
    )(o, proj, x, w_out)


def _natural_order(ref, slab_ref, dil):
    _, n, c = ref.shape
    if dil == 1:
        return ref[0].astype(jnp.float32)
    pieces = []
    for s in range(c // LANES):
        for r in range(dil):
            slab_ref[s, pl.ds(r, n, stride=dil), :] = (
                ref[r, :, s * LANES:(s + 1) * LANES].astype(jnp.float32))
        pieces.append(slab_ref[s])
    return pieces[0] if len(pieces) == 1 else jnp.concatenate(pieces, axis=1)


def _out1_kernel(o0_ref, o1_ref, o2_ref, l0_ref, l1_ref, l2_ref, gate_ref, x_ref,
                 w_ref, e_ref, g_ref, y_ref, slab_ref):
    dils = [dil for _, dil in DIL_PAIRS]
    lses = []
    for ref, dil in zip((l0_ref, l1_ref, l2_ref), dils):
        wide = _natural_order(ref.at[0], slab_ref, dil)
        lses.append(sum(wide[:, g * LSE_LANES:(g + 1) * LSE_LANES] for g in range(N_SLABS)))
    m = jnp.maximum(jnp.maximum(lses[0], lses[1]), lses[2])
    es = [jnp.exp(v - m) for v in lses]
    denom = es[0] + es[1] + es[2]
    o = None
    for e, o_ref, dil in zip(es, (o0_ref, o1_ref, o2_ref), dils):
        w = e / denom
        hi = w.astype(jnp.bfloat16)
        lo = (w - hi.astype(jnp.float32)).astype(jnp.bfloat16)
        w_full = jnp.dot(jnp.concatenate([hi, lo], axis=1), e_ref[...],
                         preferred_element_type=jnp.float32)
        term = w_full * _natural_order(o_ref.at[0], slab_ref, dil)
        o = term if o is None else o + term
    gate = gate_ref[0].astype(jnp.float32)
    z = (o * _silu(gate)).astype(jnp.bfloat16)
    x = x_ref[0] + jnp.dot(z, w_ref[...], preferred_element_type=jnp.float32)
    ms = jnp.mean(x * x, axis=-1, keepdims=True)
    y_ref[0] = x * lax.rsqrt(ms + RMS_EPS) * g_ref[...]


def _head_expansion():
    e = np.zeros((LSE_LANES, D_MODEL), np.float32)
    for h in range(N_HEADS):
        e[h, h * HEAD_DIM:(h + 1) * HEAD_DIM] = 1.0
    return jnp.asarray(np.concatenate([e, e], axis=0), dtype=jnp.bfloat16)


def _out_proj1(os_, lses, proj, x, w_out, norm_f, *, tm=512):
    b, s, d = x.shape
    gate_block = proj.shape[2] // d - 1
    dils = [dil for _, dil in DIL_PAIRS]

    def grouped(width, dil):
        return pl.BlockSpec((1, dil, tm // dil, width), lambda bi, i: (bi, 0, i, 0))

    tok = pl.BlockSpec((1, tm, d), lambda bi, i: (bi, i, 0))
    return pl.pallas_call(
        _out1_kernel,
        grid=(b, s // tm),
        in_specs=(
            [grouped(d, dil) for dil in dils]
            + [grouped(N_SLABS * LSE_LANES, dil) for dil in dils]
            + [pl.BlockSpec((1, tm, d), lambda bi, i: (bi, i, gate_block)),
               tok,
               pl.BlockSpec((d, d), lambda bi, i: (0, 0)),
               pl.BlockSpec((2 * LSE_LANES, d), lambda bi, i: (0, 0)),
               pl.BlockSpec((1, d), lambda bi, i: (0, 0))]),
        out_specs=tok,
        out_shape=jax.ShapeDtypeStruct((b, s, d), jnp.float32),
        scratch_shapes=[pltpu.VMEM((N_LANE_SLABS, tm, LANES), jnp.float32)],
        compiler_params=pltpu.CompilerParams(
            dimension_semantics=("parallel", "parallel"), vmem_limit_bytes=VMEM_LIMIT),
        name="out_proj1",
    )(*os_, *lses, proj, x, w_out, _head_expansion(), norm_f.reshape(1, d))


def _prep_w_in(w_in, q_blocks):
    scale = np.ones((w_in.shape[1],), np.float32)
    for blk in q_blocks:
        scale[blk * D_MODEL:(blk + 1) * D_MODEL] = 1.0 / math.sqrt(HEAD_DIM)
    return (w_in * jnp.asarray(scale)[None, :]).astype(jnp.bfloat16)


def kernel(x, norm_0, w_in_0, rpb_0, w_out_0, norm_1, w_in_1, w_out_1, norm_f):
    b, s, d = x.shape
    t = b * s

    proj0 = _norm_proj(x, norm_0, _prep_w_in(w_in_0, (0,)), 1)
    o0 = _na_attention(proj0, _na_bias_table(rpb_0, s // GRID_W))
    x1 = _out_proj0(o0.reshape(t, d), proj0.reshape(t, -1), x.reshape(t, d),
                    w_out_0.astype(jnp.bfloat16)).reshape(b, s, d)

    w1 = _prep_w_in(w_in_1, tuple(3 * g for g in range(N_DIL_GROUPS)))
    qkv_cols = 3 * d
    gate_w = w1[:, N_DIL_GROUPS * qkv_cols:]
    outs, lses, proj_gate = [], [], None
    for g, (window, dil) in enumerate(DIL_PAIRS):
        assert window // (2 * dil) == DIL_RADIUS
        w_g = w1[:, g * qkv_cols:(g + 1) * qkv_cols]
        if dil == 1:
            w_g = jnp.concatenate([w_g, gate_w], axis=1)
        qkv = _norm_proj(x1, norm_1, w_g, dil)
        if dil == 1:
            proj_gate = qkv.reshape(b, s, -1)
        o_g, lse_g = _dil_attention(qkv, dil)
        outs.append(o_g)
        lses.append(lse_g)
    return _out_proj1(outs, lses, proj_gate, x1, w_out_1.astype(jnp.bfloat16), norm_f)
```

```python
import functools
import math

import numpy as np
import jax
import jax.numpy as jnp
from jax import lax
from jax.experimental import pallas as pl
from jax.experimental.pallas import tpu as pltpu

D_MODEL = 1024
HEAD_DIM = 64
N_HEADS = 16
GRID_W = 64
NA_ROWS = 8
NA_COLS = 16
DIL_PAIRS = ((128, 1), (512, 4), (2048, 16))
N_DIL_GROUPS = len(DIL_PAIRS)
RMS_EPS = 1e-6
NEG_INF = -1e30

LANES = 128
N_LANE_SLABS = D_MODEL // LANES
HEADS_PER_STEP = 4
SLAB = HEADS_PER_STEP * HEAD_DIM
N_SLABS = D_MODEL // SLAB
Q_TILE = 128
NA_KEY_ROWS = 10
NA_WIN = NA_KEY_ROWS * GRID_W
ATTN_TILES_PER_STEP = 2
DIL_TILES_PER_STEP = 2
DIL_RADIUS = 64
DIL_WIN = Q_TILE + 2 * DIL_RADIUS
LSE_LANES = LANES
VMEM_LIMIT = 56 * 1024 * 1024

_NT_DIMS = (((1,), (1,)), ((), ()))


def _proj_kernel(x_ref, g_ref, w_ref, o_ref, h_ref, *slab_refs, dil):
    tm = x_ref.shape[0]

    @pl.when(pl.program_id(1) == 0)
    def _():
        x = x_ref[...]
        ms = jnp.mean(x * x, axis=-1, keepdims=True)
        hn = x * lax.rsqrt(ms + RMS_EPS) * g_ref[...]
        if dil == 1:
            h_ref[...] = hn.astype(h_ref.dtype)
        else:
            slab_ref, = slab_refs
            n = tm // dil
            for s in range(N_LANE_SLABS):
                slab_ref[s] = hn[:, s * LANES:(s + 1) * LANES]
            for r in range(dil):
                for s in range(N_LANE_SLABS):
                    rows = slab_ref[s, pl.ds(r, n, stride=dil), :]
                    h_ref[r * n:(r + 1) * n, s * LANES:(s + 1) * LANES] = rows.astype(h_ref.dtype)

    res = jnp.dot(h_ref[...], w_ref[...], preferred_element_type=jnp.float32)
    o_ref[0] = res.astype(o_ref.dtype).reshape(o_ref.shape[1:])


def _norm_proj(x, g, w, dil, *, tm=1024, tn=1024):
    b, s, d = x.shape
    n = w.shape[1]
    tiles = s // tm
    scratch = [pltpu.VMEM((tm, d), jnp.bfloat16)]
    if dil > 1:
        scratch.append(pltpu.VMEM((N_LANE_SLABS, tm, LANES), jnp.float32))
    return pl.pallas_call(
        functools.partial(_proj_kernel, dil=dil),
        grid=(b * tiles, n // tn),
        in_specs=[
            pl.BlockSpec((None, tm, d), lambda i, j: (i // tiles, i % tiles, 0)),
            pl.BlockSpec((1, d), lambda i, j: (0, 0)),
            pl.BlockSpec((d, tn), lambda i, j: (0, j)),
        ],
        out_specs=pl.BlockSpec((1, dil, tm // dil, tn),
                               lambda i, j: (i // tiles, 0, i % tiles, j)),
        out_shape=jax.ShapeDtypeStruct((b, dil, s // dil, n), jnp.bfloat16),
        scratch_shapes=scratch,
        compiler_params=pltpu.CompilerParams(
            dimension_semantics=("parallel", "arbitrary"),
            vmem_limit_bytes=VMEM_LIMIT),
        name=f"norm_proj_{dil}",
    )(x, g.reshape(1, d), w)


def _attn_kernel(q_ref, k_ref, v_ref, b_ref, *rest, win, tiles_per_seq, key_start,
                 pattern_of, emit_lse):
    if emit_lse:
        o_ref, lse_ref = rest[:2]
        scratch = rest[2:]
    else:
        o_ref, lse_ref = rest[0], None
        scratch = rest[1:]
    s_refs, p_refs, m_refs, l_refs = scratch[0:2], scratch[2:4], scratch[4:6], scratch[6:8]
    step = pl.program_id(2)
    n_tiles = k_ref.shape[1] * tiles_per_seq
    rows = HEADS_PER_STEP * Q_TILE

    @pl.when((pl.program_id(0) == 0) & (pl.program_id(1) == 0) & (step == 0))
    def _():
        for ref in (*s_refs, *p_refs, *m_refs):
            ref[...] = jnp.zeros(ref.shape, ref.dtype)
        for ref in l_refs:
            ref[...] = jnp.ones(ref.shape, ref.dtype)

    lane = lax.broadcasted_iota(jnp.int32, (Q_TILE, SLAB), 1) // HEAD_DIM

    def window(ref, t):
        seq = t // tiles_per_seq
        return ref[0, seq, pl.ds(key_start(t % tiles_per_seq), win), :]

    for u in range(ATTN_TILES_PER_STEP):
        cur, prev = u % 2, 1 - u % 2
        t = step * ATTN_TILES_PER_STEP + u
        tile_rows = slice(u * Q_TILE, (u + 1) * Q_TILE)

        t_s = jnp.minimum(t, n_tiles - 1)
        q4 = q_ref[0, 0, tile_rows, :]
        zero = jnp.zeros_like(q4)
        q_stack = jnp.concatenate(
            [jnp.where(lane == h, q4, zero) for h in range(HEADS_PER_STEP)], axis=0)
        s = lax.dot_general(q_stack, window(k_ref, t_s), _NT_DIMS,
                            preferred_element_type=jnp.float32)
        s_refs[cur][...] = s + b_ref[pattern_of(t_s % tiles_per_seq)].reshape(rows, win)

        s_prev = s_refs[prev][...]
        m = jnp.max(s_prev, axis=-1, keepdims=True)
        e = jnp.exp(s_prev - m)
        p_refs[prev][...] = e.astype(jnp.bfloat16)
        m_refs[prev][...] = jnp.broadcast_to(m, (rows, LANES))
        l_refs[prev][...] = jnp.broadcast_to(jnp.sum(e, axis=-1, keepdims=True), (rows, LANES))

        t_o = jnp.clip(t - 2, 0, n_tiles - 1)
        l = l_refs[cur][...]
        inv_l = 1.0 / l
        o_all = jnp.dot(p_refs[cur][...], window(v_ref, t_o),
                        preferred_element_type=jnp.float32)
        o_all = o_all * jnp.concatenate([inv_l] * (SLAB // LANES), axis=1)
        out = o_all[:Q_TILE]
        for h in range(1, HEADS_PER_STEP):
            out = jnp.where(lane == h, o_all[h * Q_TILE:(h + 1) * Q_TILE], out)
        o_ref[0, 0, tile_rows, :] = out.astype(o_ref.dtype)
        if emit_lse:
            lse = m_refs[cur][...] + jnp.log(l)
            lse_lane = lax.broadcasted_iota(jnp.int32, (Q_TILE, LSE_LANES), 1)
            first_head = pl.program_id(1) * HEADS_PER_STEP
            lse_tile = jnp.zeros((Q_TILE, LSE_LANES), jnp.float32)
            for h in range(HEADS_PER_STEP):
                lse_tile = jnp.where(lse_lane == first_head + h,
                                     lse[h * Q_TILE:(h + 1) * Q_TILE], lse_tile)
            lse_ref[0, 0, tile_rows, :] = lse_tile


def _windowed_attention(qkv, bias_table, *, win, key_start, pattern_of, emit_lse, name):
    b, n_seq, l, _ = qkv.shape
    tile = Q_TILE * ATTN_TILES_PER_STEP
    assert ATTN_TILES_PER_STEP % 2 == 0 and l % tile == 0
    tiles_per_seq = l // Q_TILE
    blocks_per_seq = l // tile
    n_blocks = n_seq * blocks_per_seq
    rows = HEADS_PER_STEP * Q_TILE

    def q_map(bi, g, j):
        jq = jnp.minimum(j, n_blocks - 1)
        return (bi, jq // blocks_per_seq, jq % blocks_per_seq, g)

    def out_map(bi, g, j):
        jo = jnp.maximum(j - 1, 0)
        return (bi, jo // blocks_per_seq, jo % blocks_per_seq, g)

    out_specs = [pl.BlockSpec((1, 1, tile, SLAB), out_map)]
    out_shape = [jax.ShapeDtypeStruct((b, n_seq, l, D_MODEL), jnp.bfloat16)]
    if emit_lse:
        out_specs.append(pl.BlockSpec((1, 1, tile, LSE_LANES), out_map))
        out_shape.append(jax.ShapeDtypeStruct((b, n_seq, l, N_SLABS * LSE_LANES), jnp.float32))
    scratch = ([pltpu.VMEM((rows, win), jnp.float32)] * 2
               + [pltpu.VMEM((rows, win), jnp.bfloat16)] * 2
               + [pltpu.VMEM((rows, LANES), jnp.float32)] * 4)
    kern = functools.partial(_attn_kernel, win=win, tiles_per_seq=tiles_per_seq,
                             key_start=key_start, pattern_of=pattern_of, emit_lse=emit_lse)
    res = pl.pallas_call(
        kern,
        grid=(b, N_SLABS, n_blocks + 1),
        in_specs=[
            pl.BlockSpec((1, 1, tile, SLAB), q_map),
            pl.BlockSpec((1, n_seq, l, SLAB), lambda bi, g, j: (bi, 0, 0, N_SLABS + g)),
            pl.BlockSpec((1, n_seq, l, SLAB), lambda bi, g, j: (bi, 0, 0, 2 * N_SLABS + g)),
            pl.BlockSpec((bias_table.shape[0], HEADS_PER_STEP, Q_TILE, win),
                         lambda bi, g, j: (0, g, 0, 0)),
        ],
        out_specs=out_specs,
        out_shape=out_shape,
        scratch_shapes=scratch,
        compiler_params=pltpu.CompilerParams(
            dimension_semantics=("arbitrary", "arbitrary", "arbitrary"),
            vmem_limit_bytes=VMEM_LIMIT),
        name=name,
    )(qkv, qkv, qkv, bias_table)
    return res if emit_lse else res[0]


def _na_pattern(i, n_pairs):
    return jnp.where(i < 2, i + 1, jnp.where(i >= n_pairs - 2, i - (n_pairs - 2) + 3, 0))


def _na_bias_table(rpb, rows):
    n_pairs = rows // 2
    reps = (5, 0, 1, n_pairs - 2, n_pairs - 1)
    kh = min(NA_ROWS, rows)
    n_col_off = 2 * NA_COLS - 1
    padded = jnp.pad(rpb.astype(jnp.float32), ((0, 0), (0, 0), (GRID_W, GRID_W)))
    col_part = jnp.stack(
        [padded[:, :, GRID_W + NA_COLS - 1 - c: 2 * GRID_W + NA_COLS - 1 - c] for c in range(GRID_W)],
        axis=2)
    c = np.arange(GRID_W)[:, None]
    kc = np.arange(GRID_W)[None, :]
    cs = np.clip(c - NA_COLS // 2, 0, GRID_W - NA_COLS)
    col_ok = (kc >= cs) & (kc < cs + NA_COLS)
    assert np.all(((kc - c + NA_COLS - 1 >= 0) & (kc - c + NA_COLS - 1 < n_col_off))[col_ok])
    col_part = jnp.where(jnp.asarray(col_ok), col_part, NEG_INF)
    masked = jnp.full((N_HEADS, GRID_W, GRID_W), NEG_INF, jnp.float32)
    patterns = []
    for i in reps:
        ks = int(np.clip(2 * i - NA_ROWS // 2, 0, rows - NA_KEY_ROWS))
        q_rows = []
        for qr in range(Q_TILE // GRID_W):
            r = 2 * i + qr
            rs = int(np.clip(r - kh // 2, 0, rows - kh))
            blocks = []
            for kr in range(NA_KEY_ROWS):
                k_abs = ks + kr
                ok = rs <= k_abs < rs + kh
                blocks.append(col_part[:, k_abs - r + NA_ROWS - 1] if ok else masked)
            q_rows.append(jnp.concatenate(blocks, axis=-1))
        patterns.append(jnp.concatenate(q_rows, axis=1))
    return jnp.stack(patterns)


def _na_attention(proj, bias_table):
    rows = proj.shape[2] // GRID_W
    n_pairs = rows // 2

    def key_start(i):
        first_row = jnp.clip(2 * i - NA_ROWS // 2, 0, rows - NA_KEY_ROWS)
        return pl.multiple_of(first_row * GRID_W, Q_TILE)

    return _windowed_attention(
        proj, bias_table, win=NA_WIN, key_start=key_start,
        pattern_of=lambda i: _na_pattern(i, n_pairs), emit_lse=False, name="na_attention")


def _alibi_slopes():
    return np.asarray(2.0 ** (-8.0 * (np.arange(N_HEADS) + 1) / N_HEADS), dtype=np.float32)


def _slab_attention(q4, k4, v4, bias):
    lane = lax.broadcasted_iota(jnp.int32, q4.shape, 1) // HEAD_DIM
    zero = jnp.zeros_like(q4)
    q_stack = jnp.concatenate(
        [jnp.where(lane == h, q4, zero) for h in range(HEADS_PER_STEP)], axis=0)
    s = lax.dot_general(q_stack, k4, _NT_DIMS, preferred_element_type=jnp.float32)
    s = s + bias
    m = jnp.max(s, axis=-1, keepdims=True)
    p = jnp.exp(s - m)
    l = jnp.sum(p, axis=-1, keepdims=True)
    p = p.astype(jnp.bfloat16)
    inv_l = 1.0 / l
    out = None
    for h in range(HEADS_PER_STEP):
        rows = slice(h * Q_TILE, (h + 1) * Q_TILE)
        o_h = jnp.dot(p[rows], v4, preferred_element_type=jnp.float32) * inv_l[rows]
        out = o_h if out is None else jnp.where(lane == h, o_h, out)
    return out, m, l


def _dil_kernel(q_ref, ka_ref, kb_ref, kc_ref, va_ref, vb_ref, vc_ref, b_ref,
                o_ref, lse_ref):
    step = pl.program_id(2)
    n_steps = pl.num_programs(2)
    k_win = jnp.concatenate([ka_ref[0, 0], kb_ref[0, 0], kc_ref[0, 0]], axis=0)
    v_win = jnp.concatenate([va_ref[0, 0], vb_ref[0, 0], vc_ref[0, 0]], axis=0)
    lane = lax.broadcasted_iota(jnp.int32, (Q_TILE, LSE_LANES), 1)
    for u in range(DIL_TILES_PER_STEP):
        pattern = 0
        if u == 0:
            pattern = jnp.where(step == 0, 1, pattern)
        if u == DIL_TILES_PER_STEP - 1:
            pattern = jnp.where(step == n_steps - 1, 2, pattern)
        tile = slice(u * Q_TILE, (u + 1) * Q_TILE)
        win = slice(u * Q_TILE, u * Q_TILE + DIL_WIN)
        lse_tile = jnp.zeros((Q_TILE, LSE_LANES), jnp.float32)
        for hg in range(N_SLABS):
            cols = slice(hg * SLAB, (hg + 1) * SLAB)
            bias = b_ref[pattern, hg * HEADS_PER_STEP:(hg + 1) * HEADS_PER_STEP]
            bias = bias.reshape(HEADS_PER_STEP * Q_TILE, DIL_WIN)
            out, m, l = _slab_attention(q_ref[0, 0, tile, cols], k_win[win, cols],
                                        v_win[win, cols], bias)
            o_ref[0, 0, tile, cols] = out.astype(o_ref.dtype)
            lse = m + jnp.log(l)
            for h in range(HEADS_PER_STEP):
                lse_h = lse[h * Q_TILE:(h + 1) * Q_TILE]
                lse_tile = jnp.where(lane == hg * HEADS_PER_STEP + h, lse_h, lse_tile)
        lse_ref[0, 0, tile, :] = lse_tile


def _dil_bias_table(dil):
    qi = np.arange(Q_TILE)[:, None]
    kj = np.arange(DIL_WIN)[None, :]
    delta = kj - DIL_RADIUS - qi
    in_band = np.abs(delta) <= DIL_RADIUS
    valid = np.stack([in_band,
                      in_band & (kj >= DIL_RADIUS),
                      in_band & (kj < DIL_WIN - DIL_RADIUS)])
    dist = jnp.asarray(np.abs(delta) * dil, dtype=jnp.float32)
    bias = -jnp.asarray(_alibi_slopes())[:, None, None] * dist[None]
    return jnp.where(jnp.asarray(valid)[:, None], bias[None], NEG_INF)


def _dil_attention(qkv, dil):
    b, _, l, _ = qkv.shape
    tile = Q_TILE * DIL_TILES_PER_STEP
    n_steps = l // tile
    assert n_steps >= 2
    half = DIL_RADIUS
    per_tile = tile // half
    n_half = l // half

    def spec_mid(which):
        return pl.BlockSpec((1, 1, tile, D_MODEL), lambda bi, r, i: (bi, r, i, which))

    def spec_lo(which):
        return pl.BlockSpec((1, 1, half, D_MODEL),
                            lambda bi, r, i: (bi, r, jnp.maximum(per_tile * i - 1, 0), which))

    def spec_hi(which):
        return pl.BlockSpec((1, 1, half, D_MODEL),
                            lambda bi, r, i: (bi, r, jnp.minimum(per_tile * (i + 1), n_half - 1), which))

    bias = _dil_bias_table(dil)
    return pl.pallas_call(
        _dil_kernel,
        grid=(b, dil, n_steps),
        in_specs=[
            spec_mid(0),
            spec_lo(1), spec_mid(1), spec_hi(1),
            spec_lo(2), spec_mid(2), spec_hi(2),
            pl.BlockSpec(bias.shape, lambda bi, r, i: (0, 0, 0, 0)),
        ],
        out_specs=[
            pl.BlockSpec((1, 1, tile, D_MODEL), lambda bi, r, i: (bi, r, i, 0)),
            pl.BlockSpec((1, 1, tile, LSE_LANES), lambda bi, r, i: (bi, r, i, 0)),
        ],
        out_shape=[
            jax.ShapeDtypeStruct((b, dil, l, D_MODEL), jnp.bfloat16),
            jax.ShapeDtypeStruct((b, dil, l, LSE_LANES), jnp.float32),
        ],
        compiler_params=pltpu.CompilerParams(
            dimension_semantics=("parallel", "parallel", "arbitrary"),
            vmem_limit_bytes=VMEM_LIMIT),
        name=f"dilated_attention_{dil}",
    )(qkv, qkv, qkv, qkv, qkv, qkv, qkv, bias)


def _silu(x):
    return x * jax.nn.sigmoid(x)


def _out0_kernel(o_ref, gate_ref, x_ref, w_ref, y_ref):
    o = o_ref[...].astype(jnp.float32)
    gate = gate_ref[...].astype(jnp.float32)
    z = (o * _silu(gate)).astype(jnp.bfloat16)
    y_ref[...] = x_ref[...] + jnp.dot(z, w_ref[...], preferred_element_type=jnp.float32)


def _out_proj0(o, proj, x, w_out, *, tm=512):
    t, d = x.shape
    gate_block = proj.shape[1] // d - 1
    return pl.pallas_call(
        _out0_kernel,
        grid=(t // tm,),
        in_specs=[
            pl.BlockSpec((tm, d), lambda i: (i, 0)),
            pl.BlockSpec((tm, d), lambda i: (i, gate_block)),
            pl.BlockSpec((tm, d), lambda i: (i, 0)),
            pl.BlockSpec((d, d), lambda i: (0, 0)),
        ],
        out_specs=pl.BlockSpec((tm, d), lambda i: (i, 0)),
        out_shape=jax.ShapeDtypeStruct((t, d), jnp.float32),
        compiler_params=pltpu.CompilerParams(
            dimension_semantics=("parallel",), vmem_limit_bytes=VMEM_LIMIT),
        name="out_proj0",
    )(o, proj, x, w_out)


def _natural_order(ref, slab_ref, dil):
    _, n, c = ref.shape
    if dil == 1:
        return ref[0].astype(jnp.float32)
    pieces = []
    for s in range(c // LANES):
        for r in range(dil):
            slab_ref[s, pl.ds(r, n, stride=dil), :] = (
                ref[r, :, s * LANES:(s + 1) * LANES].astype(jnp.float32))
        pieces.append(slab_ref[s])
    return pieces[0] if len(pieces) == 1 else jnp.concatenate(pieces, axis=1)


def _out1_kernel(o0_ref, o1_ref, o2_ref, l0_ref, l1_ref, l2_ref, gate_ref, x_ref,
                 w_ref, e_ref, g_ref, y_ref, slab_ref):
    dils = [dil for _, dil in DIL_PAIRS]
    lses = [_natural_order(ref.at[0], slab_ref, dil)
            for ref, dil in zip((l0_ref, l1_ref, l2_ref), dils)]
    m = jnp.maximum(jnp.maximum(lses[0], lses[1]), lses[2])
    es = [jnp.exp(v - m) for v in lses]
    denom = es[0] + es[1] + es[2]
    o = None
    for e, o_ref, dil in zip(es, (o0_ref, o1_ref, o2_ref), dils):
        w = e / denom
        hi = w.astype(jnp.bfloat16)
        lo = (w - hi.astype(jnp.float32)).astype(jnp.bfloat16)
        w_full = jnp.dot(jnp.concatenate([hi, lo], axis=1), e_ref[...],
                         preferred_element_type=jnp.float32)
        term = w_full * _natural_order(o_ref.at[0], slab_ref, dil)
        o = term if o is None else o + term
    gate = gate_ref[0].astype(jnp.float32)
    z = (o * _silu(gate)).astype(jnp.bfloat16)
    x = x_ref[0] + jnp.dot(z, w_ref[...], preferred_element_type=jnp.float32)
    ms = jnp.mean(x * x, axis=-1, keepdims=True)
    y_ref[0] = x * lax.rsqrt(ms + RMS_EPS) * g_ref[...]


def _head_expansion():
    e = np.zeros((LSE_LANES, D_MODEL), np.float32)
    for h in range(N_HEADS):
        e[h, h * HEAD_DIM:(h + 1) * HEAD_DIM] = 1.0
    return jnp.asarray(np.concatenate([e, e], axis=0), dtype=jnp.bfloat16)


def _out_proj1(os_, lses, proj, x, w_out, norm_f, *, tm=512):
    b, s, d = x.shape
    gate_block = proj.shape[2] // d - 1
    dils = [dil for _, dil in DIL_PAIRS]

    def grouped(width, dil):
        return pl.BlockSpec((1, dil, tm // dil, width), lambda bi, i: (bi, 0, i, 0))

    tok = pl.BlockSpec((1, tm, d), lambda bi, i: (bi, i, 0))
    return pl.pallas_call(
        _out1_kernel,
        grid=(b, s // tm),
        in_specs=(
            [grouped(d, dil) for dil in dils]
            + [grouped(LSE_LANES, dil) for dil in dils]
            + [pl.BlockSpec((1, tm, d), lambda bi, i: (bi, i, gate_block)),
               tok,
               pl.BlockSpec((d, d), lambda bi, i: (0, 0)),
               pl.BlockSpec((2 * LSE_LANES, d), lambda bi, i: (0, 0)),
               pl.BlockSpec((1, d), lambda bi, i: (0, 0))]),
        out_specs=tok,
        out_shape=jax.ShapeDtypeStruct((b, s, d), jnp.float32),
        scratch_shapes=[pltpu.VMEM((N_LANE_SLABS, tm, LANES), jnp.float32)],
        compiler_params=pltpu.CompilerParams(
            dimension_semantics=("parallel", "parallel"), vmem_limit_bytes=VMEM_LIMIT),
        name="out_proj1",
    )(*os_, *lses, proj, x, w_out, _head_expansion(), norm_f.reshape(1, d))


def _prep_w_in(w_in, q_blocks):
    scale = np.ones((w_in.shape[1],), np.float32)
    for blk in q_blocks:
        scale[blk * D_MODEL:(blk + 1) * D_MODEL] = 1.0 / math.sqrt(HEAD_DIM)
    return (w_in * jnp.asarray(scale)[None, :]).astype(jnp.bfloat16)


def kernel(x, norm_0, w_in_0, rpb_0, w_out_0, norm_1, w_in_1, w_out_1, norm_f):
    b, s, d = x.shape
    t = b * s

    proj0 = _norm_proj(x, norm_0, _prep_w_in(w_in_0, (0,)), 1)
    o0 = _na_attention(proj0, _na_bias_table(rpb_0, s // GRID_W))
    x1 = _out_proj0(o0.reshape(t, d), proj0.reshape(t, -1), x.reshape(t, d),
                    w_out_0.astype(jnp.bfloat16)).reshape(b, s, d)

    w1 = _prep_w_in(w_in_1, tuple(3 * g for g in range(N_DIL_GROUPS)))
    qkv_cols = 3 * d
    gate_w = w1[:, N_DIL_GROUPS * qkv_cols:]
    outs, lses, proj_gate = [], [], None
    for g, (window, dil) in enumerate(DIL_PAIRS):
        assert window // (2 * dil) == DIL_RADIUS
        w_g = w1[:, g * qkv_cols:(g + 1) * qkv_cols]
        if dil == 1:
            w_g = jnp.concatenate([w_g, gate_w], axis=1)
        qkv = _norm_proj(x1, norm_1, w_g, dil)
        if dil == 1:
            proj_gate = qkv.reshape(b, s, -1)
        o_g, lse_g = _dil_attention(qkv, dil)
        outs.append(o_g)
        lses.append(lse_g)
    return _out_proj1(outs, lses, proj_gate, x1, w_out_1.astype(jnp.bfloat16), norm_f)
```

```python
import functools
import math

import numpy as np
import jax
import jax.numpy as jnp
from jax import lax
from jax.experimental import pallas as pl
from jax.experimental.pallas import tpu as pltpu

D_MODEL = 1024
HEAD_DIM = 64
N_HEADS = 16
GRID_W = 64
NA_ROWS = 8
NA_COLS = 16
DIL_PAIRS = ((128, 1), (512, 4), (2048, 16))
N_DIL_GROUPS = len(DIL_PAIRS)
RMS_EPS = 1e-6
NEG_INF = -1e30

LANES = 128
N_LANE_SLABS = D_MODEL // LANES
HEADS_PER_STEP = 4
SLAB = HEADS_PER_STEP * HEAD_DIM
N_SLABS = D_MODEL // SLAB
Q_TILE = 128
NA_KEY_ROWS = 10
NA_WIN = NA_KEY_ROWS * GRID_W
ATTN_TILES_PER_STEP = 2
DIL_TILES_PER_STEP = 4
DIL_RADIUS = 64
DIL_WIN = Q_TILE + 2 * DIL_RADIUS
LSE_LANES = LANES
VMEM_LIMIT = 56 * 1024 * 1024

_NT_DIMS = (((1,), (1,)), ((), ()))


def _rmsnorm(x, g):
    ms = jnp.mean(x * x, axis=-1, keepdims=True)
    return x * lax.rsqrt(ms + RMS_EPS) * g


def _norm_proj_kernel(x_ref, g_ref, w_ref, o_ref, h_ref):
    @pl.when(pl.program_id(1) == 0)
    def _():
        h_ref[...] = _rmsnorm(x_ref[...], g_ref[...]).astype(h_ref.dtype)

    o_ref[...] = jnp.dot(h_ref[...], w_ref[...],
                         preferred_element_type=jnp.float32).astype(o_ref.dtype)


def _norm_proj(x, g, w, *, tm=1024, tn=1024):
    t, d = x.shape
    n = w.shape[1]
    return pl.pallas_call(
        _norm_proj_kernel,
        grid=(t // tm, n // tn),
        in_specs=[
            pl.BlockSpec((tm, d), lambda i, j: (i, 0)),
            pl.BlockSpec((1, d), lambda i, j: (0, 0)),
            pl.BlockSpec((d, tn), lambda i, j: (0, j)),
        ],
        out_specs=pl.BlockSpec((tm, tn), lambda i, j: (i, j)),
        out_shape=jax.ShapeDtypeStruct((t, n), jnp.bfloat16),
        scratch_shapes=[pltpu.VMEM((tm, d), jnp.bfloat16)],
        compiler_params=pltpu.CompilerParams(
            dimension_semantics=("parallel", "arbitrary"),
            vmem_limit_bytes=VMEM_LIMIT),
        name="norm_proj",
    )(x, g.reshape(1, d), w)


def _proj_kernel(h_ref, w_ref, o_ref):
    o_ref[...] = jnp.dot(h_ref[...], w_ref[...],
                         preferred_element_type=jnp.float32).astype(o_ref.dtype)


def _proj(h, w, *, tm=2048, tn=1024):
    t, d = h.shape
    n = w.shape[1]
    return pl.pallas_call(
        _proj_kernel,
        grid=(t // tm, n // tn),
        in_specs=[
            pl.BlockSpec((tm, d), lambda i, j: (i, 0)),
            pl.BlockSpec((d, tn), lambda i, j: (0, j)),
        ],
        out_specs=pl.BlockSpec((tm, tn), lambda i, j: (i, j)),
        out_shape=jax.ShapeDtypeStruct((t, n), jnp.bfloat16),
        compiler_params=pltpu.CompilerParams(
            dimension_semantics=("parallel", "parallel"),
            vmem_limit_bytes=VMEM_LIMIT),
        name="proj",
    )(h, w)


def _attn_kernel(q_ref, k_ref, v_ref, b_ref, *rest, win, tiles_per_seq, key_start,
                 pattern_of, emit_lse):
    if emit_lse:
        o_ref, lse_ref = rest[:2]
        scratch = rest[2:]
    else:
        o_ref, lse_ref = rest[0], None
        scratch = rest[1:]
    s_refs, p_refs, m_refs, l_refs = scratch[0:2], scratch[2:4], scratch[4:6], scratch[6:8]
    step = pl.program_id(2)
    n_tiles = k_ref.shape[1] * tiles_per_seq
    rows = HEADS_PER_STEP * Q_TILE

    @pl.when((pl.program_id(0) == 0) & (pl.program_id(1) == 0) & (step == 0))
    def _():
        for ref in (*s_refs, *p_refs, *m_refs):
            ref[...] = jnp.zeros(ref.shape, ref.dtype)
        for ref in l_refs:
            ref[...] = jnp.ones(ref.shape, ref.dtype)

    lane = lax.broadcasted_iota(jnp.int32, (Q_TILE, SLAB), 1) // HEAD_DIM

    def window(ref, t):
        seq = t // tiles_per_seq
        return ref[0, seq, pl.ds(key_start(t % tiles_per_seq), win), :]

    for u in range(ATTN_TILES_PER_STEP):
        cur, prev = u % 2, 1 - u % 2
        t = step * ATTN_TILES_PER_STEP + u
        tile_rows = slice(u * Q_TILE, (u + 1) * Q_TILE)

        t_s = jnp.minimum(t, n_tiles - 1)
        q4 = q_ref[0, 0, tile_rows, :]
        zero = jnp.zeros_like(q4)
        q_stack = jnp.concatenate(
            [jnp.where(lane == h, q4, zero) for h in range(HEADS_PER_STEP)], axis=0)
        s = lax.dot_general(q_stack, window(k_ref, t_s), _NT_DIMS,
                            preferred_element_type=jnp.float32)
        s_refs[cur][...] = s + b_ref[pattern_of(t_s % tiles_per_seq)].reshape(rows, win)

        s_prev = s_refs[prev][...]
        m = jnp.max(s_prev, axis=-1, keepdims=True)
        e = jnp.exp(s_prev - m)
        p_refs[prev][...] = e.astype(jnp.bfloat16)
        m_refs[prev][...] = jnp.broadcast_to(m, (rows, LANES))
        l_refs[prev][...] = jnp.broadcast_to(jnp.sum(e, axis=-1, keepdims=True), (rows, LANES))

        t_o = jnp.clip(t - 2, 0, n_tiles - 1)
        l = l_refs[cur][...]
        inv_l = 1.0 / l
        o_all = jnp.dot(p_refs[cur][...], window(v_ref, t_o),
                        preferred_element_type=jnp.float32)
        o_all = o_all * jnp.concatenate([inv_l] * (SLAB // LANES), axis=1)
        out = o_all[:Q_TILE]
        for h in range(1, HEADS_PER_STEP):
            out = jnp.where(lane == h, o_all[h * Q_TILE:(h + 1) * Q_TILE], out)
        o_ref[0, 0, tile_rows, :] = out.astype(o_ref.dtype)
        if emit_lse:
            lse = m_refs[cur][...] + jnp.log(l)
            lse_lane = lax.broadcasted_iota(jnp.int32, (Q_TILE, LSE_LANES), 1)
            first_head = pl.program_id(1) * HEADS_PER_STEP
            lse_tile = jnp.zeros((Q_TILE, LSE_LANES), jnp.float32)
            for h in range(HEADS_PER_STEP):
                lse_tile = jnp.where(lse_lane == first_head + h,
                                     lse[h * Q_TILE:(h + 1) * Q_TILE], lse_tile)
            lse_ref[0, 0, tile_rows, :] = lse_tile


def _windowed_attention(qkv, bias_table, *, win, key_start, pattern_of, emit_lse, name):
    b, n_seq, l, _ = qkv.shape
    tile = Q_TILE * ATTN_TILES_PER_STEP
    assert ATTN_TILES_PER_STEP % 2 == 0 and l % tile == 0
    tiles_per_seq = l // Q_TILE
    blocks_per_seq = l // tile
    n_blocks = n_seq * blocks_per_seq
    rows = HEADS_PER_STEP * Q_TILE

    def q_map(bi, g, j):
        jq = jnp.minimum(j, n_blocks - 1)
        return (bi, jq // blocks_per_seq, jq % blocks_per_seq, g)

    def out_map(bi, g, j):
        jo = jnp.maximum(j - 1, 0)
        return (bi, jo // blocks_per_seq, jo % blocks_per_seq, g)

    out_specs = [pl.BlockSpec((1, 1, tile, SLAB), out_map)]
    out_shape = [jax.ShapeDtypeStruct((b, n_seq, l, D_MODEL), jnp.bfloat16)]
    if emit_lse:
        out_specs.append(pl.BlockSpec((1, 1, tile, LSE_LANES), out_map))
        out_shape.append(jax.ShapeDtypeStruct((b, n_seq, l, N_SLABS * LSE_LANES), jnp.float32))
    scratch = ([pltpu.VMEM((rows, win), jnp.float32)] * 2
               + [pltpu.VMEM((rows, win), jnp.bfloat16)] * 2
               + [pltpu.VMEM((rows, LANES), jnp.float32)] * 4)
    kern = functools.partial(_attn_kernel, win=win, tiles_per_seq=tiles_per_seq,
                             key_start=key_start, pattern_of=pattern_of, emit_lse=emit_lse)
    res = pl.pallas_call(
        kern,
        grid=(b, N_SLABS, n_blocks + 1),
        in_specs=[
            pl.BlockSpec((1, 1, tile, SLAB), q_map),
            pl.BlockSpec((1, n_seq, l, SLAB), lambda bi, g, j: (bi, 0, 0, N_SLABS + g)),
            pl.BlockSpec((1, n_seq, l, SLAB), lambda bi, g, j: (bi, 0, 0, 2 * N_SLABS + g)),
            pl.BlockSpec((bias_table.shape[0], HEADS_PER_STEP, Q_TILE, win),
                         lambda bi, g, j: (0, g, 0, 0)),
        ],
        out_specs=out_specs,
        out_shape=out_shape,
        scratch_shapes=scratch,
        compiler_params=pltpu.CompilerParams(
            dimension_semantics=("arbitrary", "arbitrary", "arbitrary"),
            vmem_limit_bytes=VMEM_LIMIT),
        name=name,
    )(qkv, qkv, qkv, bias_table)
    return res if emit_lse else res[0]


def _na_pattern(i, n_pairs):
    return jnp.where(i < 2, i + 1, jnp.where(i >= n_pairs - 2, i - (n_pairs - 2) + 3, 0))


def _na_bias_table(rpb, rows):
    n_pairs = rows // 2
    reps = (5, 0, 1, n_pairs - 2, n_pairs - 1)
    kh = min(NA_ROWS, rows)
    n_col_off = 2 * NA_COLS - 1
    padded = jnp.pad(rpb.astype(jnp.float32), ((0, 0), (0, 0), (GRID_W, GRID_W)))
    col_part = jnp.stack(
        [padded[:, :, GRID_W + NA_COLS - 1 - c: 2 * GRID_W + NA_COLS - 1 - c] for c in range(GRID_W)],
        axis=2)
    c = np.arange(GRID_W)[:, None]
    kc = np.arange(GRID_W)[None, :]
    cs = np.clip(c - NA_COLS // 2, 0, GRID_W - NA_COLS)
    col_ok = (kc >= cs) & (kc < cs + NA_COLS)
    assert np.all(((kc - c + NA_COLS - 1 >= 0) & (kc - c + NA_COLS - 1 < n_col_off))[col_ok])
    col_part = jnp.where(jnp.asarray(col_ok), col_part, NEG_INF)
    masked = jnp.full((N_HEADS, GRID_W, GRID_W), NEG_INF, jnp.float32)
    patterns = []
    for i in reps:
        ks = int(np.clip(2 * i - NA_ROWS // 2, 0, rows - NA_KEY_ROWS))
        q_rows = []
        for qr in range(Q_TILE // GRID_W):
            r = 2 * i + qr
            rs = int(np.clip(r - kh // 2, 0, rows - kh))
            blocks = []
            for kr in range(NA_KEY_ROWS):
                k_abs = ks + kr
                ok = rs <= k_abs < rs + kh
                blocks.append(col_part[:, k_abs - r + NA_ROWS - 1] if ok else masked)
            q_rows.append(jnp.concatenate(blocks, axis=-1))
        patterns.append(jnp.concatenate(q_rows, axis=1))
    return jnp.stack(patterns)


def _na_attention(proj, bias_table):
    rows = proj.shape[2] // GRID_W
    n_pairs = rows // 2

    def key_start(i):
        first_row = jnp.clip(2 * i - NA_ROWS // 2, 0, rows - NA_KEY_ROWS)
        return pl.multiple_of(first_row * GRID_W, Q_TILE)

    return _windowed_attention(
        proj, bias_table, win=NA_WIN, key_start=key_start,
        pattern_of=lambda i: _na_pattern(i, n_pairs), emit_lse=False, name="na_attention")


def _alibi_slopes():
    return np.asarray(2.0 ** (-8.0 * (np.arange(N_HEADS) + 1) / N_HEADS), dtype=np.float32)


def _slab_attention(q4, k4, v4, bias):
    lane = lax.broadcasted_iota(jnp.int32, q4.shape, 1) // HEAD_DIM
    zero = jnp.zeros_like(q4)
    q_stack = jnp.concatenate(
        [jnp.where(lane == h, q4, zero) for h in range(HEADS_PER_STEP)], axis=0)
    s = lax.dot_general(q_stack, k4, _NT_DIMS, preferred_element_type=jnp.float32)
    s = s + bias
    m = jnp.max(s, axis=-1, keepdims=True)
    p = jnp.exp(s - m)
    l = jnp.sum(p, axis=-1, keepdims=True)
    p = p.astype(jnp.bfloat16)
    inv_l = 1.0 / l
    out = None
    for h in range(HEADS_PER_STEP):
        rows = slice(h * Q_TILE, (h + 1) * Q_TILE)
        o_h = jnp.dot(p[rows], v4, preferred_element_type=jnp.float32) * inv_l[rows]
        out = o_h if out is None else jnp.where(lane == h, o_h, out)
    return out, m, l


def _dil_kernel(q_ref, ka_ref, kb_ref, kc_ref, va_ref, vb_ref, vc_ref, b_ref,
                o_ref, lse_ref):
    step = pl.program_id(2)
    n_steps = pl.num_programs(2)
    k_win = jnp.concatenate([ka_ref[0, 0], kb_ref[0, 0], kc_ref[0, 0]], axis=0)
    v_win = jnp.concatenate([va_ref[0, 0], vb_ref[0, 0], vc_ref[0, 0]], axis=0)
    lane = lax.broadcasted_iota(jnp.int32, (Q_TILE, LSE_LANES), 1)
    for u in range(DIL_TILES_PER_STEP):
        pattern = 0
        if u == 0:
            pattern = jnp.where(step == 0, 1, pattern)
        if u == DIL_TILES_PER_STEP - 1:
            pattern = jnp.where(step == n_steps - 1, 2, pattern)
        tile = slice(u * Q_TILE, (u + 1) * Q_TILE)
        win = slice(u * Q_TILE, u * Q_TILE + DIL_WIN)
        lse_tile = jnp.zeros((Q_TILE, LSE_LANES), jnp.float32)
        for hg in range(N_SLABS):
            cols = slice(hg * SLAB, (hg + 1) * SLAB)
            bias = b_ref[pattern, hg * HEADS_PER_STEP:(hg + 1) * HEADS_PER_STEP]
            bias = bias.reshape(HEADS_PER_STEP * Q_TILE, DIL_WIN)
            out, m, l = _slab_attention(q_ref[0, 0, tile, cols], k_win[win, cols],
                                        v_win[win, cols], bias)
            o_ref[0, 0, tile, cols] = out.astype(o_ref.dtype)
            lse = m + jnp.log(l)
            for h in range(HEADS_PER_STEP):
                lse_h = lse[h * Q_TILE:(h + 1) * Q_TILE]
                lse_tile = jnp.where(lane == hg * HEADS_PER_STEP + h, lse_h, lse_tile)
        lse_ref[0, 0, tile, :] = lse_tile


def _dil_bias_table(dil):
    qi = np.arange(Q_TILE)[:, None]
    kj = np.arange(DIL_WIN)[None, :]
    delta = kj - DIL_RADIUS - qi
    in_band = np.abs(delta) <= DIL_RADIUS
    valid = np.stack([in_band,
                      in_band & (kj >= DIL_RADIUS),
                      in_band & (kj < DIL_WIN - DIL_RADIUS)])
    dist = jnp.asarray(np.abs(delta) * dil, dtype=jnp.float32)
    bias = -jnp.asarray(_alibi_slopes())[:, None, None] * dist[None]
    return jnp.where(jnp.asarray(valid)[:, None], bias[None], NEG_INF)


def _dil_attention(qkv, dil):
    b, _, l, _ = qkv.shape
    tile = Q_TILE * DIL_TILES_PER_STEP
    n_steps = l // tile
    assert l % tile == 0 and DIL_TILES_PER_STEP >= 2
    half = DIL_RADIUS
    per_tile = tile // half
    n_half = l // half

    def spec_mid(which):
        return pl.BlockSpec((1, 1, tile, D_MODEL), lambda bi, r, i: (bi, r, i, which))

    def spec_lo(which):
        return pl.BlockSpec((1, 1, half, D_MODEL),
                            lambda bi, r, i: (bi, r, jnp.maximum(per_tile * i - 1, 0), which))

    def spec_hi(which):
        return pl.BlockSpec((1, 1, half, D_MODEL),
                            lambda bi, r, i: (bi, r, jnp.minimum(per_tile * (i + 1), n_half - 1), which))

    bias = _dil_bias_table(dil)
    return pl.pallas_call(
        _dil_kernel,
        grid=(b, dil, n_steps),
        in_specs=[
            spec_mid(0),
            spec_lo(1), spec_mid(1), spec_hi(1),
            spec_lo(2), spec_mid(2), spec_hi(2),
            pl.BlockSpec(bias.shape, lambda bi, r, i: (0, 0, 0, 0)),
        ],
        out_specs=[
            pl.BlockSpec((1, 1, tile, D_MODEL), lambda bi, r, i: (bi, r, i, 0)),
            pl.BlockSpec((1, 1, tile, LSE_LANES), lambda bi, r, i: (bi, r, i, 0)),
        ],
        out_shape=[
            jax.ShapeDtypeStruct((b, dil, l, D_MODEL), jnp.bfloat16),
            jax.ShapeDtypeStruct((b, dil, l, LSE_LANES), jnp.float32),
        ],
        compiler_params=pltpu.CompilerParams(
            dimension_semantics=("parallel", "parallel", "arbitrary"),
            vmem_limit_bytes=VMEM_LIMIT),
        name=f"dilated_attention_{dil}",
    )(qkv, qkv, qkv, qkv, qkv, qkv, qkv, bias)


def _silu(x):
    return x * jax.nn.sigmoid(x)


def _out0_kernel(o_ref, gate_ref, x_ref, w_ref, g_ref, y_ref, *rest):
    h_refs, slab_ref = rest[:-1], rest[-1]
    o = o_ref[0].astype(jnp.float32)
    gate = gate_ref[0].astype(jnp.float32)
    z = (o * _silu(gate)).astype(jnp.bfloat16)
    y = x_ref[0] + jnp.dot(z, w_ref[...], preferred_element_type=jnp.float32)
    y_ref[0] = y
    hn = _rmsnorm(y, g_ref[...])
    tm = hn.shape[0]
    staged = False
    for h_ref, (_, dil) in zip(h_refs, DIL_PAIRS):
        if dil == 1:
            h_ref[0, 0] = hn.astype(h_ref.dtype)
            continue
        if not staged:
            for s in range(N_LANE_SLABS):
                slab_ref[s] = hn[:, s * LANES:(s + 1) * LANES]
            staged = True
        n = tm // dil
        for r in range(dil):
            for s in range(N_LANE_SLABS):
                rows = slab_ref[s, pl.ds(r, n, stride=dil), :]
                h_ref[0, r, :, s * LANES:(s + 1) * LANES] = rows.astype(h_ref.dtype)


def _out_proj0(o, proj, x, w_out, g_next, *, tm=512):
    b, s, d = x.shape
    gate_block = proj.shape[2] // d - 1
    tok = pl.BlockSpec((1, tm, d), lambda bi, i: (bi, i, 0))
    dils = [dil for _, dil in DIL_PAIRS]
    return pl.pallas_call(
        _out0_kernel,
        grid=(b, s // tm),
        in_specs=[
            tok,
            pl.BlockSpec((1, tm, d), lambda bi, i: (bi, i, gate_block)),
            tok,
            pl.BlockSpec((d, d), lambda bi, i: (0, 0)),
            pl.BlockSpec((1, d), lambda bi, i: (0, 0)),
        ],
        out_specs=[tok] + [pl.BlockSpec((1, dil, tm // dil, d), lambda bi, i: (bi, 0, i, 0))
                           for dil in dils],
        out_shape=([jax.ShapeDtypeStruct((b, s, d), jnp.float32)]
                   + [jax.ShapeDtypeStruct((b, dil, s // dil, d), jnp.bfloat16) for dil in dils]),
        scratch_shapes=[pltpu.VMEM((N_LANE_SLABS, tm, LANES), jnp.float32)],
        compiler_params=pltpu.CompilerParams(
            dimension_semantics=("parallel", "parallel"), vmem_limit_bytes=VMEM_LIMIT),
        name="out_proj0",
    )(o, proj, x, w_out, g_next.reshape(1, d))


def _natural_order(ref, slab_ref, dil):
    _, n, c = ref.shape
    if dil == 1:
        return ref[0].astype(jnp.float32)
    pieces = []
    for s in range(c // LANES):
        for r in range(dil):
            slab_ref[s, pl.ds(r, n, stride=dil), :] = (
                ref[r, :, s * LANES:(s + 1) * LANES].astype(jnp.float32))
        pieces.append(slab_ref[s])
    return pieces[0] if len(pieces) == 1 else jnp.concatenate(pieces, axis=1)


def _out1_kernel(o0_ref, o1_ref, o2_ref, l0_ref, l1_ref, l2_ref, gate_ref, x_ref,
                 w_ref, e_ref, g_ref, y_ref, slab_ref):
    dils = [dil for _, dil in DIL_PAIRS]
    lses = [_natural_order(ref.at[0], slab_ref, dil)
            for ref, dil in zip((l0_ref, l1_ref, l2_ref), dils)]
    m = jnp.maximum(jnp.maximum(lses[0], lses[1]), lses[2])
    es = [jnp.exp(v - m) for v in lses]
    denom = es[0] + es[1] + es[2]
    o = None
    for e, o_ref, dil in zip(es, (o0_ref, o1_ref, o2_ref), dils):
        w = e / denom
        hi = w.astype(jnp.bfloat16)
        lo = (w - hi.astype(jnp.float32)).astype(jnp.bfloat16)
        w_full = jnp.dot(jnp.concatenate([hi, lo], axis=1), e_ref[...],
                         preferred_element_type=jnp.float32)
        term = w_full * _natural_order(o_ref.at[0], slab_ref, dil)
        o = term if o is None else o + term
    gate = gate_ref[0].astype(jnp.float32)
    z = (o * _silu(gate)).astype(jnp.bfloat16)
    x = x_ref[0] + jnp.dot(z, w_ref[...], preferred_element_type=jnp.float32)
    ms = jnp.mean(x * x, axis=-1, keepdims=True)
    y_ref[0] = x * lax.rsqrt(ms + RMS_EPS) * g_ref[...]


def _head_expansion():
    e = np.zeros((LSE_LANES, D_MODEL), np.float32)
    for h in range(N_HEADS):
        e[h, h * HEAD_DIM:(h + 1) * HEAD_DIM] = 1.0
    return jnp.asarray(np.concatenate([e, e], axis=0), dtype=jnp.bfloat16)


def _out_proj1(os_, lses, proj, x, w_out, norm_f, *, tm=512):
    b, s, d = x.shape
    gate_block = proj.shape[2] // d - 1
    dils = [dil for _, dil in DIL_PAIRS]

    def grouped(width, dil):
        return pl.BlockSpec((1, dil, tm // dil, width), lambda bi, i: (bi, 0, i, 0))

    tok = pl.BlockSpec((1, tm, d), lambda bi, i: (bi, i, 0))
    return pl.pallas_call(
        _out1_kernel,
        grid=(b, s // tm),
        in_specs=(
            [grouped(d, dil) for dil in dils]
            + [grouped(LSE_LANES, dil) for dil in dils]
            + [pl.BlockSpec((1, tm, d), lambda bi, i: (bi, i, gate_block)),
               tok,
               pl.BlockSpec((d, d), lambda bi, i: (0, 0)),
               pl.BlockSpec((2 * LSE_LANES, d), lambda bi, i: (0, 0)),
               pl.BlockSpec((1, d), lambda bi, i: (0, 0))]),
        out_specs=tok,
        out_shape=jax.ShapeDtypeStruct((b, s, d), jnp.float32),
        scratch_shapes=[pltpu.VMEM((N_LANE_SLABS, tm, LANES), jnp.float32)],
        compiler_params=pltpu.CompilerParams(
            dimension_semantics=("parallel", "parallel"), vmem_limit_bytes=VMEM_LIMIT),
        name="out_proj1",
    )(*os_, *lses, proj, x, w_out, _head_expansion(), norm_f.reshape(1, d))


def _prep_w_in(w_in, q_blocks):
    scale = np.ones((w_in.shape[1],), np.float32)
    for blk in q_blocks:
        scale[blk * D_MODEL:(blk + 1) * D_MODEL] = 1.0 / math.sqrt(HEAD_DIM)
    return (w_in * jnp.asarray(scale)[None, :]).astype(jnp.bfloat16)


def kernel(x, norm_0, w_in_0, rpb_0, w_out_0, norm_1, w_in_1, w_out_1, norm_f):
    b, s, d = x.shape
    t = b * s

    proj0 = _norm_proj(x.reshape(t, d), norm_0, _prep_w_in(w_in_0, (0,)))
    o0 = _na_attention(proj0.reshape(b, 1, s, -1), _na_bias_table(rpb_0, s // GRID_W))
    x1, *h1 = _out_proj0(o0.reshape(b, s, d), proj0.reshape(b, s, -1), x,
                         w_out_0.astype(jnp.bfloat16), norm_1)

    w1 = _prep_w_in(w_in_1, tuple(3 * g for g in range(N_DIL_GROUPS)))
    qkv_cols = 3 * d
    gate_w = w1[:, N_DIL_GROUPS * qkv_cols:]
    outs, lses, proj_gate = [], [], None
    for g, (window, dil) in enumerate(DIL_PAIRS):
        assert window // (2 * dil) == DIL_RADIUS
        w_g = w1[:, g * qkv_cols:(g + 1) * qkv_cols]
        if dil == 1:
            w_g = jnp.concatenate([w_g, gate_w], axis=1)
        qkv = _proj(h1[g].reshape(t, d), w_g).reshape(b, dil, s // dil, -1)
        if dil == 1:
            proj_gate = qkv.reshape(b, s, -1)
        o_g, lse_g = _dil_attention(qkv, dil)
        outs.append(o_g)
        lses.append(lse_g)
    return _out_proj1(outs, lses, proj_gate, x1, w_out_1.astype(jnp.bfloat16), norm_f)
```

```python
import functools
import math

import numpy as np
import jax
import jax.numpy as jnp
from jax import lax
from jax.experimental import pallas as pl
from jax.experimental.pallas import tpu as pltpu

D_MODEL = 1024
HEAD_DIM = 64
N_HEADS = 16
GRID_W = 64
NA_ROWS = 8
NA_COLS = 16
DIL_PAIRS = ((128, 1), (512, 4), (2048, 16))
N_DIL_GROUPS = len(DIL_PAIRS)
RMS_EPS = 1e-6
NEG_INF = -1e30

LANES = 128
N_LANE_SLABS = D_MODEL // LANES
HEADS_PER_STEP = 4
SLAB = HEADS_PER_STEP * HEAD_DIM
N_SLABS = D_MODEL // SLAB
Q_TILE = 128
NA_Q_TILE = GRID_W
NA_WIN = NA_ROWS * GRID_W
NA_TILES_PER_STEP = 4
DIL_TILES_PER_STEP = 4
DIL_RADIUS = 64
DIL_WIN = Q_TILE + 2 * DIL_RADIUS
LSE_LANES = LANES
VMEM_LIMIT = 56 * 1024 * 1024

_NT_DIMS = (((1,), (1,)), ((), ()))


def _rmsnorm(x, g):
    ms = jnp.mean(x * x, axis=-1, keepdims=True)
    return x * lax.rsqrt(ms + RMS_EPS) * g


def _norm_proj_kernel(x_ref, g_ref, w_ref, o_ref, h_ref):
    @pl.when(pl.program_id(1) == 0)
    def _():
        h_ref[...] = _rmsnorm(x_ref[...], g_ref[...]).astype(h_ref.dtype)

    o_ref[...] = jnp.dot(h_ref[...], w_ref[...],
                         preferred_element_type=jnp.float32).astype(o_ref.dtype)


def _norm_proj(x, g, w, *, tm=1024, tn=1024):
    t, d = x.shape
    n = w.shape[1]
    return pl.pallas_call(
        _norm_proj_kernel,
        grid=(t // tm, n // tn),
        in_specs=[
            pl.BlockSpec((tm, d), lambda i, j: (i, 0)),
            pl.BlockSpec((1, d), lambda i, j: (0, 0)),
            pl.BlockSpec((d, tn), lambda i, j: (0, j)),
        ],
        out_specs=pl.BlockSpec((tm, tn), lambda i, j: (i, j)),
        out_shape=jax.ShapeDtypeStruct((t, n), jnp.bfloat16),
        scratch_shapes=[pltpu.VMEM((tm, d), jnp.bfloat16)],
        compiler_params=pltpu.CompilerParams(
            dimension_semantics=("parallel", "arbitrary"),
            vmem_limit_bytes=VMEM_LIMIT),
        name="norm_proj",
    )(x, g.reshape(1, d), w)


def _proj_kernel(h_ref, w_ref, o_ref):
    o_ref[...] = jnp.dot(h_ref[...], w_ref[...],
                         preferred_element_type=jnp.float32).astype(o_ref.dtype)


def _proj(h, w, *, tm=2048, tn=1024):
    t, d = h.shape
    n = w.shape[1]
    return pl.pallas_call(
        _proj_kernel,
        grid=(t // tm, n // tn),
        in_specs=[
            pl.BlockSpec((tm, d), lambda i, j: (i, 0)),
            pl.BlockSpec((d, tn), lambda i, j: (0, j)),
        ],
        out_specs=pl.BlockSpec((tm, tn), lambda i, j: (i, j)),
        out_shape=jax.ShapeDtypeStruct((t, n), jnp.bfloat16),
        compiler_params=pltpu.CompilerParams(
            dimension_semantics=("parallel", "parallel"),
            vmem_limit_bytes=VMEM_LIMIT),
        name="proj",
    )(h, w)


def _na_kernel(q_ref, k_ref, v_ref, b_ref, o_ref, s0, s1, p0, p1, l0, l1, stage_ref):
    s_refs, p_refs, l_refs = (s0, s1), (p0, p1), (l0, l1)
    step = pl.program_id(2)
    n_tiles = k_ref.shape[1] // GRID_W
    rows = HEADS_PER_STEP * NA_Q_TILE
    early = (NA_TILES_PER_STEP - 2) * NA_Q_TILE

    @pl.when((pl.program_id(0) == 0) & (pl.program_id(1) == 0) & (step == 0))
    def _():
        for ref in (s0, s1, p0, p1, stage_ref):
            ref[...] = jnp.zeros(ref.shape, ref.dtype)
        for ref in l_refs:
            ref[...] = jnp.ones(ref.shape, ref.dtype)

    o_ref[0, :early, :] = stage_ref[...]
    lane = lax.broadcasted_iota(jnp.int32, (NA_Q_TILE, SLAB), 1) // HEAD_DIM

    def window(ref, r):
        first_row = jnp.clip(r - NA_ROWS // 2, 0, n_tiles - NA_ROWS)
        return ref[0, pl.ds(pl.multiple_of(first_row * GRID_W, GRID_W), NA_WIN), :]

    for u in range(NA_TILES_PER_STEP):
        cur, prev = u % 2, 1 - u % 2
        t = step * NA_TILES_PER_STEP + u

        t_s = jnp.minimum(t, n_tiles - 1)
        q4 = q_ref[0, u * NA_Q_TILE:(u + 1) * NA_Q_TILE, :]
        zero = jnp.zeros_like(q4)
        q_stack = jnp.concatenate(
            [jnp.where(lane == h, q4, zero) for h in range(HEADS_PER_STEP)], axis=0)
        s = lax.dot_general(q_stack, window(k_ref, t_s), _NT_DIMS,
                            preferred_element_type=jnp.float32)
        s_refs[cur][...] = s + b_ref[_na_pattern(t_s, n_tiles)].reshape(rows, NA_WIN)

        s_prev = s_refs[prev][...]
        m = jnp.max(s_prev, axis=-1, keepdims=True)
        e = jnp.exp(s_prev - m)
        p_refs[prev][...] = e.astype(jnp.bfloat16)
        l_refs[prev][...] = jnp.broadcast_to(jnp.sum(e, axis=-1, keepdims=True), (rows, LANES))

        t_o = jnp.clip(t - 2, 0, n_tiles - 1)
        inv_l = 1.0 / l_refs[cur][...]
        o_all = jnp.dot(p_refs[cur][...], window(v_ref, t_o),
                        preferred_element_type=jnp.float32)
        o_all = o_all * jnp.concatenate([inv_l] * (SLAB // LANES), axis=1)
        out = o_all[:NA_Q_TILE]
        for h in range(1, HEADS_PER_STEP):
            out = jnp.where(lane == h, o_all[h * NA_Q_TILE:(h + 1) * NA_Q_TILE], out)
        out = out.astype(o_ref.dtype)
        if u < 2:
            o_ref[0, early + u * NA_Q_TILE:early + (u + 1) * NA_Q_TILE, :] = out
        else:
            stage_ref[(u - 2) * NA_Q_TILE:(u - 1) * NA_Q_TILE, :] = out


def _na_pattern(r, rows):
    lo = NA_ROWS // 2
    hi = rows - NA_ROWS // 2
    return jnp.where(r < lo, 1 + r, jnp.where(r > hi, lo + r - hi, 0))


def _na_bias_table(rpb, rows):
    n_col_off = 2 * NA_COLS - 1
    padded = jnp.pad(rpb.astype(jnp.float32), ((0, 0), (0, 0), (GRID_W, GRID_W)))
    col_part = jnp.stack(
        [padded[:, :, GRID_W + NA_COLS - 1 - c: 2 * GRID_W + NA_COLS - 1 - c] for c in range(GRID_W)],
        axis=2)
    c = np.arange(GRID_W)[:, None]
    kc = np.arange(GRID_W)[None, :]
    cs = np.clip(c - NA_COLS // 2, 0, GRID_W - NA_COLS)
    col_ok = (kc >= cs) & (kc < cs + NA_COLS)
    assert np.all(((kc - c + NA_COLS - 1 >= 0) & (kc - c + NA_COLS - 1 < n_col_off))[col_ok])
    col_part = jnp.where(jnp.asarray(col_ok), col_part, NEG_INF)
    lo, hi = NA_ROWS // 2, rows - NA_ROWS // 2
    reps = [lo] + list(range(lo)) + list(range(hi + 1, rows))
    patterns = []
    for r in reps:
        first = int(np.clip(r - NA_ROWS // 2, 0, rows - NA_ROWS)) - r + NA_ROWS - 1
        block = col_part[:, first:first + NA_ROWS]
        patterns.append(block.transpose(0, 2, 1, 3).reshape(N_HEADS, GRID_W, NA_WIN))
    return jnp.stack(patterns)


def _na_attention(proj, bias_table):
    b, s, _ = proj.shape
    tile = NA_Q_TILE * NA_TILES_PER_STEP
    assert NA_TILES_PER_STEP % 2 == 0 and s % tile == 0 and NA_Q_TILE == GRID_W
    n_blocks = s // tile
    rows = HEADS_PER_STEP * NA_Q_TILE
    scratch = ([pltpu.VMEM((rows, NA_WIN), jnp.float32)] * 2
               + [pltpu.VMEM((rows, NA_WIN), jnp.bfloat16)] * 2
               + [pltpu.VMEM((rows, LANES), jnp.float32)] * 2
               + [pltpu.VMEM(((NA_TILES_PER_STEP - 2) * NA_Q_TILE, SLAB), jnp.bfloat16)])
    return pl.pallas_call(
        _na_kernel,
        grid=(b, N_SLABS, n_blocks + 1),
        in_specs=[
            pl.BlockSpec((1, tile, SLAB), lambda bi, g, j: (bi, jnp.minimum(j, n_blocks - 1), g)),
            pl.BlockSpec((1, s, SLAB), lambda bi, g, j: (bi, 0, N_SLABS + g)),
            pl.BlockSpec((1, s, SLAB), lambda bi, g, j: (bi, 0, 2 * N_SLABS + g)),
            pl.BlockSpec((bias_table.shape[0], HEADS_PER_STEP, NA_Q_TILE, NA_WIN),
                         lambda bi, g, j: (0, g, 0, 0)),
        ],
        out_specs=pl.BlockSpec((1, tile, SLAB), lambda bi, g, j: (bi, jnp.maximum(j - 1, 0), g)),
        out_shape=jax.ShapeDtypeStruct((b, s, D_MODEL), jnp.bfloat16),
        scratch_shapes=scratch,
        compiler_params=pltpu.CompilerParams(
            dimension_semantics=("arbitrary", "arbitrary", "arbitrary"),
            vmem_limit_bytes=VMEM_LIMIT),
        name="na_attention",
    )(proj, proj, proj, bias_table)


def _alibi_slopes():
    return np.asarray(2.0 ** (-8.0 * (np.arange(N_HEADS) + 1) / N_HEADS), dtype=np.float32)


def _slab_attention(q4, k4, v4, bias):
    lane = lax.broadcasted_iota(jnp.int32, q4.shape, 1) // HEAD_DIM
    zero = jnp.zeros_like(q4)
    q_stack = jnp.concatenate(
        [jnp.where(lane == h, q4, zero) for h in range(HEADS_PER_STEP)], axis=0)
    s = lax.dot_general(q_stack, k4, _NT_DIMS, preferred_element_type=jnp.float32)
    s = s + bias
    m = jnp.max(s, axis=-1, keepdims=True)
    p = jnp.exp(s - m)
    l = jnp.sum(p, axis=-1, keepdims=True)
    p = p.astype(jnp.bfloat16)
    inv_l = 1.0 / l
    out = None
    for h in range(HEADS_PER_STEP):
        rows = slice(h * Q_TILE, (h + 1) * Q_TILE)
        o_h = jnp.dot(p[rows], v4, preferred_element_type=jnp.float32) * inv_l[rows]
        out = o_h if out is None else jnp.where(lane == h, o_h, out)
    return out, m, l


def _dil_kernel(q_ref, ka_ref, kb_ref, kc_ref, va_ref, vb_ref, vc_ref, b_ref,
                o_ref, lse_ref):
    step = pl.program_id(2)
    n_steps = pl.num_programs(2)
    k_win = jnp.concatenate([ka_ref[0, 0], kb_ref[0, 0], kc_ref[0, 0]], axis=0)
    v_win = jnp.concatenate([va_ref[0, 0], vb_ref[0, 0], vc_ref[0, 0]], axis=0)
    lane = lax.broadcasted_iota(jnp.int32, (Q_TILE, LSE_LANES), 1)
    for u in range(DIL_TILES_PER_STEP):
        pattern = 0
        if u == 0:
            pattern = jnp.where(step == 0, 1, pattern)
        if u == DIL_TILES_PER_STEP - 1:
            pattern = jnp.where(step == n_steps - 1, 2, pattern)
        tile = slice(u * Q_TILE, (u + 1) * Q_TILE)
        win = slice(u * Q_TILE, u * Q_TILE + DIL_WIN)
        lse_tile = jnp.zeros((Q_TILE, LSE_LANES), jnp.float32)
        for hg in range(N_SLABS):
            cols = slice(hg * SLAB, (hg + 1) * SLAB)
            bias = b_ref[pattern, hg * HEADS_PER_STEP:(hg + 1) * HEADS_PER_STEP]
            bias = bias.reshape(HEADS_PER_STEP * Q_TILE, DIL_WIN)
            out, m, l = _slab_attention(q_ref[0, 0, tile, cols], k_win[win, cols],
                                        v_win[win, cols], bias)
            o_ref[0, 0, tile, cols] = out.astype(o_ref.dtype)
            lse = m + jnp.log(l)
            for h in range(HEADS_PER_STEP):
                lse_h = lse[h * Q_TILE:(h + 1) * Q_TILE]
                lse_tile = jnp.where(lane == hg * HEADS_PER_STEP + h, lse_h, lse_tile)
        lse_ref[0, 0, tile, :] = lse_tile


def _dil_bias_table(dil):
    qi = np.arange(Q_TILE)[:, None]
    kj = np.arange(DIL_WIN)[None, :]
    delta = kj - DIL_RADIUS - qi
    in_band = np.abs(delta) <= DIL_RADIUS
    valid = np.stack([in_band,
                      in_band & (kj >= DIL_RADIUS),
                      in_band & (kj < DIL_WIN - DIL_RADIUS)])
    dist = jnp.asarray(np.abs(delta) * dil, dtype=jnp.float32)
    bias = -jnp.asarray(_alibi_slopes())[:, None, None] * dist[None]
    return jnp.where(jnp.asarray(valid)[:, None], bias[None], NEG_INF)


def _dil_attention(qkv, dil):
    b, _, l, _ = qkv.shape
    tile = Q_TILE * DIL_TILES_PER_STEP
    n_steps = l // tile
    assert l % tile == 0 and DIL_TILES_PER_STEP >= 2
    half = DIL_RADIUS
    per_tile = tile // half
    n_half = l // half

    def spec_mid(which):
        return pl.BlockSpec((1, 1, tile, D_MODEL), lambda bi, r, i: (bi, r, i, which))

    def spec_lo(which):
        return pl.BlockSpec((1, 1, half, D_MODEL),
                            lambda bi, r, i: (bi, r, jnp.maximum(per_tile * i - 1, 0), which))

    def spec_hi(which):
        return pl.BlockSpec((1, 1, half, D_MODEL),
                            lambda bi, r, i: (bi, r, jnp.minimum(per_tile * (i + 1), n_half - 1), which))

    bias = _dil_bias_table(dil)
    return pl.pallas_call(
        _dil_kernel,
        grid=(b, dil, n_steps),
        in_specs=[
            spec_mid(0),
            spec_lo(1), spec_mid(1), spec_hi(1),
            spec_lo(2), spec_mid(2), spec_hi(2),
            pl.BlockSpec(bias.shape, lambda bi, r, i: (0, 0, 0, 0)),
        ],
        out_specs=[
            pl.BlockSpec((1, 1, tile, D_MODEL), lambda bi, r, i: (bi, r, i, 0)),
            pl.BlockSpec((1, 1, tile, LSE_LANES), lambda bi, r, i: (bi, r, i, 0)),
        ],
        out_shape=[
            jax.ShapeDtypeStruct((b, dil, l, D_MODEL), jnp.bfloat16),
            jax.ShapeDtypeStruct((b, dil, l, LSE_LANES), jnp.float32),
        ],
        compiler_params=pltpu.CompilerParams(
            dimension_semantics=("parallel", "parallel", "arbitrary"),
            vmem_limit_bytes=VMEM_LIMIT),
        name=f"dilated_attention_{dil}",
    )(qkv, qkv, qkv, qkv, qkv, qkv, qkv, bias)


def _silu(x):
    return x * jax.nn.sigmoid(x)


def _out0_kernel(o_ref, gate_ref, x_ref, w_ref, g_ref, y_ref, *rest):
    h_refs, slab_ref = rest[:-1], rest[-1]
    o = o_ref[0].astype(jnp.float32)
    gate = gate_ref[0].astype(jnp.float32)
    z = (o * _silu(gate)).astype(jnp.bfloat16)
    y = x_ref[0] + jnp.dot(z, w_ref[...], preferred_element_type=jnp.float32)
    y_ref[0] = y
    hn = _rmsnorm(y, g_ref[...])
    tm = hn.shape[0]
    staged = False
    for h_ref, (_, dil) in zip(h_refs, DIL_PAIRS):
        if dil == 1:
            h_ref[0, 0] = hn.astype(h_ref.dtype)
            continue
        if not staged:
            for s in range(N_LANE_SLABS):
                slab_ref[s] = hn[:, s * LANES:(s + 1) * LANES]
            staged = True
        n = tm // dil
        for r in range(dil):
            for s in range(N_LANE_SLABS):
                rows = slab_ref[s, pl.ds(r, n, stride=dil), :]
                h_ref[0, r, :, s * LANES:(s + 1) * LANES] = rows.astype(h_ref.dtype)


def _out_proj0(o, proj, x, w_out, g_next, *, tm=512):
    b, s, d = x.shape
    gate_block = proj.shape[2] // d - 1
    tok = pl.BlockSpec((1, tm, d), lambda bi, i: (bi, i, 0))
    dils = [dil for _, dil in DIL_PAIRS]
    return pl.pallas_call(
        _out0_kernel,
        grid=(b, s // tm),
        in_specs=[
            tok,
            pl.BlockSpec((1, tm, d), lambda bi, i: (bi, i, gate_block)),
            tok,
            pl.BlockSpec((d, d), lambda bi, i: (0, 0)),
            pl.BlockSpec((1, d), lambda bi, i: (0, 0)),
        ],
        out_specs=[tok] + [pl.BlockSpec((1, dil, tm // dil, d), lambda bi, i: (bi, 0, i, 0))
                           for dil in dils],
        out_shape=([jax.ShapeDtypeStruct((b, s, d), jnp.float32)]
                   + [jax.ShapeDtypeStruct((b, dil, s // dil, d), jnp.bfloat16) for dil in dils]),
        scratch_shapes=[pltpu.VMEM((N_LANE_SLABS, tm, LANES), jnp.float32)],
        compiler_params=pltpu.CompilerParams(
            dimension_semantics=("parallel", "parallel"), vmem_limit_bytes=VMEM_LIMIT),
        name="out_proj0",
    )(o, proj, x, w_out, g_next.reshape(1, d))


def _natural_order(ref, slab_ref, dil):
    _, n, c = ref.shape
    if dil == 1:
        return ref[0].astype(jnp.float32)
    pieces = []
    for s in range(c // LANES):
        for r in range(dil):
            slab_ref[s, pl.ds(r, n, stride=dil), :] = (
                ref[r, :, s * LANES:(s + 1) * LANES].astype(jnp.float32))
        pieces.append(slab_ref[s])
    return pieces[0] if len(pieces) == 1 else jnp.concatenate(pieces, axis=1)


def _out1_kernel(o0_ref, o1_ref, o2_ref, l0_ref, l1_ref, l2_ref, gate_ref, x_ref,
                 w_ref, e_ref, g_ref, y_ref, slab_ref):
    dils = [dil for _, dil in DIL_PAIRS]
    lses = [_natural_order(ref.at[0], slab_ref, dil)
            for ref, dil in zip((l0_ref, l1_ref, l2_ref), dils)]
    m = jnp.maximum(jnp.maximum(lses[0], lses[1]), lses[2])
    es = [jnp.exp(v - m) for v in lses]
    denom = es[0] + es[1] + es[2]
    o = None
    for e, o_ref, dil in zip(es, (o0_ref, o1_ref, o2_ref), dils):
        w = e / denom
        hi = w.astype(jnp.bfloat16)
        lo = (w - hi.astype(jnp.float32)).astype(jnp.bfloat16)
        w_full = jnp.dot(jnp.concatenate([hi, lo], axis=1), e_ref[...],
                         preferred_element_type=jnp.float32)
        term = w_full * _natural_order(o_ref.at[0], slab_ref, dil)
        o = term if o is None else o + term
    gate = gate_ref[0].astype(jnp.float32)
    z = (o * _silu(gate)).astype(jnp.bfloat16)
    x = x_ref[0] + jnp.dot(z, w_ref[...], preferred_element_type=jnp.float32)
    y_ref[0] = _rmsnorm(x, g_ref[...])


def _head_expansion():
    e = np.zeros((LSE_LANES, D_MODEL), np.float32)
    for h in range(N_HEADS):
        e[h, h * HEAD_DIM:(h + 1) * HEAD_DIM] = 1.0
    return jnp.asarray(np.concatenate([e, e], axis=0), dtype=jnp.bfloat16)


def _out_proj1(os_, lses, proj, x, w_out, norm_f, *, tm=512):
    b, s, d = x.shape
    gate_block = proj.shape[2] // d - 1
    dils = [dil for _, dil in DIL_PAIRS]

    def grouped(width, dil):
        return pl.BlockSpec((1, dil, tm // dil, width), lambda bi, i: (bi, 0, i, 0))

    tok = pl.BlockSpec((1, tm, d), lambda bi, i: (bi, i, 0))
    return pl.pallas_call(
        _out1_kernel,
        grid=(b, s // tm),
        in_specs=(
            [grouped(d, dil) for dil in dils]
            + [grouped(LSE_LANES, dil) for dil in dils]
            + [pl.BlockSpec((1, tm, d), lambda bi, i: (bi, i, gate_block)),
               tok,
               pl.BlockSpec((d, d), lambda bi, i: (0, 0)),
               pl.BlockSpec((2 * LSE_LANES, d), lambda bi, i: (0, 0)),
               pl.BlockSpec((1, d), lambda bi, i: (0, 0))]),
        out_specs=tok,
        out_shape=jax.ShapeDtypeStruct((b, s, d), jnp.float32),
        scratch_shapes=[pltpu.VMEM((N_LANE_SLABS, tm, LANES), jnp.float32)],
        compiler_params=pltpu.CompilerParams(
            dimension_semantics=("parallel", "parallel"), vmem_limit_bytes=VMEM_LIMIT),
        name="out_proj1",
    )(*os_, *lses, proj, x, w_out, _head_expansion(), norm_f.reshape(1, d))


def _prep_w_in(w_in, q_blocks):
    scale = np.ones((w_in.shape[1],), np.float32)
    for blk in q_blocks:
        scale[blk * D_MODEL:(blk + 1) * D_MODEL] = 1.0 / math.sqrt(HEAD_DIM)
    return (w_in * jnp.asarray(scale)[None, :]).astype(jnp.bfloat16)


def kernel(x, norm_0, w_in_0, rpb_0, w_out_0, norm_1, w_in_1, w_out_1, norm_f):
    b, s, d = x.shape
    t = b * s

    proj0 = _norm_proj(x.reshape(t, d), norm_0, _prep_w_in(w_in_0, (0,))).reshape(b, s, -1)
    o0 = _na_attention(proj0, _na_bias_table(rpb_0, s // GRID_W))
    x1, *h1 = _out_proj0(o0, proj0, x, w_out_0.astype(jnp.bfloat16), norm_1)

    w1 = _prep_w_in(w_in_1, tuple(3 * g for g in range(N_DIL_GROUPS)))
    qkv_cols = 3 * d
    gate_w = w1[:, N_DIL_GROUPS * qkv_cols:]
    outs, lses, proj_gate = [], [], None
    for g, (window, dil) in enumerate(DIL_PAIRS):
        assert window // (2 * dil) == DIL_RADIUS
        w_g = w1[:, g * qkv_cols:(g + 1) * qkv_cols]
        if dil == 1:
            w_g = jnp.concatenate([w_g, gate_w], axis=1)
        qkv = _proj(h1[g].reshape(t, d), w_g).reshape(b, dil, s // dil, -1)
        if dil == 1:
            proj_gate = qkv.reshape(b, s, -1)
        o_g, lse_g = _dil_attention(qkv, dil)
        outs.append(o_g)
        lses.append(lse_g)
    return _out_proj1(outs, lses, proj_gate, x1, w_out_1.astype(jnp.bfloat16), norm_f)
```

```python
import functools
import math

import numpy as np
import jax
import jax.numpy as jnp
from jax import lax
from jax.experimental import pallas as pl
from jax.experimental.pallas import tpu as pltpu

D_MODEL = 1024
HEAD_DIM = 64
N_HEADS = 16
GRID_W = 64
NA_ROWS = 8
NA_COLS = 16
DIL_PAIRS = ((128, 1), (512, 4), (2048, 16))
N_DIL_GROUPS = len(DIL_PAIRS)
RMS_EPS = 1e-6
NEG_INF = -1e30

LANES = 128
N_LANE_SLABS = D_MODEL // LANES
HEADS_PER_STEP = 4
SLAB = HEADS_PER_STEP * HEAD_DIM
N_SLABS = D_MODEL // SLAB
Q_TILE = 128
NA_Q_TILE = GRID_W
NA_WIN = NA_ROWS * GRID_W
NA_TILES_PER_STEP = 4
DIL_TILES_PER_STEP = 4
DIL_RADIUS = 64
DIL_WIN = Q_TILE + 2 * DIL_RADIUS
LSE_LANES = LANES
VMEM_LIMIT = 56 * 1024 * 1024

_NT_DIMS = (((1,), (1,)), ((), ()))


def _rmsnorm(x, g):
    ms = jnp.mean(x * x, axis=-1, keepdims=True)
    return x * lax.rsqrt(ms + RMS_EPS) * g


def _norm_proj_kernel(x_ref, g_ref, w_ref, o_ref, h_ref):
    @pl.when(pl.program_id(1) == 0)
    def _():
        h_ref[...] = _rmsnorm(x_ref[...], g_ref[...]).astype(h_ref.dtype)

    o_ref[...] = jnp.dot(h_ref[...], w_ref[...],
                         preferred_element_type=jnp.float32).astype(o_ref.dtype)


def _norm_proj(x, g, w, *, tm=1024, tn=1024):
    t, d = x.shape
    n = w.shape[1]
    return pl.pallas_call(
        _norm_proj_kernel,
        grid=(t // tm, n // tn),
        in_specs=[
            pl.BlockSpec((tm, d), lambda i, j: (i, 0)),
            pl.BlockSpec((1, d), lambda i, j: (0, 0)),
            pl.BlockSpec((d, tn), lambda i, j: (0, j)),
        ],
        out_specs=pl.BlockSpec((tm, tn), lambda i, j: (i, j)),
        out_shape=jax.ShapeDtypeStruct((t, n), jnp.bfloat16),
        scratch_shapes=[pltpu.VMEM((tm, d), jnp.bfloat16)],
        compiler_params=pltpu.CompilerParams(
            dimension_semantics=("parallel", "arbitrary"),
            vmem_limit_bytes=VMEM_LIMIT),
        name="norm_proj",
    )(x, g.reshape(1, d), w)


def _proj_kernel(h_ref, w_ref, o_ref):
    o_ref[...] = jnp.dot(h_ref[...], w_ref[...],
                         preferred_element_type=jnp.float32).astype(o_ref.dtype)


def _proj(h, w, col_blocks, *, tm=2048):
    t, d = h.shape
    tn = d
    n = len(col_blocks) * tn

    def w_block(j):
        blk = col_blocks[-1]
        for k in range(len(col_blocks) - 2, -1, -1):
            blk = jnp.where(j == k, col_blocks[k], blk)
        return blk

    return pl.pallas_call(
        _proj_kernel,
        grid=(t // tm, n // tn),
        in_specs=[
            pl.BlockSpec((tm, d), lambda i, j: (i, 0)),
            pl.BlockSpec((d, tn), lambda i, j: (0, w_block(j))),
        ],
        out_specs=pl.BlockSpec((tm, tn), lambda i, j: (i, j)),
        out_shape=jax.ShapeDtypeStruct((t, n), jnp.bfloat16),
        compiler_params=pltpu.CompilerParams(
            dimension_semantics=("parallel", "parallel"),
            vmem_limit_bytes=VMEM_LIMIT),
        name="proj",
    )(h, w)


def _na_kernel(q_ref, k_ref, v_ref, b_ref, o_ref, s0, s1, p0, p1, l0, l1, stage_ref):
    s_refs, p_refs, l_refs = (s0, s1), (p0, p1), (l0, l1)
    step = pl.program_id(2)
    n_tiles = k_ref.shape[1] // GRID_W
    rows = HEADS_PER_STEP * NA_Q_TILE
    early = (NA_TILES_PER_STEP - 2) * NA_Q_TILE

    @pl.when((pl.program_id(0) == 0) & (pl.program_id(1) == 0) & (step == 0))
    def _():
        for ref in (s0, s1, p0, p1, stage_ref):
            ref[...] = jnp.zeros(ref.shape, ref.dtype)
        for ref in l_refs:
            ref[...] = jnp.ones(ref.shape, ref.dtype)

    o_ref[0, :early, :] = stage_ref[...]
    lane = lax.broadcasted_iota(jnp.int32, (NA_Q_TILE, SLAB), 1) // HEAD_DIM

    def window(ref, r):
        first_row = jnp.clip(r - NA_ROWS // 2, 0, n_tiles - NA_ROWS)
        return ref[0, pl.ds(pl.multiple_of(first_row * GRID_W, GRID_W), NA_WIN), :]

    for u in range(NA_TILES_PER_STEP):
        cur, prev = u % 2, 1 - u % 2
        t = step * NA_TILES_PER_STEP + u

        t_s = jnp.minimum(t, n_tiles - 1)
        q4 = q_ref[0, u * NA_Q_TILE:(u + 1) * NA_Q_TILE, :]
        zero = jnp.zeros_like(q4)
        q_stack = jnp.concatenate(
            [jnp.where(lane == h, q4, zero) for h in range(HEADS_PER_STEP)], axis=0)
        s = lax.dot_general(q_stack, window(k_ref, t_s), _NT_DIMS,
                            preferred_element_type=jnp.float32)
        first_off = jnp.clip(t_s - NA_ROWS // 2, 0, n_tiles - NA_ROWS) - t_s + NA_ROWS - 1
        bias = jnp.concatenate(
            [jnp.concatenate([b_ref[h, first_off + 2 * j] for j in range(NA_ROWS // 2)], axis=1)
             for h in range(HEADS_PER_STEP)], axis=0)
        s_refs[cur][...] = s + bias

        s_prev = s_refs[prev][...]
        m = jnp.max(s_prev, axis=-1, keepdims=True)
        e = jnp.exp(s_prev - m)
        p_refs[prev][...] = e.astype(jnp.bfloat16)
        l_refs[prev][...] = jnp.broadcast_to(jnp.sum(e, axis=-1, keepdims=True), (rows, LANES))

        t_o = jnp.clip(t - 2, 0, n_tiles - 1)
        inv_l = 1.0 / l_refs[cur][...]
        o_all = jnp.dot(p_refs[cur][...], window(v_ref, t_o),
                        preferred_element_type=jnp.float32)
        o_all = o_all * jnp.concatenate([inv_l] * (SLAB // LANES), axis=1)
        out = o_all[:NA_Q_TILE]
        for h in range(1, HEADS_PER_STEP):
            out = jnp.where(lane == h, o_all[h * NA_Q_TILE:(h + 1) * NA_Q_TILE], out)
        out = out.astype(o_ref.dtype)
        if u < 2:
            o_ref[0, early + u * NA_Q_TILE:early + (u + 1) * NA_Q_TILE, :] = out
        else:
            stage_ref[(u - 2) * NA_Q_TILE:(u - 1) * NA_Q_TILE, :] = out


def _na_bias_table(rpb):
    n_col_off = 2 * NA_COLS - 1
    padded = jnp.pad(rpb.astype(jnp.float32), ((0, 0), (0, 0), (GRID_W, GRID_W)))
    col_part = jnp.stack(
        [padded[:, :, GRID_W + NA_COLS - 1 - c: 2 * GRID_W + NA_COLS - 1 - c] for c in range(GRID_W)],
        axis=2)
    c = np.arange(GRID_W)[:, None]
    kc = np.arange(GRID_W)[None, :]
    cs = np.clip(c - NA_COLS // 2, 0, GRID_W - NA_COLS)
    col_ok = (kc >= cs) & (kc < cs + NA_COLS)
    assert np.all(((kc - c + NA_COLS - 1 >= 0) & (kc - c + NA_COLS - 1 < n_col_off))[col_ok])
    col_part = jnp.where(jnp.asarray(col_ok), col_part, NEG_INF)
    return jnp.concatenate([col_part[:, :-1], col_part[:, 1:]], axis=-1)


def _na_attention(proj, bias_table):
    b, s, _ = proj.shape
    tile = NA_Q_TILE * NA_TILES_PER_STEP
    assert NA_TILES_PER_STEP % 2 == 0 and s % tile == 0 and NA_Q_TILE == GRID_W
    n_blocks = s // tile
    rows = HEADS_PER_STEP * NA_Q_TILE
    scratch = ([pltpu.VMEM((rows, NA_WIN), jnp.float32)] * 2
               + [pltpu.VMEM((rows, NA_WIN), jnp.bfloat16)] * 2
               + [pltpu.VMEM((rows, LANES), jnp.float32)] * 2
               + [pltpu.VMEM(((NA_TILES_PER_STEP - 2) * NA_Q_TILE, SLAB), jnp.bfloat16)])
    return pl.pallas_call(
        _na_kernel,
        grid=(b, N_SLABS, n_blocks + 1),
        in_specs=[
            pl.BlockSpec((1, tile, SLAB), lambda bi, g, j: (bi, jnp.minimum(j, n_blocks - 1), g)),
            pl.BlockSpec((1, s, SLAB), lambda bi, g, j: (bi, 0, N_SLABS + g)),
            pl.BlockSpec((1, s, SLAB), lambda bi, g, j: (bi, 0, 2 * N_SLABS + g)),
            pl.BlockSpec((HEADS_PER_STEP,) + bias_table.shape[1:], lambda bi, g, j: (g, 0, 0, 0)),
        ],
        out_specs=pl.BlockSpec((1, tile, SLAB), lambda bi, g, j: (bi, jnp.maximum(j - 1, 0), g)),
        out_shape=jax.ShapeDtypeStruct((b, s, D_MODEL), jnp.bfloat16),
        scratch_shapes=scratch,
        compiler_params=pltpu.CompilerParams(
            dimension_semantics=("arbitrary", "arbitrary", "arbitrary"),
            vmem_limit_bytes=VMEM_LIMIT),
        name="na_attention",
    )(proj, proj, proj, bias_table)


def _alibi_slopes():
    return np.asarray(2.0 ** (-8.0 * (np.arange(N_HEADS) + 1) / N_HEADS), dtype=np.float32)


def _slab_attention(q4, k4, v4, bias):
    lane = lax.broadcasted_iota(jnp.int32, q4.shape, 1) // HEAD_DIM
    zero = jnp.zeros_like(q4)
    q_stack = jnp.concatenate(
        [jnp.where(lane == h, q4, zero) for h in range(HEADS_PER_STEP)], axis=0)
    s = lax.dot_general(q_stack, k4, _NT_DIMS, preferred_element_type=jnp.float32)
    s = s + bias
    m = jnp.max(s, axis=-1, keepdims=True)
    p = jnp.exp(s - m)
    l = jnp.sum(p, axis=-1, keepdims=True)
    p = p.astype(jnp.bfloat16)
    inv_l = 1.0 / l
    out = None
    for h in range(HEADS_PER_STEP):
        rows = slice(h * Q_TILE, (h + 1) * Q_TILE)
        o_h = jnp.dot(p[rows], v4, preferred_element_type=jnp.float32) * inv_l[rows]
        out = o_h if out is None else jnp.where(lane == h, o_h, out)
    return out, m, l


def _dil_kernel(q_ref, ka_ref, kb_ref, kc_ref, va_ref, vb_ref, vc_ref, b_ref,
                o_ref, lse_ref):
    step = pl.program_id(2)
    n_steps = pl.num_programs(2)
    k_win = jnp.concatenate([ka_ref[0, 0], kb_ref[0, 0], kc_ref[0, 0]], axis=0)
    v_win = jnp.concatenate([va_ref[0, 0], vb_ref[0, 0], vc_ref[0, 0]], axis=0)
    lane = lax.broadcasted_iota(jnp.int32, (Q_TILE, LSE_LANES), 1)
    for u in range(DIL_TILES_PER_STEP):
        pattern = 0
        if u == 0:
            pattern = jnp.where(step == 0, 1, pattern)
        if u == DIL_TILES_PER_STEP - 1:
            pattern = jnp.where(step == n_steps - 1, 2, pattern)
        tile = slice(u * Q_TILE, (u + 1) * Q_TILE)
        win = slice(u * Q_TILE, u * Q_TILE + DIL_WIN)
        lse_tile = jnp.zeros((Q_TILE, LSE_LANES), jnp.float32)
        for hg in range(N_SLABS):
            cols = slice(hg * SLAB, (hg + 1) * SLAB)
            bias = b_ref[pattern, hg * HEADS_PER_STEP:(hg + 1) * HEADS_PER_STEP]
            bias = bias.reshape(HEADS_PER_STEP * Q_TILE, DIL_WIN)
            out, m, l = _slab_attention(q_ref[0, 0, tile, cols], k_win[win, cols],
                                        v_win[win, cols], bias)
            o_ref[0, 0, tile, cols] = out.astype(o_ref.dtype)
            lse = m + jnp.log(l)
            for h in range(HEADS_PER_STEP):
                lse_h = lse[h * Q_TILE:(h + 1) * Q_TILE]
                lse_tile = jnp.where(lane == hg * HEADS_PER_STEP + h, lse_h, lse_tile)
        lse_ref[0, 0, tile, :] = lse_tile


def _dil_bias_table(dil):
    qi = np.arange(Q_TILE)[:, None]
    kj = np.arange(DIL_WIN)[None, :]
    delta = kj - DIL_RADIUS - qi
    in_band = np.abs(delta) <= DIL_RADIUS
    valid = np.stack([in_band,
                      in_band & (kj >= DIL_RADIUS),
                      in_band & (kj < DIL_WIN - DIL_RADIUS)])
    dist = jnp.asarray(np.abs(delta) * dil, dtype=jnp.float32)
    bias = -jnp.asarray(_alibi_slopes())[:, None, None] * dist[None]
    return jnp.where(jnp.asarray(valid)[:, None], bias[None], NEG_INF)


def _dil_attention(qkv, dil):
    b, _, l, _ = qkv.shape
    tile = Q_TILE * DIL_TILES_PER_STEP
    n_steps = l // tile
    assert l % tile == 0 and DIL_TILES_PER_STEP >= 2
    half = DIL_RADIUS
    per_tile = tile // half
    n_half = l // half

    def spec_mid(which):
        return pl.BlockSpec((1, 1, tile, D_MODEL), lambda bi, r, i: (bi, r, i, which))

    def spec_lo(which):
        return pl.BlockSpec((1, 1, half, D_MODEL),
                            lambda bi, r, i: (bi, r, jnp.maximum(per_tile * i - 1, 0), which))

    def spec_hi(which):
        return pl.BlockSpec((1, 1, half, D_MODEL),
                            lambda bi, r, i: (bi, r, jnp.minimum(per_tile * (i + 1), n_half - 1), which))

    bias = _dil_bias_table(dil)
    return pl.pallas_call(
        _dil_kernel,
        grid=(b, dil, n_steps),
        in_specs=[
            spec_mid(0),
            spec_lo(1), spec_mid(1), spec_hi(1),
            spec_lo(2), spec_mid(2), spec_hi(2),
            pl.BlockSpec(bias.shape, lambda bi, r, i: (0, 0, 0, 0)),
        ],
        out_specs=[
            pl.BlockSpec((1, 1, tile, D_MODEL), lambda bi, r, i: (bi, r, i, 0)),
            pl.BlockSpec((1, 1, tile, LSE_LANES), lambda bi, r, i: (bi, r, i, 0)),
        ],
        out_shape=[
            jax.ShapeDtypeStruct((b, dil, l, D_MODEL), jnp.bfloat16),
            jax.ShapeDtypeStruct((b, dil, l, LSE_LANES), jnp.float32),
        ],
        compiler_params=pltpu.CompilerParams(
            dimension_semantics=("parallel", "parallel", "arbitrary"),
            vmem_limit_bytes=VMEM_LIMIT),
        name=f"dilated_attention_{dil}",
    )(qkv, qkv, qkv, qkv, qkv, qkv, qkv, bias)


def _silu(x):
    return x * jax.nn.sigmoid(x)


def _out0_kernel(o_ref, gate_ref, x_ref, w_ref, g_ref, y_ref, *rest):
    h_refs, slab_ref = rest[:-1], rest[-1]
    o = o_ref[0].astype(jnp.float32)
    gate = gate_ref[0].astype(jnp.float32)
    z = (o * _silu(gate)).astype(jnp.bfloat16)
    y = x_ref[0] + jnp.dot(z, w_ref[...], preferred_element_type=jnp.float32)
    y_ref[0] = y
    hn = _rmsnorm(y, g_ref[...])
    tm = hn.shape[0]
    staged = False
    for h_ref, (_, dil) in zip(h_refs, DIL_PAIRS):
        if dil == 1:
            h_ref[0, 0] = hn.astype(h_ref.dtype)
            continue
        if not staged:
            for s in range(N_LANE_SLABS):
                slab_ref[s] = hn[:, s * LANES:(s + 1) * LANES]
            staged = True
        n = tm // dil
        for r in range(dil):
            for s in range(N_LANE_SLABS):
                rows = slab_ref[s, pl.ds(r, n, stride=dil), :]
                h_ref[0, r, :, s * LANES:(s + 1) * LANES] = rows.astype(h_ref.dtype)


def _out_proj0(o, proj, x, w_out, g_next, *, tm=512):
    b, s, d = x.shape
    gate_block = proj.shape[2] // d - 1
    tok = pl.BlockSpec((1, tm, d), lambda bi, i: (bi, i, 0))
    dils = [dil for _, dil in DIL_PAIRS]
    return pl.pallas_call(
        _out0_kernel,
        grid=(b, s // tm),
        in_specs=[
            tok,
            pl.BlockSpec((1, tm, d), lambda bi, i: (bi, i, gate_block)),
            tok,
            pl.BlockSpec((d, d), lambda bi, i: (0, 0)),
            pl.BlockSpec((1, d), lambda bi, i: (0, 0)),
        ],
        out_specs=[tok] + [pl.BlockSpec((1, dil, tm // dil, d), lambda bi, i: (bi, 0, i, 0))
                           for dil in dils],
        out_shape=([jax.ShapeDtypeStruct((b, s, d), jnp.float32)]
                   + [jax.ShapeDtypeStruct((b, dil, s // dil, d), jnp.bfloat16) for dil in dils]),
        scratch_shapes=[pltpu.VMEM((N_LANE_SLABS, tm, LANES), jnp.float32)],
        compiler_params=pltpu.CompilerParams(
            dimension_semantics=("parallel", "parallel"), vmem_limit_bytes=VMEM_LIMIT),
        name="out_proj0",
    )(o, proj, x, w_out, g_next.reshape(1, d))


def _natural_order(ref, slab_ref, dil):
    _, n, c = ref.shape
    if dil == 1:
        return ref[0].astype(jnp.float32)
    pieces = []
    for s in range(c // LANES):
        for r in range(dil):
            slab_ref[s, pl.ds(r, n, stride=dil), :] = (
                ref[r, :, s * LANES:(s + 1) * LANES].astype(jnp.float32))
        pieces.append(slab_ref[s])
    return pieces[0] if len(pieces) == 1 else jnp.concatenate(pieces, axis=1)


def _out1_kernel(o0_ref, o1_ref, o2_ref, l0_ref, l1_ref, l2_ref, gate_ref, x_ref,
                 w_ref, e_ref, g_ref, y_ref, slab_ref):
    dils = [dil for _, dil in DIL_PAIRS]
    lses = [_natural_order(ref.at[0], slab_ref, dil)
            for ref, dil in zip((l0_ref, l1_ref, l2_ref), dils)]
    m = jnp.maximum(jnp.maximum(lses[0], lses[1]), lses[2])
    es = [jnp.exp(v - m) for v in lses]
    denom = es[0] + es[1] + es[2]
    o = None
    for e, o_ref, dil in zip(es, (o0_ref, o1_ref, o2_ref), dils):
        w = e / denom
        hi = w.astype(jnp.bfloat16)
        lo = (w - hi.astype(jnp.float32)).astype(jnp.bfloat16)
        w_full = jnp.dot(jnp.concatenate([hi, lo], axis=1), e_ref[...],
                         preferred_element_type=jnp.float32)
        term = w_full * _natural_order(o_ref.at[0], slab_ref, dil)
        o = term if o is None else o + term
    gate = gate_ref[0].astype(jnp.float32)
    z = (o * _silu(gate)).astype(jnp.bfloat16)
    x = x_ref[0] + jnp.dot(z, w_ref[...], preferred_element_type=jnp.float32)
    y_ref[0] = _rmsnorm(x, g_ref[...])


def _head_expansion():
    e = np.zeros((LSE_LANES, D_MODEL), np.float32)
    for h in range(N_HEADS):
        e[h, h * HEAD_DIM:(h + 1) * HEAD_DIM] = 1.0
    return jnp.asarray(np.concatenate([e, e], axis=0), dtype=jnp.bfloat16)


def _out_proj1(os_, lses, proj, x, w_out, norm_f, *, tm=512):
    b, s, d = x.shape
    gate_block = proj.shape[2] // d - 1
    dils = [dil for _, dil in DIL_PAIRS]

    def grouped(width, dil):
        return pl.BlockSpec((1, dil, tm // dil, width), lambda bi, i: (bi, 0, i, 0))

    tok = pl.BlockSpec((1, tm, d), lambda bi, i: (bi, i, 0))
    return pl.pallas_call(
        _out1_kernel,
        grid=(b, s // tm),
        in_specs=(
            [grouped(d, dil) for dil in dils]
            + [grouped(LSE_LANES, dil) for dil in dils]
            + [pl.BlockSpec((1, tm, d), lambda bi, i: (bi, i, gate_block)),
               tok,
               pl.BlockSpec((d, d), lambda bi, i: (0, 0)),
               pl.BlockSpec((2 * LSE_LANES, d), lambda bi, i: (0, 0)),
               pl.BlockSpec((1, d), lambda bi, i: (0, 0))]),
        out_specs=tok,
        out_shape=jax.ShapeDtypeStruct((b, s, d), jnp.float32),
        scratch_shapes=[pltpu.VMEM((N_LANE_SLABS, tm, LANES), jnp.float32)],
        compiler_params=pltpu.CompilerParams(
            dimension_semantics=("parallel", "parallel"), vmem_limit_bytes=VMEM_LIMIT),
        name="out_proj1",
    )(*os_, *lses, proj, x, w_out, _head_expansion(), norm_f.reshape(1, d))


def _prep_w_in(w_in, q_blocks):
    scale = np.ones((w_in.shape[1],), np.float32)
    for blk in q_blocks:
        scale[blk * D_MODEL:(blk + 1) * D_MODEL] = 1.0 / math.sqrt(HEAD_DIM)
    return (w_in * jnp.asarray(scale)[None, :]).astype(jnp.bfloat16)


def kernel(x, norm_0, w_in_0, rpb_0, w_out_0, norm_1, w_in_1, w_out_1, norm_f):
    b, s, d = x.shape
    t = b * s

    proj0 = _norm_proj(x.reshape(t, d), norm_0, _prep_w_in(w_in_0, (0,))).reshape(b, s, -1)
    o0 = _na_attention(proj0, _na_bias_table(rpb_0))
    x1, *h1 = _out_proj0(o0, proj0, x, w_out_0.astype(jnp.bfloat16), norm_1)

    w1 = _prep_w_in(w_in_1, tuple(3 * g for g in range(N_DIL_GROUPS)))
    gate_block = 3 * N_DIL_GROUPS
    outs, lses, proj_gate = [], [], None
    for g, (window, dil) in enumerate(DIL_PAIRS):
        assert window // (2 * dil) == DIL_RADIUS
        blocks = (3 * g, 3 * g + 1, 3 * g + 2) + ((gate_block,) if dil == 1 else ())
        qkv = _proj(h1[g].reshape(t, d), w1, blocks).reshape(b, dil, s // dil, -1)
        if dil == 1:
            proj_gate = qkv.reshape(b, s, -1)
        o_g, lse_g = _dil_attention(qkv, dil)
        outs.append(o_g)
        lses.append(lse_g)
    return _out_proj1(outs, lses, proj_gate, x1, w_out_1.astype(jnp.bfloat16), norm_f)
```

```python
import functools
import math

import numpy as np
import jax
import jax.numpy as jnp
from jax import lax
from jax.experimental import pallas as pl
from jax.experimental.pallas import tpu as pltpu

D_MODEL = 1024
HEAD_DIM = 64
N_HEADS = 16
GRID_W = 64
NA_ROWS = 8
NA_COLS = 16
DIL_PAIRS = ((128, 1), (512, 4), (2048, 16))
N_DIL_GROUPS = len(DIL_PAIRS)
RMS_EPS = 1e-6
NEG_INF = -1e30

LANES = 128
N_LANE_SLABS = D_MODEL // LANES
HEADS_PER_STEP = 4
SLAB = HEADS_PER_STEP * HEAD_DIM
N_SLABS = D_MODEL // SLAB
Q_TILE = 128
NA_Q_TILE = GRID_W
NA_WIN = NA_ROWS * GRID_W
NA_TILES_PER_STEP = 4
DIL_TILES_PER_STEP = 4
DIL_RADIUS = 64
DIL_WIN = Q_TILE + 2 * DIL_RADIUS
LSE_LANES = LANES
VMEM_LIMIT = 56 * 1024 * 1024

_NT_DIMS = (((1,), (1,)), ((), ()))


def _rmsnorm(x, g):
    ms = jnp.mean(x * x, axis=-1, keepdims=True)
    return x * lax.rsqrt(ms + RMS_EPS) * g


def _q_scaled_bf16(w_ref):
    scale = jnp.where(pl.program_id(1) == 0, 1.0 / math.sqrt(HEAD_DIM), 1.0)
    return (w_ref[...] * scale).astype(jnp.bfloat16)


def _norm_proj_kernel(x_ref, g_ref, w_ref, o_ref, h_ref):
    @pl.when(pl.program_id(1) == 0)
    def _():
        h_ref[...] = _rmsnorm(x_ref[...], g_ref[...]).astype(h_ref.dtype)

    o_ref[...] = jnp.dot(h_ref[...], _q_scaled_bf16(w_ref),
                         preferred_element_type=jnp.float32).astype(o_ref.dtype)


def _norm_proj(x, g, w, *, tm=2048, tn=1024):
    t, d = x.shape
    n = w.shape[1]
    return pl.pallas_call(
        _norm_proj_kernel,
        grid=(t // tm, n // tn),
        in_specs=[
            pl.BlockSpec((tm, d), lambda i, j: (i, 0)),
            pl.BlockSpec((1, d), lambda i, j: (0, 0)),
            pl.BlockSpec((d, tn), lambda i, j: (0, j)),
        ],
        out_specs=pl.BlockSpec((tm, tn), lambda i, j: (i, j)),
        out_shape=jax.ShapeDtypeStruct((t, n), jnp.bfloat16),
        scratch_shapes=[pltpu.VMEM((tm, d), jnp.bfloat16)],
        compiler_params=pltpu.CompilerParams(
            dimension_semantics=("parallel", "arbitrary"),
            vmem_limit_bytes=VMEM_LIMIT),
        name="norm_proj",
    )(x, g.reshape(1, d), w)


def _proj_kernel(h_ref, w_ref, o_ref):
    o_ref[...] = jnp.dot(h_ref[...], _q_scaled_bf16(w_ref),
                         preferred_element_type=jnp.float32).astype(o_ref.dtype)


def _proj(h, w, col_blocks, *, tm=2048):
    t, d = h.shape
    tn = d
    n = len(col_blocks) * tn

    def w_block(j):
        blk = col_blocks[-1]
        for k in range(len(col_blocks) - 2, -1, -1):
            blk = jnp.where(j == k, col_blocks[k], blk)
        return blk

    return pl.pallas_call(
        _proj_kernel,
        grid=(t // tm, n // tn),
        in_specs=[
            pl.BlockSpec((tm, d), lambda i, j: (i, 0)),
            pl.BlockSpec((d, tn), lambda i, j: (0, w_block(j))),
        ],
        out_specs=pl.BlockSpec((tm, tn), lambda i, j: (i, j)),
        out_shape=jax.ShapeDtypeStruct((t, n), jnp.bfloat16),
        compiler_params=pltpu.CompilerParams(
            dimension_semantics=("parallel", "parallel"),
            vmem_limit_bytes=VMEM_LIMIT),
        name="proj",
    )(h, w)


def _na_kernel(q_ref, k_ref, v_ref, b_ref, o_ref, s0, s1, p0, p1, l0, l1, stage_ref):
    s_refs, p_refs, l_refs = (s0, s1), (p0, p1), (l0, l1)
    step = pl.program_id(2)
    n_tiles = k_ref.shape[1] // GRID_W
    rows = HEADS_PER_STEP * NA_Q_TILE
    early = (NA_TILES_PER_STEP - 2) * NA_Q_TILE

    @pl.when((pl.program_id(0) == 0) & (pl.program_id(1) == 0) & (step == 0))
    def _():
        for ref in (s0, s1, p0, p1, stage_ref):
            ref[...] = jnp.zeros(ref.shape, ref.dtype)
        for ref in l_refs:
            ref[...] = jnp.ones(ref.shape, ref.dtype)

    o_ref[0, :early, :] = stage_ref[...]
    lane = lax.broadcasted_iota(jnp.int32, (NA_Q_TILE, SLAB), 1) // HEAD_DIM

    def window(ref, r):
        first_row = jnp.clip(r - NA_ROWS // 2, 0, n_tiles - NA_ROWS)
        return ref[0, pl.ds(pl.multiple_of(first_row * GRID_W, GRID_W), NA_WIN), :]

    for u in range(NA_TILES_PER_STEP):
        cur, prev = u % 2, 1 - u % 2
        t = step * NA_TILES_PER_STEP + u

        t_s = jnp.minimum(t, n_tiles - 1)
        q4 = q_ref[0, u * NA_Q_TILE:(u + 1) * NA_Q_TILE, :]
        zero = jnp.zeros_like(q4)
        q_stack = jnp.concatenate(
            [jnp.where(lane == h, q4, zero) for h in range(HEADS_PER_STEP)], axis=0)
        s = lax.dot_general(q_stack, window(k_ref, t_s), _NT_DIMS,
                            preferred_element_type=jnp.float32)
        first_off = jnp.clip(t_s - NA_ROWS // 2, 0, n_tiles - NA_ROWS) - t_s + NA_ROWS - 1
        bias = jnp.concatenate(
            [jnp.concatenate([b_ref[h, first_off + 2 * j] for j in range(NA_ROWS // 2)], axis=1)
             for h in range(HEADS_PER_STEP)], axis=0)
        s_refs[cur][...] = s + bias

        s_prev = s_refs[prev][...]
        m = jnp.max(s_prev, axis=-1, keepdims=True)
        e = jnp.exp(s_prev - m)
        p_refs[prev][...] = e.astype(jnp.bfloat16)
        l_refs[prev][...] = jnp.broadcast_to(jnp.sum(e, axis=-1, keepdims=True), (rows, LANES))

        t_o = jnp.clip(t - 2, 0, n_tiles - 1)
        inv_l = 1.0 / l_refs[cur][...]
        o_all = jnp.dot(p_refs[cur][...], window(v_ref, t_o),
                        preferred_element_type=jnp.float32)
        o_all = o_all * jnp.concatenate([inv_l] * (SLAB // LANES), axis=1)
        out = o_all[:NA_Q_TILE]
        for h in range(1, HEADS_PER_STEP):
            out = jnp.where(lane == h, o_all[h * NA_Q_TILE:(h + 1) * NA_Q_TILE], out)
        out = out.astype(o_ref.dtype)
        if u < 2:
            o_ref[0, early + u * NA_Q_TILE:early + (u + 1) * NA_Q_TILE, :] = out
        else:
            stage_ref[(u - 2) * NA_Q_TILE:(u - 1) * NA_Q_TILE, :] = out


def _na_bias_table(rpb):
    n_col_off = 2 * NA_COLS - 1
    c = np.arange(GRID_W)[:, None]
    kc = np.arange(GRID_W)[None, :]
    cs = np.clip(c - NA_COLS // 2, 0, GRID_W - NA_COLS)
    col_ok = (kc >= cs) & (kc < cs + NA_COLS)
    col_off = kc - c + NA_COLS - 1
    assert np.all(((col_off >= 0) & (col_off < n_col_off))[col_ok])
    pick = (np.arange(n_col_off)[:, None, None] == col_off[None]) & col_ok[None]
    col_part = jnp.einsum('hrm,mck->hrck', rpb.astype(jnp.float32),
                          jnp.asarray(pick, dtype=jnp.float32), precision=lax.Precision.HIGHEST)
    col_part = jnp.where(jnp.asarray(col_ok), col_part, NEG_INF)
    return jnp.concatenate([col_part[:, :-1], col_part[:, 1:]], axis=-1)


def _na_attention(proj, bias_table):
    b, s, _ = proj.shape
    tile = NA_Q_TILE * NA_TILES_PER_STEP
    assert NA_TILES_PER_STEP % 2 == 0 and s % tile == 0 and NA_Q_TILE == GRID_W
    n_blocks = s // tile
    rows = HEADS_PER_STEP * NA_Q_TILE
    scratch = ([pltpu.VMEM((rows, NA_WIN), jnp.float32)] * 2
               + [pltpu.VMEM((rows, NA_WIN), jnp.bfloat16)] * 2
               + [pltpu.VMEM((rows, LANES), jnp.float32)] * 2
               + [pltpu.VMEM(((NA_TILES_PER_STEP - 2) * NA_Q_TILE, SLAB), jnp.bfloat16)])
    return pl.pallas_call(
        _na_kernel,
        grid=(b, N_SLABS, n_blocks + 1),
        in_specs=[
            pl.BlockSpec((1, tile, SLAB), lambda bi, g, j: (bi, jnp.minimum(j, n_blocks - 1), g)),
            pl.BlockSpec((1, s, SLAB), lambda bi, g, j: (bi, 0, N_SLABS + g)),
            pl.BlockSpec((1, s, SLAB), lambda bi, g, j: (bi, 0, 2 * N_SLABS + g)),
            pl.BlockSpec((HEADS_PER_STEP,) + bias_table.shape[1:], lambda bi, g, j: (g, 0, 0, 0)),
        ],
        out_specs=pl.BlockSpec((1, tile, SLAB), lambda bi, g, j: (bi, jnp.maximum(j - 1, 0), g)),
        out_shape=jax.ShapeDtypeStruct((b, s, D_MODEL), jnp.bfloat16),
        scratch_shapes=scratch,
        compiler_params=pltpu.CompilerParams(
            dimension_semantics=("arbitrary", "arbitrary", "arbitrary"),
            vmem_limit_bytes=VMEM_LIMIT),
        name="na_attention",
    )(proj, proj, proj, bias_table)


def _alibi_slopes():
    return np.asarray(2.0 ** (-8.0 * (np.arange(N_HEADS) + 1) / N_HEADS), dtype=np.float32)


def _slab_attention(q4, k4, v4, bias):
    lane = lax.broadcasted_iota(jnp.int32, q4.shape, 1) // HEAD_DIM
    zero = jnp.zeros_like(q4)
    q_stack = jnp.concatenate(
        [jnp.where(lane == h, q4, zero) for h in range(HEADS_PER_STEP)], axis=0)
    s = lax.dot_general(q_stack, k4, _NT_DIMS, preferred_element_type=jnp.float32)
    s = s + bias
    m = jnp.max(s, axis=-1, keepdims=True)
    p = jnp.exp(s - m)
    l = jnp.sum(p, axis=-1, keepdims=True)
    p = p.astype(jnp.bfloat16)
    inv_l = 1.0 / l
    out = None
    for h in range(HEADS_PER_STEP):
        rows = slice(h * Q_TILE, (h + 1) * Q_TILE)
        o_h = jnp.dot(p[rows], v4, preferred_element_type=jnp.float32) * inv_l[rows]
        out = o_h if out is None else jnp.where(lane == h, o_h, out)
    return out, m, l


def _dil_kernel(q_ref, ka_ref, kb_ref, kc_ref, va_ref, vb_ref, vc_ref, b_ref,
                o_ref, lse_ref):
    step = pl.program_id(2)
    n_steps = pl.num_programs(2)
    k_win = jnp.concatenate([ka_ref[0, 0], kb_ref[0, 0], kc_ref[0, 0]], axis=0)
    v_win = jnp.concatenate([va_ref[0, 0], vb_ref[0, 0], vc_ref[0, 0]], axis=0)
    lane = lax.broadcasted_iota(jnp.int32, (Q_TILE, LSE_LANES), 1)
    for u in range(DIL_TILES_PER_STEP):
        pattern = 0
        if u == 0:
            pattern = jnp.where(step == 0, 1, pattern)
        if u == DIL_TILES_PER_STEP - 1:
            pattern = jnp.where(step == n_steps - 1, 2, pattern)
        tile = slice(u * Q_TILE, (u + 1) * Q_TILE)
        win = slice(u * Q_TILE, u * Q_TILE + DIL_WIN)
        lse_tile = jnp.zeros((Q_TILE, LSE_LANES), jnp.float32)
        for hg in range(N_SLABS):
            cols = slice(hg * SLAB, (hg + 1) * SLAB)
            bias = b_ref[pattern, hg * HEADS_PER_STEP:(hg + 1) * HEADS_PER_STEP]
            bias = bias.reshape(HEADS_PER_STEP * Q_TILE, DIL_WIN)
            out, m, l = _slab_attention(q_ref[0, 0, tile, cols], k_win[win, cols],
                                        v_win[win, cols], bias)
            o_ref[0, 0, tile, cols] = out.astype(o_ref.dtype)
            lse = m + jnp.log(l)
            for h in range(HEADS_PER_STEP):
                lse_h = lse[h * Q_TILE:(h + 1) * Q_TILE]
                lse_tile = jnp.where(lane == hg * HEADS_PER_STEP + h, lse_h, lse_tile)
        lse_ref[0, 0, tile, :] = lse_tile


def _dil_bias_table(dil):
    qi = np.arange(Q_TILE)[:, None]
    kj = np.arange(DIL_WIN)[None, :]
    delta = kj - DIL_RADIUS - qi
    in_band = np.abs(delta) <= DIL_RADIUS
    valid = np.stack([in_band,
                      in_band & (kj >= DIL_RADIUS),
                      in_band & (kj < DIL_WIN - DIL_RADIUS)])
    dist = jnp.asarray(np.abs(delta) * dil, dtype=jnp.float32)
    bias = -jnp.asarray(_alibi_slopes())[:, None, None] * dist[None]
    return jnp.where(jnp.asarray(valid)[:, None], bias[None], NEG_INF)


def _dil_attention(qkv, dil):
    b, _, l, _ = qkv.shape
    tile = Q_TILE * DIL_TILES_PER_STEP
    n_steps = l // tile
    assert l % tile == 0 and DIL_TILES_PER_STEP >= 2
    half = DIL_RADIUS
    per_tile = tile // half
    n_half = l // half

    def spec_mid(which):
        return pl.BlockSpec((1, 1, tile, D_MODEL), lambda bi, r, i: (bi, r, i, which))

    def spec_lo(which):
        return pl.BlockSpec((1, 1, half, D_MODEL),
                            lambda bi, r, i: (bi, r, jnp.maximum(per_tile * i - 1, 0), which))

    def spec_hi(which):
        return pl.BlockSpec((1, 1, half, D_MODEL),
                            lambda bi, r, i: (bi, r, jnp.minimum(per_tile * (i + 1), n_half - 1), which))

    bias = _dil_bias_table(dil)
    return pl.pallas_call(
        _dil_kernel,
        grid=(b, dil, n_steps),
        in_specs=[
            spec_mid(0),
            spec_lo(1), spec_mid(1), spec_hi(1),
            spec_lo(2), spec_mid(2), spec_hi(2),
            pl.BlockSpec(bias.shape, lambda bi, r, i: (0, 0, 0, 0)),
        ],
        out_specs=[
            pl.BlockSpec((1, 1, tile, D_MODEL), lambda bi, r, i: (bi, r, i, 0)),
            pl.BlockSpec((1, 1, tile, LSE_LANES), lambda bi, r, i: (bi, r, i, 0)),
        ],
        out_shape=[
            jax.ShapeDtypeStruct((b, dil, l, D_MODEL), jnp.bfloat16),
            jax.ShapeDtypeStruct((b, dil, l, LSE_LANES), jnp.float32),
        ],
        compiler_params=pltpu.CompilerParams(
            dimension_semantics=("parallel", "parallel", "arbitrary"),
            vmem_limit_bytes=VMEM_LIMIT),
        name=f"dilated_attention_{dil}",
    )(qkv, qkv, qkv, qkv, qkv, qkv, qkv, bias)


def _silu(x):
    return x * jax.nn.sigmoid(x)


def _out0_kernel(o_ref, gate_ref, x_ref, w_ref, g_ref, y_ref, *rest):
    h_refs, slab_ref = rest[:-1], rest[-1]
    o = o_ref[0].astype(jnp.float32)
    gate = gate_ref[0].astype(jnp.float32)
    z = (o * _silu(gate)).astype(jnp.bfloat16)
    y = x_ref[0] + jnp.dot(z, w_ref[...], preferred_element_type=jnp.float32)
    y_ref[0] = y
    hn = _rmsnorm(y, g_ref[...])
    tm = hn.shape[0]
    staged = False
    for h_ref, (_, dil) in zip(h_refs, DIL_PAIRS):
        if dil == 1:
            h_ref[0, 0] = hn.astype(h_ref.dtype)
            continue
        if not staged:
            for s in range(N_LANE_SLABS):
                slab_ref[s] = hn[:, s * LANES:(s + 1) * LANES]
            staged = True
        n = tm // dil
        for r in range(dil):
            for s in range(N_LANE_SLABS):
                rows = slab_ref[s, pl.ds(r, n, stride=dil), :]
                h_ref[0, r, :, s * LANES:(s + 1) * LANES] = rows.astype(h_ref.dtype)


def _out_proj0(o, proj, x, w_out, g_next, *, tm=512):
    b, s, d = x.shape
    gate_block = proj.shape[2] // d - 1
    tok = pl.BlockSpec((1, tm, d), lambda bi, i: (bi, i, 0))
    dils = [dil for _, dil in DIL_PAIRS]
    return pl.pallas_call(
        _out0_kernel,
        grid=(b, s // tm),
        in_specs=[
            tok,
            pl.BlockSpec((1, tm, d), lambda bi, i: (bi, i, gate_block)),
            tok,
            pl.BlockSpec((d, d), lambda bi, i: (0, 0)),
            pl.BlockSpec((1, d), lambda bi, i: (0, 0)),
        ],
        out_specs=[tok] + [pl.BlockSpec((1, dil, tm // dil, d), lambda bi, i: (bi, 0, i, 0))
                           for dil in dils],
        out_shape=([jax.ShapeDtypeStruct((b, s, d), jnp.float32)]
                   + [jax.ShapeDtypeStruct((b, dil, s // dil, d), jnp.bfloat16) for dil in dils]),
        scratch_shapes=[pltpu.VMEM((N_LANE_SLABS, tm, LANES), jnp.float32)],
        compiler_params=pltpu.CompilerParams(
            dimension_semantics=("parallel", "parallel"), vmem_limit_bytes=VMEM_LIMIT),
        name="out_proj0",
    )(o, proj, x, w_out, g_next.reshape(1, d))


def _natural_order(ref, slab_ref, dil):
    _, n, c = ref.shape
    if dil == 1:
        return ref[0].astype(jnp.float32)
    pieces = []
    for s in range(c // LANES):
        for r in range(dil):
            slab_ref[s, pl.ds(r, n, stride=dil), :] = (
                ref[r, :, s * LANES:(s + 1) * LANES].astype(jnp.float32))
        pieces.append(slab_ref[s])
    return pieces[0] if len(pieces) == 1 else jnp.concatenate(pieces, axis=1)


def _out1_kernel(o0_ref, o1_ref, o2_ref, l0_ref, l1_ref, l2_ref, gate_ref, x_ref,
                 w_ref, e_ref, g_ref, y_ref, slab_ref):
    dils = [dil for _, dil in DIL_PAIRS]
    lses = [_natural_order(ref.at[0], slab_ref, dil)
            for ref, dil in zip((l0_ref, l1_ref, l2_ref), dils)]
    m = jnp.maximum(jnp.maximum(lses[0], lses[1]), lses[2])
    es = [jnp.exp(v - m) for v in lses]
    denom = es[0] + es[1] + es[2]
    o = None
    for e, o_ref, dil in zip(es, (o0_ref, o1_ref, o2_ref), dils):
        w = e / denom
        hi = w.astype(jnp.bfloat16)
        lo = (w - hi.astype(jnp.float32)).astype(jnp.bfloat16)
        w_full = jnp.dot(jnp.concatenate([hi, lo], axis=1), e_ref[...],
                         preferred_element_type=jnp.float32)
        term = w_full * _natural_order(o_ref.at[0], slab_ref, dil)
        o = term if o is None else o + term
    gate = gate_ref[0].astype(jnp.float32)
    z = (o * _silu(gate)).astype(jnp.bfloat16)
    x = x_ref[0] + jnp.dot(z, w_ref[...], preferred_element_type=jnp.float32)
    y_ref[0] = _rmsnorm(x, g_ref[...])


def _head_expansion():
    e = np.zeros((LSE_LANES, D_MODEL), np.float32)
    for h in range(N_HEADS):
        e[h, h * HEAD_DIM:(h + 1) * HEAD_DIM] = 1.0
    return jnp.asarray(np.concatenate([e, e], axis=0), dtype=jnp.bfloat16)


def _out_proj1(os_, lses, proj, x, w_out, norm_f, *, tm=512):
    b, s, d = x.shape
    gate_block = proj.shape[2] // d - 1
    dils = [dil for _, dil in DIL_PAIRS]

    def grouped(width, dil):
        return pl.BlockSpec((1, dil, tm // dil, width), lambda bi, i: (bi, 0, i, 0))

    tok = pl.BlockSpec((1, tm, d), lambda bi, i: (bi, i, 0))
    return pl.pallas_call(
        _out1_kernel,
        grid=(b, s // tm),
        in_specs=(
            [grouped(d, dil) for dil in dils]
            + [grouped(LSE_LANES, dil) for dil in dils]
            + [pl.BlockSpec((1, tm, d), lambda bi, i: (bi, i, gate_block)),
               tok,
               pl.BlockSpec((d, d), lambda bi, i: (0, 0)),
               pl.BlockSpec((2 * LSE_LANES, d), lambda bi, i: (0, 0)),
               pl.BlockSpec((1, d), lambda bi, i: (0, 0))]),
        out_specs=tok,
        out_shape=jax.ShapeDtypeStruct((b, s, d), jnp.float32),
        scratch_shapes=[pltpu.VMEM((N_LANE_SLABS, tm, LANES), jnp.float32)],
        compiler_params=pltpu.CompilerParams(
            dimension_semantics=("parallel", "parallel"), vmem_limit_bytes=VMEM_LIMIT),
        name="out_proj1",
    )(*os_, *lses, proj, x, w_out, _head_expansion(), norm_f.reshape(1, d))


def kernel(x, norm_0, w_in_0, rpb_0, w_out_0, norm_1, w_in_1, w_out_1, norm_f):
    b, s, d = x.shape
    t = b * s

    proj0 = _norm_proj(x.reshape(t, d), norm_0, w_in_0).reshape(b, s, -1)
    o0 = _na_attention(proj0, _na_bias_table(rpb_0))
    x1, *h1 = _out_proj0(o0, proj0, x, w_out_0.astype(jnp.bfloat16), norm_1)

    gate_block = 3 * N_DIL_GROUPS
    outs, lses, proj_gate = [], [], None
    for g, (window, dil) in enumerate(DIL_PAIRS):
        assert window // (2 * dil) == DIL_RADIUS
        blocks = (3 * g, 3 * g + 1, 3 * g + 2) + ((gate_block,) if dil == 1 else ())
        qkv = _proj(h1[g].reshape(t, d), w_in_1, blocks).reshape(b, dil, s // dil, -1)
        if dil == 1:
            proj_gate = qkv.reshape(b, s, -1)
        o_g, lse_g = _dil_attention(qkv, dil)
        outs.append(o_g)
        lses.append(lse_g)
    return _out_proj1(outs, lses, proj_gate, x1, w_out_1.astype(jnp.bfloat16), norm_f)
```

```python
import functools
import math

import numpy as np
import jax
import jax.numpy as jnp
from jax import lax
from jax.experimental import pallas as pl
from jax.experimental.pallas import tpu as pltpu

D_MODEL = 1024
HEAD_DIM = 64
N_HEADS = 16
GRID_W = 64
NA_ROWS = 8
NA_COLS = 16
DIL_PAIRS = ((128, 1), (512, 4), (2048, 16))
N_DIL_GROUPS = len(DIL_PAIRS)
RMS_EPS = 1e-6
NEG_INF = -1e30

LANES = 128
N_LANE_SLABS = D_MODEL // LANES
HEADS_PER_STEP = 4
SLAB = HEADS_PER_STEP * HEAD_DIM
N_SLABS = D_MODEL // SLAB
Q_TILE = 128
NA_Q_TILE = GRID_W
NA_WIN = NA_ROWS * GRID_W
NA_TILES_PER_STEP = 8
DIL_TILES_PER_STEP = 4
DIL_RADIUS = 64
DIL_WIN = Q_TILE + 2 * DIL_RADIUS
LSE_LANES = LANES
VMEM_LIMIT = 56 * 1024 * 1024

_NT_DIMS = (((1,), (1,)), ((), ()))


def _rmsnorm(x, g):
    ms = jnp.mean(x * x, axis=-1, keepdims=True)
    return x * lax.rsqrt(ms + RMS_EPS) * g


def _q_scaled_bf16(w_ref):
    scale = jnp.where(pl.program_id(1) == 0, 1.0 / math.sqrt(HEAD_DIM), 1.0)
    return (w_ref[...] * scale).astype(jnp.bfloat16)


def _norm_proj_kernel(x_ref, g_ref, w_ref, o_ref, h_ref):
    @pl.when(pl.program_id(1) == 0)
    def _():
        h_ref[...] = _rmsnorm(x_ref[...], g_ref[...]).astype(h_ref.dtype)

    o_ref[...] = jnp.dot(h_ref[...], _q_scaled_bf16(w_ref),
                         preferred_element_type=jnp.float32).astype(o_ref.dtype)


def _norm_proj(x, g, w, *, tm=2048, tn=1024):
    t, d = x.shape
    n = w.shape[1]
    return pl.pallas_call(
        _norm_proj_kernel,
        grid=(t // tm, n // tn),
        in_specs=[
            pl.BlockSpec((tm, d), lambda i, j: (i, 0)),
            pl.BlockSpec((1, d), lambda i, j: (0, 0)),
            pl.BlockSpec((d, tn), lambda i, j: (0, j)),
        ],
        out_specs=pl.BlockSpec((tm, tn), lambda i, j: (i, j)),
        out_shape=jax.ShapeDtypeStruct((t, n), jnp.bfloat16),
        scratch_shapes=[pltpu.VMEM((tm, d), jnp.bfloat16)],
        compiler_params=pltpu.CompilerParams(
            dimension_semantics=("parallel", "arbitrary"),
            vmem_limit_bytes=VMEM_LIMIT),
        name="norm_proj",
    )(x, g.reshape(1, d), w)


def _proj_kernel(h_ref, w_ref, o_ref):
    o_ref[...] = jnp.dot(h_ref[...], _q_scaled_bf16(w_ref),
                         preferred_element_type=jnp.float32).astype(o_ref.dtype)


def _proj(h, w, col_blocks, *, tm=2048):
    t, d = h.shape
    tn = d
    n = len(col_blocks) * tn

    def w_block(j):
        blk = col_blocks[-1]
        for k in range(len(col_blocks) - 2, -1, -1):
            blk = jnp.where(j == k, col_blocks[k], blk)
        return blk

    return pl.pallas_call(
        _proj_kernel,
        grid=(t // tm, n // tn),
        in_specs=[
            pl.BlockSpec((tm, d), lambda i, j: (i, 0)),
            pl.BlockSpec((d, tn), lambda i, j: (0, w_block(j))),
        ],
        out_specs=pl.BlockSpec((tm, tn), lambda i, j: (i, j)),
        out_shape=jax.ShapeDtypeStruct((t, n), jnp.bfloat16),
        compiler_params=pltpu.CompilerParams(
            dimension_semantics=("parallel", "parallel"),
            vmem_limit_bytes=VMEM_LIMIT),
        name="proj",
    )(h, w)


def _na_kernel(q_ref, k_ref, v_ref, b_ref, o_ref, s0, s1, p0, p1, l0, l1, stage_ref):
    s_refs, p_refs, l_refs = (s0, s1), (p0, p1), (l0, l1)
    step = pl.program_id(2)
    n_tiles = k_ref.shape[1] // GRID_W
    rows = HEADS_PER_STEP * NA_Q_TILE
    early = (NA_TILES_PER_STEP - 2) * NA_Q_TILE

    @pl.when((pl.program_id(0) == 0) & (pl.program_id(1) == 0) & (step == 0))
    def _():
        for ref in (s0, s1, p0, p1, stage_ref):
            ref[...] = jnp.zeros(ref.shape, ref.dtype)
        for ref in l_refs:
            ref[...] = jnp.ones(ref.shape, ref.dtype)

    o_ref[0, :early, :] = stage_ref[...]
    lane = lax.broadcasted_iota(jnp.int32, (NA_Q_TILE, SLAB), 1) // HEAD_DIM

    def window(ref, r):
        first_row = jnp.clip(r - NA_ROWS // 2, 0, n_tiles - NA_ROWS)
        return ref[0, pl.ds(pl.multiple_of(first_row * GRID_W, GRID_W), NA_WIN), :]

    for u in range(NA_TILES_PER_STEP):
        cur, prev = u % 2, 1 - u % 2
        t = step * NA_TILES_PER_STEP + u

        t_s = jnp.minimum(t, n_tiles - 1)
        q4 = q_ref[0, u * NA_Q_TILE:(u + 1) * NA_Q_TILE, :]
        zero = jnp.zeros_like(q4)
        q_stack = jnp.concatenate(
            [jnp.where(lane == h, q4, zero) for h in range(HEADS_PER_STEP)], axis=0)
        s = lax.dot_general(q_stack, window(k_ref, t_s), _NT_DIMS,
                            preferred_element_type=jnp.float32)
        first_off = jnp.clip(t_s - NA_ROWS // 2, 0, n_tiles - NA_ROWS) - t_s + NA_ROWS - 1
        bias = jnp.concatenate(
            [jnp.concatenate([b_ref[h, first_off + 2 * j] for j in range(NA_ROWS // 2)], axis=1)
             for h in range(HEADS_PER_STEP)], axis=0)
        s_refs[cur][...] = s + bias

        s_prev = s_refs[prev][...]
        m = jnp.max(s_prev, axis=-1, keepdims=True)
        e = jnp.exp(s_prev - m)
        p_refs[prev][...] = e.astype(jnp.bfloat16)
        l_refs[prev][...] = jnp.broadcast_to(jnp.sum(e, axis=-1, keepdims=True), (rows, LANES))

        t_o = jnp.clip(t - 2, 0, n_tiles - 1)
        inv_l = 1.0 / l_refs[cur][...]
        o_all = jnp.dot(p_refs[cur][...], window(v_ref, t_o),
                        preferred_element_type=jnp.float32)
        o_all = o_all * jnp.concatenate([inv_l] * (SLAB // LANES), axis=1)
        out = o_all[:NA_Q_TILE]
        for h in range(1, HEADS_PER_STEP):
            out = jnp.where(lane == h, o_all[h * NA_Q_TILE:(h + 1) * NA_Q_TILE], out)
        out = out.astype(o_ref.dtype)
        if u < 2:
            o_ref[0, early + u * NA_Q_TILE:early + (u + 1) * NA_Q_TILE, :] = out
        else:
            stage_ref[(u - 2) * NA_Q_TILE:(u - 1) * NA_Q_TILE, :] = out


def _na_bias_table(rpb):
    n_col_off = 2 * NA_COLS - 1
    c = np.arange(GRID_W)[:, None]
    kc = np.arange(GRID_W)[None, :]
    cs = np.clip(c - NA_COLS // 2, 0, GRID_W - NA_COLS)
    col_ok = (kc >= cs) & (kc < cs + NA_COLS)
    col_off = kc - c + NA_COLS - 1
    assert np.all(((col_off >= 0) & (col_off < n_col_off))[col_ok])
    pick = (np.arange(n_col_off)[:, None, None] == col_off[None]) & col_ok[None]
    col_part = jnp.einsum('hrm,mck->hrck', rpb.astype(jnp.float32),
                          jnp.asarray(pick, dtype=jnp.float32), precision=lax.Precision.HIGHEST)
    col_part = jnp.where(jnp.asarray(col_ok), col_part, NEG_INF)
    return jnp.concatenate([col_part[:, :-1], col_part[:, 1:]], axis=-1)


def _na_attention(proj, bias_table):
    b, s, _ = proj.shape
    tile = NA_Q_TILE * NA_TILES_PER_STEP
    assert NA_TILES_PER_STEP % 2 == 0 and s % tile == 0 and NA_Q_TILE == GRID_W
    n_blocks = s // tile
    rows = HEADS_PER_STEP * NA_Q_TILE
    scratch = ([pltpu.VMEM((rows, NA_WIN), jnp.float32)] * 2
               + [pltpu.VMEM((rows, NA_WIN), jnp.bfloat16)] * 2
               + [pltpu.VMEM((rows, LANES), jnp.float32)] * 2
               + [pltpu.VMEM(((NA_TILES_PER_STEP - 2) * NA_Q_TILE, SLAB), jnp.bfloat16)])
    return pl.pallas_call(
        _na_kernel,
        grid=(b, N_SLABS, n_blocks + 1),
        in_specs=[
            pl.BlockSpec((1, tile, SLAB), lambda bi, g, j: (bi, jnp.minimum(j, n_blocks - 1), g)),
            pl.BlockSpec((1, s, SLAB), lambda bi, g, j: (bi, 0, N_SLABS + g)),
            pl.BlockSpec((1, s, SLAB), lambda bi, g, j: (bi, 0, 2 * N_SLABS + g)),
            pl.BlockSpec((HEADS_PER_STEP,) + bias_table.shape[1:], lambda bi, g, j: (g, 0, 0, 0)),
        ],
        out_specs=pl.BlockSpec((1, tile, SLAB), lambda bi, g, j: (bi, jnp.maximum(j - 1, 0), g)),
        out_shape=jax.ShapeDtypeStruct((b, s, D_MODEL), jnp.bfloat16),
        scratch_shapes=scratch,
        compiler_params=pltpu.CompilerParams(
            dimension_semantics=("arbitrary", "arbitrary", "arbitrary"),
            vmem_limit_bytes=VMEM_LIMIT),
        name="na_attention",
    )(proj, proj, proj, bias_table)


def _alibi_slopes():
    return np.asarray(2.0 ** (-8.0 * (np.arange(N_HEADS) + 1) / N_HEADS), dtype=np.float32)


def _slab_attention(q4, k4, v4, bias):
    lane = lax.broadcasted_iota(jnp.int32, q4.shape, 1) // HEAD_DIM
    zero = jnp.zeros_like(q4)
    q_stack = jnp.concatenate(
        [jnp.where(lane == h, q4, zero) for h in range(HEADS_PER_STEP)], axis=0)
    s = lax.dot_general(q_stack, k4, _NT_DIMS, preferred_element_type=jnp.float32)
    s = s + bias
    m = jnp.max(s, axis=-1, keepdims=True)
    p = jnp.exp(s - m)
    l = jnp.sum(p, axis=-1, keepdims=True)
    p = p.astype(jnp.bfloat16)
    inv_l = 1.0 / l
    out = None
    for h in range(HEADS_PER_STEP):
        rows = slice(h * Q_TILE, (h + 1) * Q_TILE)
        o_h = jnp.dot(p[rows], v4, preferred_element_type=jnp.float32) * inv_l[rows]
        out = o_h if out is None else jnp.where(lane == h, o_h, out)
    return out, m, l


def _dil_kernel(q_ref, ka_ref, kb_ref, kc_ref, va_ref, vb_ref, vc_ref, b_ref,
                o_ref, lse_ref):
    step = pl.program_id(2)
    n_steps = pl.num_programs(2)
    k_win = jnp.concatenate([ka_ref[0, 0], kb_ref[0, 0], kc_ref[0, 0]], axis=0)
    v_win = jnp.concatenate([va_ref[0, 0], vb_ref[0, 0], vc_ref[0, 0]], axis=0)
    lane = lax.broadcasted_iota(jnp.int32, (Q_TILE, LSE_LANES), 1)
    for u in range(DIL_TILES_PER_STEP):
        pattern = 0
        if u == 0:
            pattern = jnp.where(step == 0, 1, pattern)
        if u == DIL_TILES_PER_STEP - 1:
            pattern = jnp.where(step == n_steps - 1, 2, pattern)
        tile = slice(u * Q_TILE, (u + 1) * Q_TILE)
        win = slice(u * Q_TILE, u * Q_TILE + DIL_WIN)
        lse_tile = jnp.zeros((Q_TILE, LSE_LANES), jnp.float32)
        for hg in range(N_SLABS):
            cols = slice(hg * SLAB, (hg + 1) * SLAB)
            bias = b_ref[pattern, hg * HEADS_PER_STEP:(hg + 1) * HEADS_PER_STEP]
            bias = bias.reshape(HEADS_PER_STEP * Q_TILE, DIL_WIN)
            out, m, l = _slab_attention(q_ref[0, 0, tile, cols], k_win[win, cols],
                                        v_win[win, cols], bias)
            o_ref[0, 0, tile, cols] = out.astype(o_ref.dtype)
            lse = m + jnp.log(l)
            for h in range(HEADS_PER_STEP):
                lse_h = lse[h * Q_TILE:(h + 1) * Q_TILE]
                lse_tile = jnp.where(lane == hg * HEADS_PER_STEP + h, lse_h, lse_tile)
        lse_ref[0, 0, tile, :] = lse_tile


def _dil_bias_table(dil):
    qi = np.arange(Q_TILE)[:, None]
    kj = np.arange(DIL_WIN)[None, :]
    delta = kj - DIL_RADIUS - qi
    in_band = np.abs(delta) <= DIL_RADIUS
    valid = np.stack([in_band,
                      in_band & (kj >= DIL_RADIUS),
                      in_band & (kj < DIL_WIN - DIL_RADIUS)])
    dist = jnp.asarray(np.abs(delta) * dil, dtype=jnp.float32)
    bias = -jnp.asarray(_alibi_slopes())[:, None, None] * dist[None]
    return jnp.where(jnp.asarray(valid)[:, None], bias[None], NEG_INF)


def _dil_attention(qkv, dil):
    b, _, l, _ = qkv.shape
    tile = Q_TILE * DIL_TILES_PER_STEP
    n_steps = l // tile
    assert l % tile == 0 and DIL_TILES_PER_STEP >= 2
    half = DIL_RADIUS
    per_tile = tile // half
    n_half = l // half

    def spec_mid(which):
        return pl.BlockSpec((1, 1, tile, D_MODEL), lambda bi, r, i: (bi, r, i, which))

    def spec_lo(which):
        return pl.BlockSpec((1, 1, half, D_MODEL),
                            lambda bi, r, i: (bi, r, jnp.maximum(per_tile * i - 1, 0), which))

    def spec_hi(which):
        return pl.BlockSpec((1, 1, half, D_MODEL),
                            lambda bi, r, i: (bi, r, jnp.minimum(per_tile * (i + 1), n_half - 1), which))

    bias = _dil_bias_table(dil)
    return pl.pallas_call(
        _dil_kernel,
        grid=(b, dil, n_steps),
        in_specs=[
            spec_mid(0),
            spec_lo(1), spec_mid(1), spec_hi(1),
            spec_lo(2), spec_mid(2), spec_hi(2),
            pl.BlockSpec(bias.shape, lambda bi, r, i: (0, 0, 0, 0)),
        ],
        out_specs=[
            pl.BlockSpec((1, 1, tile, D_MODEL), lambda bi, r, i: (bi, r, i, 0)),
            pl.BlockSpec((1, 1, tile, LSE_LANES), lambda bi, r, i: (bi, r, i, 0)),
        ],
        out_shape=[
            jax.ShapeDtypeStruct((b, dil, l, D_MODEL), jnp.bfloat16),
            jax.ShapeDtypeStruct((b, dil, l, LSE_LANES), jnp.float32),
        ],
        compiler_params=pltpu.CompilerParams(
            dimension_semantics=("parallel", "parallel", "arbitrary"),
            vmem_limit_bytes=VMEM_LIMIT),
        name=f"dilated_attention_{dil}",
    )(qkv, qkv, qkv, qkv, qkv, qkv, qkv, bias)


def _silu(x):
    half = 0.5 * x
    return half + half * jnp.tanh(half)


def _out0_kernel(o_ref, gate_ref, x_ref, w_ref, g_ref, y_ref, *rest):
    h_refs, slab_ref = rest[:-1], rest[-1]
    o = o_ref[0].astype(jnp.float32)
    gate = gate_ref[0].astype(jnp.float32)
    z = (o * _silu(gate)).astype(jnp.bfloat16)
    y = x_ref[0] + jnp.dot(z, w_ref[...], preferred_element_type=jnp.float32)
    y_ref[0] = y
    hn = _rmsnorm(y, g_ref[...])
    tm = hn.shape[0]
    staged = False
    for h_ref, (_, dil) in zip(h_refs, DIL_PAIRS):
        if dil == 1:
            h_ref[0, 0] = hn.astype(h_ref.dtype)
            continue
        if not staged:
            for s in range(N_LANE_SLABS):
                slab_ref[s] = hn[:, s * LANES:(s + 1) * LANES]
            staged = True
        n = tm // dil
        for r in range(dil):
            for s in range(N_LANE_SLABS):
                rows = slab_ref[s, pl.ds(r, n, stride=dil), :]
                h_ref[0, r, :, s * LANES:(s + 1) * LANES] = rows.astype(h_ref.dtype)


def _out_proj0(o, proj, x, w_out, g_next, *, tm=512):
    b, s, d = x.shape
    gate_block = proj.shape[2] // d - 1
    tok = pl.BlockSpec((1, tm, d), lambda bi, i: (bi, i, 0))
    dils = [dil for _, dil in DIL_PAIRS]
    return pl.pallas_call(
        _out0_kernel,
        grid=(b, s // tm),
        in_specs=[
            tok,
            pl.BlockSpec((1, tm, d), lambda bi, i: (bi, i, gate_block)),
            tok,
            pl.BlockSpec((d, d), lambda bi, i: (0, 0)),
            pl.BlockSpec((1, d), lambda bi, i: (0, 0)),
        ],
        out_specs=[tok] + [pl.BlockSpec((1, dil, tm // dil, d), lambda bi, i: (bi, 0, i, 0))
                           for dil in dils],
        out_shape=([jax.ShapeDtypeStruct((b, s, d), jnp.float32)]
                   + [jax.ShapeDtypeStruct((b, dil, s // dil, d), jnp.bfloat16) for dil in dils]),
        scratch_shapes=[pltpu.VMEM((N_LANE_SLABS, tm, LANES), jnp.float32)],
        compiler_params=pltpu.CompilerParams(
            dimension_semantics=("parallel", "parallel"), vmem_limit_bytes=VMEM_LIMIT),
        name="out_proj0",
    )(o, proj, x, w_out, g_next.reshape(1, d))


def _natural_order(ref, slab_ref, dil):
    _, n, c = ref.shape
    if dil == 1:
        return ref[0].astype(jnp.float32)
    pieces = []
    for s in range(c // LANES):
        for r in range(dil):
            slab_ref[s, pl.ds(r, n, stride=dil), :] = (
                ref[r, :, s * LANES:(s + 1) * LANES].astype(jnp.float32))
        pieces.append(slab_ref[s])
    return pieces[0] if len(pieces) == 1 else jnp.concatenate(pieces, axis=1)


def _out1_kernel(o0_ref, o1_ref, o2_ref, l0_ref, l1_ref, l2_ref, gate_ref, x_ref,
                 w_ref, e_ref, g_ref, y_ref, slab_ref):
    dils = [dil for _, dil in DIL_PAIRS]
    lses = [_natural_order(ref.at[0], slab_ref, dil)
            for ref, dil in zip((l0_ref, l1_ref, l2_ref), dils)]
    m = jnp.maximum(jnp.maximum(lses[0], lses[1]), lses[2])
    es = [jnp.exp(v - m) for v in lses]
    denom = es[0] + es[1] + es[2]
    o = None
    for e, o_ref, dil in zip(es, (o0_ref, o1_ref, o2_ref), dils):
        w = e / denom
        hi = w.astype(jnp.bfloat16)
        lo = (w - hi.astype(jnp.float32)).astype(jnp.bfloat16)
        w_full = jnp.dot(jnp.concatenate([hi, lo], axis=1), e_ref[...],
                         preferred_element_type=jnp.float32)
        term = w_full * _natural_order(o_ref.at[0], slab_ref, dil)
        o = term if o is None else o + term
    gate = gate_ref[0].astype(jnp.float32)
    z = (o * _silu(gate)).astype(jnp.bfloat16)
    x = x_ref[0] + jnp.dot(z, w_ref[...], preferred_element_type=jnp.float32)
    y_ref[0] = _rmsnorm(x, g_ref[...])


def _head_expansion():
    e = np.zeros((LSE_LANES, D_MODEL), np.float32)
    for h in range(N_HEADS):
        e[h, h * HEAD_DIM:(h + 1) * HEAD_DIM] = 1.0
    return jnp.asarray(np.concatenate([e, e], axis=0), dtype=jnp.bfloat16)


def _out_proj1(os_, lses, proj, x, w_out, norm_f, *, tm=512):
    b, s, d = x.shape
    gate_block = proj.shape[2] // d - 1
    dils = [dil for _, dil in DIL_PAIRS]

    def grouped(width, dil):
        return pl.BlockSpec((1, dil, tm // dil, width), lambda bi, i: (bi, 0, i, 0))

    tok = pl.BlockSpec((1, tm, d), lambda bi, i: (bi, i, 0))
    return pl.pallas_call(
        _out1_kernel,
        grid=(b, s // tm),
        in_specs=(
            [grouped(d, dil) for dil in dils]
            + [grouped(LSE_LANES, dil) for dil in dils]
            + [pl.BlockSpec((1, tm, d), lambda bi, i: (bi, i, gate_block)),
               tok,
               pl.BlockSpec((d, d), lambda bi, i: (0, 0)),
               pl.BlockSpec((2 * LSE_LANES, d), lambda bi, i: (0, 0)),
               pl.BlockSpec((1, d), lambda bi, i: (0, 0))]),
        out_specs=tok,
        out_shape=jax.ShapeDtypeStruct((b, s, d), jnp.float32),
        scratch_shapes=[pltpu.VMEM((N_LANE_SLABS, tm, LANES), jnp.float32)],
        compiler_params=pltpu.CompilerParams(
            dimension_semantics=("parallel", "parallel"), vmem_limit_bytes=VMEM_LIMIT),
        name="out_proj1",
    )(*os_, *lses, proj, x, w_out, _head_expansion(), norm_f.reshape(1, d))


def kernel(x, norm_0, w_in_0, rpb_0, w_out_0, norm_1, w_in_1, w_out_1, norm_f):
    b, s, d = x.shape
    t = b * s

    proj0 = _norm_proj(x.reshape(t, d), norm_0, w_in_0).reshape(b, s, -1)
    o0 = _na_attention(proj0, _na_bias_table(rpb_0))
    x1, *h1 = _out_proj0(o0, proj0, x, w_out_0.astype(jnp.bfloat16), norm_1)

    gate_block = 3 * N_DIL_GROUPS
    outs, lses, proj_gate = [], [], None
    for g, (window, dil) in enumerate(DIL_PAIRS):
        assert window // (2 * dil) == DIL_RADIUS
        blocks = (3 * g, 3 * g + 1, 3 * g + 2) + ((gate_block,) if dil == 1 else ())
        qkv = _proj(h1[g].reshape(t, d), w_in_1, blocks).reshape(b, dil, s // dil, -1)
        if dil == 1:
            proj_gate = qkv.reshape(b, s, -1)
        o_g, lse_g = _dil_attention(qkv, dil)
        outs.append(o_g)
        lses.append(lse_g)
    return _out_proj1(outs, lses, proj_gate, x1, w_out_1.astype(jnp.bfloat16), norm_f)
```

```python
import functools
import math

import numpy as np
import jax
import jax.numpy as jnp
from jax import lax
from jax.experimental import pallas as pl
from jax.experimental.pallas import tpu as pltpu

D_MODEL = 1024
HEAD_DIM = 64
N_HEADS = 16
GRID_W = 64
NA_ROWS = 8
NA_COLS = 16
DIL_PAIRS = ((128, 1), (512, 4), (2048, 16))
N_DIL_GROUPS = len(DIL_PAIRS)
RMS_EPS = 1e-6
NEG_INF = -1e30

LANES = 128
N_LANE_SLABS = D_MODEL // LANES
HEADS_PER_STEP = 4
SLAB = HEADS_PER_STEP * HEAD_DIM
N_SLABS = D_MODEL // SLAB
Q_TILE = 128
NA_Q_TILE = GRID_W
NA_WIN = NA_ROWS * GRID_W
NA_TILES_PER_STEP = 8
DIL_TILES_PER_STEP = 4
DIL_RADIUS = 64
DIL_WIN = Q_TILE + 2 * DIL_RADIUS
LSE_LANES = LANES
VMEM_LIMIT = 56 * 1024 * 1024

_NT_DIMS = (((1,), (1,)), ((), ()))


def _rmsnorm(x, g):
    ms = jnp.mean(x * x, axis=-1, keepdims=True)
    return x * lax.rsqrt(ms + RMS_EPS) * g


def _q_scaled_bf16(w_ref):
    scale = jnp.where(pl.program_id(1) == 0, 1.0 / math.sqrt(HEAD_DIM), 1.0)
    return (w_ref[...] * scale).astype(jnp.bfloat16)


def _norm_proj_kernel(x_ref, g_ref, w_ref, o_ref, h_ref):
    @pl.when(pl.program_id(1) == 0)
    def _():
        h_ref[...] = _rmsnorm(x_ref[...], g_ref[...]).astype(h_ref.dtype)

    res = jnp.dot(h_ref[...], _q_scaled_bf16(w_ref),
                  preferred_element_type=jnp.float32).astype(o_ref.dtype)
    for k in range(o_ref.shape[0]):
        o_ref[k] = res[:, k * SLAB:(k + 1) * SLAB]


def _norm_proj(x, g, w, *, tm=2048, tn=1024):
    t, d = x.shape
    n = w.shape[1]
    return pl.pallas_call(
        _norm_proj_kernel,
        grid=(t // tm, n // tn),
        in_specs=[
            pl.BlockSpec((tm, d), lambda i, j: (i, 0)),
            pl.BlockSpec((1, d), lambda i, j: (0, 0)),
            pl.BlockSpec((d, tn), lambda i, j: (0, j)),
        ],
        out_specs=pl.BlockSpec((tn // SLAB, tm, SLAB), lambda i, j: (j, i, 0)),
        out_shape=jax.ShapeDtypeStruct((n // SLAB, t, SLAB), jnp.bfloat16),
        scratch_shapes=[pltpu.VMEM((tm, d), jnp.bfloat16)],
        compiler_params=pltpu.CompilerParams(
            dimension_semantics=("parallel", "arbitrary"),
            vmem_limit_bytes=VMEM_LIMIT),
        name="norm_proj",
    )(x, g.reshape(1, d), w)


def _proj_kernel(h_ref, w_ref, o_ref):
    o_ref[...] = jnp.dot(h_ref[...], _q_scaled_bf16(w_ref),
                         preferred_element_type=jnp.float32).astype(o_ref.dtype)


def _proj(h, w, col_blocks, *, tm=2048):
    t, d = h.shape
    tn = d
    n = len(col_blocks) * tn

    def w_block(j):
        blk = col_blocks[-1]
        for k in range(len(col_blocks) - 2, -1, -1):
            blk = jnp.where(j == k, col_blocks[k], blk)
        return blk

    return pl.pallas_call(
        _proj_kernel,
        grid=(t // tm, n // tn),
        in_specs=[
            pl.BlockSpec((tm, d), lambda i, j: (i, 0)),
            pl.BlockSpec((d, tn), lambda i, j: (0, w_block(j))),
        ],
        out_specs=pl.BlockSpec((tm, tn), lambda i, j: (i, j)),
        out_shape=jax.ShapeDtypeStruct((t, n), jnp.bfloat16),
        compiler_params=pltpu.CompilerParams(
            dimension_semantics=("parallel", "parallel"),
            vmem_limit_bytes=VMEM_LIMIT),
        name="proj",
    )(h, w)


def _na_kernel(q_ref, k_ref, v_ref, b_ref, o_ref, s0, s1, p0, p1, l0, l1, stage_ref):
    s_refs, p_refs, l_refs = (s0, s1), (p0, p1), (l0, l1)
    step = pl.program_id(2)
    n_tiles = k_ref.shape[1] // GRID_W
    rows = HEADS_PER_STEP * NA_Q_TILE
    early = (NA_TILES_PER_STEP - 2) * NA_Q_TILE

    @pl.when((pl.program_id(0) == 0) & (pl.program_id(1) == 0) & (step == 0))
    def _():
        for ref in (s0, s1, p0, p1, stage_ref):
            ref[...] = jnp.zeros(ref.shape, ref.dtype)
        for ref in l_refs:
            ref[...] = jnp.ones(ref.shape, ref.dtype)

    o_ref[0, :early, :] = stage_ref[...]
    lane = lax.broadcasted_iota(jnp.int32, (NA_Q_TILE, SLAB), 1) // HEAD_DIM

    def window(ref, r):
        first_row = jnp.clip(r - NA_ROWS // 2, 0, n_tiles - NA_ROWS)
        return ref[0, pl.ds(pl.multiple_of(first_row * GRID_W, GRID_W), NA_WIN), :]

    for u in range(NA_TILES_PER_STEP):
        cur, prev = u % 2, 1 - u % 2
        t = step * NA_TILES_PER_STEP + u

        t_s = jnp.minimum(t, n_tiles - 1)
        q4 = q_ref[0, u * NA_Q_TILE:(u + 1) * NA_Q_TILE, :]
        zero = jnp.zeros_like(q4)
        q_stack = jnp.concatenate(
            [jnp.where(lane == h, q4, zero) for h in range(HEADS_PER_STEP)], axis=0)
        s = lax.dot_general(q_stack, window(k_ref, t_s), _NT_DIMS,
                            preferred_element_type=jnp.float32)
        first_off = jnp.clip(t_s - NA_ROWS // 2, 0, n_tiles - NA_ROWS) - t_s + NA_ROWS - 1
        bias = jnp.concatenate(
            [jnp.concatenate([b_ref[h, first_off + 2 * j] for j in range(NA_ROWS // 2)], axis=1)
             for h in range(HEADS_PER_STEP)], axis=0)
        s_refs[cur][...] = s + bias

        s_prev = s_refs[prev][...]
        m = jnp.max(s_prev, axis=-1, keepdims=True)
        e = jnp.exp(s_prev - m)
        p_refs[prev][...] = e.astype(jnp.bfloat16)
        l_refs[prev][...] = jnp.broadcast_to(jnp.sum(e, axis=-1, keepdims=True), (rows, LANES))

        t_o = jnp.clip(t - 2, 0, n_tiles - 1)
        inv_l = 1.0 / l_refs[cur][...]
        o_all = jnp.dot(p_refs[cur][...], window(v_ref, t_o),
                        preferred_element_type=jnp.float32)
        o_all = o_all * jnp.concatenate([inv_l] * (SLAB // LANES), axis=1)
        out = o_all[:NA_Q_TILE]
        for h in range(1, HEADS_PER_STEP):
            out = jnp.where(lane == h, o_all[h * NA_Q_TILE:(h + 1) * NA_Q_TILE], out)
        out = out.astype(o_ref.dtype)
        if u < 2:
            o_ref[0, early + u * NA_Q_TILE:early + (u + 1) * NA_Q_TILE, :] = out
        else:
            stage_ref[(u - 2) * NA_Q_TILE:(u - 1) * NA_Q_TILE, :] = out


def _na_bias_table(rpb):
    n_col_off = 2 * NA_COLS - 1
    c = np.arange(GRID_W)[:, None]
    kc = np.arange(GRID_W)[None, :]
    cs = np.clip(c - NA_COLS // 2, 0, GRID_W - NA_COLS)
    col_ok = (kc >= cs) & (kc < cs + NA_COLS)
    col_off = kc - c + NA_COLS - 1
    assert np.all(((col_off >= 0) & (col_off < n_col_off))[col_ok])
    pick = (np.arange(n_col_off)[:, None, None] == col_off[None]) & col_ok[None]
    col_part = jnp.einsum('hrm,mck->hrck', rpb.astype(jnp.float32),
                          jnp.asarray(pick, dtype=jnp.float32), precision=lax.Precision.HIGHEST)
    col_part = jnp.where(jnp.asarray(col_ok), col_part, NEG_INF)
    return jnp.concatenate([col_part[:, :-1], col_part[:, 1:]], axis=-1)


def _na_attention(proj, bias_table, b):
    s = proj.shape[1] // b
    tile = NA_Q_TILE * NA_TILES_PER_STEP
    assert NA_TILES_PER_STEP % 2 == 0 and s % tile == 0 and NA_Q_TILE == GRID_W
    n_blocks = s // tile
    rows = HEADS_PER_STEP * NA_Q_TILE
    scratch = ([pltpu.VMEM((rows, NA_WIN), jnp.float32)] * 2
               + [pltpu.VMEM((rows, NA_WIN), jnp.bfloat16)] * 2
               + [pltpu.VMEM((rows, LANES), jnp.float32)] * 2
               + [pltpu.VMEM(((NA_TILES_PER_STEP - 2) * NA_Q_TILE, SLAB), jnp.bfloat16)])
    return pl.pallas_call(
        _na_kernel,
        grid=(b, N_SLABS, n_blocks + 1),
        in_specs=[
            pl.BlockSpec((1, tile, SLAB),
                         lambda bi, g, j: (g, bi * n_blocks + jnp.minimum(j, n_blocks - 1), 0)),
            pl.BlockSpec((1, s, SLAB), lambda bi, g, j: (N_SLABS + g, bi, 0)),
            pl.BlockSpec((1, s, SLAB), lambda bi, g, j: (2 * N_SLABS + g, bi, 0)),
            pl.BlockSpec((HEADS_PER_STEP,) + bias_table.shape[1:], lambda bi, g, j: (g, 0, 0, 0)),
        ],
        out_specs=pl.BlockSpec((1, tile, SLAB),
                               lambda bi, g, j: (g, bi * n_blocks + jnp.maximum(j - 1, 0), 0)),
        out_shape=jax.ShapeDtypeStruct((N_SLABS, b * s, SLAB), jnp.bfloat16),
        scratch_shapes=scratch,
        compiler_params=pltpu.CompilerParams(
            dimension_semantics=("arbitrary", "arbitrary", "arbitrary"),
            vmem_limit_bytes=VMEM_LIMIT),
        name="na_attention",
    )(proj, proj, proj, bias_table)


def _alibi_slopes():
    return np.asarray(2.0 ** (-8.0 * (np.arange(N_HEADS) + 1) / N_HEADS), dtype=np.float32)


def _slab_attention(q4, k4, v4, bias):
    lane = lax.broadcasted_iota(jnp.int32, q4.shape, 1) // HEAD_DIM
    zero = jnp.zeros_like(q4)
    q_stack = jnp.concatenate(
        [jnp.where(lane == h, q4, zero) for h in range(HEADS_PER_STEP)], axis=0)
    s = lax.dot_general(q_stack, k4, _NT_DIMS, preferred_element_type=jnp.float32)
    s = s + bias
    m = jnp.max(s, axis=-1, keepdims=True)
    p = jnp.exp(s - m)
    l = jnp.sum(p, axis=-1, keepdims=True)
    p = p.astype(jnp.bfloat16)
    inv_l = 1.0 / l
    out = None
    for h in range(HEADS_PER_STEP):
        rows = slice(h * Q_TILE, (h + 1) * Q_TILE)
        o_h = jnp.dot(p[rows], v4, preferred_element_type=jnp.float32) * inv_l[rows]
        out = o_h if out is None else jnp.where(lane == h, o_h, out)
    return out, m, l


def _dil_kernel(q_ref, ka_ref, kb_ref, kc_ref, va_ref, vb_ref, vc_ref, b_ref,
                o_ref, lse_ref):
    step = pl.program_id(2)
    n_steps = pl.num_programs(2)
    k_win = jnp.concatenate([ka_ref[0, 0], kb_ref[0, 0], kc_ref[0, 0]], axis=0)
    v_win = jnp.concatenate([va_ref[0, 0], vb_ref[0, 0], vc_ref[0, 0]], axis=0)
    lane = lax.broadcasted_iota(jnp.int32, (Q_TILE, LSE_LANES), 1)
    for u in range(DIL_TILES_PER_STEP):
        pattern = 0
        if u == 0:
            pattern = jnp.where(step == 0, 1, pattern)
        if u == DIL_TILES_PER_STEP - 1:
            pattern = jnp.where(step == n_steps - 1, 2, pattern)
        tile = slice(u * Q_TILE, (u + 1) * Q_TILE)
        win = slice(u * Q_TILE, u * Q_TILE + DIL_WIN)
        lse_tile = jnp.zeros((Q_TILE, LSE_LANES), jnp.float32)
        for hg in range(N_SLABS):
            cols = slice(hg * SLAB, (hg + 1) * SLAB)
            bias = b_ref[pattern, hg * HEADS_PER_STEP:(hg + 1) * HEADS_PER_STEP]
            bias = bias.reshape(HEADS_PER_STEP * Q_TILE, DIL_WIN)
            out, m, l = _slab_attention(q_ref[0, 0, tile, cols], k_win[win, cols],
                                        v_win[win, cols], bias)
            o_ref[0, 0, tile, cols] = out.astype(o_ref.dtype)
            lse = m + jnp.log(l)
            for h in range(HEADS_PER_STEP):
                lse_h = lse[h * Q_TILE:(h + 1) * Q_TILE]
                lse_tile = jnp.where(lane == hg * HEADS_PER_STEP + h, lse_h, lse_tile)
        lse_ref[0, 0, tile, :] = lse_tile


def _dil_bias_table(dil):
    qi = np.arange(Q_TILE)[:, None]
    kj = np.arange(DIL_WIN)[None, :]
    delta = kj - DIL_RADIUS - qi
    in_band = np.abs(delta) <= DIL_RADIUS
    valid = np.stack([in_band,
                      in_band & (kj >= DIL_RADIUS),
                      in_band & (kj < DIL_WIN - DIL_RADIUS)])
    dist = jnp.asarray(np.abs(delta) * dil, dtype=jnp.float32)
    bias = -jnp.asarray(_alibi_slopes())[:, None, None] * dist[None]
    return jnp.where(jnp.asarray(valid)[:, None], bias[None], NEG_INF)


def _dil_attention(qkv, dil):
    b, _, l, _ = qkv.shape
    tile = Q_TILE * DIL_TILES_PER_STEP
    n_steps = l // tile
    assert l % tile == 0 and DIL_TILES_PER_STEP >= 2
    half = DIL_RADIUS
    per_tile = tile // half
    n_half = l // half

    def spec_mid(which):
        return pl.BlockSpec((1, 1, tile, D_MODEL), lambda bi, r, i: (bi, r, i, which))

    def spec_lo(which):
        return pl.BlockSpec((1, 1, half, D_MODEL),
                            lambda bi, r, i: (bi, r, jnp.maximum(per_tile * i - 1, 0), which))

    def spec_hi(which):
        return pl.BlockSpec((1, 1, half, D_MODEL),
                            lambda bi, r, i: (bi, r, jnp.minimum(per_tile * (i + 1), n_half - 1), which))

    bias = _dil_bias_table(dil)
    return pl.pallas_call(
        _dil_kernel,
        grid=(b, dil, n_steps),
        in_specs=[
            spec_mid(0),
            spec_lo(1), spec_mid(1), spec_hi(1),
            spec_lo(2), spec_mid(2), spec_hi(2),
            pl.BlockSpec(bias.shape, lambda bi, r, i: (0, 0, 0, 0)),
        ],
        out_specs=[
            pl.BlockSpec((1, 1, tile, D_MODEL), lambda bi, r, i: (bi, r, i, 0)),
            pl.BlockSpec((1, 1, tile, LSE_LANES), lambda bi, r, i: (bi, r, i, 0)),
        ],
        out_shape=[
            jax.ShapeDtypeStruct((b, dil, l, D_MODEL), jnp.bfloat16),
            jax.ShapeDtypeStruct((b, dil, l, LSE_LANES), jnp.float32),
        ],
        compiler_params=pltpu.CompilerParams(
            dimension_semantics=("parallel", "parallel", "arbitrary"),
            vmem_limit_bytes=VMEM_LIMIT),
        name=f"dilated_attention_{dil}",
    )(qkv, qkv, qkv, qkv, qkv, qkv, qkv, bias)


def _silu(x):
    half = 0.5 * x
    return half + half * jnp.tanh(half)


def _out0_kernel(o_ref, gate_ref, x_ref, w_ref, g_ref, y_ref, *rest):
    h_refs, slab_ref = rest[:-1], rest[-1]
    o = jnp.concatenate([o_ref[k] for k in range(N_SLABS)], axis=1).astype(jnp.float32)
    gate = jnp.concatenate([gate_ref[k] for k in range(N_SLABS)], axis=1).astype(jnp.float32)
    z = (o * _silu(gate)).astype(jnp.bfloat16)
    y = x_ref[0] + jnp.dot(z, w_ref[...], preferred_element_type=jnp.float32)
    y_ref[0] = y
    hn = _rmsnorm(y, g_ref[...])
    tm = hn.shape[0]
    staged = False
    for h_ref, (_, dil) in zip(h_refs, DIL_PAIRS):
        if dil == 1:
            h_ref[0, 0] = hn.astype(h_ref.dtype)
            continue
        if not staged:
            for s in range(N_LANE_SLABS):
                slab_ref[s] = hn[:, s * LANES:(s + 1) * LANES]
            staged = True
        n = tm // dil
        for r in range(dil):
            for s in range(N_LANE_SLABS):
                rows = slab_ref[s, pl.ds(r, n, stride=dil), :]
                h_ref[0, r, :, s * LANES:(s + 1) * LANES] = rows.astype(h_ref.dtype)


def _out_proj0(o, proj, x, w_out, g_next, *, tm=512):
    b, s, d = x.shape
    gate_block = proj.shape[0] // N_SLABS - 1
    tiles = s // tm
    tok = pl.BlockSpec((1, tm, d), lambda bi, i: (bi, i, 0))
    dils = [dil for _, dil in DIL_PAIRS]
    return pl.pallas_call(
        _out0_kernel,
        grid=(b, s // tm),
        in_specs=[
            pl.BlockSpec((N_SLABS, tm, SLAB), lambda bi, i: (0, bi * tiles + i, 0)),
            pl.BlockSpec((N_SLABS, tm, SLAB), lambda bi, i: (gate_block, bi * tiles + i, 0)),
            tok,
            pl.BlockSpec((d, d), lambda bi, i: (0, 0)),
            pl.BlockSpec((1, d), lambda bi, i: (0, 0)),
        ],
        out_specs=[tok] + [pl.BlockSpec((1, dil, tm // dil, d), lambda bi, i: (bi, 0, i, 0))
                           for dil in dils],
        out_shape=([jax.ShapeDtypeStruct((b, s, d), jnp.float32)]
                   + [jax.ShapeDtypeStruct((b, dil, s // dil, d), jnp.bfloat16) for dil in dils]),
        scratch_shapes=[pltpu.VMEM((N_LANE_SLABS, tm, LANES), jnp.float32)],
        compiler_params=pltpu.CompilerParams(
            dimension_semantics=("parallel", "parallel"), vmem_limit_bytes=VMEM_LIMIT),
        name="out_proj0",
    )(o, proj, x, w_out, g_next.reshape(1, d))


def _natural_order(ref, slab_ref, dil):
    _, n, c = ref.shape
    if dil == 1:
        return ref[0].astype(jnp.float32)
    pieces = []
    for s in range(c // LANES):
        for r in range(dil):
            slab_ref[s, pl.ds(r, n, stride=dil), :] = (
                ref[r, :, s * LANES:(s + 1) * LANES].astype(jnp.float32))
        pieces.append(slab_ref[s])
    return pieces[0] if len(pieces) == 1 else jnp.concatenate(pieces, axis=1)


def _out1_kernel(o0_ref, o1_ref, o2_ref, l0_ref, l1_ref, l2_ref, gate_ref, x_ref,
                 w_ref, e_ref, g_ref, y_ref, slab_ref):
    dils = [dil for _, dil in DIL_PAIRS]
    lses = [_natural_order(ref.at[0], slab_ref, dil)
            for ref, dil in zip((l0_ref, l1_ref, l2_ref), dils)]
    m = jnp.maximum(jnp.maximum(lses[0], lses[1]), lses[2])
    es = [jnp.exp(v - m) for v in lses]
    denom = es[0] + es[1] + es[2]
    o = None
    for e, o_ref, dil in zip(es, (o0_ref, o1_ref, o2_ref), dils):
        w = e / denom
        hi = w.astype(jnp.bfloat16)
        lo = (w - hi.astype(jnp.float32)).astype(jnp.bfloat16)
        w_full = jnp.dot(jnp.concatenate([hi, lo], axis=1), e_ref[...],
                         preferred_element_type=jnp.float32)
        term = w_full * _natural_order(o_ref.at[0], slab_ref, dil)
        o = term if o is None else o + term
    gate = gate_ref[0].astype(jnp.float32)
    z = (o * _silu(gate)).astype(jnp.bfloat16)
    x = x_ref[0] + jnp.dot(z, w_ref[...], preferred_element_type=jnp.float32)
    y_ref[0] = _rmsnorm(x, g_ref[...])


def _head_expansion():
    e = np.zeros((LSE_LANES, D_MODEL), np.float32)
    for h in range(N_HEADS):
        e[h, h * HEAD_DIM:(h + 1) * HEAD_DIM] = 1.0
    return jnp.asarray(np.concatenate([e, e], axis=0), dtype=jnp.bfloat16)


def _out_proj1(os_, lses, proj, x, w_out, norm_f, *, tm=512):
    b, s, d = x.shape
    gate_block = proj.shape[2] // d - 1
    dils = [dil for _, dil in DIL_PAIRS]

    def grouped(width, dil):
        return pl.BlockSpec((1, dil, tm // dil, width), lambda bi, i: (bi, 0, i, 0))

    tok = pl.BlockSpec((1, tm, d), lambda bi, i: (bi, i, 0))
    return pl.pallas_call(
        _out1_kernel,
        grid=(b, s // tm),
        in_specs=(
            [grouped(d, dil) for dil in dils]
            + [grouped(LSE_LANES, dil) for dil in dils]
            + [pl.BlockSpec((1, tm, d), lambda bi, i: (bi, i, gate_block)),
               tok,
               pl.BlockSpec((d, d), lambda bi, i: (0, 0)),
               pl.BlockSpec((2 * LSE_LANES, d), lambda bi, i: (0, 0)),
               pl.BlockSpec((1, d), lambda bi, i: (0, 0))]),
        out_specs=tok,
        out_shape=jax.ShapeDtypeStruct((b, s, d), jnp.float32),
        scratch_shapes=[pltpu.VMEM((N_LANE_SLABS, tm, LANES), jnp.float32)],
        compiler_params=pltpu.CompilerParams(
            dimension_semantics=("parallel", "parallel"), vmem_limit_bytes=VMEM_LIMIT),
        name="out_proj1",
    )(*os_, *lses, proj, x, w_out, _head_expansion(), norm_f.reshape(1, d))


def kernel(x, norm_0, w_in_0, rpb_0, w_out_0, norm_1, w_in_1, w_out_1, norm_f):
    b, s, d = x.shape
    t = b * s

    proj0 = _norm_proj(x.reshape(t, d), norm_0, w_in_0)
    o0 = _na_attention(proj0, _na_bias_table(rpb_0), b)
    x1, *h1 = _out_proj0(o0, proj0, x, w_out_0.astype(jnp.bfloat16), norm_1)

    gate_block = 3 * N_DIL_GROUPS
    outs, lses, proj_gate = [], [], None
    for g, (window, dil) in enumerate(DIL_PAIRS):
        assert window // (2 * dil) == DIL_RADIUS
        blocks = (3 * g, 3 * g + 1, 3 * g + 2) + ((gate_block,) if dil == 1 else ())
        qkv = _proj(h1[g].reshape(t, d), w_in_1, blocks).reshape(b, dil, s // dil, -1)
        if dil == 1:
            proj_gate = qkv.reshape(b, s, -1)
        o_g, lse_g = _dil_attention(qkv, dil)
        outs.append(o_g)
        lses.append(lse_g)
    return _out_proj1(outs, lses, proj_gate, x1, w_out_1.astype(jnp.bfloat16), norm_f)
```

```python
import functools
import math

import numpy as np
import jax
import jax.numpy as jnp
from jax import lax
from jax.experimental import pallas as pl
from jax.experimental.pallas import tpu as pltpu

D_MODEL = 1024
HEAD_DIM = 64
N_HEADS = 16
GRID_W = 64
NA_ROWS = 8
NA_COLS = 16
DIL_PAIRS = ((128, 1), (512, 4), (2048, 16))
N_DIL_GROUPS = len(DIL_PAIRS)
RMS_EPS = 1e-6
NEG_INF = -1e30
LOG2E = math.log2(math.e)

LANES = 128
N_LANE_SLABS = D_MODEL // LANES
HEADS_PER_STEP = 4
SLAB = HEADS_PER_STEP * HEAD_DIM
N_SLABS = D_MODEL // SLAB
Q_TILE = 128
NA_Q_TILE = GRID_W
NA_WIN = NA_ROWS * GRID_W
NA_TILES_PER_STEP = 8
DIL_TILES_PER_STEP = 4
DIL_RADIUS = 64
DIL_WIN = Q_TILE + 2 * DIL_RADIUS
LSE_LANES = LANES
VMEM_LIMIT = 56 * 1024 * 1024

_NT_DIMS = (((1,), (1,)), ((), ()))


def _rmsnorm(x, g):
    ms = jnp.mean(x * x, axis=-1, keepdims=True)
    return x * lax.rsqrt(ms + RMS_EPS) * g


def _q_scaled_bf16(w_ref):
    scale = jnp.where(pl.program_id(1) == 0, LOG2E / math.sqrt(HEAD_DIM), 1.0)
    return (w_ref[...] * scale).astype(jnp.bfloat16)


def _norm_proj_kernel(x_ref, g_ref, w_ref, o_ref, h_ref):
    @pl.when(pl.program_id(1) == 0)
    def _():
        h_ref[...] = _rmsnorm(x_ref[...], g_ref[...]).astype(h_ref.dtype)

    res = jnp.dot(h_ref[...], _q_scaled_bf16(w_ref),
                  preferred_element_type=jnp.float32).astype(o_ref.dtype)
    for k in range(o_ref.shape[0]):
        o_ref[k] = res[:, k * SLAB:(k + 1) * SLAB]


def _norm_proj(x, g, w, *, tm=2048, tn=1024):
    t, d = x.shape
    n = w.shape[1]
    return pl.pallas_call(
        _norm_proj_kernel,
        grid=(t // tm, n // tn),
        in_specs=[
            pl.BlockSpec((tm, d), lambda i, j: (i, 0)),
            pl.BlockSpec((1, d), lambda i, j: (0, 0)),
            pl.BlockSpec((d, tn), lambda i, j: (0, j)),
        ],
        out_specs=pl.BlockSpec((tn // SLAB, tm, SLAB), lambda i, j: (j, i, 0)),
        out_shape=jax.ShapeDtypeStruct((n // SLAB, t, SLAB), jnp.bfloat16),
        scratch_shapes=[pltpu.VMEM((tm, d), jnp.bfloat16)],
        compiler_params=pltpu.CompilerParams(
            dimension_semantics=("parallel", "arbitrary"),
            vmem_limit_bytes=VMEM_LIMIT),
        name="norm_proj",
    )(x, g.reshape(1, d), w)


def _proj_kernel(h_ref, w_ref, o_ref):
    o_ref[...] = jnp.dot(h_ref[...], _q_scaled_bf16(w_ref),
                         preferred_element_type=jnp.float32).astype(o_ref.dtype)


def _proj(h, w, col_blocks, *, tm=2048):
    t, d = h.shape
    tn = d
    n = len(col_blocks) * tn

    def w_block(j):
        blk = col_blocks[-1]
        for k in range(len(col_blocks) - 2, -1, -1):
            blk = jnp.where(j == k, col_blocks[k], blk)
        return blk

    return pl.pallas_call(
        _proj_kernel,
        grid=(t // tm, n // tn),
        in_specs=[
            pl.BlockSpec((tm, d), lambda i, j: (i, 0)),
            pl.BlockSpec((d, tn), lambda i, j: (0, w_block(j))),
        ],
        out_specs=pl.BlockSpec((tm, tn), lambda i, j: (i, j)),
        out_shape=jax.ShapeDtypeStruct((t, n), jnp.bfloat16),
        compiler_params=pltpu.CompilerParams(
            dimension_semantics=("parallel", "parallel"),
            vmem_limit_bytes=VMEM_LIMIT),
        name="proj",
    )(h, w)


def _na_kernel(q_ref, k_ref, v_ref, b_ref, o_ref, s0, s1, p0, p1, l0, l1, stage_ref):
    s_refs, p_refs, l_refs = (s0, s1), (p0, p1), (l0, l1)
    step = pl.program_id(2)
    n_tiles = k_ref.shape[1] // GRID_W
    rows = HEADS_PER_STEP * NA_Q_TILE
    early = (NA_TILES_PER_STEP - 2) * NA_Q_TILE

    @pl.when((pl.program_id(0) == 0) & (pl.program_id(1) == 0) & (step == 0))
    def _():
        for ref in (s0, s1, p0, p1, stage_ref):
            ref[...] = jnp.zeros(ref.shape, ref.dtype)
        for ref in l_refs:
            ref[...] = jnp.ones(ref.shape, ref.dtype)

    o_ref[0, :early, :] = stage_ref[...]
    lane = lax.broadcasted_iota(jnp.int32, (NA_Q_TILE, SLAB), 1) // HEAD_DIM

    def window(ref, r):
        first_row = jnp.clip(r - NA_ROWS // 2, 0, n_tiles - NA_ROWS)
        return ref[0, pl.ds(pl.multiple_of(first_row * GRID_W, GRID_W), NA_WIN), :]

    for u in range(NA_TILES_PER_STEP):
        cur, prev = u % 2, 1 - u % 2
        t = step * NA_TILES_PER_STEP + u

        t_s = jnp.minimum(t, n_tiles - 1)
        q4 = q_ref[0, u * NA_Q_TILE:(u + 1) * NA_Q_TILE, :]
        zero = jnp.zeros_like(q4)
        q_stack = jnp.concatenate(
            [jnp.where(lane == h, q4, zero) for h in range(HEADS_PER_STEP)], axis=0)
        s = lax.dot_general(q_stack, window(k_ref, t_s), _NT_DIMS,
                            preferred_element_type=jnp.float32)
        first_off = jnp.clip(t_s - NA_ROWS // 2, 0, n_tiles - NA_ROWS) - t_s + NA_ROWS - 1
        bias = jnp.concatenate(
            [jnp.concatenate([b_ref[h, first_off + 2 * j] for j in range(NA_ROWS // 2)], axis=1)
             for h in range(HEADS_PER_STEP)], axis=0)
        s_refs[cur][...] = s + bias

        s_prev = s_refs[prev][...]
        m = jnp.max(s_prev, axis=-1, keepdims=True)
        e = jnp.exp2(s_prev - m)
        p_refs[prev][...] = e.astype(jnp.bfloat16)
        l_refs[prev][...] = jnp.broadcast_to(jnp.sum(e, axis=-1, keepdims=True), (rows, LANES))

        t_o = jnp.clip(t - 2, 0, n_tiles - 1)
        inv_l = 1.0 / l_refs[cur][...]
        o_all = jnp.dot(p_refs[cur][...], window(v_ref, t_o),
                        preferred_element_type=jnp.float32)
        o_all = o_all * jnp.concatenate([inv_l] * (SLAB // LANES), axis=1)
        out = o_all[:NA_Q_TILE]
        for h in range(1, HEADS_PER_STEP):
            out = jnp.where(lane == h, o_all[h * NA_Q_TILE:(h + 1) * NA_Q_TILE], out)
        out = out.astype(o_ref.dtype)
        if u < 2:
            o_ref[0, early + u * NA_Q_TILE:early + (u + 1) * NA_Q_TILE, :] = out
        else:
            stage_ref[(u - 2) * NA_Q_TILE:(u - 1) * NA_Q_TILE, :] = out


def _na_bias_table(rpb):
    n_col_off = 2 * NA_COLS - 1
    c = np.arange(GRID_W)[:, None]
    kc = np.arange(GRID_W)[None, :]
    cs = np.clip(c - NA_COLS // 2, 0, GRID_W - NA_COLS)
    col_ok = (kc >= cs) & (kc < cs + NA_COLS)
    col_off = kc - c + NA_COLS - 1
    assert np.all(((col_off >= 0) & (col_off < n_col_off))[col_ok])
    pick = (np.arange(n_col_off)[:, None, None] == col_off[None]) & col_ok[None]
    col_part = jnp.einsum('hrm,mck->hrck', rpb.astype(jnp.float32),
                          jnp.asarray(pick, dtype=jnp.float32), precision=lax.Precision.HIGHEST)
    col_part = jnp.where(jnp.asarray(col_ok), col_part * LOG2E, NEG_INF)
    return jnp.concatenate([col_part[:, :-1], col_part[:, 1:]], axis=-1)


def _na_attention(proj, bias_table, b):
    s = proj.shape[1] // b
    tile = NA_Q_TILE * NA_TILES_PER_STEP
    assert NA_TILES_PER_STEP % 2 == 0 and s % tile == 0 and NA_Q_TILE == GRID_W
    n_blocks = s // tile
    rows = HEADS_PER_STEP * NA_Q_TILE
    scratch = ([pltpu.VMEM((rows, NA_WIN), jnp.float32)] * 2
               + [pltpu.VMEM((rows, NA_WIN), jnp.bfloat16)] * 2
               + [pltpu.VMEM((rows, LANES), jnp.float32)] * 2
               + [pltpu.VMEM(((NA_TILES_PER_STEP - 2) * NA_Q_TILE, SLAB), jnp.bfloat16)])
    return pl.pallas_call(
        _na_kernel,
        grid=(b, N_SLABS, n_blocks + 1),
        in_specs=[
            pl.BlockSpec((1, tile, SLAB),
                         lambda bi, g, j: (g, bi * n_blocks + jnp.minimum(j, n_blocks - 1), 0)),
            pl.BlockSpec((1, s, SLAB), lambda bi, g, j: (N_SLABS + g, bi, 0)),
            pl.BlockSpec((1, s, SLAB), lambda bi, g, j: (2 * N_SLABS + g, bi, 0)),
            pl.BlockSpec((HEADS_PER_STEP,) + bias_table.shape[1:], lambda bi, g, j: (g, 0, 0, 0)),
        ],
        out_specs=pl.BlockSpec((1, tile, SLAB),
                               lambda bi, g, j: (g, bi * n_blocks + jnp.maximum(j - 1, 0), 0)),
        out_shape=jax.ShapeDtypeStruct((N_SLABS, b * s, SLAB), jnp.bfloat16),
        scratch_shapes=scratch,
        compiler_params=pltpu.CompilerParams(
            dimension_semantics=("arbitrary", "arbitrary", "arbitrary"),
            vmem_limit_bytes=VMEM_LIMIT),
        name="na_attention",
    )(proj, proj, proj, bias_table)


def _alibi_slopes():
    return np.asarray(2.0 ** (-8.0 * (np.arange(N_HEADS) + 1) / N_HEADS), dtype=np.float32)


def _slab_attention(q4, k4, v4, bias):
    lane = lax.broadcasted_iota(jnp.int32, q4.shape, 1) // HEAD_DIM
    zero = jnp.zeros_like(q4)
    q_stack = jnp.concatenate(
        [jnp.where(lane == h, q4, zero) for h in range(HEADS_PER_STEP)], axis=0)
    s = lax.dot_general(q_stack, k4, _NT_DIMS, preferred_element_type=jnp.float32)
    s = s + bias
    m = jnp.max(s, axis=-1, keepdims=True)
    p = jnp.exp2(s - m)
    l = jnp.sum(p, axis=-1, keepdims=True)
    p = p.astype(jnp.bfloat16)
    inv_l = 1.0 / l
    out = None
    for h in range(HEADS_PER_STEP):
        rows = slice(h * Q_TILE, (h + 1) * Q_TILE)
        o_h = jnp.dot(p[rows], v4, preferred_element_type=jnp.float32) * inv_l[rows]
        out = o_h if out is None else jnp.where(lane == h, o_h, out)
    return out, m, l


def _dil_kernel(q_ref, ka_ref, kb_ref, kc_ref, va_ref, vb_ref, vc_ref, b_ref,
                o_ref, lse_ref):
    step = pl.program_id(2)
    n_steps = pl.num_programs(2)
    k_win = jnp.concatenate([ka_ref[0, 0], kb_ref[0, 0], kc_ref[0, 0]], axis=0)
    v_win = jnp.concatenate([va_ref[0, 0], vb_ref[0, 0], vc_ref[0, 0]], axis=0)
    lane = lax.broadcasted_iota(jnp.int32, (Q_TILE, LSE_LANES), 1)
    for u in range(DIL_TILES_PER_STEP):
        pattern = 0
        if u == 0:
            pattern = jnp.where(step == 0, 1, pattern)
        if u == DIL_TILES_PER_STEP - 1:
            pattern = jnp.where(step == n_steps - 1, 2, pattern)
        tile = slice(u * Q_TILE, (u + 1) * Q_TILE)
        win = slice(u * Q_TILE, u * Q_TILE + DIL_WIN)
        lse_tile = jnp.zeros((Q_TILE, LSE_LANES), jnp.float32)
        for hg in range(N_SLABS):
            cols = slice(hg * SLAB, (hg + 1) * SLAB)
            bias = b_ref[pattern, hg * HEADS_PER_STEP:(hg + 1) * HEADS_PER_STEP]
            bias = bias.reshape(HEADS_PER_STEP * Q_TILE, DIL_WIN)
            out, m, l = _slab_attention(q_ref[0, 0, tile, cols], k_win[win, cols],
                                        v_win[win, cols], bias)
            o_ref[0, 0, tile, cols] = out.astype(o_ref.dtype)
            lse = m + jnp.log2(l)
            for h in range(HEADS_PER_STEP):
                lse_h = lse[h * Q_TILE:(h + 1) * Q_TILE]
                lse_tile = jnp.where(lane == hg * HEADS_PER_STEP + h, lse_h, lse_tile)
        lse_ref[0, 0, tile, :] = lse_tile


def _dil_bias_table(dil):
    qi = np.arange(Q_TILE)[:, None]
    kj = np.arange(DIL_WIN)[None, :]
    delta = kj - DIL_RADIUS - qi
    in_band = np.abs(delta) <= DIL_RADIUS
    valid = np.stack([in_band,
                      in_band & (kj >= DIL_RADIUS),
                      in_band & (kj < DIL_WIN - DIL_RADIUS)])
    dist = jnp.asarray(np.abs(delta) * dil, dtype=jnp.float32)
    bias = -jnp.asarray(_alibi_slopes())[:, None, None] * dist[None]
    return jnp.where(jnp.asarray(valid)[:, None], bias[None] * LOG2E, NEG_INF)


def _dil_attention(qkv, dil):
    b, _, l, _ = qkv.shape
    tile = Q_TILE * DIL_TILES_PER_STEP
    n_steps = l // tile
    assert l % tile == 0 and DIL_TILES_PER_STEP >= 2
    half = DIL_RADIUS
    per_tile = tile // half
    n_half = l // half

    def spec_mid(which):
        return pl.BlockSpec((1, 1, tile, D_MODEL), lambda bi, r, i: (bi, r, i, which))

    def spec_lo(which):
        return pl.BlockSpec((1, 1, half, D_MODEL),
                            lambda bi, r, i: (bi, r, jnp.maximum(per_tile * i - 1, 0), which))

    def spec_hi(which):
        return pl.BlockSpec((1, 1, half, D_MODEL),
                            lambda bi, r, i: (bi, r, jnp.minimum(per_tile * (i + 1), n_half - 1), which))

    bias = _dil_bias_table(dil)
    return pl.pallas_call(
        _dil_kernel,
        grid=(b, dil, n_steps),
        in_specs=[
            spec_mid(0),
            spec_lo(1), spec_mid(1), spec_hi(1),
            spec_lo(2), spec_mid(2), spec_hi(2),
            pl.BlockSpec(bias.shape, lambda bi, r, i: (0, 0, 0, 0)),
        ],
        out_specs=[
            pl.BlockSpec((1, 1, tile, D_MODEL), lambda bi, r, i: (bi, r, i, 0)),
            pl.BlockSpec((1, 1, tile, LSE_LANES), lambda bi, r, i: (bi, r, i, 0)),
        ],
        out_shape=[
            jax.ShapeDtypeStruct((b, dil, l, D_MODEL), jnp.bfloat16),
            jax.ShapeDtypeStruct((b, dil, l, LSE_LANES), jnp.float32),
        ],
        compiler_params=pltpu.CompilerParams(
            dimension_semantics=("parallel", "parallel", "arbitrary"),
            vmem_limit_bytes=VMEM_LIMIT),
        name=f"dilated_attention_{dil}",
    )(qkv, qkv, qkv, qkv, qkv, qkv, qkv, bias)


def _silu(x):
    half = 0.5 * x
    return half + half * jnp.tanh(half)


def _out0_kernel(o_ref, gate_ref, x_ref, w_ref, g_ref, y_ref, *rest):
    h_refs, slab_ref = rest[:-1], rest[-1]
    o = jnp.concatenate([o_ref[k] for k in range(N_SLABS)], axis=1).astype(jnp.float32)
    gate = jnp.concatenate([gate_ref[k] for k in range(N_SLABS)], axis=1).astype(jnp.float32)
    z = (o * _silu(gate)).astype(jnp.bfloat16)
    y = x_ref[0] + jnp.dot(z, w_ref[...], preferred_element_type=jnp.float32)
    y_ref[0] = y
    hn = _rmsnorm(y, g_ref[...])
    tm = hn.shape[0]
    staged = False
    for h_ref, (_, dil) in zip(h_refs, DIL_PAIRS):
        if dil == 1:
            h_ref[0, 0] = hn.astype(h_ref.dtype)
            continue
        if not staged:
            for s in range(N_LANE_SLABS):
                slab_ref[s] = hn[:, s * LANES:(s + 1) * LANES]
            staged = True
        n = tm // dil
        for r in range(dil):
            for s in range(N_LANE_SLABS):
                rows = slab_ref[s, pl.ds(r, n, stride=dil), :]
                h_ref[0, r, :, s * LANES:(s + 1) * LANES] = rows.astype(h_ref.dtype)


def _out_proj0(o, proj, x, w_out, g_next, *, tm=512):
    b, s, d = x.shape
    gate_block = proj.shape[0] // N_SLABS - 1
    tiles = s // tm
    tok = pl.BlockSpec((1, tm, d), lambda bi, i: (bi, i, 0))
    dils = [dil for _, dil in DIL_PAIRS]
    return pl.pallas_call(
        _out0_kernel,
        grid=(b, s // tm),
        in_specs=[
            pl.BlockSpec((N_SLABS, tm, SLAB), lambda bi, i: (0, bi * tiles + i, 0)),
            pl.BlockSpec((N_SLABS, tm, SLAB), lambda bi, i: (gate_block, bi * tiles + i, 0)),
            tok,
            pl.BlockSpec((d, d), lambda bi, i: (0, 0)),
            pl.BlockSpec((1, d), lambda bi, i: (0, 0)),
        ],
        out_specs=[tok] + [pl.BlockSpec((1, dil, tm // dil, d), lambda bi, i: (bi, 0, i, 0))
                           for dil in dils],
        out_shape=([jax.ShapeDtypeStruct((b, s, d), jnp.float32)]
                   + [jax.ShapeDtypeStruct((b, dil, s // dil, d), jnp.bfloat16) for dil in dils]),
        scratch_shapes=[pltpu.VMEM((N_LANE_SLABS, tm, LANES), jnp.float32)],
        compiler_params=pltpu.CompilerParams(
            dimension_semantics=("parallel", "parallel"), vmem_limit_bytes=VMEM_LIMIT),
        name="out_proj0",
    )(o, proj, x, w_out, g_next.reshape(1, d))


def _natural_order(ref, slab_ref, dil):
    _, n, c = ref.shape
    if dil == 1:
        return ref[0].astype(jnp.float32)
    pieces = []
    for s in range(c // LANES):
        for r in range(dil):
            slab_ref[s, pl.ds(r, n, stride=dil), :] = (
                ref[r, :, s * LANES:(s + 1) * LANES].astype(jnp.float32))
        pieces.append(slab_ref[s])
    return pieces[0] if len(pieces) == 1 else jnp.concatenate(pieces, axis=1)


def _out1_kernel(o0_ref, o1_ref, o2_ref, l0_ref, l1_ref, l2_ref, gate_ref, x_ref,
                 w_ref, e_ref, g_ref, y_ref, slab_ref):
    dils = [dil for _, dil in DIL_PAIRS]
    lses = [_natural_order(ref.at[0], slab_ref, dil)
            for ref, dil in zip((l0_ref, l1_ref, l2_ref), dils)]
    m = jnp.maximum(jnp.maximum(lses[0], lses[1]), lses[2])
    es = [jnp.exp2(v - m) for v in lses]
    denom = es[0] + es[1] + es[2]
    o = None
    for e, o_ref, dil in zip(es, (o0_ref, o1_ref, o2_ref), dils):
        w = e / denom
        hi = w.astype(jnp.bfloat16)
        lo = (w - hi.astype(jnp.float32)).astype(jnp.bfloat16)
        w_full = jnp.dot(jnp.concatenate([hi, lo], axis=1), e_ref[...],
                         preferred_element_type=jnp.float32)
        term = w_full * _natural_order(o_ref.at[0], slab_ref, dil)
        o = term if o is None else o + term
    gate = gate_ref[0].astype(jnp.float32)
    z = (o * _silu(gate)).astype(jnp.bfloat16)
    x = x_ref[0] + jnp.dot(z, w_ref[...], preferred_element_type=jnp.float32)
    y_ref[0] = _rmsnorm(x, g_ref[...])


def _head_expansion():
    e = np.zeros((LSE_LANES, D_MODEL), np.float32)
    for h in range(N_HEADS):
        e[h, h * HEAD_DIM:(h + 1) * HEAD_DIM] = 1.0
    return jnp.asarray(np.concatenate([e, e], axis=0), dtype=jnp.bfloat16)


def _out_proj1(os_, lses, proj, x, w_out, norm_f, *, tm=512):
    b, s, d = x.shape
    gate_block = proj.shape[2] // d - 1
    dils = [dil for _, dil in DIL_PAIRS]

    def grouped(width, dil):
        return pl.BlockSpec((1, dil, tm // dil, width), lambda bi, i: (bi, 0, i, 0))

    tok = pl.BlockSpec((1, tm, d), lambda bi, i: (bi, i, 0))
    return pl.pallas_call(
        _out1_kernel,
        grid=(b, s // tm),
        in_specs=(
            [grouped(d, dil) for dil in dils]
            + [grouped(LSE_LANES, dil) for dil in dils]
            + [pl.BlockSpec((1, tm, d), lambda bi, i: (bi, i, gate_block)),
               tok,
               pl.BlockSpec((d, d), lambda bi, i: (0, 0)),
               pl.BlockSpec((2 * LSE_LANES, d), lambda bi, i: (0, 0)),
               pl.BlockSpec((1, d), lambda bi, i: (0, 0))]),
        out_specs=tok,
        out_shape=jax.ShapeDtypeStruct((b, s, d), jnp.float32),
        scratch_shapes=[pltpu.VMEM((N_LANE_SLABS, tm, LANES), jnp.float32)],
        compiler_params=pltpu.CompilerParams(
            dimension_semantics=("parallel", "parallel"), vmem_limit_bytes=VMEM_LIMIT),
        name="out_proj1",
    )(*os_, *lses, proj, x, w_out, _head_expansion(), norm_f.reshape(1, d))


def kernel(x, norm_0, w_in_0, rpb_0, w_out_0, norm_1, w_in_1, w_out_1, norm_f):
    b, s, d = x.shape
    t = b * s

    proj0 = _norm_proj(x.reshape(t, d), norm_0, w_in_0)
    o0 = _na_attention(proj0, _na_bias_table(rpb_0), b)
    x1, *h1 = _out_proj0(o0, proj0, x, w_out_0.astype(jnp.bfloat16), norm_1)

    gate_block = 3 * N_DIL_GROUPS
    outs, lses, proj_gate = [], [], None
    for g, (window, dil) in enumerate(DIL_PAIRS):
        assert window // (2 * dil) == DIL_RADIUS
        blocks = (3 * g, 3 * g + 1, 3 * g + 2) + ((gate_block,) if dil == 1 else ())
        qkv = _proj(h1[g].reshape(t, d), w_in_1, blocks).reshape(b, dil, s // dil, -1)
        if dil == 1:
            proj_gate = qkv.reshape(b, s, -1)
        o_g, lse_g = _dil_attention(qkv, dil)
        outs.append(o_g)
        lses.append(lse_g)
    return _out_proj1(outs, lses, proj_gate, x1, w_out_1.astype(jnp.bfloat16), norm_f)
```

```python
import functools
import math

import numpy as np
import jax
import jax.numpy as jnp
from jax import lax
from jax.experimental import pallas as pl
from jax.experimental.pallas import tpu as pltpu

D_MODEL = 1024
HEAD_DIM = 64
N_HEADS = 16
GRID_W = 64
NA_ROWS = 8
NA_COLS = 16
DIL_PAIRS = ((128, 1), (512, 4), (2048, 16))
N_DIL_GROUPS = len(DIL_PAIRS)
RMS_EPS = 1e-6
NEG_INF = -1e30
LOG2E = math.log2(math.e)

LANES = 128
N_LANE_SLABS = D_MODEL // LANES
HEADS_PER_STEP = 4
SLAB = HEADS_PER_STEP * HEAD_DIM
N_SLABS = D_MODEL // SLAB
Q_TILE = 128
NA_Q_TILE = GRID_W
NA_WIN = NA_ROWS * GRID_W
NA_TILES_PER_STEP = 8
DIL_TILES_PER_STEP = 4
DIL_RADIUS = 64
DIL_WIN = Q_TILE + 2 * DIL_RADIUS
LSE_LANES = LANES
VMEM_LIMIT = 56 * 1024 * 1024

_NT_DIMS = (((1,), (1,)), ((), ()))


def _rmsnorm(x, g):
    ms = jnp.mean(x * x, axis=-1, keepdims=True)
    return x * lax.rsqrt(ms + RMS_EPS) * g


def _q_scaled_bf16(w_ref):
    scale = jnp.where(pl.program_id(1) == 0, LOG2E / math.sqrt(HEAD_DIM), 1.0)
    return (w_ref[...] * scale).astype(jnp.bfloat16)


def _norm_proj_kernel(x_ref, g_ref, w_ref, o_ref, h_ref):
    @pl.when(pl.program_id(1) == 0)
    def _():
        h_ref[...] = _rmsnorm(x_ref[...], g_ref[...]).astype(h_ref.dtype)

    res = jnp.dot(h_ref[...], _q_scaled_bf16(w_ref),
                  preferred_element_type=jnp.float32).astype(o_ref.dtype)
    for k in range(o_ref.shape[0]):
        o_ref[k] = res[:, k * SLAB:(k + 1) * SLAB]


def _norm_proj(x, g, w, *, tm=2048, tn=1024):
    t, d = x.shape
    n = w.shape[1]
    return pl.pallas_call(
        _norm_proj_kernel,
        grid=(t // tm, n // tn),
        in_specs=[
            pl.BlockSpec((tm, d), lambda i, j: (i, 0)),
            pl.BlockSpec((1, d), lambda i, j: (0, 0)),
            pl.BlockSpec((d, tn), lambda i, j: (0, j)),
        ],
        out_specs=pl.BlockSpec((tn // SLAB, tm, SLAB), lambda i, j: (j, i, 0)),
        out_shape=jax.ShapeDtypeStruct((n // SLAB, t, SLAB), jnp.bfloat16),
        scratch_shapes=[pltpu.VMEM((tm, d), jnp.bfloat16)],
        compiler_params=pltpu.CompilerParams(
            dimension_semantics=("parallel", "arbitrary"),
            vmem_limit_bytes=VMEM_LIMIT),
        name="norm_proj",
    )(x, g.reshape(1, d), w)


def _proj_kernel(h_ref, w_ref, o_ref):
    o_ref[...] = jnp.dot(h_ref[...], _q_scaled_bf16(w_ref),
                         preferred_element_type=jnp.float32).astype(o_ref.dtype)


def _proj(h, w, col_blocks, *, tm=2048):
    t, d = h.shape
    tn = d
    n = len(col_blocks) * tn

    def w_block(j):
        blk = col_blocks[-1]
        for k in range(len(col_blocks) - 2, -1, -1):
            blk = jnp.where(j == k, col_blocks[k], blk)
        return blk

    return pl.pallas_call(
        _proj_kernel,
        grid=(t // tm, n // tn),
        in_specs=[
            pl.BlockSpec((tm, d), lambda i, j: (i, 0)),
            pl.BlockSpec((d, tn), lambda i, j: (0, w_block(j))),
        ],
        out_specs=pl.BlockSpec((tm, tn), lambda i, j: (i, j)),
        out_shape=jax.ShapeDtypeStruct((t, n), jnp.bfloat16),
        compiler_params=pltpu.CompilerParams(
            dimension_semantics=("parallel", "parallel"),
            vmem_limit_bytes=VMEM_LIMIT),
        name="proj",
    )(h, w)


def _na_kernel(q_ref, k_ref, v_ref, b_ref, o_ref):
    step = pl.program_id(2)
    n_tiles = k_ref.shape[1] // GRID_W
    lane = lax.broadcasted_iota(jnp.int32, (NA_Q_TILE, SLAB), 1) // HEAD_DIM

    for u in range(NA_TILES_PER_STEP):
        r = step * NA_TILES_PER_STEP + u
        first_row = jnp.clip(r - NA_ROWS // 2, 0, n_tiles - NA_ROWS)
        start = pl.multiple_of(first_row * GRID_W, GRID_W)
        tile_rows = slice(u * NA_Q_TILE, (u + 1) * NA_Q_TILE)
        q4 = q_ref[0, tile_rows, :]
        zero = jnp.zeros_like(q4)
        q_stack = jnp.concatenate(
            [jnp.where(lane == h, q4, zero) for h in range(HEADS_PER_STEP)], axis=0)
        s = lax.dot_general(q_stack, k_ref[0, pl.ds(start, NA_WIN), :], _NT_DIMS,
                            preferred_element_type=jnp.float32)
        first_off = first_row - r + NA_ROWS - 1
        s = s + jnp.concatenate(
            [jnp.concatenate([b_ref[h, first_off + 2 * j] for j in range(NA_ROWS // 2)], axis=1)
             for h in range(HEADS_PER_STEP)], axis=0)
        m = jnp.max(s, axis=-1, keepdims=True)
        e = jnp.exp2(s - m)
        inv_l = 1.0 / jnp.sum(e, axis=-1, keepdims=True)
        o_all = jnp.dot(e.astype(jnp.bfloat16), v_ref[0, pl.ds(start, NA_WIN), :],
                        preferred_element_type=jnp.float32) * inv_l
        out = o_all[:NA_Q_TILE]
        for h in range(1, HEADS_PER_STEP):
            out = jnp.where(lane == h, o_all[h * NA_Q_TILE:(h + 1) * NA_Q_TILE], out)
        o_ref[0, tile_rows, :] = out.astype(o_ref.dtype)


def _na_bias_table(rpb):
    n_col_off = 2 * NA_COLS - 1
    c = np.arange(GRID_W)[:, None]
    kc = np.arange(GRID_W)[None, :]
    cs = np.clip(c - NA_COLS // 2, 0, GRID_W - NA_COLS)
    col_ok = (kc >= cs) & (kc < cs + NA_COLS)
    col_off = kc - c + NA_COLS - 1
    assert np.all(((col_off >= 0) & (col_off < n_col_off))[col_ok])
    pick = (np.arange(n_col_off)[:, None, None] == col_off[None]) & col_ok[None]
    col_part = jnp.einsum('hrm,mck->hrck', rpb.astype(jnp.float32),
                          jnp.asarray(pick, dtype=jnp.float32), precision=lax.Precision.HIGHEST)
    col_part = jnp.where(jnp.asarray(col_ok), col_part * LOG2E, NEG_INF)
    return jnp.concatenate([col_part[:, :-1], col_part[:, 1:]], axis=-1)


def _na_attention(proj, bias_table, b):
    s = proj.shape[1] // b
    tile = NA_Q_TILE * NA_TILES_PER_STEP
    assert s % tile == 0 and NA_Q_TILE == GRID_W
    n_blocks = s // tile
    return pl.pallas_call(
        _na_kernel,
        grid=(b, N_SLABS, n_blocks),
        in_specs=[
            pl.BlockSpec((1, tile, SLAB), lambda bi, g, j: (g, bi * n_blocks + j, 0)),
            pl.BlockSpec((1, s, SLAB), lambda bi, g, j: (N_SLABS + g, bi, 0)),
            pl.BlockSpec((1, s, SLAB), lambda bi, g, j: (2 * N_SLABS + g, bi, 0)),
            pl.BlockSpec((HEADS_PER_STEP,) + bias_table.shape[1:], lambda bi, g, j: (g, 0, 0, 0)),
        ],
        out_specs=pl.BlockSpec((1, tile, SLAB), lambda bi, g, j: (g, bi * n_blocks + j, 0)),
        out_shape=jax.ShapeDtypeStruct((N_SLABS, b * s, SLAB), jnp.bfloat16),
        compiler_params=pltpu.CompilerParams(
            dimension_semantics=("parallel", "parallel", "arbitrary"),
            vmem_limit_bytes=VMEM_LIMIT),
        name="na_attention",
    )(proj, proj, proj, bias_table)


def _alibi_slopes():
    return np.asarray(2.0 ** (-8.0 * (np.arange(N_HEADS) + 1) / N_HEADS), dtype=np.float32)


def _slab_attention(q4, k4, v4, bias):
    lane = lax.broadcasted_iota(jnp.int32, q4.shape, 1) // HEAD_DIM
    zero = jnp.zeros_like(q4)
    q_stack = jnp.concatenate(
        [jnp.where(lane == h, q4, zero) for h in range(HEADS_PER_STEP)], axis=0)
    s = lax.dot_general(q_stack, k4, _NT_DIMS, preferred_element_type=jnp.float32)
    s = s + bias
    m = jnp.max(s, axis=-1, keepdims=True)
    p = jnp.exp2(s - m)
    l = jnp.sum(p, axis=-1, keepdims=True)
    p = p.astype(jnp.bfloat16)
    inv_l = 1.0 / l
    out = None
    for h in range(HEADS_PER_STEP):
        rows = slice(h * Q_TILE, (h + 1) * Q_TILE)
        o_h = jnp.dot(p[rows], v4, preferred_element_type=jnp.float32) * inv_l[rows]
        out = o_h if out is None else jnp.where(lane == h, o_h, out)
    return out, m, l


def _dil_kernel(q_ref, ka_ref, kb_ref, kc_ref, va_ref, vb_ref, vc_ref, b_ref,
                o_ref, lse_ref):
    step = pl.program_id(2)
    n_steps = pl.num_programs(2)
    k_win = jnp.concatenate([ka_ref[0, 0], kb_ref[0, 0], kc_ref[0, 0]], axis=0)
    v_win = jnp.concatenate([va_ref[0, 0], vb_ref[0, 0], vc_ref[0, 0]], axis=0)
    lane = lax.broadcasted_iota(jnp.int32, (Q_TILE, LSE_LANES), 1)
    for u in range(DIL_TILES_PER_STEP):
        pattern = 0
        if u == 0:
            pattern = jnp.where(step == 0, 1, pattern)
        if u == DIL_TILES_PER_STEP - 1:
            pattern = jnp.where(step == n_steps - 1, 2, pattern)
        tile = slice(u * Q_TILE, (u + 1) * Q_TILE)
        win = slice(u * Q_TILE, u * Q_TILE + DIL_WIN)
        lse_tile = jnp.zeros((Q_TILE, LSE_LANES), jnp.float32)
        for hg in range(N_SLABS):
            cols = slice(hg * SLAB, (hg + 1) * SLAB)
            bias = b_ref[pattern, hg * HEADS_PER_STEP:(hg + 1) * HEADS_PER_STEP]
            bias = bias.reshape(HEADS_PER_STEP * Q_TILE, DIL_WIN)
            out, m, l = _slab_attention(q_ref[0, 0, tile, cols], k_win[win, cols],
                                        v_win[win, cols], bias)
            o_ref[0, 0, tile, cols] = out.astype(o_ref.dtype)
            lse = m + jnp.log2(l)
            for h in range(HEADS_PER_STEP):
                lse_h = lse[h * Q_TILE:(h + 1) * Q_TILE]
                lse_tile = jnp.where(lane == hg * HEADS_PER_STEP + h, lse_h, lse_tile)
        lse_ref[0, 0, tile, :] = lse_tile


def _dil_bias_table(dil):
    qi = np.arange(Q_TILE)[:, None]
    kj = np.arange(DIL_WIN)[None, :]
    delta = kj - DIL_RADIUS - qi
    in_band = np.abs(delta) <= DIL_RADIUS
    valid = np.stack([in_band,
                      in_band & (kj >= DIL_RADIUS),
                      in_band & (kj < DIL_WIN - DIL_RADIUS)])
    dist = jnp.asarray(np.abs(delta) * dil, dtype=jnp.float32)
    bias = -jnp.asarray(_alibi_slopes())[:, None, None] * dist[None]
    return jnp.where(jnp.asarray(valid)[:, None], bias[None] * LOG2E, NEG_INF)


def _dil_attention(qkv, dil):
    b, _, l, _ = qkv.shape
    tile = Q_TILE * DIL_TILES_PER_STEP
    n_steps = l // tile
    assert l % tile == 0 and DIL_TILES_PER_STEP >= 2
    half = DIL_RADIUS
    per_tile = tile // half
    n_half = l // half

    def spec_mid(which):
        return pl.BlockSpec((1, 1, tile, D_MODEL), lambda bi, r, i: (bi, r, i, which))

    def spec_lo(which):
        return pl.BlockSpec((1, 1, half, D_MODEL),
                            lambda bi, r, i: (bi, r, jnp.maximum(per_tile * i - 1, 0), which))

    def spec_hi(which):
        return pl.BlockSpec((1, 1, half, D_MODEL),
                            lambda bi, r, i: (bi, r, jnp.minimum(per_tile * (i + 1), n_half - 1), which))

    bias = _dil_bias_table(dil)
    return pl.pallas_call(
        _dil_kernel,
        grid=(b, dil, n_steps),
        in_specs=[
            spec_mid(0),
            spec_lo(1), spec_mid(1), spec_hi(1),
            spec_lo(2), spec_mid(2), spec_hi(2),
            pl.BlockSpec(bias.shape, lambda bi, r, i: (0, 0, 0, 0)),
        ],
        out_specs=[
            pl.BlockSpec((1, 1, tile, D_MODEL), lambda bi, r, i: (bi, r, i, 0)),
            pl.BlockSpec((1, 1, tile, LSE_LANES), lambda bi, r, i: (bi, r, i, 0)),
        ],
        out_shape=[
            jax.ShapeDtypeStruct((b, dil, l, D_MODEL), jnp.bfloat16),
            jax.ShapeDtypeStruct((b, dil, l, LSE_LANES), jnp.float32),
        ],
        compiler_params=pltpu.CompilerParams(
            dimension_semantics=("parallel", "parallel", "arbitrary"),
            vmem_limit_bytes=VMEM_LIMIT),
        name=f"dilated_attention_{dil}",
    )(qkv, qkv, qkv, qkv, qkv, qkv, qkv, bias)


def _silu(x):
    half = 0.5 * x
    return half + half * jnp.tanh(half)


def _out0_kernel(o_ref, gate_ref, x_ref, w_ref, g_ref, y_ref, *rest):
    h_refs, slab_ref = rest[:-1], rest[-1]
    o = jnp.concatenate([o_ref[k] for k in range(N_SLABS)], axis=1).astype(jnp.float32)
    gate = jnp.concatenate([gate_ref[k] for k in range(N_SLABS)], axis=1).astype(jnp.float32)
    z = (o * _silu(gate)).astype(jnp.bfloat16)
    y = x_ref[0] + jnp.dot(z, w_ref[...], preferred_element_type=jnp.float32)
    y_ref[0] = y
    hn = _rmsnorm(y, g_ref[...])
    tm = hn.shape[0]
    staged = False
    for h_ref, (_, dil) in zip(h_refs, DIL_PAIRS):
        if dil == 1:
            h_ref[0, 0] = hn.astype(h_ref.dtype)
            continue
        if not staged:
            for s in range(N_LANE_SLABS):
                slab_ref[s] = hn[:, s * LANES:(s + 1) * LANES]
            staged = True
        n = tm // dil
        for r in range(dil):
            for s in range(N_LANE_SLABS):
                rows = slab_ref[s, pl.ds(r, n, stride=dil), :]
                h_ref[0, r, :, s * LANES:(s + 1) * LANES] = rows.astype(h_ref.dtype)


def _out_proj0(o, proj, x, w_out, g_next, *, tm=512):
    b, s, d = x.shape
    gate_block = proj.shape[0] // N_SLABS - 1
    tiles = s // tm
    tok = pl.BlockSpec((1, tm, d), lambda bi, i: (bi, i, 0))
    dils = [dil for _, dil in DIL_PAIRS]
    return pl.pallas_call(
        _out0_kernel,
        grid=(b, s // tm),
        in_specs=[
            pl.BlockSpec((N_SLABS, tm, SLAB), lambda bi, i: (0, bi * tiles + i, 0)),
            pl.BlockSpec((N_SLABS, tm, SLAB), lambda bi, i: (gate_block, bi * tiles + i, 0)),
            tok,
            pl.BlockSpec((d, d), lambda bi, i: (0, 0)),
            pl.BlockSpec((1, d), lambda bi, i: (0, 0)),
        ],
        out_specs=[tok] + [pl.BlockSpec((1, dil, tm // dil, d), lambda bi, i: (bi, 0, i, 0))
                           for dil in dils],
        out_shape=([jax.ShapeDtypeStruct((b, s, d), jnp.float32)]
                   + [jax.ShapeDtypeStruct((b, dil, s // dil, d), jnp.bfloat16) for dil in dils]),
        scratch_shapes=[pltpu.VMEM((N_LANE_SLABS, tm, LANES), jnp.float32)],
        compiler_params=pltpu.CompilerParams(
            dimension_semantics=("parallel", "parallel"), vmem_limit_bytes=VMEM_LIMIT),
        name="out_proj0",
    )(o, proj, x, w_out, g_next.reshape(1, d))


def _natural_order(ref, slab_ref, dil):
    _, n, c = ref.shape
    if dil == 1:
        return ref[0].astype(jnp.float32)
    pieces = []
    for s in range(c // LANES):
        for r in range(dil):
            slab_ref[s, pl.ds(r, n, stride=dil), :] = (
                ref[r, :, s * LANES:(s + 1) * LANES].astype(jnp.float32))
        pieces.append(slab_ref[s])
    return pieces[0] if len(pieces) == 1 else jnp.concatenate(pieces, axis=1)


def _out1_kernel(o0_ref, o1_ref, o2_ref, l0_ref, l1_ref, l2_ref, gate_ref, x_ref,
                 w_ref, e_ref, g_ref, y_ref, slab_ref):
    dils = [dil for _, dil in DIL_PAIRS]
    lses = [_natural_order(ref.at[0], slab_ref, dil)
            for ref, dil in zip((l0_ref, l1_ref, l2_ref), dils)]
    m = jnp.maximum(jnp.maximum(lses[0], lses[1]), lses[2])
    es = [jnp.exp2(v - m) for v in lses]
    denom = es[0] + es[1] + es[2]
    o = None
    for e, o_ref, dil in zip(es, (o0_ref, o1_ref, o2_ref), dils):
        w = e / denom
        hi = w.astype(jnp.bfloat16)
        lo = (w - hi.astype(jnp.float32)).astype(jnp.bfloat16)
        w_full = jnp.dot(jnp.concatenate([hi, lo], axis=1), e_ref[...],
                         preferred_element_type=jnp.float32)
        term = w_full * _natural_order(o_ref.at[0], slab_ref, dil)
        o = term if o is None else o + term
    gate = gate_ref[0].astype(jnp.float32)
    z = (o * _silu(gate)).astype(jnp.bfloat16)
    x = x_ref[0] + jnp.dot(z, w_ref[...], preferred_element_type=jnp.float32)
    y_ref[0] = _rmsnorm(x, g_ref[...])


def _head_expansion():
    e = np.zeros((LSE_LANES, D_MODEL), np.float32)
    for h in range(N_HEADS):
        e[h, h * HEAD_DIM:(h + 1) * HEAD_DIM] = 1.0
    return jnp.asarray(np.concatenate([e, e], axis=0), dtype=jnp.bfloat16)


def _out_proj1(os_, lses, proj, x, w_out, norm_f, *, tm=512):
    b, s, d = x.shape
    gate_block = proj.shape[2] // d - 1
    dils = [dil for _, dil in DIL_PAIRS]

    def grouped(width, dil):
        return pl.BlockSpec((1, dil, tm // dil, width), lambda bi, i: (bi, 0, i, 0))

    tok = pl.BlockSpec((1, tm, d), lambda bi, i: (bi, i, 0))
    return pl.pallas_call(
        _out1_kernel,
        grid=(b, s // tm),
        in_specs=(
            [grouped(d, dil) for dil in dils]
            + [grouped(LSE_LANES, dil) for dil in dils]
            + [pl.BlockSpec((1, tm, d), lambda bi, i: (bi, i, gate_block)),
               tok,
               pl.BlockSpec((d, d), lambda bi, i: (0, 0)),
               pl.BlockSpec((2 * LSE_LANES, d), lambda bi, i: (0, 0)),
               pl.BlockSpec((1, d), lambda bi, i: (0, 0))]),
        out_specs=tok,
        out_shape=jax.ShapeDtypeStruct((b, s, d), jnp.float32),
        scratch_shapes=[pltpu.VMEM((N_LANE_SLABS, tm, LANES), jnp.float32)],
        compiler_params=pltpu.CompilerParams(
            dimension_semantics=("parallel", "parallel"), vmem_limit_bytes=VMEM_LIMIT),
        name="out_proj1",
    )(*os_, *lses, proj, x, w_out, _head_expansion(), norm_f.reshape(1, d))


def kernel(x, norm_0, w_in_0, rpb_0, w_out_0, norm_1, w_in_1, w_out_1, norm_f):
    b, s, d = x.shape
    t = b * s

    proj0 = _norm_proj(x.reshape(t, d), norm_0, w_in_0)
    o0 = _na_attention(proj0, _na_bias_table(rpb_0), b)
    x1, *h1 = _out_proj0(o0, proj0, x, w_out_0.astype(jnp.bfloat16), norm_1)

    gate_block = 3 * N_DIL_GROUPS
    outs, lses, proj_gate = [], [], None
    for g, (window, dil) in enumerate(DIL_PAIRS):
        assert window // (2 * dil) == DIL_RADIUS
        blocks = (3 * g, 3 * g + 1, 3 * g + 2) + ((gate_block,) if dil == 1 else ())
        qkv = _proj(h1[g].reshape(t, d), w_in_1, blocks).reshape(b, dil, s // dil, -1)
        if dil == 1:
            proj_gate = qkv.reshape(b, s, -1)
        o_g, lse_g = _dil_attention(qkv, dil)
        outs.append(o_g)
        lses.append(lse_g)
    return _out_proj1(outs, lses, proj_gate, x1, w_out_1.astype(jnp.bfloat16), norm_f)
```

```python
import functools
import math

import numpy as np
import jax
import jax.numpy as jnp
from jax import lax
from jax.experimental import pallas as pl
from jax.experimental.pallas import tpu as pltpu

D_MODEL = 1024
HEAD_DIM = 64
N_HEADS = 16
GRID_W = 64
NA_ROWS = 8
NA_COLS = 16
DIL_PAIRS = ((128, 1), (512, 4), (2048, 16))
N_DIL_GROUPS = len(DIL_PAIRS)
RMS_EPS = 1e-6
NEG_INF = -1e30
LOG2E = math.log2(math.e)

LANES = 128
N_LANE_SLABS = D_MODEL // LANES
HEADS_PER_STEP = 4
SLAB = HEADS_PER_STEP * HEAD_DIM
N_SLABS = D_MODEL // SLAB
Q_TILE = 128
NA_Q_TILE = GRID_W
NA_WIN = NA_ROWS * GRID_W
NA_TILES_PER_STEP = 16
DIL_TILES_PER_STEP = 8
DIL_RADIUS = 64
DIL_WIN = Q_TILE + 2 * DIL_RADIUS
LSE_LANES = LANES
VMEM_LIMIT = 56 * 1024 * 1024

_NT_DIMS = (((1,), (1,)), ((), ()))


def _rmsnorm(x, g):
    ms = jnp.mean(x * x, axis=-1, keepdims=True)
    return x * lax.rsqrt(ms + RMS_EPS) * g


def _q_scaled_bf16(w_ref):
    scale = jnp.where(pl.program_id(1) == 0, LOG2E / math.sqrt(HEAD_DIM), 1.0)
    return (w_ref[...] * scale).astype(jnp.bfloat16)


def _norm_proj_kernel(x_ref, g_ref, w_ref, o_ref, h_ref):
    @pl.when(pl.program_id(1) == 0)
    def _():
        h_ref[...] = _rmsnorm(x_ref[...], g_ref[...]).astype(h_ref.dtype)

    res = jnp.dot(h_ref[...], _q_scaled_bf16(w_ref),
                  preferred_element_type=jnp.float32).astype(o_ref.dtype)
    for k in range(o_ref.shape[0]):
        o_ref[k] = res[:, k * SLAB:(k + 1) * SLAB]


def _norm_proj(x, g, w, *, tm=2048, tn=1024):
    t, d = x.shape
    n = w.shape[1]
    return pl.pallas_call(
        _norm_proj_kernel,
        grid=(t // tm, n // tn),
        in_specs=[
            pl.BlockSpec((tm, d), lambda i, j: (i, 0)),
            pl.BlockSpec((1, d), lambda i, j: (0, 0)),
            pl.BlockSpec((d, tn), lambda i, j: (0, j)),
        ],
        out_specs=pl.BlockSpec((tn // SLAB, tm, SLAB), lambda i, j: (j, i, 0)),
        out_shape=jax.ShapeDtypeStruct((n // SLAB, t, SLAB), jnp.bfloat16),
        scratch_shapes=[pltpu.VMEM((tm, d), jnp.bfloat16)],
        compiler_params=pltpu.CompilerParams(
            dimension_semantics=("parallel", "arbitrary"),
            vmem_limit_bytes=VMEM_LIMIT),
        name="norm_proj",
    )(x, g.reshape(1, d), w)


def _proj_kernel(h_ref, w_ref, o_ref):
    o_ref[...] = jnp.dot(h_ref[...], _q_scaled_bf16(w_ref),
                         preferred_element_type=jnp.float32).astype(o_ref.dtype)


def _proj(h, w, col_blocks, *, tm=2048):
    t, d = h.shape
    tn = d
    n = len(col_blocks) * tn

    def w_block(j):
        blk = col_blocks[-1]
        for k in range(len(col_blocks) - 2, -1, -1):
            blk = jnp.where(j == k, col_blocks[k], blk)
        return blk

    return pl.pallas_call(
        _proj_kernel,
        grid=(t // tm, n // tn),
        in_specs=[
            pl.BlockSpec((tm, d), lambda i, j: (i, 0)),
            pl.BlockSpec((d, tn), lambda i, j: (0, w_block(j))),
        ],
        out_specs=pl.BlockSpec((tm, tn), lambda i, j: (i, j)),
        out_shape=jax.ShapeDtypeStruct((t, n), jnp.bfloat16),
        compiler_params=pltpu.CompilerParams(
            dimension_semantics=("parallel", "parallel"),
            vmem_limit_bytes=VMEM_LIMIT),
        name="proj",
    )(h, w)


def _na_kernel(q_ref, k_ref, v_ref, b_ref, o_ref):
    step = pl.program_id(2)
    n_tiles = k_ref.shape[1] // GRID_W
    lane = lax.broadcasted_iota(jnp.int32, (NA_Q_TILE, SLAB), 1) // HEAD_DIM

    for u in range(NA_TILES_PER_STEP):
        r = step * NA_TILES_PER_STEP + u
        first_row = jnp.clip(r - NA_ROWS // 2, 0, n_tiles - NA_ROWS)
        start = pl.multiple_of(first_row * GRID_W, GRID_W)
        tile_rows = slice(u * NA_Q_TILE, (u + 1) * NA_Q_TILE)
        q4 = q_ref[0, tile_rows, :]
        zero = jnp.zeros_like(q4)
        q_stack = jnp.concatenate(
            [jnp.where(lane == h, q4, zero) for h in range(HEADS_PER_STEP)], axis=0)
        s = lax.dot_general(q_stack, k_ref[0, pl.ds(start, NA_WIN), :], _NT_DIMS,
                            preferred_element_type=jnp.float32)
        first_off = first_row - r + NA_ROWS - 1
        s = s + jnp.concatenate(
            [jnp.concatenate([b_ref[h, first_off + 2 * j] for j in range(NA_ROWS // 2)], axis=1)
             for h in range(HEADS_PER_STEP)], axis=0)
        m = jnp.max(s, axis=-1, keepdims=True)
        e = jnp.exp2(s - m)
        inv_l = 1.0 / jnp.sum(e, axis=-1, keepdims=True)
        o_all = jnp.dot(e.astype(jnp.bfloat16), v_ref[0, pl.ds(start, NA_WIN), :],
                        preferred_element_type=jnp.float32) * inv_l
        out = o_all[:NA_Q_TILE]
        for h in range(1, HEADS_PER_STEP):
            out = jnp.where(lane == h, o_all[h * NA_Q_TILE:(h + 1) * NA_Q_TILE], out)
        o_ref[0, tile_rows, :] = out.astype(o_ref.dtype)


def _na_bias_table(rpb):
    n_col_off = 2 * NA_COLS - 1
    c = np.arange(GRID_W)[:, None]
    kc = np.arange(GRID_W)[None, :]
    cs = np.clip(c - NA_COLS // 2, 0, GRID_W - NA_COLS)
    col_ok = (kc >= cs) & (kc < cs + NA_COLS)
    col_off = kc - c + NA_COLS - 1
    assert np.all(((col_off >= 0) & (col_off < n_col_off))[col_ok])
    pick = (np.arange(n_col_off)[:, None, None] == col_off[None]) & col_ok[None]
    col_part = jnp.einsum('hrm,mck->hrck', rpb.astype(jnp.float32),
                          jnp.asarray(pick, dtype=jnp.float32), precision=lax.Precision.HIGHEST)
    col_part = jnp.where(jnp.asarray(col_ok), col_part * LOG2E, NEG_INF)
    return jnp.concatenate([col_part[:, :-1], col_part[:, 1:]], axis=-1)


def _na_attention(proj, bias_table, b):
    s = proj.shape[1] // b
    tile = NA_Q_TILE * NA_TILES_PER_STEP
    assert s % tile == 0 and NA_Q_TILE == GRID_W
    n_blocks = s // tile
    return pl.pallas_call(
        _na_kernel,
        grid=(b, N_SLABS, n_blocks),
        in_specs=[
            pl.BlockSpec((1, tile, SLAB), lambda bi, g, j: (g, bi * n_blocks + j, 0)),
            pl.BlockSpec((1, s, SLAB), lambda bi, g, j: (N_SLABS + g, bi, 0)),
            pl.BlockSpec((1, s, SLAB), lambda bi, g, j: (2 * N_SLABS + g, bi, 0)),
            pl.BlockSpec((HEADS_PER_STEP,) + bias_table.shape[1:], lambda bi, g, j: (g, 0, 0, 0)),
        ],
        out_specs=pl.BlockSpec((1, tile, SLAB), lambda bi, g, j: (g, bi * n_blocks + j, 0)),
        out_shape=jax.ShapeDtypeStruct((N_SLABS, b * s, SLAB), jnp.bfloat16),
        compiler_params=pltpu.CompilerParams(
            dimension_semantics=("parallel", "parallel", "arbitrary"),
            vmem_limit_bytes=VMEM_LIMIT),
        name="na_attention",
    )(proj, proj, proj, bias_table)


def _alibi_slopes():
    return np.asarray(2.0 ** (-8.0 * (np.arange(N_HEADS) + 1) / N_HEADS), dtype=np.float32)


def _slab_attention(q4, k4, v4, bias):
    lane = lax.broadcasted_iota(jnp.int32, q4.shape, 1) // HEAD_DIM
    zero = jnp.zeros_like(q4)
    q_stack = jnp.concatenate(
        [jnp.where(lane == h, q4, zero) for h in range(HEADS_PER_STEP)], axis=0)
    s = lax.dot_general(q_stack, k4, _NT_DIMS, preferred_element_type=jnp.float32)
    s = s + bias
    m = jnp.max(s, axis=-1, keepdims=True)
    p = jnp.exp2(s - m)
    l = jnp.sum(p, axis=-1, keepdims=True)
    p = p.astype(jnp.bfloat16)
    inv_l = 1.0 / l
    out = None
    for h in range(HEADS_PER_STEP):
        rows = slice(h * Q_TILE, (h + 1) * Q_TILE)
        o_h = jnp.dot(p[rows], v4, preferred_element_type=jnp.float32) * inv_l[rows]
        out = o_h if out is None else jnp.where(lane == h, o_h, out)
    return out, m, l


def _dil_kernel(q_ref, ka_ref, kb_ref, kc_ref, va_ref, vb_ref, vc_ref, b_ref,
                o_ref, lse_ref):
    step = pl.program_id(2)
    n_steps = pl.num_programs(2)
    k_win = jnp.concatenate([ka_ref[0, 0], kb_ref[0, 0], kc_ref[0, 0]], axis=0)
    v_win = jnp.concatenate([va_ref[0, 0], vb_ref[0, 0], vc_ref[0, 0]], axis=0)
    lane = lax.broadcasted_iota(jnp.int32, (Q_TILE, LSE_LANES), 1)
    tiles = q_ref.shape[2] // Q_TILE
    for u in range(tiles):
        pattern = 0
        if u == 0:
            pattern = jnp.where(step == 0, 1, pattern)
        if u == tiles - 1:
            pattern = jnp.where(step == n_steps - 1, 2, pattern)
        tile = slice(u * Q_TILE, (u + 1) * Q_TILE)
        win = slice(u * Q_TILE, u * Q_TILE + DIL_WIN)
        lse_tile = jnp.zeros((Q_TILE, LSE_LANES), jnp.float32)
        for hg in range(N_SLABS):
            cols = slice(hg * SLAB, (hg + 1) * SLAB)
            bias = b_ref[pattern, hg * HEADS_PER_STEP:(hg + 1) * HEADS_PER_STEP]
            bias = bias.reshape(HEADS_PER_STEP * Q_TILE, DIL_WIN)
            out, m, l = _slab_attention(q_ref[0, 0, tile, cols], k_win[win, cols],
                                        v_win[win, cols], bias)
            o_ref[0, 0, tile, cols] = out.astype(o_ref.dtype)
            lse = m + jnp.log2(l)
            for h in range(HEADS_PER_STEP):
                lse_h = lse[h * Q_TILE:(h + 1) * Q_TILE]
                lse_tile = jnp.where(lane == hg * HEADS_PER_STEP + h, lse_h, lse_tile)
        lse_ref[0, 0, tile, :] = lse_tile


def _dil_bias_table(dil):
    qi = np.arange(Q_TILE)[:, None]
    kj = np.arange(DIL_WIN)[None, :]
    delta = kj - DIL_RADIUS - qi
    in_band = np.abs(delta) <= DIL_RADIUS
    valid = np.stack([in_band,
                      in_band & (kj >= DIL_RADIUS),
                      in_band & (kj < DIL_WIN - DIL_RADIUS)])
    dist = jnp.asarray(np.abs(delta) * dil, dtype=jnp.float32)
    bias = -jnp.asarray(_alibi_slopes())[:, None, None] * dist[None]
    return jnp.where(jnp.asarray(valid)[:, None], bias[None] * LOG2E, NEG_INF)


def _dil_attention(qkv, dil):
    b, _, l, _ = qkv.shape
    tiles = min(DIL_TILES_PER_STEP, l // Q_TILE)
    tile = Q_TILE * tiles
    n_steps = l // tile
    assert l % tile == 0 and tiles >= 2
    half = DIL_RADIUS
    per_tile = tile // half
    n_half = l // half

    def spec_mid(which):
        return pl.BlockSpec((1, 1, tile, D_MODEL), lambda bi, r, i: (bi, r, i, which))

    def spec_lo(which):
        return pl.BlockSpec((1, 1, half, D_MODEL),
                            lambda bi, r, i: (bi, r, jnp.maximum(per_tile * i - 1, 0), which))

    def spec_hi(which):
        return pl.BlockSpec((1, 1, half, D_MODEL),
                            lambda bi, r, i: (bi, r, jnp.minimum(per_tile * (i + 1), n_half - 1), which))

    bias = _dil_bias_table(dil)
    return pl.pallas_call(
        _dil_kernel,
        grid=(b, dil, n_steps),
        in_specs=[
            spec_mid(0),
            spec_lo(1), spec_mid(1), spec_hi(1),
            spec_lo(2), spec_mid(2), spec_hi(2),
            pl.BlockSpec(bias.shape, lambda bi, r, i: (0, 0, 0, 0)),
        ],
        out_specs=[
            pl.BlockSpec((1, 1, tile, D_MODEL), lambda bi, r, i: (bi, r, i, 0)),
            pl.BlockSpec((1, 1, tile, LSE_LANES), lambda bi, r, i: (bi, r, i, 0)),
        ],
        out_shape=[
            jax.ShapeDtypeStruct((b, dil, l, D_MODEL), jnp.bfloat16),
            jax.ShapeDtypeStruct((b, dil, l, LSE_LANES), jnp.float32),
        ],
        compiler_params=pltpu.CompilerParams(
            dimension_semantics=("parallel", "parallel", "arbitrary"),
            vmem_limit_bytes=VMEM_LIMIT),
        name=f"dilated_attention_{dil}",
    )(qkv, qkv, qkv, qkv, qkv, qkv, qkv, bias)


def _silu(x):
    half = 0.5 * x
    return half + half * jnp.tanh(half)


def _out0_kernel(o_ref, gate_ref, x_ref, w_ref, g_ref, y_ref, *rest):
    h_refs, slab_ref = rest[:-1], rest[-1]
    o = jnp.concatenate([o_ref[k] for k in range(N_SLABS)], axis=1).astype(jnp.float32)
    gate = jnp.concatenate([gate_ref[k] for k in range(N_SLABS)], axis=1).astype(jnp.float32)
    z = (o * _silu(gate)).astype(jnp.bfloat16)
    y = x_ref[0] + jnp.dot(z, w_ref[...], preferred_element_type=jnp.float32)
    y_ref[0] = y
    hn = _rmsnorm(y, g_ref[...])
    tm = hn.shape[0]
    staged = False
    for h_ref, (_, dil) in zip(h_refs, DIL_PAIRS):
        if dil == 1:
            h_ref[0, 0] = hn.astype(h_ref.dtype)
            continue
        if not staged:
            for s in range(N_LANE_SLABS):
                slab_ref[s] = hn[:, s * LANES:(s + 1) * LANES]
            staged = True
        n = tm // dil
        for r in range(dil):
            for s in range(N_LANE_SLABS):
                rows = slab_ref[s, pl.ds(r, n, stride=dil), :]
                h_ref[0, r, :, s * LANES:(s + 1) * LANES] = rows.astype(h_ref.dtype)


def _out_proj0(o, proj, x, w_out, g_next, *, tm=512):
    b, s, d = x.shape
    gate_block = proj.shape[0] // N_SLABS - 1
    tiles = s // tm
    tok = pl.BlockSpec((1, tm, d), lambda bi, i: (bi, i, 0))
    dils = [dil for _, dil in DIL_PAIRS]
    return pl.pallas_call(
        _out0_kernel,
        grid=(b, s // tm),
        in_specs=[
            pl.BlockSpec((N_SLABS, tm, SLAB), lambda bi, i: (0, bi * tiles + i, 0)),
            pl.BlockSpec((N_SLABS, tm, SLAB), lambda bi, i: (gate_block, bi * tiles + i, 0)),
            tok,
            pl.BlockSpec((d, d), lambda bi, i: (0, 0)),
            pl.BlockSpec((1, d), lambda bi, i: (0, 0)),
        ],
        out_specs=[tok] + [pl.BlockSpec((1, dil, tm // dil, d), lambda bi, i: (bi, 0, i, 0))
                           for dil in dils],
        out_shape=([jax.ShapeDtypeStruct((b, s, d), jnp.float32)]
                   + [jax.ShapeDtypeStruct((b, dil, s // dil, d), jnp.bfloat16) for dil in dils]),
        scratch_shapes=[pltpu.VMEM((N_LANE_SLABS, tm, LANES), jnp.float32)],
        compiler_params=pltpu.CompilerParams(
            dimension_semantics=("parallel", "parallel"), vmem_limit_bytes=VMEM_LIMIT),
        name="out_proj0",
    )(o, proj, x, w_out, g_next.reshape(1, d))


def _natural_order(ref, slab_ref, dil):
    _, n, c = ref.shape
    if dil == 1:
        return ref[0].astype(jnp.float32)
    pieces = []
    for s in range(c // LANES):
        for r in range(dil):
            slab_ref[s, pl.ds(r, n, stride=dil), :] = (
                ref[r, :, s * LANES:(s + 1) * LANES].astype(jnp.float32))
        pieces.append(slab_ref[s])
    return pieces[0] if len(pieces) == 1 else jnp.concatenate(pieces, axis=1)


def _out1_kernel(o0_ref, o1_ref, o2_ref, l0_ref, l1_ref, l2_ref, gate_ref, x_ref,
                 w_ref, e_ref, g_ref, y_ref, slab_ref):
    dils = [dil for _, dil in DIL_PAIRS]
    lses = [_natural_order(ref.at[0], slab_ref, dil)
            for ref, dil in zip((l0_ref, l1_ref, l2_ref), dils)]
    m = jnp.maximum(jnp.maximum(lses[0], lses[1]), lses[2])
    es = [jnp.exp2(v - m) for v in lses]
    denom = es[0] + es[1] + es[2]
    o = None
    for e, o_ref, dil in zip(es, (o0_ref, o1_ref, o2_ref), dils):
        w = e / denom
        hi = w.astype(jnp.bfloat16)
        lo = (w - hi.astype(jnp.float32)).astype(jnp.bfloat16)
        w_full = jnp.dot(jnp.concatenate([hi, lo], axis=1), e_ref[...],
                         preferred_element_type=jnp.float32)
        term = w_full * _natural_order(o_ref.at[0], slab_ref, dil)
        o = term if o is None else o + term
    gate = gate_ref[0].astype(jnp.float32)
    z = (o * _silu(gate)).astype(jnp.bfloat16)
    x = x_ref[0] + jnp.dot(z, w_ref[...], preferred_element_type=jnp.float32)
    y_ref[0] = _rmsnorm(x, g_ref[...])


def _head_expansion():
    e = np.zeros((LSE_LANES, D_MODEL), np.float32)
    for h in range(N_HEADS):
        e[h, h * HEAD_DIM:(h + 1) * HEAD_DIM] = 1.0
    return jnp.asarray(np.concatenate([e, e], axis=0), dtype=jnp.bfloat16)


def _out_proj1(os_, lses, proj, x, w_out, norm_f, *, tm=512):
    b, s, d = x.shape
    gate_block = proj.shape[2] // d - 1
    dils = [dil for _, dil in DIL_PAIRS]

    def grouped(width, dil):
        return pl.BlockSpec((1, dil, tm // dil, width), lambda bi, i: (bi, 0, i, 0))

    tok = pl.BlockSpec((1, tm, d), lambda bi, i: (bi, i, 0))
    return pl.pallas_call(
        _out1_kernel,
        grid=(b, s // tm),
        in_specs=(
            [grouped(d, dil) for dil in dils]
            + [grouped(LSE_LANES, dil) for dil in dils]
            + [pl.BlockSpec((1, tm, d), lambda bi, i: (bi, i, gate_block)),
               tok,
               pl.BlockSpec((d, d), lambda bi, i: (0, 0)),
               pl.BlockSpec((2 * LSE_LANES, d), lambda bi, i: (0, 0)),
               pl.BlockSpec((1, d), lambda bi, i: (0, 0))]),
        out_specs=tok,
        out_shape=jax.ShapeDtypeStruct((b, s, d), jnp.float32),
        scratch_shapes=[pltpu.VMEM((N_LANE_SLABS, tm, LANES), jnp.float32)],
        compiler_params=pltpu.CompilerParams(
            dimension_semantics=("parallel", "parallel"), vmem_limit_bytes=VMEM_LIMIT),
        name="out_proj1",
    )(*os_, *lses, proj, x, w_out, _head_expansion(), norm_f.reshape(1, d))


def kernel(x, norm_0, w_in_0, rpb_0, w_out_0, norm_1, w_in_1, w_out_1, norm_f):
    b, s, d = x.shape
    t = b * s

    proj0 = _norm_proj(x.reshape(t, d), norm_0, w_in_0)
    o0 = _na_attention(proj0, _na_bias_table(rpb_0), b)
    x1, *h1 = _out_proj0(o0, proj0, x, w_out_0.astype(jnp.bfloat16), norm_1)

    gate_block = 3 * N_DIL_GROUPS
    outs, lses, proj_gate = [], [], None
    for g, (window, dil) in enumerate(DIL_PAIRS):
        assert window // (2 * dil) == DIL_RADIUS
        blocks = (3 * g, 3 * g + 1, 3 * g + 2) + ((gate_block,) if dil == 1 else ())
        qkv = _proj(h1[g].reshape(t, d), w_in_1, blocks).reshape(b, dil, s // dil, -1)
        if dil == 1:
            proj_gate = qkv.reshape(b, s, -1)
        o_g, lse_g = _dil_attention(qkv, dil)
        outs.append(o_g)
        lses.append(lse_g)
    return _out_proj1(outs, lses, proj_gate, x1, w_out_1.astype(jnp.bfloat16), norm_f)
```

```python
import functools
import math

import numpy as np
import jax
import jax.numpy as jnp
from jax import lax
from jax.experimental import pallas as pl
from jax.experimental.pallas import tpu as pltpu

D_MODEL = 1024
HEAD_DIM = 64
N_HEADS = 16
GRID_W = 64
NA_ROWS = 8
NA_COLS = 16
DIL_PAIRS = ((128, 1), (512, 4), (2048, 16))
N_DIL_GROUPS = len(DIL_PAIRS)
RMS_EPS = 1e-6
NEG_INF = -1e30
LOG2E = math.log2(math.e)

LANES = 128
N_LANE_SLABS = D_MODEL // LANES
HEADS_PER_STEP = 4
SLAB = HEADS_PER_STEP * HEAD_DIM
N_SLABS = D_MODEL // SLAB
Q_TILE = 128
NA_Q_TILE = GRID_W
NA_WIN = NA_ROWS * GRID_W
NA_TILES_PER_STEP = 32
DIL_TILES_PER_STEP = 8
DIL_RADIUS = 64
DIL_WIN = Q_TILE + 2 * DIL_RADIUS
LSE_LANES = LANES
VMEM_LIMIT = 56 * 1024 * 1024

_NT_DIMS = (((1,), (1,)), ((), ()))


def _rmsnorm(x, g):
    ms = jnp.mean(x * x, axis=-1, keepdims=True)
    return x * lax.rsqrt(ms + RMS_EPS) * g


def _q_scaled_bf16(w_ref):
    scale = jnp.where(pl.program_id(1) == 0, LOG2E / math.sqrt(HEAD_DIM), 1.0)
    return (w_ref[...] * scale).astype(jnp.bfloat16)


def _norm_proj_kernel(x_ref, g_ref, w_ref, o_ref, h_ref):
    @pl.when(pl.program_id(1) == 0)
    def _():
        h_ref[...] = _rmsnorm(x_ref[...], g_ref[...]).astype(h_ref.dtype)

    res = jnp.dot(h_ref[...], _q_scaled_bf16(w_ref),
                  preferred_element_type=jnp.float32).astype(o_ref.dtype)
    for k in range(o_ref.shape[0]):
        o_ref[k] = res[:, k * SLAB:(k + 1) * SLAB]


def _norm_proj(x, g, w, *, tm=2048, tn=1024):
    t, d = x.shape
    n = w.shape[1]
    return pl.pallas_call(
        _norm_proj_kernel,
        grid=(t // tm, n // tn),
        in_specs=[
            pl.BlockSpec((tm, d), lambda i, j: (i, 0)),
            pl.BlockSpec((1, d), lambda i, j: (0, 0)),
            pl.BlockSpec((d, tn), lambda i, j: (0, j)),
        ],
        out_specs=pl.BlockSpec((tn // SLAB, tm, SLAB), lambda i, j: (j, i, 0)),
        out_shape=jax.ShapeDtypeStruct((n // SLAB, t, SLAB), jnp.bfloat16),
        scratch_shapes=[pltpu.VMEM((tm, d), jnp.bfloat16)],
        compiler_params=pltpu.CompilerParams(
            dimension_semantics=("parallel", "arbitrary"),
            vmem_limit_bytes=VMEM_LIMIT),
        name="norm_proj",
    )(x, g.reshape(1, d), w)


def _proj_kernel(h_ref, w_ref, o_ref):
    o_ref[...] = jnp.dot(h_ref[...], _q_scaled_bf16(w_ref),
                         preferred_element_type=jnp.float32).astype(o_ref.dtype)


def _proj(h, w, col_blocks, *, tm=2048):
    t, d = h.shape
    tn = d
    n = len(col_blocks) * tn

    def w_block(j):
        blk = col_blocks[-1]
        for k in range(len(col_blocks) - 2, -1, -1):
            blk = jnp.where(j == k, col_blocks[k], blk)
        return blk

    return pl.pallas_call(
        _proj_kernel,
        grid=(t // tm, n // tn),
        in_specs=[
            pl.BlockSpec((tm, d), lambda i, j: (i, 0)),
            pl.BlockSpec((d, tn), lambda i, j: (0, w_block(j))),
        ],
        out_specs=pl.BlockSpec((tm, tn), lambda i, j: (i, j)),
        out_shape=jax.ShapeDtypeStruct((t, n), jnp.bfloat16),
        compiler_params=pltpu.CompilerParams(
            dimension_semantics=("parallel", "parallel"),
            vmem_limit_bytes=VMEM_LIMIT),
        name="proj",
    )(h, w)


def _na_kernel(q_ref, k_ref, v_ref, b_ref, o_ref):
    step = pl.program_id(2)
    n_tiles = k_ref.shape[1] // GRID_W
    lane = lax.broadcasted_iota(jnp.int32, (NA_Q_TILE, SLAB), 1) // HEAD_DIM

    for u in range(NA_TILES_PER_STEP):
        r = step * NA_TILES_PER_STEP + u
        first_row = jnp.clip(r - NA_ROWS // 2, 0, n_tiles - NA_ROWS)
        start = pl.multiple_of(first_row * GRID_W, GRID_W)
        tile_rows = slice(u * NA_Q_TILE, (u + 1) * NA_Q_TILE)
        q4 = q_ref[0, tile_rows, :]
        zero = jnp.zeros_like(q4)
        q_stack = jnp.concatenate(
            [jnp.where(lane == h, q4, zero) for h in range(HEADS_PER_STEP)], axis=0)
        s = lax.dot_general(q_stack, k_ref[0, pl.ds(start, NA_WIN), :], _NT_DIMS,
                            preferred_element_type=jnp.float32)
        first_off = first_row - r + NA_ROWS - 1
        s = s + jnp.concatenate(
            [jnp.concatenate([b_ref[h, first_off + 2 * j] for j in range(NA_ROWS // 2)], axis=1)
             for h in range(HEADS_PER_STEP)], axis=0)
        m = jnp.max(s, axis=-1, keepdims=True)
        e = jnp.exp2(s - m)
        inv_l = 1.0 / jnp.sum(e, axis=-1, keepdims=True)
        o_all = jnp.dot(e.astype(jnp.bfloat16), v_ref[0, pl.ds(start, NA_WIN), :],
                        preferred_element_type=jnp.float32) * inv_l
        out = o_all[:NA_Q_TILE]
        for h in range(1, HEADS_PER_STEP):
            out = jnp.where(lane == h, o_all[h * NA_Q_TILE:(h + 1) * NA_Q_TILE], out)
        o_ref[0, tile_rows, :] = out.astype(o_ref.dtype)


def _na_bias_table(rpb):
    n_col_off = 2 * NA_COLS - 1
    c = np.arange(GRID_W)[:, None]
    kc = np.arange(GRID_W)[None, :]
    cs = np.clip(c - NA_COLS // 2, 0, GRID_W - NA_COLS)
    col_ok = (kc >= cs) & (kc < cs + NA_COLS)
    col_off = kc - c + NA_COLS - 1
    assert np.all(((col_off >= 0) & (col_off < n_col_off))[col_ok])
    pick = (np.arange(n_col_off)[:, None, None] == col_off[None]) & col_ok[None]
    col_part = jnp.einsum('hrm,mck->hrck', rpb.astype(jnp.float32),
                          jnp.asarray(pick, dtype=jnp.float32), precision=lax.Precision.HIGHEST)
    col_part = jnp.where(jnp.asarray(col_ok), col_part * LOG2E, NEG_INF)
    return jnp.concatenate([col_part[:, :-1], col_part[:, 1:]], axis=-1)


def _na_attention(proj, bias_table, b):
    s = proj.shape[1] // b
    tile = NA_Q_TILE * NA_TILES_PER_STEP
    assert s % tile == 0 and NA_Q_TILE == GRID_W
    n_blocks = s // tile
    return pl.pallas_call(
        _na_kernel,
        grid=(b, N_SLABS, n_blocks),
        in_specs=[
            pl.BlockSpec((1, tile, SLAB), lambda bi, g, j: (g, bi * n_blocks + j, 0)),
            pl.BlockSpec((1, s, SLAB), lambda bi, g, j: (N_SLABS + g, bi, 0)),
            pl.BlockSpec((1, s, SLAB), lambda bi, g, j: (2 * N_SLABS + g, bi, 0)),
            pl.BlockSpec((HEADS_PER_STEP,) + bias_table.shape[1:], lambda bi, g, j: (g, 0, 0, 0)),
        ],
        out_specs=pl.BlockSpec((1, tile, SLAB), lambda bi, g, j: (g, bi * n_blocks + j, 0)),
        out_shape=jax.ShapeDtypeStruct((N_SLABS, b * s, SLAB), jnp.bfloat16),
        compiler_params=pltpu.CompilerParams(
            dimension_semantics=("parallel", "parallel", "arbitrary"),
            vmem_limit_bytes=VMEM_LIMIT),
        name="na_attention",
    )(proj, proj, proj, bias_table)


def _alibi_slopes():
    return np.asarray(2.0 ** (-8.0 * (np.arange(N_HEADS) + 1) / N_HEADS), dtype=np.float32)


def _slab_attention(q4, k4, v4, bias):
    lane = lax.broadcasted_iota(jnp.int32, q4.shape, 1) // HEAD_DIM
    zero = jnp.zeros_like(q4)
    q_stack = jnp.concatenate(
        [jnp.where(lane == h, q4, zero) for h in range(HEADS_PER_STEP)], axis=0)
    s = lax.dot_general(q_stack, k4, _NT_DIMS, preferred_element_type=jnp.float32)
    s = s + bias
    m = jnp.max(s, axis=-1, keepdims=True)
    p = jnp.exp2(s - m)
    l = jnp.sum(p, axis=-1, keepdims=True)
    p = p.astype(jnp.bfloat16)
    inv_l = 1.0 / l
    out = None
    for h in range(HEADS_PER_STEP):
        rows = slice(h * Q_TILE, (h + 1) * Q_TILE)
        o_h = jnp.dot(p[rows], v4, preferred_element_type=jnp.float32) * inv_l[rows]
        out = o_h if out is None else jnp.where(lane == h, o_h, out)
    return out, m, l


def _dil_kernel(q_ref, ka_ref, kb_ref, kc_ref, va_ref, vb_ref, vc_ref, b_ref,
                o_ref, lse_ref):
    step = pl.program_id(2)
    n_steps = pl.num_programs(2)
    k_win = jnp.concatenate([ka_ref[0, 0], kb_ref[0, 0], kc_ref[0, 0]], axis=0)
    v_win = jnp.concatenate([va_ref[0, 0], vb_ref[0, 0], vc_ref[0, 0]], axis=0)
    lane = lax.broadcasted_iota(jnp.int32, (Q_TILE, LSE_LANES), 1)
    tiles = q_ref.shape[2] // Q_TILE
    for u in range(tiles):
        pattern = 0
        if u == 0:
            pattern = jnp.where(step == 0, 1, pattern)
        if u == tiles - 1:
            pattern = jnp.where(step == n_steps - 1, 2, pattern)
        tile = slice(u * Q_TILE, (u + 1) * Q_TILE)
        win = slice(u * Q_TILE, u * Q_TILE + DIL_WIN)
        lse_tile = jnp.zeros((Q_TILE, LSE_LANES), jnp.float32)
        for hg in range(N_SLABS):
            cols = slice(hg * SLAB, (hg + 1) * SLAB)
            bias = b_ref[pattern, hg * HEADS_PER_STEP:(hg + 1) * HEADS_PER_STEP]
            bias = bias.reshape(HEADS_PER_STEP * Q_TILE, DIL_WIN)
            out, m, l = _slab_attention(q_ref[0, 0, tile, cols], k_win[win, cols],
                                        v_win[win, cols], bias)
            o_ref[0, 0, tile, cols] = out.astype(o_ref.dtype)
            lse = m + jnp.log2(l)
            for h in range(HEADS_PER_STEP):
                lse_h = lse[h * Q_TILE:(h + 1) * Q_TILE]
                lse_tile = jnp.where(lane == hg * HEADS_PER_STEP + h, lse_h, lse_tile)
        lse_ref[0, 0, tile, :] = lse_tile


def _dil_bias_table(dil):
    qi = np.arange(Q_TILE)[:, None]
    kj = np.arange(DIL_WIN)[None, :]
    delta = kj - DIL_RADIUS - qi
    in_band = np.abs(delta) <= DIL_RADIUS
    valid = np.stack([in_band,
                      in_band & (kj >= DIL_RADIUS),
                      in_band & (kj < DIL_WIN - DIL_RADIUS)])
    dist = jnp.asarray(np.abs(delta) * dil, dtype=jnp.float32)
    bias = -jnp.asarray(_alibi_slopes())[:, None, None] * dist[None]
    return jnp.where(jnp.asarray(valid)[:, None], bias[None] * LOG2E, NEG_INF)


def _dil_attention(qkv, dil):
    b, _, l, _ = qkv.shape
    tiles = min(DIL_TILES_PER_STEP, l // Q_TILE)
    tile = Q_TILE * tiles
    n_steps = l // tile
    assert l % tile == 0 and tiles >= 2
    half = DIL_RADIUS
    per_tile = tile // half
    n_half = l // half

    def spec_mid(which):
        return pl.BlockSpec((1, 1, tile, D_MODEL), lambda bi, r, i: (bi, r, i, which))

    def spec_lo(which):
        return pl.BlockSpec((1, 1, half, D_MODEL),
                            lambda bi, r, i: (bi, r, jnp.maximum(per_tile * i - 1, 0), which))

    def spec_hi(which):
        return pl.BlockSpec((1, 1, half, D_MODEL),
                            lambda bi, r, i: (bi, r, jnp.minimum(per_tile * (i + 1), n_half - 1), which))

    bias = _dil_bias_table(dil)
    return pl.pallas_call(
        _dil_kernel,
        grid=(b, dil, n_steps),
        in_specs=[
            spec_mid(0),
            spec_lo(1), spec_mid(1), spec_hi(1),
            spec_lo(2), spec_mid(2), spec_hi(2),
            pl.BlockSpec(bias.shape, lambda bi, r, i: (0, 0, 0, 0)),
        ],
        out_specs=[
            pl.BlockSpec((1, 1, tile, D_MODEL), lambda bi, r, i: (bi, r, i, 0)),
            pl.BlockSpec((1, 1, tile, LSE_LANES), lambda bi, r, i: (bi, r, i, 0)),
        ],
        out_shape=[
            jax.ShapeDtypeStruct((b, dil, l, D_MODEL), jnp.bfloat16),
            jax.ShapeDtypeStruct((b, dil, l, LSE_LANES), jnp.float32),
        ],
        compiler_params=pltpu.CompilerParams(
            dimension_semantics=("parallel", "parallel", "arbitrary"),
            vmem_limit_bytes=VMEM_LIMIT),
        name=f"dilated_attention_{dil}",
    )(qkv, qkv, qkv, qkv, qkv, qkv, qkv, bias)


def _silu(x):
    half = 0.5 * x
    return half + half * jnp.tanh(half)


def _out0_kernel(o_ref, gate_ref, x_ref, w_ref, g_ref, y_ref, *rest):
    h_refs, slab_ref = rest[:-1], rest[-1]
    o = jnp.concatenate([o_ref[k] for k in range(N_SLABS)], axis=1).astype(jnp.float32)
    gate = jnp.concatenate([gate_ref[k] for k in range(N_SLABS)], axis=1).astype(jnp.float32)
    z = (o * _silu(gate)).astype(jnp.bfloat16)
    y = x_ref[0] + jnp.dot(z, w_ref[...], preferred_element_type=jnp.float32)
    y_ref[0] = y
    hn = _rmsnorm(y, g_ref[...])
    tm = hn.shape[0]
    staged = False
    for h_ref, (_, dil) in zip(h_refs, DIL_PAIRS):
        if dil == 1:
            h_ref[0, 0] = hn.astype(h_ref.dtype)
            continue
        if not staged:
            for s in range(N_LANE_SLABS):
                slab_ref[s] = hn[:, s * LANES:(s + 1) * LANES]
            staged = True
        n = tm // dil
        for r in range(dil):
            for s in range(N_LANE_SLABS):
                rows = slab_ref[s, pl.ds(r, n, stride=dil), :]
                h_ref[0, r, :, s * LANES:(s + 1) * LANES] = rows.astype(h_ref.dtype)


def _out_proj0(o, proj, x, w_out, g_next, *, tm=512):
    b, s, d = x.shape
    gate_block = proj.shape[0] // N_SLABS - 1
    tiles = s // tm
    tok = pl.BlockSpec((1, tm, d), lambda bi, i: (bi, i, 0))
    dils = [dil for _, dil in DIL_PAIRS]
    return pl.pallas_call(
        _out0_kernel,
        grid=(b, s // tm),
        in_specs=[
            pl.BlockSpec((N_SLABS, tm, SLAB), lambda bi, i: (0, bi * tiles + i, 0)),
            pl.BlockSpec((N_SLABS, tm, SLAB), lambda bi, i: (gate_block, bi * tiles + i, 0)),
            tok,
            pl.BlockSpec((d, d), lambda bi, i: (0, 0)),
            pl.BlockSpec((1, d), lambda bi, i: (0, 0)),
        ],
        out_specs=[tok] + [pl.BlockSpec((1, dil, tm // dil, d), lambda bi, i: (bi, 0, i, 0))
                           for dil in dils],
        out_shape=([jax.ShapeDtypeStruct((b, s, d), jnp.float32)]
                   + [jax.ShapeDtypeStruct((b, dil, s // dil, d), jnp.bfloat16) for dil in dils]),
        scratch_shapes=[pltpu.VMEM((N_LANE_SLABS, tm, LANES), jnp.float32)],
        compiler_params=pltpu.CompilerParams(
            dimension_semantics=("parallel", "parallel"), vmem_limit_bytes=VMEM_LIMIT),
        name="out_proj0",
    )(o, proj, x, w_out, g_next.reshape(1, d))


def _natural_order(ref, slab_ref, dil):
    _, n, c = ref.shape
    if dil == 1:
        return ref[0].astype(jnp.float32)
    pieces = []
    for s in range(c // LANES):
        for r in range(dil):
            slab_ref[s, pl.ds(r, n, stride=dil), :] = (
                ref[r, :, s * LANES:(s + 1) * LANES].astype(jnp.float32))
        pieces.append(slab_ref[s])
    return pieces[0] if len(pieces) == 1 else jnp.concatenate(pieces, axis=1)


def _out1_kernel(o0_ref, o1_ref, o2_ref, l0_ref, l1_ref, l2_ref, gate_ref, x_ref,
                 w_ref, e_ref, g_ref, y_ref, slab_ref):
    dils = [dil for _, dil in DIL_PAIRS]
    lses = [_natural_order(ref.at[0], slab_ref, dil)
            for ref, dil in zip((l0_ref, l1_ref, l2_ref), dils)]
    m = jnp.maximum(jnp.maximum(lses[0], lses[1]), lses[2])
    es = [jnp.exp2(v - m) for v in lses]
    denom = es[0] + es[1] + es[2]
    o = None
    for e, o_ref, dil in zip(es, (o0_ref, o1_ref, o2_ref), dils):
        w = e / denom
        hi = w.astype(jnp.bfloat16)
        lo = (w - hi.astype(jnp.float32)).astype(jnp.bfloat16)
        w_full = jnp.dot(jnp.concatenate([hi, lo], axis=1), e_ref[...],
                         preferred_element_type=jnp.float32)
        term = w_full * _natural_order(o_ref.at[0], slab_ref, dil)
        o = term if o is None else o + term
    gate = gate_ref[0].astype(jnp.float32)
    z = (o * _silu(gate)).astype(jnp.bfloat16)
    x = x_ref[0] + jnp.dot(z, w_ref[...], preferred_element_type=jnp.float32)
    y_ref[0] = _rmsnorm(x, g_ref[...])


def _head_expansion():
    e = np.zeros((LSE_LANES, D_MODEL), np.float32)
    for h in range(N_HEADS):
        e[h, h * HEAD_DIM:(h + 1) * HEAD_DIM] = 1.0
    return jnp.asarray(np.concatenate([e, e], axis=0), dtype=jnp.bfloat16)


def _out_proj1(os_, lses, proj, x, w_out, norm_f, *, tm=512):
    b, s, d = x.shape
    gate_block = proj.shape[2] // d - 1
    dils = [dil for _, dil in DIL_PAIRS]

    def grouped(width, dil):
        return pl.BlockSpec((1, dil, tm // dil, width), lambda bi, i: (bi, 0, i, 0))

    tok = pl.BlockSpec((1, tm, d), lambda bi, i: (bi, i, 0))
    return pl.pallas_call(
        _out1_kernel,
        grid=(b, s // tm),
        in_specs=(
            [grouped(d, dil) for dil in dils]
            + [grouped(LSE_LANES, dil) for dil in dils]
            + [pl.BlockSpec((1, tm, d), lambda bi, i: (bi, i, gate_block)),
               tok,
               pl.BlockSpec((d, d), lambda bi, i: (0, 0)),
               pl.BlockSpec((2 * LSE_LANES, d), lambda bi, i: (0, 0)),
               pl.BlockSpec((1, d), lambda bi, i: (0, 0))]),
        out_specs=tok,
        out_shape=jax.ShapeDtypeStruct((b, s, d), jnp.float32),
        scratch_shapes=[pltpu.VMEM((N_LANE_SLABS, tm, LANES), jnp.float32)],
        compiler_params=pltpu.CompilerParams(
            dimension_semantics=("parallel", "parallel"), vmem_limit_bytes=VMEM_LIMIT),
        name="out_proj1",
    )(*os_, *lses, proj, x, w_out, _head_expansion(), norm_f.reshape(1, d))


def kernel(x, norm_0, w_in_0, rpb_0, w_out_0, norm_1, w_in_1, w_out_1, norm_f):
    b, s, d = x.shape
    t = b * s

    proj0 = _norm_proj(x.reshape(t, d), norm_0, w_in_0)
    o0 = _na_attention(proj0, _na_bias_table(rpb_0), b)
    x1, *h1 = _out_proj0(o0, proj0, x, w_out_0.astype(jnp.bfloat16), norm_1)

    gate_block = 3 * N_DIL_GROUPS
    outs, lses, proj_gate = [], [], None
    for g, (window, dil) in enumerate(DIL_PAIRS):
        assert window // (2 * dil) == DIL_RADIUS
        blocks = (3 * g, 3 * g + 1, 3 * g + 2) + ((gate_block,) if dil == 1 else ())
        qkv = _proj(h1[g].reshape(t, d), w_in_1, blocks).reshape(b, dil, s // dil, -1)
        if dil == 1:
            proj_gate = qkv.reshape(b, s, -1)
        o_g, lse_g = _dil_attention(qkv, dil)
        outs.append(o_g)
        lses.append(lse_g)
    return _out_proj1(outs, lses, proj_gate, x1, w_out_1.astype(jnp.bfloat16), norm_f)
```

```python
import functools
import math

import numpy as np
import jax
import jax.numpy as jnp
from jax import lax
from jax.experimental import pallas as pl
from jax.experimental.pallas import tpu as pltpu

D_MODEL = 1024
HEAD_DIM = 64
N_HEADS = 16
GRID_W = 64
NA_ROWS = 8
NA_COLS = 16
DIL_PAIRS = ((128, 1), (512, 4), (2048, 16))
N_DIL_GROUPS = len(DIL_PAIRS)
RMS_EPS = 1e-6
NEG_INF = -1e30
LOG2E = math.log2(math.e)

LANES = 128
N_LANE_SLABS = D_MODEL // LANES
HEADS_PER_STEP = 4
SLAB = HEADS_PER_STEP * HEAD_DIM
N_SLABS = D_MODEL // SLAB
Q_TILE = 128
NA_Q_TILE = GRID_W
NA_WIN = NA_ROWS * GRID_W
NA_TILES_PER_STEP = 32
DIL_TILES_PER_STEP = 8
DIL_RADIUS = 64
DIL_WIN = Q_TILE + 2 * DIL_RADIUS
LSE_LANES = LANES
VMEM_LIMIT = 56 * 1024 * 1024

_NT_DIMS = (((1,), (1,)), ((), ()))


def _rmsnorm(x, g):
    ms = jnp.mean(x * x, axis=-1, keepdims=True)
    return x * lax.rsqrt(ms + RMS_EPS) * g


def _q_scaled_bf16(w_ref):
    scale = jnp.where(pl.program_id(1) == 0, LOG2E / math.sqrt(HEAD_DIM), 1.0)
    return (w_ref[...] * scale).astype(jnp.bfloat16)


def _norm_proj_kernel(x_ref, g_ref, w_ref, o_ref, h_ref):
    @pl.when(pl.program_id(1) == 0)
    def _():
        h_ref[...] = _rmsnorm(x_ref[...], g_ref[...]).astype(h_ref.dtype)

    res = jnp.dot(h_ref[...], _q_scaled_bf16(w_ref),
                  preferred_element_type=jnp.float32).astype(o_ref.dtype)
    for k in range(o_ref.shape[0]):
        o_ref[k] = res[:, k * SLAB:(k + 1) * SLAB]


def _norm_proj(x, g, w, *, tm=2048, tn=1024):
    t, d = x.shape
    n = w.shape[1]
    return pl.pallas_call(
        _norm_proj_kernel,
        grid=(t // tm, n // tn),
        in_specs=[
            pl.BlockSpec((tm, d), lambda i, j: (i, 0)),
            pl.BlockSpec((1, d), lambda i, j: (0, 0)),
            pl.BlockSpec((d, tn), lambda i, j: (0, j)),
        ],
        out_specs=pl.BlockSpec((tn // SLAB, tm, SLAB), lambda i, j: (j, i, 0)),
        out_shape=jax.ShapeDtypeStruct((n // SLAB, t, SLAB), jnp.bfloat16),
        scratch_shapes=[pltpu.VMEM((tm, d), jnp.bfloat16)],
        compiler_params=pltpu.CompilerParams(
            dimension_semantics=("parallel", "arbitrary"),
            vmem_limit_bytes=VMEM_LIMIT),
        name="norm_proj",
    )(x, g.reshape(1, d), w)


def _proj_kernel(h_ref, w_ref, o_ref):
    o_ref[...] = jnp.dot(h_ref[...], _q_scaled_bf16(w_ref),
                         preferred_element_type=jnp.float32).astype(o_ref.dtype)


def _proj(h, w, col_blocks, *, tm=2048):
    t, d = h.shape
    tn = d
    n = len(col_blocks) * tn

    def w_block(j):
        blk = col_blocks[-1]
        for k in range(len(col_blocks) - 2, -1, -1):
            blk = jnp.where(j == k, col_blocks[k], blk)
        return blk

    return pl.pallas_call(
        _proj_kernel,
        grid=(t // tm, n // tn),
        in_specs=[
            pl.BlockSpec((tm, d), lambda i, j: (i, 0)),
            pl.BlockSpec((d, tn), lambda i, j: (0, w_block(j))),
        ],
        out_specs=pl.BlockSpec((tm, tn), lambda i, j: (i, j)),
        out_shape=jax.ShapeDtypeStruct((t, n), jnp.bfloat16),
        compiler_params=pltpu.CompilerParams(
            dimension_semantics=("parallel", "parallel"),
            vmem_limit_bytes=VMEM_LIMIT),
        name="proj",
    )(h, w)


def _na_kernel(q_ref, k_ref, v_ref, b_ref, o_ref):
    step = pl.program_id(2)
    n_tiles = k_ref.shape[1] // GRID_W
    lane = lax.broadcasted_iota(jnp.int32, (NA_Q_TILE, SLAB), 1) // HEAD_DIM

    for u in range(NA_TILES_PER_STEP):
        r = step * NA_TILES_PER_STEP + u
        first_row = jnp.clip(r - NA_ROWS // 2, 0, n_tiles - NA_ROWS)
        start = pl.multiple_of(first_row * GRID_W, GRID_W)
        tile_rows = slice(u * NA_Q_TILE, (u + 1) * NA_Q_TILE)
        q4 = q_ref[0, tile_rows, :]
        zero = jnp.zeros_like(q4)
        q_stack = jnp.concatenate(
            [jnp.where(lane == h, q4, zero) for h in range(HEADS_PER_STEP)], axis=0)
        s = lax.dot_general(q_stack, k_ref[0, pl.ds(start, NA_WIN), :], _NT_DIMS,
                            preferred_element_type=jnp.float32)
        first_off = first_row - r + NA_ROWS - 1
        s = s + jnp.concatenate(
            [jnp.concatenate([b_ref[h, first_off + 2 * j] for j in range(NA_ROWS // 2)], axis=1)
             for h in range(HEADS_PER_STEP)], axis=0)
        m = jnp.max(s, axis=-1, keepdims=True)
        e = jnp.exp2(s - m)
        inv_l = 1.0 / jnp.sum(e, axis=-1, keepdims=True)
        o_all = jnp.dot(e.astype(jnp.bfloat16), v_ref[0, pl.ds(start, NA_WIN), :],
                        preferred_element_type=jnp.float32) * inv_l
        out = o_all[:NA_Q_TILE]
        for h in range(1, HEADS_PER_STEP):
            out = jnp.where(lane == h, o_all[h * NA_Q_TILE:(h + 1) * NA_Q_TILE], out)
        o_ref[0, tile_rows, :] = out.astype(o_ref.dtype)


def _na_bias_table(rpb):
    n_col_off = 2 * NA_COLS - 1
    c = np.arange(GRID_W)[:, None]
    kc = np.arange(GRID_W)[None, :]
    cs = np.clip(c - NA_COLS // 2, 0, GRID_W - NA_COLS)
    col_ok = (kc >= cs) & (kc < cs + NA_COLS)
    col_off = kc - c + NA_COLS - 1
    assert np.all(((col_off >= 0) & (col_off < n_col_off))[col_ok])
    pick = (np.arange(n_col_off)[:, None, None] == col_off[None]) & col_ok[None]
    pick2 = np.zeros((2, n_col_off, GRID_W, 2, GRID_W), np.float32)
    for half in range(2):
        pick2[half, :, :, half, :] = pick
    pick2 = pick2.reshape(2 * n_col_off, GRID_W, 2 * GRID_W)
    rpb = rpb.astype(jnp.float32)
    row_pairs = jnp.concatenate([rpb[:, :-1], rpb[:, 1:]], axis=-1)
    table = jnp.einsum('hrm,mck->hrck', row_pairs, jnp.asarray(pick2),
                       precision=lax.Precision.HIGHEST)
    ok2 = np.concatenate([col_ok, col_ok], axis=1)
    return jnp.where(jnp.asarray(ok2), table * LOG2E, NEG_INF)


def _na_attention(proj, bias_table, b):
    s = proj.shape[1] // b
    tile = NA_Q_TILE * NA_TILES_PER_STEP
    assert s % tile == 0 and NA_Q_TILE == GRID_W
    n_blocks = s // tile
    return pl.pallas_call(
        _na_kernel,
        grid=(b, N_SLABS, n_blocks),
        in_specs=[
            pl.BlockSpec((1, tile, SLAB), lambda bi, g, j: (g, bi * n_blocks + j, 0)),
            pl.BlockSpec((1, s, SLAB), lambda bi, g, j: (N_SLABS + g, bi, 0)),
            pl.BlockSpec((1, s, SLAB), lambda bi, g, j: (2 * N_SLABS + g, bi, 0)),
            pl.BlockSpec((HEADS_PER_STEP,) + bias_table.shape[1:], lambda bi, g, j: (g, 0, 0, 0)),
        ],
        out_specs=pl.BlockSpec((1, tile, SLAB), lambda bi, g, j: (g, bi * n_blocks + j, 0)),
        out_shape=jax.ShapeDtypeStruct((N_SLABS, b * s, SLAB), jnp.bfloat16),
        compiler_params=pltpu.CompilerParams(
            dimension_semantics=("parallel", "parallel", "arbitrary"),
            vmem_limit_bytes=VMEM_LIMIT),
        name="na_attention",
    )(proj, proj, proj, bias_table)


def _alibi_slopes():
    return np.asarray(2.0 ** (-8.0 * (np.arange(N_HEADS) + 1) / N_HEADS), dtype=np.float32)


def _slab_attention(q4, k4, v4, bias):
    lane = lax.broadcasted_iota(jnp.int32, q4.shape, 1) // HEAD_DIM
    zero = jnp.zeros_like(q4)
    q_stack = jnp.concatenate(
        [jnp.where(lane == h, q4, zero) for h in range(HEADS_PER_STEP)], axis=0)
    s = lax.dot_general(q_stack, k4, _NT_DIMS, preferred_element_type=jnp.float32)
    s = s + bias
    m = jnp.max(s, axis=-1, keepdims=True)
    p = jnp.exp2(s - m)
    l = jnp.sum(p, axis=-1, keepdims=True)
    p = p.astype(jnp.bfloat16)
    inv_l = 1.0 / l
    out = None
    for h in range(HEADS_PER_STEP):
        rows = slice(h * Q_TILE, (h + 1) * Q_TILE)
        o_h = jnp.dot(p[rows], v4, preferred_element_type=jnp.float32) * inv_l[rows]
        out = o_h if out is None else jnp.where(lane == h, o_h, out)
    return out, m, l


def _dil_kernel(q_ref, ka_ref, kb_ref, kc_ref, va_ref, vb_ref, vc_ref, b_ref,
                o_ref, lse_ref):
    step = pl.program_id(2)
    n_steps = pl.num_programs(2)
    k_win = jnp.concatenate([ka_ref[0, 0], kb_ref[0, 0], kc_ref[0, 0]], axis=0)
    v_win = jnp.concatenate([va_ref[0, 0], vb_ref[0, 0], vc_ref[0, 0]], axis=0)
    lane = lax.broadcasted_iota(jnp.int32, (Q_TILE, LSE_LANES), 1)
    tiles = q_ref.shape[2] // Q_TILE
    for u in range(tiles):
        pattern = 0
        if u == 0:
            pattern = jnp.where(step == 0, 1, pattern)
        if u == tiles - 1:
            pattern = jnp.where(step == n_steps - 1, 2, pattern)
        tile = slice(u * Q_TILE, (u + 1) * Q_TILE)
        win = slice(u * Q_TILE, u * Q_TILE + DIL_WIN)
        lse_tile = jnp.zeros((Q_TILE, LSE_LANES), jnp.float32)
        for hg in range(N_SLABS):
            cols = slice(hg * SLAB, (hg + 1) * SLAB)
            bias = b_ref[pattern, hg * HEADS_PER_STEP:(hg + 1) * HEADS_PER_STEP]
            bias = bias.reshape(HEADS_PER_STEP * Q_TILE, DIL_WIN)
            out, m, l = _slab_attention(q_ref[0, 0, tile, cols], k_win[win, cols],
                                        v_win[win, cols], bias)
            o_ref[0, 0, tile, cols] = out.astype(o_ref.dtype)
            lse = m + jnp.log2(l)
            for h in range(HEADS_PER_STEP):
                lse_h = lse[h * Q_TILE:(h + 1) * Q_TILE]
                lse_tile = jnp.where(lane == hg * HEADS_PER_STEP + h, lse_h, lse_tile)
        lse_ref[0, 0, tile, :] = lse_tile


def _dil_bias_tables():
    qi = np.arange(Q_TILE)[:, None]
    kj = np.arange(DIL_WIN)[None, :]
    delta = kj - DIL_RADIUS - qi
    in_band = np.abs(delta) <= DIL_RADIUS
    valid = np.stack([in_band,
                      in_band & (kj >= DIL_RADIUS),
                      in_band & (kj < DIL_WIN - DIL_RADIUS)])
    dils = np.asarray([dil for _, dil in DIL_PAIRS])
    dist = jnp.asarray(np.abs(delta)[None] * dils[:, None, None], dtype=jnp.float32)
    bias = -jnp.asarray(_alibi_slopes())[None, :, None, None] * dist[:, None]
    return jnp.where(jnp.asarray(valid)[None, :, None], bias[:, None] * LOG2E, NEG_INF)


def _dil_attention(qkv, bias_tables, group):
    b, dil, l, _ = qkv.shape
    tiles = min(DIL_TILES_PER_STEP, l // Q_TILE)
    tile = Q_TILE * tiles
    n_steps = l // tile
    assert l % tile == 0 and tiles >= 2
    half = DIL_RADIUS
    per_tile = tile // half
    n_half = l // half

    def spec_mid(which):
        return pl.BlockSpec((1, 1, tile, D_MODEL), lambda bi, r, i: (bi, r, i, which))

    def spec_lo(which):
        return pl.BlockSpec((1, 1, half, D_MODEL),
                            lambda bi, r, i: (bi, r, jnp.maximum(per_tile * i - 1, 0), which))

    def spec_hi(which):
        return pl.BlockSpec((1, 1, half, D_MODEL),
                            lambda bi, r, i: (bi, r, jnp.minimum(per_tile * (i + 1), n_half - 1), which))

    return pl.pallas_call(
        _dil_kernel,
        grid=(b, dil, n_steps),
        in_specs=[
            spec_mid(0),
            spec_lo(1), spec_mid(1), spec_hi(1),
            spec_lo(2), spec_mid(2), spec_hi(2),
            pl.BlockSpec((None,) + bias_tables.shape[1:], lambda bi, r, i: (group, 0, 0, 0, 0)),
        ],
        out_specs=[
            pl.BlockSpec((1, 1, tile, D_MODEL), lambda bi, r, i: (bi, r, i, 0)),
            pl.BlockSpec((1, 1, tile, LSE_LANES), lambda bi, r, i: (bi, r, i, 0)),
        ],
        out_shape=[
            jax.ShapeDtypeStruct((b, dil, l, D_MODEL), jnp.bfloat16),
            jax.ShapeDtypeStruct((b, dil, l, LSE_LANES), jnp.float32),
        ],
        compiler_params=pltpu.CompilerParams(
            dimension_semantics=("parallel", "parallel", "arbitrary"),
            vmem_limit_bytes=VMEM_LIMIT),
        name=f"dilated_attention_{dil}",
    )(qkv, qkv, qkv, qkv, qkv, qkv, qkv, bias_tables)


def _silu(x):
    half = 0.5 * x
    return half + half * jnp.tanh(half)


def _out0_kernel(o_ref, gate_ref, x_ref, w_ref, g_ref, y_ref, *rest):
    h_refs, slab_ref = rest[:-1], rest[-1]
    o = jnp.concatenate([o_ref[k] for k in range(N_SLABS)], axis=1).astype(jnp.float32)
    gate = jnp.concatenate([gate_ref[k] for k in range(N_SLABS)], axis=1).astype(jnp.float32)
    z = (o * _silu(gate)).astype(jnp.bfloat16)
    y = x_ref[0] + jnp.dot(z, w_ref[...], preferred_element_type=jnp.float32)
    y_ref[0] = y
    hn = _rmsnorm(y, g_ref[...])
    tm = hn.shape[0]
    staged = False
    for h_ref, (_, dil) in zip(h_refs, DIL_PAIRS):
        if dil == 1:
            h_ref[0, 0] = hn.astype(h_ref.dtype)
            continue
        if not staged:
            for s in range(N_LANE_SLABS):
                slab_ref[s] = hn[:, s * LANES:(s + 1) * LANES]
            staged = True
        n = tm // dil
        for r in range(dil):
            for s in range(N_LANE_SLABS):
                rows = slab_ref[s, pl.ds(r, n, stride=dil), :]
                h_ref[0, r, :, s * LANES:(s + 1) * LANES] = rows.astype(h_ref.dtype)


def _out_proj0(o, proj, x, w_out, g_next, *, tm=512):
    b, s, d = x.shape
    gate_block = proj.shape[0] // N_SLABS - 1
    tiles = s // tm
    tok = pl.BlockSpec((1, tm, d), lambda bi, i: (bi, i, 0))
    dils = [dil for _, dil in DIL_PAIRS]
    return pl.pallas_call(
        _out0_kernel,
        grid=(b, s // tm),
        in_specs=[
            pl.BlockSpec((N_SLABS, tm, SLAB), lambda bi, i: (0, bi * tiles + i, 0)),
            pl.BlockSpec((N_SLABS, tm, SLAB), lambda bi, i: (gate_block, bi * tiles + i, 0)),
            tok,
            pl.BlockSpec((d, d), lambda bi, i: (0, 0)),
            pl.BlockSpec((1, d), lambda bi, i: (0, 0)),
        ],
        out_specs=[tok] + [pl.BlockSpec((1, dil, tm // dil, d), lambda bi, i: (bi, 0, i, 0))
                           for dil in dils],
        out_shape=([jax.ShapeDtypeStruct((b, s, d), jnp.float32)]
                   + [jax.ShapeDtypeStruct((b, dil, s // dil, d), jnp.bfloat16) for dil in dils]),
        scratch_shapes=[pltpu.VMEM((N_LANE_SLABS, tm, LANES), jnp.float32)],
        compiler_params=pltpu.CompilerParams(
            dimension_semantics=("parallel", "parallel"), vmem_limit_bytes=VMEM_LIMIT),
        name="out_proj0",
    )(o, proj, x, w_out, g_next.reshape(1, d))


def _natural_order(ref, slab_ref, dil):
    _, n, c = ref.shape
    if dil == 1:
        return ref[0].astype(jnp.float32)
    pieces = []
    for s in range(c // LANES):
        for r in range(dil):
            slab_ref[s, pl.ds(r, n, stride=dil), :] = (
                ref[r, :, s * LANES:(s + 1) * LANES].astype(jnp.float32))
        pieces.append(slab_ref[s])
    return pieces[0] if len(pieces) == 1 else jnp.concatenate(pieces, axis=1)


def _out1_kernel(o0_ref, o1_ref, o2_ref, l0_ref, l1_ref, l2_ref, gate_ref, x_ref,
                 w_ref, e_ref, g_ref, y_ref, slab_ref):
    dils = [dil for _, dil in DIL_PAIRS]
    lses = [_natural_order(ref.at[0], slab_ref, dil)
            for ref, dil in zip((l0_ref, l1_ref, l2_ref), dils)]
    m = jnp.maximum(jnp.maximum(lses[0], lses[1]), lses[2])
    es = [jnp.exp2(v - m) for v in lses]
    denom = es[0] + es[1] + es[2]
    o = None
    for e, o_ref, dil in zip(es, (o0_ref, o1_ref, o2_ref), dils):
        w = e / denom
        hi = w.astype(jnp.bfloat16)
        lo = (w - hi.astype(jnp.float32)).astype(jnp.bfloat16)
        w_full = jnp.dot(jnp.concatenate([hi, lo], axis=1), e_ref[...],
                         preferred_element_type=jnp.float32)
        term = w_full * _natural_order(o_ref.at[0], slab_ref, dil)
        o = term if o is None else o + term
    gate = gate_ref[0].astype(jnp.float32)
    z = (o * _silu(gate)).astype(jnp.bfloat16)
    x = x_ref[0] + jnp.dot(z, w_ref[...], preferred_element_type=jnp.float32)
    y_ref[0] = _rmsnorm(x, g_ref[...])


def _head_expansion():
    e = np.zeros((LSE_LANES, D_MODEL), np.float32)
    for h in range(N_HEADS):
        e[h, h * HEAD_DIM:(h + 1) * HEAD_DIM] = 1.0
    return jnp.asarray(np.concatenate([e, e], axis=0), dtype=jnp.bfloat16)


def _out_proj1(os_, lses, proj, x, w_out, norm_f, *, tm=512):
    b, s, d = x.shape
    gate_block = proj.shape[2] // d - 1
    dils = [dil for _, dil in DIL_PAIRS]

    def grouped(width, dil):
        return pl.BlockSpec((1, dil, tm // dil, width), lambda bi, i: (bi, 0, i, 0))

    tok = pl.BlockSpec((1, tm, d), lambda bi, i: (bi, i, 0))
    return pl.pallas_call(
        _out1_kernel,
        grid=(b, s // tm),
        in_specs=(
            [grouped(d, dil) for dil in dils]
            + [grouped(LSE_LANES, dil) for dil in dils]
            + [pl.BlockSpec((1, tm, d), lambda bi, i: (bi, i, gate_block)),
               tok,
               pl.BlockSpec((d, d), lambda bi, i: (0, 0)),
               pl.BlockSpec((2 * LSE_LANES, d), lambda bi, i: (0, 0)),
               pl.BlockSpec((1, d), lambda bi, i: (0, 0))]),
        out_specs=tok,
        out_shape=jax.ShapeDtypeStruct((b, s, d), jnp.float32),
        scratch_shapes=[pltpu.VMEM((N_LANE_SLABS, tm, LANES), jnp.float32)],
        compiler_params=pltpu.CompilerParams(
            dimension_semantics=("parallel", "parallel"), vmem_limit_bytes=VMEM_LIMIT),
        name="out_proj1",
    )(*os_, *lses, proj, x, w_out, _head_expansion(), norm_f.reshape(1, d))


def kernel(x, norm_0, w_in_0, rpb_0, w_out_0, norm_1, w_in_1, w_out_1, norm_f):
    b, s, d = x.shape
    t = b * s

    proj0 = _norm_proj(x.reshape(t, d), norm_0, w_in_0)
    o0 = _na_attention(proj0, _na_bias_table(rpb_0), b)
    x1, *h1 = _out_proj0(o0, proj0, x, w_out_0.astype(jnp.bfloat16), norm_1)

    gate_block = 3 * N_DIL_GROUPS
    alibi = _dil_bias_tables()
    outs, lses, proj_gate = [], [], None
    for g, (window, dil) in enumerate(DIL_PAIRS):
        assert window // (2 * dil) == DIL_RADIUS
        blocks = (3 * g, 3 * g + 1, 3 * g + 2) + ((gate_block,) if dil == 1 else ())
        qkv = _proj(h1[g].reshape(t, d), w_in_1, blocks).reshape(b, dil, s // dil, -1)
        if dil == 1:
            proj_gate = qkv.reshape(b, s, -1)
        o_g, lse_g = _dil_attention(qkv, alibi, g)
        outs.append(o_g)
        lses.append(lse_g)
    return _out_proj1(outs, lses, proj_gate, x1, w_out_1.astype(jnp.bfloat16), norm_f)
```

```python
import functools
import math

import numpy as np
import jax
import jax.numpy as jnp
from jax import lax
from jax.experimental import pallas as pl
from jax.experimental.pallas import tpu as pltpu

D_MODEL = 1024
HEAD_DIM = 64
N_HEADS = 16
GRID_W = 64
NA_ROWS = 8
NA_COLS = 16
DIL_PAIRS = ((128, 1), (512, 4), (2048, 16))
N_DIL_GROUPS = len(DIL_PAIRS)
RMS_EPS = 1e-6
NEG_INF = -1e30
LOG2E = math.log2(math.e)

LANES = 128
N_LANE_SLABS = D_MODEL // LANES
HEADS_PER_STEP = 4
SLAB = HEADS_PER_STEP * HEAD_DIM
N_SLABS = D_MODEL // SLAB
Q_TILE = 128
NA_Q_TILE = GRID_W
NA_WIN = NA_ROWS * GRID_W
NA_TILES_PER_STEP = 32
DIL_TILES_PER_STEP = 8
DIL_RADIUS = 64
DIL_WIN = Q_TILE + 2 * DIL_RADIUS
LSE_LANES = LANES
VMEM_LIMIT = 56 * 1024 * 1024

_NT_DIMS = (((1,), (1,)), ((), ()))


def _rmsnorm(x, g):
    ms = jnp.mean(x * x, axis=-1, keepdims=True)
    return x * lax.rsqrt(ms + RMS_EPS) * g


def _q_scaled_bf16(w_ref):
    scale = jnp.where(pl.program_id(1) == 0, LOG2E / math.sqrt(HEAD_DIM), 1.0)
    return (w_ref[...] * scale).astype(jnp.bfloat16)


def _norm_proj_kernel(x_ref, g_ref, w_ref, o_ref, h_ref):
    @pl.when(pl.program_id(1) == 0)
    def _():
        h_ref[...] = _rmsnorm(x_ref[...], g_ref[...]).astype(h_ref.dtype)

    res = jnp.dot(h_ref[...], _q_scaled_bf16(w_ref),
                  preferred_element_type=jnp.float32).astype(o_ref.dtype)
    for k in range(o_ref.shape[0]):
        o_ref[k] = res[:, k * SLAB:(k + 1) * SLAB]


def _norm_proj(x, g, w, *, tm=2048, tn=1024):
    t, d = x.shape
    n = w.shape[1]
    return pl.pallas_call(
        _norm_proj_kernel,
        grid=(t // tm, n // tn),
        in_specs=[
            pl.BlockSpec((tm, d), lambda i, j: (i, 0)),
            pl.BlockSpec((1, d), lambda i, j: (0, 0)),
            pl.BlockSpec((d, tn), lambda i, j: (0, j)),
        ],
        out_specs=pl.BlockSpec((tn // SLAB, tm, SLAB), lambda i, j: (j, i, 0)),
        out_shape=jax.ShapeDtypeStruct((n // SLAB, t, SLAB), jnp.bfloat16),
        scratch_shapes=[pltpu.VMEM((tm, d), jnp.bfloat16)],
        compiler_params=pltpu.CompilerParams(
            dimension_semantics=("parallel", "arbitrary"),
            vmem_limit_bytes=VMEM_LIMIT),
        name="norm_proj",
    )(x, g.reshape(1, d), w)


def _proj_kernel(h_ref, w_ref, o_ref):
    o_ref[...] = jnp.dot(h_ref[...], _q_scaled_bf16(w_ref),
                         preferred_element_type=jnp.float32).astype(o_ref.dtype)


def _proj(h, w, col_blocks, *, tm=2048):
    t, d = h.shape
    tn = d
    n = len(col_blocks) * tn

    def w_block(j):
        blk = col_blocks[-1]
        for k in range(len(col_blocks) - 2, -1, -1):
            blk = jnp.where(j == k, col_blocks[k], blk)
        return blk

    return pl.pallas_call(
        _proj_kernel,
        grid=(t // tm, n // tn),
        in_specs=[
            pl.BlockSpec((tm, d), lambda i, j: (i, 0)),
            pl.BlockSpec((d, tn), lambda i, j: (0, w_block(j))),
        ],
        out_specs=pl.BlockSpec((tm, tn), lambda i, j: (i, j)),
        out_shape=jax.ShapeDtypeStruct((t, n), jnp.bfloat16),
        compiler_params=pltpu.CompilerParams(
            dimension_semantics=("parallel", "parallel"),
            vmem_limit_bytes=VMEM_LIMIT),
        name="proj",
    )(h, w)


def _na_kernel(q_ref, k_ref, v_ref, b_ref, o_ref):
    step = pl.program_id(2)
    n_tiles = k_ref.shape[1] // GRID_W
    lane = lax.broadcasted_iota(jnp.int32, (NA_Q_TILE, SLAB), 1) // HEAD_DIM

    for u in range(NA_TILES_PER_STEP):
        r = step * NA_TILES_PER_STEP + u
        first_row = jnp.clip(r - NA_ROWS // 2, 0, n_tiles - NA_ROWS)
        start = pl.multiple_of(first_row * GRID_W, GRID_W)
        tile_rows = slice(u * NA_Q_TILE, (u + 1) * NA_Q_TILE)
        q4 = q_ref[0, tile_rows, :]
        zero = jnp.zeros_like(q4)
        q_stack = jnp.concatenate(
            [jnp.where(lane == h, q4, zero) for h in range(HEADS_PER_STEP)], axis=0)
        s = lax.dot_general(q_stack, k_ref[0, pl.ds(start, NA_WIN), :], _NT_DIMS,
                            preferred_element_type=jnp.float32)
        first_off = first_row - r + NA_ROWS - 1
        s = s + jnp.concatenate(
            [jnp.concatenate([b_ref[h, first_off + 2 * j] for j in range(NA_ROWS // 2)], axis=1)
             for h in range(HEADS_PER_STEP)], axis=0)
        m = jnp.max(s, axis=-1, keepdims=True)
        e = jnp.exp2(s - m)
        inv_l = 1.0 / jnp.sum(e, axis=-1, keepdims=True)
        o_all = jnp.dot(e.astype(jnp.bfloat16), v_ref[0, pl.ds(start, NA_WIN), :],
                        preferred_element_type=jnp.float32) * inv_l
        out = o_all[:NA_Q_TILE]
        for h in range(1, HEADS_PER_STEP):
            out = jnp.where(lane == h, o_all[h * NA_Q_TILE:(h + 1) * NA_Q_TILE], out)
        o_ref[0, tile_rows, :] = out.astype(o_ref.dtype)


def _na_bias_table(rpb):
    n_col_off = 2 * NA_COLS - 1
    c = np.arange(GRID_W)[:, None]
    kc = np.arange(GRID_W)[None, :]
    cs = np.clip(c - NA_COLS // 2, 0, GRID_W - NA_COLS)
    col_ok = (kc >= cs) & (kc < cs + NA_COLS)
    col_off = kc - c + NA_COLS - 1
    assert np.all(((col_off >= 0) & (col_off < n_col_off))[col_ok])
    pick = (np.arange(n_col_off)[:, None, None] == col_off[None]) & col_ok[None]
    pick2 = np.zeros((2, n_col_off, GRID_W, 2, GRID_W), np.float32)
    for half in range(2):
        pick2[half, :, :, half, :] = pick
    pick2 = pick2.reshape(2 * n_col_off, GRID_W, 2 * GRID_W)
    rpb = rpb.astype(jnp.float32)
    row_pairs = jnp.concatenate([rpb[:, :-1], rpb[:, 1:]], axis=-1)
    table = jnp.einsum('hrm,mck->hrck', row_pairs, jnp.asarray(pick2),
                       precision=lax.Precision.HIGHEST)
    ok2 = np.concatenate([col_ok, col_ok], axis=1)
    return jnp.where(jnp.asarray(ok2), table * LOG2E, NEG_INF)


def _na_attention(proj, bias_table, b):
    s = proj.shape[1] // b
    tile = NA_Q_TILE * NA_TILES_PER_STEP
    assert s % tile == 0 and NA_Q_TILE == GRID_W
    n_blocks = s // tile
    return pl.pallas_call(
        _na_kernel,
        grid=(b, N_SLABS, n_blocks),
        in_specs=[
            pl.BlockSpec((1, tile, SLAB), lambda bi, g, j: (g, bi * n_blocks + j, 0)),
            pl.BlockSpec((1, s, SLAB), lambda bi, g, j: (N_SLABS + g, bi, 0)),
            pl.BlockSpec((1, s, SLAB), lambda bi, g, j: (2 * N_SLABS + g, bi, 0)),
            pl.BlockSpec((HEADS_PER_STEP,) + bias_table.shape[1:], lambda bi, g, j: (g, 0, 0, 0)),
        ],
        out_specs=pl.BlockSpec((1, tile, SLAB), lambda bi, g, j: (g, bi * n_blocks + j, 0)),
        out_shape=jax.ShapeDtypeStruct((N_SLABS, b * s, SLAB), jnp.bfloat16),
        compiler_params=pltpu.CompilerParams(
            dimension_semantics=("parallel", "parallel", "arbitrary"),
            vmem_limit_bytes=VMEM_LIMIT),
        name="na_attention",
    )(proj, proj, proj, bias_table)


def _alibi_slopes():
    return np.asarray(2.0 ** (-8.0 * (np.arange(N_HEADS) + 1) / N_HEADS), dtype=np.float32)


def _slab_attention(q4, k4, v4, bias):
    lane = lax.broadcasted_iota(jnp.int32, q4.shape, 1) // HEAD_DIM
    zero = jnp.zeros_like(q4)
    q_stack = jnp.concatenate(
        [jnp.where(lane == h, q4, zero) for h in range(HEADS_PER_STEP)], axis=0)
    s = lax.dot_general(q_stack, k4, _NT_DIMS, preferred_element_type=jnp.float32)
    s = s + bias
    m = jnp.max(s, axis=-1, keepdims=True)
    p = jnp.exp2(s - m)
    l = jnp.sum(p, axis=-1, keepdims=True)
    p = p.astype(jnp.bfloat16)
    inv_l = 1.0 / l
    out = None
    for h in range(HEADS_PER_STEP):
        rows = slice(h * Q_TILE, (h + 1) * Q_TILE)
        o_h = jnp.dot(p[rows], v4, preferred_element_type=jnp.float32) * inv_l[rows]
        out = o_h if out is None else jnp.where(lane == h, o_h, out)
    return out, m, l


def _dil_kernel(q_ref, ka_ref, kb_ref, kc_ref, va_ref, vb_ref, vc_ref, b_ref,
                o_ref, lse_ref):
    step = pl.program_id(2)
    n_steps = pl.num_programs(2)
    k_win = jnp.concatenate([ka_ref[0, 0], kb_ref[0, 0], kc_ref[0, 0]], axis=0)
    v_win = jnp.concatenate([va_ref[0, 0], vb_ref[0, 0], vc_ref[0, 0]], axis=0)
    lane = lax.broadcasted_iota(jnp.int32, (Q_TILE, LSE_LANES), 1)
    tiles = q_ref.shape[2] // Q_TILE
    for u in range(tiles):
        pattern = 0
        if u == 0:
            pattern = jnp.where(step == 0, 1, pattern)
        if u == tiles - 1:
            pattern = jnp.where(step == n_steps - 1, 2, pattern)
        tile = slice(u * Q_TILE, (u + 1) * Q_TILE)
        win = slice(u * Q_TILE, u * Q_TILE + DIL_WIN)
        lse_tile = jnp.zeros((Q_TILE, LSE_LANES), jnp.float32)
        for hg in range(N_SLABS):
            cols = slice(hg * SLAB, (hg + 1) * SLAB)
            bias = b_ref[pattern, hg * HEADS_PER_STEP:(hg + 1) * HEADS_PER_STEP]
            bias = bias.reshape(HEADS_PER_STEP * Q_TILE, DIL_WIN)
            out, m, l = _slab_attention(q_ref[0, 0, tile, cols], k_win[win, cols],
                                        v_win[win, cols], bias)
            o_ref[0, 0, tile, cols] = out.astype(o_ref.dtype)
            lse = m + jnp.log2(l)
            for h in range(HEADS_PER_STEP):
                lse_h = lse[h * Q_TILE:(h + 1) * Q_TILE]
                lse_tile = jnp.where(lane == hg * HEADS_PER_STEP + h, lse_h, lse_tile)
        lse_ref[0, 0, tile, :] = lse_tile


def _dil_bias_tables():
    qi = np.arange(Q_TILE)[:, None]
    kj = np.arange(DIL_WIN)[None, :]
    delta = kj - DIL_RADIUS - qi
    in_band = np.abs(delta) <= DIL_RADIUS
    valid = np.stack([in_band,
                      in_band & (kj >= DIL_RADIUS),
                      in_band & (kj < DIL_WIN - DIL_RADIUS)])
    dils = np.asarray([dil for _, dil in DIL_PAIRS])
    dist = jnp.asarray(np.abs(delta)[None] * dils[:, None, None], dtype=jnp.float32)
    bias = -jnp.asarray(_alibi_slopes())[None, :, None, None] * dist[:, None]
    return jnp.where(jnp.asarray(valid)[None, :, None], bias[:, None] * LOG2E, NEG_INF)


def _dil_attention(qkv, bias_tables, group):
    b, dil, l, _ = qkv.shape
    tiles = min(DIL_TILES_PER_STEP, l // Q_TILE)
    tile = Q_TILE * tiles
    n_steps = l // tile
    assert l % tile == 0 and tiles >= 2
    half = DIL_RADIUS
    per_tile = tile // half
    n_half = l // half

    def spec_mid(which):
        return pl.BlockSpec((1, 1, tile, D_MODEL), lambda bi, r, i: (bi, r, i, which))

    def spec_lo(which):
        return pl.BlockSpec((1, 1, half, D_MODEL),
                            lambda bi, r, i: (bi, r, jnp.maximum(per_tile * i - 1, 0), which))

    def spec_hi(which):
        return pl.BlockSpec((1, 1, half, D_MODEL),
                            lambda bi, r, i: (bi, r, jnp.minimum(per_tile * (i + 1), n_half - 1), which))

    return pl.pallas_call(
        _dil_kernel,
        grid=(b, dil, n_steps),
        in_specs=[
            spec_mid(0),
            spec_lo(1), spec_mid(1), spec_hi(1),
            spec_lo(2), spec_mid(2), spec_hi(2),
            pl.BlockSpec((None,) + bias_tables.shape[1:], lambda bi, r, i: (group, 0, 0, 0, 0)),
        ],
        out_specs=[
            pl.BlockSpec((1, 1, tile, D_MODEL), lambda bi, r, i: (bi, r, i, 0)),
            pl.BlockSpec((1, 1, tile, LSE_LANES), lambda bi, r, i: (bi, r, i, 0)),
        ],
        out_shape=[
            jax.ShapeDtypeStruct((b, dil, l, D_MODEL), jnp.bfloat16),
            jax.ShapeDtypeStruct((b, dil, l, LSE_LANES), jnp.float32),
        ],
        compiler_params=pltpu.CompilerParams(
            dimension_semantics=("parallel", "parallel", "arbitrary"),
            vmem_limit_bytes=VMEM_LIMIT),
        name=f"dilated_attention_{dil}",
    )(qkv, qkv, qkv, qkv, qkv, qkv, qkv, bias_tables)


def _silu(x):
    half = 0.5 * x
    return half + half * jnp.tanh(half)


def _out0_kernel(o_ref, gate_ref, x_ref, w_ref, g_ref, y_ref, *rest):
    h_refs, slab_refs = rest[:-2], rest[-2:]
    o = jnp.concatenate([o_ref[k] for k in range(N_SLABS)], axis=1).astype(jnp.float32)
    gate = jnp.concatenate([gate_ref[k] for k in range(N_SLABS)], axis=1).astype(jnp.float32)
    z = (o * _silu(gate)).astype(jnp.bfloat16)
    y = x_ref[0] + jnp.dot(z, w_ref[...], preferred_element_type=jnp.float32)
    y_ref[0] = y
    hn = _rmsnorm(y, g_ref[...])
    tm = hn.shape[0]
    dils = [dil for _, dil in DIL_PAIRS]
    for s in range(N_LANE_SLABS):
        slab_refs[0][s] = hn[:, s * LANES:(s + 1) * LANES]
    staged = {1: slab_refs[0]}
    for k, (h_ref, dil) in enumerate(zip(h_refs, dils)):
        if dil == 1:
            h_ref[0, 0] = hn.astype(h_ref.dtype)
            continue
        p = max(q for q in staged if dil % q == 0)
        f, n = dil // p, tm // dil
        keep = any(later % dil == 0 for later in dils[k + 1:])
        for r in range(dil):
            start = (r % p) * (tm // p) + r // p
            for s in range(N_LANE_SLABS):
                rows = staged[p][s, pl.ds(start, n, stride=f), :]
                h_ref[0, r, :, s * LANES:(s + 1) * LANES] = rows.astype(h_ref.dtype)
                if keep:
                    slab_refs[1][s, r * n:(r + 1) * n, :] = rows
        if keep:
            staged[dil] = slab_refs[1]


def _out_proj0(o, proj, x, w_out, g_next, *, tm=512):
    b, s, d = x.shape
    gate_block = proj.shape[0] // N_SLABS - 1
    tiles = s // tm
    tok = pl.BlockSpec((1, tm, d), lambda bi, i: (bi, i, 0))
    dils = [dil for _, dil in DIL_PAIRS]
    return pl.pallas_call(
        _out0_kernel,
        grid=(b, s // tm),
        in_specs=[
            pl.BlockSpec((N_SLABS, tm, SLAB), lambda bi, i: (0, bi * tiles + i, 0)),
            pl.BlockSpec((N_SLABS, tm, SLAB), lambda bi, i: (gate_block, bi * tiles + i, 0)),
            tok,
            pl.BlockSpec((d, d), lambda bi, i: (0, 0)),
            pl.BlockSpec((1, d), lambda bi, i: (0, 0)),
        ],
        out_specs=[tok] + [pl.BlockSpec((1, dil, tm // dil, d), lambda bi, i: (bi, 0, i, 0))
                           for dil in dils],
        out_shape=([jax.ShapeDtypeStruct((b, s, d), jnp.float32)]
                   + [jax.ShapeDtypeStruct((b, dil, s // dil, d), jnp.bfloat16) for dil in dils]),
        scratch_shapes=[pltpu.VMEM((N_LANE_SLABS, tm, LANES), jnp.float32)] * 2,
        compiler_params=pltpu.CompilerParams(
            dimension_semantics=("parallel", "parallel"), vmem_limit_bytes=VMEM_LIMIT),
        name="out_proj0",
    )(o, proj, x, w_out, g_next.reshape(1, d))


_MAX_ROW_STRIDE = 4


def _natural_order(ref, slab_refs, dil):
    _, n, c = ref.shape
    if dil == 1:
        return ref[0].astype(jnp.float32)
    out_ref, mid_ref = slab_refs
    tm = dil * n
    two_level = dil > _MAX_ROW_STRIDE
    if two_level:
        f = _MAX_ROW_STRIDE
        p = dil // f
        assert p <= _MAX_ROW_STRIDE
    pieces = []
    for s in range(c // LANES):
        lanes = slice(s * LANES, (s + 1) * LANES)
        if not two_level:
            for r in range(dil):
                out_ref[s, pl.ds(r, n, stride=dil), :] = ref[r, :, lanes].astype(jnp.float32)
        else:
            for r in range(dil):
                start = (r % p) * (tm // p) + r // p
                mid_ref[s, pl.ds(start, n, stride=f), :] = ref[r, :, lanes].astype(jnp.float32)
            for b_ in range(p):
                out_ref[s, pl.ds(b_, tm // p, stride=p), :] = (
                    mid_ref[s, b_ * (tm // p):(b_ + 1) * (tm // p), :])
        pieces.append(out_ref[s])
    return pieces[0] if len(pieces) == 1 else jnp.concatenate(pieces, axis=1)


def _out1_kernel(o0_ref, o1_ref, o2_ref, l0_ref, l1_ref, l2_ref, gate_ref, x_ref,
                 w_ref, e_ref, g_ref, y_ref, *slab_refs):
    dils = [dil for _, dil in DIL_PAIRS]
    lses = [_natural_order(ref.at[0], slab_refs, dil)
            for ref, dil in zip((l0_ref, l1_ref, l2_ref), dils)]
    m = jnp.maximum(jnp.maximum(lses[0], lses[1]), lses[2])
    es = [jnp.exp2(v - m) for v in lses]
    denom = es[0] + es[1] + es[2]
    o = None
    for e, o_ref, dil in zip(es, (o0_ref, o1_ref, o2_ref), dils):
        w = e / denom
        hi = w.astype(jnp.bfloat16)
        lo = (w - hi.astype(jnp.float32)).astype(jnp.bfloat16)
        w_full = jnp.dot(jnp.concatenate([hi, lo], axis=1), e_ref[...],
                         preferred_element_type=jnp.float32)
        term = w_full * _natural_order(o_ref.at[0], slab_refs, dil)
        o = term if o is None else o + term
    gate = gate_ref[0].astype(jnp.float32)
    z = (o * _silu(gate)).astype(jnp.bfloat16)
    x = x_ref[0] + jnp.dot(z, w_ref[...], preferred_element_type=jnp.float32)
    y_ref[0] = _rmsnorm(x, g_ref[...])


def _head_expansion():
    e = np.zeros((LSE_LANES, D_MODEL), np.float32)
    for h in range(N_HEADS):
        e[h, h * HEAD_DIM:(h + 1) * HEAD_DIM] = 1.0
    return jnp.asarray(np.concatenate([e, e], axis=0), dtype=jnp.bfloat16)


def _out_proj1(os_, lses, proj, x, w_out, norm_f, *, tm=512):
    b, s, d = x.shape
    gate_block = proj.shape[2] // d - 1
    dils = [dil for _, dil in DIL_PAIRS]

    def grouped(width, dil):
        return pl.BlockSpec((1, dil, tm // dil, width), lambda bi, i: (bi, 0, i, 0))

    tok = pl.BlockSpec((1, tm, d), lambda bi, i: (bi, i, 0))
    return pl.pallas_call(
        _out1_kernel,
        grid=(b, s // tm),
        in_specs=(
            [grouped(d, dil) for dil in dils]
            + [grouped(LSE_LANES, dil) for dil in dils]
            + [pl.BlockSpec((1, tm, d), lambda bi, i: (bi, i, gate_block)),
               tok,
               pl.BlockSpec((d, d), lambda bi, i: (0, 0)),
               pl.BlockSpec((2 * LSE_LANES, d), lambda bi, i: (0, 0)),
               pl.BlockSpec((1, d), lambda bi, i: (0, 0))]),
        out_specs=tok,
        out_shape=jax.ShapeDtypeStruct((b, s, d), jnp.float32),
        scratch_shapes=[pltpu.VMEM((N_LANE_SLABS, tm, LANES), jnp.float32)] * 2,
        compiler_params=pltpu.CompilerParams(
            dimension_semantics=("parallel", "parallel"), vmem_limit_bytes=VMEM_LIMIT),
        name="out_proj1",
    )(*os_, *lses, proj, x, w_out, _head_expansion(), norm_f.reshape(1, d))


def kernel(x, norm_0, w_in_0, rpb_0, w_out_0, norm_1, w_in_1, w_out_1, norm_f):
    b, s, d = x.shape
    t = b * s

    proj0 = _norm_proj(x.reshape(t, d), norm_0, w_in_0)
    o0 = _na_attention(proj0, _na_bias_table(rpb_0), b)
    x1, *h1 = _out_proj0(o0, proj0, x, w_out_0.astype(jnp.bfloat16), norm_1)

    gate_block = 3 * N_DIL_GROUPS
    alibi = _dil_bias_tables()
    outs, lses, proj_gate = [], [], None
    for g, (window, dil) in enumerate(DIL_PAIRS):
        assert window // (2 * dil) == DIL_RADIUS
        blocks = (3 * g, 3 * g + 1, 3 * g + 2) + ((gate_block,) if dil == 1 else ())
        qkv = _proj(h1[g].reshape(t, d), w_in_1, blocks).reshape(b, dil, s // dil, -1)
        if dil == 1:
            proj_gate = qkv.reshape(b, s, -1)
        o_g, lse_g = _dil_attention(qkv, alibi, g)
        outs.append(o_g)
        lses.append(lse_g)
    return _out_proj1(outs, lses, proj_gate, x1, w_out_1.astype(jnp.bfloat16), norm_f)
```

```python
import functools
import math

import numpy as np
import jax
import jax.numpy as jnp
from jax import lax
from jax.experimental import pallas as pl
from jax.experimental.pallas import tpu as pltpu

D_MODEL = 1024
HEAD_DIM = 64
N_HEADS = 16
GRID_W = 64
NA_ROWS = 8
NA_COLS = 16
DIL_PAIRS = ((128, 1), (512, 4), (2048, 16))
N_DIL_GROUPS = len(DIL_PAIRS)
RMS_EPS = 1e-6
NEG_INF = -1e30
LOG2E = math.log2(math.e)

LANES = 128
N_LANE_SLABS = D_MODEL // LANES
HEADS_PER_STEP = 4
SLAB = HEADS_PER_STEP * HEAD_DIM
N_SLABS = D_MODEL // SLAB
Q_TILE = 128
NA_Q_TILE = GRID_W
NA_WIN = NA_ROWS * GRID_W
NA_TILES_PER_STEP = 32
DIL_TILES_PER_STEP = 8
DIL_RADIUS = 64
DIL_WIN = Q_TILE + 2 * DIL_RADIUS
LSE_LANES = LANES
VMEM_LIMIT = 56 * 1024 * 1024

_NT_DIMS = (((1,), (1,)), ((), ()))


def _rmsnorm(x, g):
    ms = jnp.mean(x * x, axis=-1, keepdims=True)
    return x * lax.rsqrt(ms + RMS_EPS) * g


def _q_scaled_bf16(w_ref):
    scale = jnp.where(pl.program_id(1) == 0, LOG2E / math.sqrt(HEAD_DIM), 1.0)
    return (w_ref[...] * scale).astype(jnp.bfloat16)


def _norm_proj_kernel(x_ref, g_ref, w_ref, o_ref, h_ref):
    @pl.when(pl.program_id(1) == 0)
    def _():
        h_ref[...] = _rmsnorm(x_ref[...], g_ref[...]).astype(h_ref.dtype)

    res = jnp.dot(h_ref[...], _q_scaled_bf16(w_ref),
                  preferred_element_type=jnp.float32).astype(o_ref.dtype)
    for k in range(o_ref.shape[0]):
        o_ref[k] = res[:, k * SLAB:(k + 1) * SLAB]


def _norm_proj(x, g, w, *, tm=2048, tn=1024):
    t, d = x.shape
    n = w.shape[1]
    return pl.pallas_call(
        _norm_proj_kernel,
        grid=(t // tm, n // tn),
        in_specs=[
            pl.BlockSpec((tm, d), lambda i, j: (i, 0)),
            pl.BlockSpec((1, d), lambda i, j: (0, 0)),
            pl.BlockSpec((d, tn), lambda i, j: (0, j)),
        ],
        out_specs=pl.BlockSpec((tn // SLAB, tm, SLAB), lambda i, j: (j, i, 0)),
        out_shape=jax.ShapeDtypeStruct((n // SLAB, t, SLAB), jnp.bfloat16),
        scratch_shapes=[pltpu.VMEM((tm, d), jnp.bfloat16)],
        compiler_params=pltpu.CompilerParams(
            dimension_semantics=("parallel", "arbitrary"),
            vmem_limit_bytes=VMEM_LIMIT),
        name="norm_proj",
    )(x, g.reshape(1, d), w)


def _proj_kernel(h_ref, w_ref, o_ref):
    o_ref[...] = jnp.dot(h_ref[...], _q_scaled_bf16(w_ref),
                         preferred_element_type=jnp.float32).astype(o_ref.dtype)


def _proj(h, w, col_blocks, *, tm=2048):
    t, d = h.shape
    tn = d
    n = len(col_blocks) * tn

    def w_block(j):
        blk = col_blocks[-1]
        for k in range(len(col_blocks) - 2, -1, -1):
            blk = jnp.where(j == k, col_blocks[k], blk)
        return blk

    return pl.pallas_call(
        _proj_kernel,
        grid=(t // tm, n // tn),
        in_specs=[
            pl.BlockSpec((tm, d), lambda i, j: (i, 0)),
            pl.BlockSpec((d, tn), lambda i, j: (0, w_block(j))),
        ],
        out_specs=pl.BlockSpec((tm, tn), lambda i, j: (i, j)),
        out_shape=jax.ShapeDtypeStruct((t, n), jnp.bfloat16),
        compiler_params=pltpu.CompilerParams(
            dimension_semantics=("parallel", "parallel"),
            vmem_limit_bytes=VMEM_LIMIT),
        name="proj",
    )(h, w)


def _na_kernel(q_ref, k_ref, v_ref, b_ref, o_ref):
    step = pl.program_id(2)
    n_tiles = k_ref.shape[1] // GRID_W
    lane = lax.broadcasted_iota(jnp.int32, (NA_Q_TILE, SLAB), 1) // HEAD_DIM

    for u in range(NA_TILES_PER_STEP):
        r = step * NA_TILES_PER_STEP + u
        first_row = jnp.clip(r - NA_ROWS // 2, 0, n_tiles - NA_ROWS)
        start = pl.multiple_of(first_row * GRID_W, GRID_W)
        tile_rows = slice(u * NA_Q_TILE, (u + 1) * NA_Q_TILE)
        q4 = q_ref[0, tile_rows, :]
        zero = jnp.zeros_like(q4)
        q_stack = jnp.concatenate(
            [jnp.where(lane == h, q4, zero) for h in range(HEADS_PER_STEP)], axis=0)
        s = lax.dot_general(q_stack, k_ref[0, pl.ds(start, NA_WIN), :], _NT_DIMS,
                            preferred_element_type=jnp.float32)
        first_off = first_row - r + NA_ROWS - 1
        s = s + jnp.concatenate(
            [jnp.concatenate([b_ref[h, first_off + 2 * j] for j in range(NA_ROWS // 2)], axis=1)
             for h in range(HEADS_PER_STEP)], axis=0)
        m = jnp.max(s, axis=-1, keepdims=True)
        e = jnp.exp2(s - m)
        inv_l = 1.0 / jnp.sum(e, axis=-1, keepdims=True)
        o_all = jnp.dot(e.astype(jnp.bfloat16), v_ref[0, pl.ds(start, NA_WIN), :],
                        preferred_element_type=jnp.float32) * inv_l
        out = o_all[:NA_Q_TILE]
        for h in range(1, HEADS_PER_STEP):
            out = jnp.where(lane == h, o_all[h * NA_Q_TILE:(h + 1) * NA_Q_TILE], out)
        o_ref[0, tile_rows, :] = out.astype(o_ref.dtype)


def _na_bias_table(rpb):
    n_col_off = 2 * NA_COLS - 1
    c = np.arange(GRID_W)[:, None]
    kc = np.arange(GRID_W)[None, :]
    cs = np.clip(c - NA_COLS // 2, 0, GRID_W - NA_COLS)
    col_ok = (kc >= cs) & (kc < cs + NA_COLS)
    col_off = kc - c + NA_COLS - 1
    assert np.all(((col_off >= 0) & (col_off < n_col_off))[col_ok])
    pick = (np.arange(n_col_off)[:, None, None] == col_off[None]) & col_ok[None]
    pick2 = np.zeros((2, n_col_off, GRID_W, 2, GRID_W), np.float32)
    for half in range(2):
        pick2[half, :, :, half, :] = pick
    pick2 = pick2.reshape(2 * n_col_off, GRID_W, 2 * GRID_W)
    rpb = rpb.astype(jnp.float32)
    row_pairs = jnp.concatenate([rpb[:, :-1], rpb[:, 1:]], axis=-1)
    table = jnp.einsum('hrm,mck->hrck', row_pairs, jnp.asarray(pick2),
                       precision=lax.Precision.HIGHEST)
    ok2 = np.concatenate([col_ok, col_ok], axis=1)
    return jnp.where(jnp.asarray(ok2), table * LOG2E, NEG_INF)


def _na_attention(proj, bias_table, b):
    s = proj.shape[1] // b
    tile = NA_Q_TILE * NA_TILES_PER_STEP
    assert s % tile == 0 and NA_Q_TILE == GRID_W
    n_blocks = s // tile
    return pl.pallas_call(
        _na_kernel,
        grid=(b, N_SLABS, n_blocks),
        in_specs=[
            pl.BlockSpec((1, tile, SLAB), lambda bi, g, j: (g, bi * n_blocks + j, 0)),
            pl.BlockSpec((1, s, SLAB), lambda bi, g, j: (N_SLABS + g, bi, 0)),
            pl.BlockSpec((1, s, SLAB), lambda bi, g, j: (2 * N_SLABS + g, bi, 0)),
            pl.BlockSpec((HEADS_PER_STEP,) + bias_table.shape[1:], lambda bi, g, j: (g, 0, 0, 0)),
        ],
        out_specs=pl.BlockSpec((1, tile, SLAB), lambda bi, g, j: (g, bi * n_blocks + j, 0)),
        out_shape=jax.ShapeDtypeStruct((N_SLABS, b * s, SLAB), jnp.bfloat16),
        compiler_params=pltpu.CompilerParams(
            dimension_semantics=("parallel", "parallel", "arbitrary"),
            vmem_limit_bytes=VMEM_LIMIT),
        name="na_attention",
    )(proj, proj, proj, bias_table)


def _alibi_slopes():
    return np.asarray(2.0 ** (-8.0 * (np.arange(N_HEADS) + 1) / N_HEADS), dtype=np.float32)


def _slab_attention(q4, k4, v4, bias):
    lane = lax.broadcasted_iota(jnp.int32, q4.shape, 1) // HEAD_DIM
    zero = jnp.zeros_like(q4)
    q_stack = jnp.concatenate(
        [jnp.where(lane == h, q4, zero) for h in range(HEADS_PER_STEP)], axis=0)
    s = lax.dot_general(q_stack, k4, _NT_DIMS, preferred_element_type=jnp.float32)
    s = s + bias
    m = jnp.max(s, axis=-1, keepdims=True)
    p = jnp.exp2(s - m)
    l = jnp.sum(p, axis=-1, keepdims=True)
    p = p.astype(jnp.bfloat16)
    o_all = jnp.dot(p, v4, preferred_element_type=jnp.float32) * (1.0 / l)
    out = o_all[:Q_TILE]
    for h in range(1, HEADS_PER_STEP):
        out = jnp.where(lane == h, o_all[h * Q_TILE:(h + 1) * Q_TILE], out)
    return out, m, l


def _dil_kernel(q_ref, ka_ref, kb_ref, kc_ref, va_ref, vb_ref, vc_ref, b_ref,
                o_ref, lse_ref):
    step = pl.program_id(2)
    n_steps = pl.num_programs(2)
    k_win = jnp.concatenate([ka_ref[0, 0], kb_ref[0, 0], kc_ref[0, 0]], axis=0)
    v_win = jnp.concatenate([va_ref[0, 0], vb_ref[0, 0], vc_ref[0, 0]], axis=0)
    lane = lax.broadcasted_iota(jnp.int32, (Q_TILE, LSE_LANES), 1)
    tiles = q_ref.shape[2] // Q_TILE
    for u in range(tiles):
        pattern = 0
        if u == 0:
            pattern = jnp.where(step == 0, 1, pattern)
        if u == tiles - 1:
            pattern = jnp.where(step == n_steps - 1, 2, pattern)
        tile = slice(u * Q_TILE, (u + 1) * Q_TILE)
        win = slice(u * Q_TILE, u * Q_TILE + DIL_WIN)
        lse_tile = jnp.zeros((Q_TILE, LSE_LANES), jnp.float32)
        for hg in range(N_SLABS):
            cols = slice(hg * SLAB, (hg + 1) * SLAB)
            bias = b_ref[pattern, hg * HEADS_PER_STEP:(hg + 1) * HEADS_PER_STEP]
            bias = bias.reshape(HEADS_PER_STEP * Q_TILE, DIL_WIN)
            out, m, l = _slab_attention(q_ref[0, 0, tile, cols], k_win[win, cols],
                                        v_win[win, cols], bias)
            o_ref[0, 0, tile, cols] = out.astype(o_ref.dtype)
            lse = m + jnp.log2(l)
            for h in range(HEADS_PER_STEP):
                lse_h = lse[h * Q_TILE:(h + 1) * Q_TILE]
                lse_tile = jnp.where(lane == hg * HEADS_PER_STEP + h, lse_h, lse_tile)
        lse_ref[0, 0, tile, :] = lse_tile


def _dil_bias_tables():
    qi = np.arange(Q_TILE)[:, None]
    kj = np.arange(DIL_WIN)[None, :]
    delta = kj - DIL_RADIUS - qi
    in_band = np.abs(delta) <= DIL_RADIUS
    valid = np.stack([in_band,
                      in_band & (kj >= DIL_RADIUS),
                      in_band & (kj < DIL_WIN - DIL_RADIUS)])
    dils = np.asarray([dil for _, dil in DIL_PAIRS])
    dist = jnp.asarray(np.abs(delta)[None] * dils[:, None, None], dtype=jnp.float32)
    bias = -jnp.asarray(_alibi_slopes())[None, :, None, None] * dist[:, None]
    return jnp.where(jnp.asarray(valid)[None, :, None], bias[:, None] * LOG2E, NEG_INF)


def _dil_attention(qkv, bias_tables, group):
    b, dil, l, _ = qkv.shape
    tiles = min(DIL_TILES_PER_STEP, l // Q_TILE)
    tile = Q_TILE * tiles
    n_steps = l // tile
    assert l % tile == 0 and tiles >= 2
    half = DIL_RADIUS
    per_tile = tile // half
    n_half = l // half

    def spec_mid(which):
        return pl.BlockSpec((1, 1, tile, D_MODEL), lambda bi, r, i: (bi, r, i, which))

    def spec_lo(which):
        return pl.BlockSpec((1, 1, half, D_MODEL),
                            lambda bi, r, i: (bi, r, jnp.maximum(per_tile * i - 1, 0), which))

    def spec_hi(which):
        return pl.BlockSpec((1, 1, half, D_MODEL),
                            lambda bi, r, i: (bi, r, jnp.minimum(per_tile * (i + 1), n_half - 1), which))

    return pl.pallas_call(
        _dil_kernel,
        grid=(b, dil, n_steps),
        in_specs=[
            spec_mid(0),
            spec_lo(1), spec_mid(1), spec_hi(1),
            spec_lo(2), spec_mid(2), spec_hi(2),
            pl.BlockSpec((None,) + bias_tables.shape[1:], lambda bi, r, i: (group, 0, 0, 0, 0)),
        ],
        out_specs=[
            pl.BlockSpec((1, 1, tile, D_MODEL), lambda bi, r, i: (bi, r, i, 0)),
            pl.BlockSpec((1, 1, tile, LSE_LANES), lambda bi, r, i: (bi, r, i, 0)),
        ],
        out_shape=[
            jax.ShapeDtypeStruct((b, dil, l, D_MODEL), jnp.bfloat16),
            jax.ShapeDtypeStruct((b, dil, l, LSE_LANES), jnp.float32),
        ],
        compiler_params=pltpu.CompilerParams(
            dimension_semantics=("parallel", "parallel", "arbitrary"),
            vmem_limit_bytes=VMEM_LIMIT),
        name=f"dilated_attention_{dil}",
    )(qkv, qkv, qkv, qkv, qkv, qkv, qkv, bias_tables)


def _silu(x):
    half = 0.5 * x
    return half + half * jnp.tanh(half)


def _out0_kernel(o_ref, gate_ref, x_ref, w_ref, g_ref, y_ref, *rest):
    h_refs, slab_refs = rest[:-2], rest[-2:]
    o = jnp.concatenate([o_ref[k] for k in range(N_SLABS)], axis=1).astype(jnp.float32)
    gate = jnp.concatenate([gate_ref[k] for k in range(N_SLABS)], axis=1).astype(jnp.float32)
    z = (o * _silu(gate)).astype(jnp.bfloat16)
    y = x_ref[0] + jnp.dot(z, w_ref[...], preferred_element_type=jnp.float32)
    y_ref[0] = y
    hn = _rmsnorm(y, g_ref[...])
    tm = hn.shape[0]
    dils = [dil for _, dil in DIL_PAIRS]
    for s in range(N_LANE_SLABS):
        slab_refs[0][s] = hn[:, s * LANES:(s + 1) * LANES]
    staged = {1: slab_refs[0]}
    for k, (h_ref, dil) in enumerate(zip(h_refs, dils)):
        if dil == 1:
            h_ref[0, 0] = hn.astype(h_ref.dtype)
            continue
        p = max(q for q in staged if dil % q == 0)
        f, n = dil // p, tm // dil
        keep = any(later % dil == 0 for later in dils[k + 1:])
        for r in range(dil):
            start = (r % p) * (tm // p) + r // p
            for s in range(N_LANE_SLABS):
                rows = staged[p][s, pl.ds(start, n, stride=f), :]
                h_ref[0, r, :, s * LANES:(s + 1) * LANES] = rows.astype(h_ref.dtype)
                if keep:
                    slab_refs[1][s, r * n:(r + 1) * n, :] = rows
        if keep:
            staged[dil] = slab_refs[1]


def _out_proj0(o, proj, x, w_out, g_next, *, tm=512):
    b, s, d = x.shape
    gate_block = proj.shape[0] // N_SLABS - 1
    tiles = s // tm
    tok = pl.BlockSpec((1, tm, d), lambda bi, i: (bi, i, 0))
    dils = [dil for _, dil in DIL_PAIRS]
    return pl.pallas_call(
        _out0_kernel,
        grid=(b, s // tm),
        in_specs=[
            pl.BlockSpec((N_SLABS, tm, SLAB), lambda bi, i: (0, bi * tiles + i, 0)),
            pl.BlockSpec((N_SLABS, tm, SLAB), lambda bi, i: (gate_block, bi * tiles + i, 0)),
            tok,
            pl.BlockSpec((d, d), lambda bi, i: (0, 0)),
            pl.BlockSpec((1, d), lambda bi, i: (0, 0)),
        ],
        out_specs=[tok] + [pl.BlockSpec((1, dil, tm // dil, d), lambda bi, i: (bi, 0, i, 0))
                           for dil in dils],
        out_shape=([jax.ShapeDtypeStruct((b, s, d), jnp.float32)]
                   + [jax.ShapeDtypeStruct((b, dil, s // dil, d), jnp.bfloat16) for dil in dils]),
        scratch_shapes=[pltpu.VMEM((N_LANE_SLABS, tm, LANES), jnp.float32)] * 2,
        compiler_params=pltpu.CompilerParams(
            dimension_semantics=("parallel", "parallel"), vmem_limit_bytes=VMEM_LIMIT),
        name="out_proj0",
    )(o, proj, x, w_out, g_next.reshape(1, d))


_MAX_ROW_STRIDE = 4


def _natural_order(ref, slab_refs, dil):
    _, n, c = ref.shape
    if dil == 1:
        return ref[0].astype(jnp.float32)
    out_ref, mid_ref = slab_refs
    tm = dil * n
    two_level = dil > _MAX_ROW_STRIDE
    if two_level:
        f = _MAX_ROW_STRIDE
        p = dil // f
        assert p <= _MAX_ROW_STRIDE
    pieces = []
    for s in range(c // LANES):
        lanes = slice(s * LANES, (s + 1) * LANES)
        if not two_level:
            for r in range(dil):
                out_ref[s, pl.ds(r, n, stride=dil), :] = ref[r, :, lanes].astype(jnp.float32)
        else:
            for r in range(dil):
                start = (r % p) * (tm // p) + r // p
                mid_ref[s, pl.ds(start, n, stride=f), :] = ref[r, :, lanes].astype(jnp.float32)
            for b_ in range(p):
                out_ref[s, pl.ds(b_, tm // p, stride=p), :] = (
                    mid_ref[s, b_ * (tm // p):(b_ + 1) * (tm // p), :])
        pieces.append(out_ref[s])
    return pieces[0] if len(pieces) == 1 else jnp.concatenate(pieces, axis=1)


def _out1_kernel(o0_ref, o1_ref, o2_ref, l0_ref, l1_ref, l2_ref, gate_ref, x_ref,
                 w_ref, e_ref, g_ref, y_ref, *slab_refs):
    dils = [dil for _, dil in DIL_PAIRS]
    lses = [_natural_order(ref.at[0], slab_refs, dil)
            for ref, dil in zip((l0_ref, l1_ref, l2_ref), dils)]
    m = jnp.maximum(jnp.maximum(lses[0], lses[1]), lses[2])
    es = [jnp.exp2(v - m) for v in lses]
    denom = es[0] + es[1] + es[2]
    o = None
    for e, o_ref, dil in zip(es, (o0_ref, o1_ref, o2_ref), dils):
        w = e / denom
        hi = w.astype(jnp.bfloat16)
        lo = (w - hi.astype(jnp.float32)).astype(jnp.bfloat16)
        w_full = jnp.dot(jnp.concatenate([hi, lo], axis=1), e_ref[...],
                         preferred_element_type=jnp.float32)
        term = w_full * _natural_order(o_ref.at[0], slab_refs, dil)
        o = term if o is None else o + term
    gate = gate_ref[0].astype(jnp.float32)
    z = (o * _silu(gate)).astype(jnp.bfloat16)
    x = x_ref[0] + jnp.dot(z, w_ref[...], preferred_element_type=jnp.float32)
    y_ref[0] = _rmsnorm(x, g_ref[...])


def _head_expansion():
    e = np.zeros((LSE_LANES, D_MODEL), np.float32)
    for h in range(N_HEADS):
        e[h, h * HEAD_DIM:(h + 1) * HEAD_DIM] = 1.0
    return jnp.asarray(np.concatenate([e, e], axis=0), dtype=jnp.bfloat16)


def _out_proj1(os_, lses, proj, x, w_out, norm_f, *, tm=512):
    b, s, d = x.shape
    gate_block = proj.shape[2] // d - 1
    dils = [dil for _, dil in DIL_PAIRS]

    def grouped(width, dil):
        return pl.BlockSpec((1, dil, tm // dil, width), lambda bi, i: (bi, 0, i, 0))

    tok = pl.BlockSpec((1, tm, d), lambda bi, i: (bi, i, 0))
    return pl.pallas_call(
        _out1_kernel,
        grid=(b, s // tm),
        in_specs=(
            [grouped(d, dil) for dil in dils]
            + [grouped(LSE_LANES, dil) for dil in dils]
            + [pl.BlockSpec((1, tm, d), lambda bi, i: (bi, i, gate_block)),
               tok,
               pl.BlockSpec((d, d), lambda bi, i: (0, 0)),
               pl.BlockSpec((2 * LSE_LANES, d), lambda bi, i: (0, 0)),
               pl.BlockSpec((1, d), lambda bi, i: (0, 0))]),
        out_specs=tok,
        out_shape=jax.ShapeDtypeStruct((b, s, d), jnp.float32),
        scratch_shapes=[pltpu.VMEM((N_LANE_SLABS, tm, LANES), jnp.float32)] * 2,
        compiler_params=pltpu.CompilerParams(
            dimension_semantics=("parallel", "parallel"), vmem_limit_bytes=VMEM_LIMIT),
        name="out_proj1",
    )(*os_, *lses, proj, x, w_out, _head_expansion(), norm_f.reshape(1, d))


def kernel(x, norm_0, w_in_0, rpb_0, w_out_0, norm_1, w_in_1, w_out_1, norm_f):
    b, s, d = x.shape
    t = b * s

    proj0 = _norm_proj(x.reshape(t, d), norm_0, w_in_0)
    o0 = _na_attention(proj0, _na_bias_table(rpb_0), b)
    x1, *h1 = _out_proj0(o0, proj0, x, w_out_0.astype(jnp.bfloat16), norm_1)

    gate_block = 3 * N_DIL_GROUPS
    alibi = _dil_bias_tables()
    outs, lses, proj_gate = [], [], None
    for g, (window, dil) in enumerate(DIL_PAIRS):
        assert window // (2 * dil) == DIL_RADIUS
        blocks = (3 * g, 3 * g + 1, 3 * g + 2) + ((gate_block,) if dil == 1 else ())
        qkv = _proj(h1[g].reshape(t, d), w_in_1, blocks).reshape(b, dil, s // dil, -1)
        if dil == 1:
            proj_gate = qkv.reshape(b, s, -1)
        o_g, lse_g = _dil_attention(qkv, alibi, g)
        outs.append(o_g)
        lses.append(lse_g)
    return _out_proj1(outs, lses, proj_gate, x1, w_out_1.astype(jnp.bfloat16), norm_f)
```

```python
import functools
import math

import numpy as np
import jax
import jax.numpy as jnp
from jax import lax
from jax.experimental import pallas as pl
from jax.experimental.pallas import tpu as pltpu

D_MODEL = 1024
HEAD_DIM = 64
N_HEADS = 16
GRID_W = 64
NA_ROWS = 8
NA_COLS = 16
DIL_PAIRS = ((128, 1), (512, 4), (2048, 16))
N_DIL_GROUPS = len(DIL_PAIRS)
RMS_EPS = 1e-6
NEG_INF = -1e30
LOG2E = math.log2(math.e)

LANES = 128
N_LANE_SLABS = D_MODEL // LANES
HEADS_PER_STEP = 4
SLAB = HEADS_PER_STEP * HEAD_DIM
N_SLABS = D_MODEL // SLAB
Q_TILE = 128
NA_Q_TILE = GRID_W
NA_WIN = NA_ROWS * GRID_W
NA_TILES_PER_STEP = 32
DIL_TILES_PER_STEP = 8
DIL_RADIUS = 64
DIL_WIN = Q_TILE + 2 * DIL_RADIUS
LSE_LANES = LANES
VMEM_LIMIT = 56 * 1024 * 1024

_NT_DIMS = (((1,), (1,)), ((), ()))


def _rmsnorm(x, g):
    ms = jnp.mean(x * x, axis=-1, keepdims=True)
    return x * lax.rsqrt(ms + RMS_EPS) * g


def _q_scaled_bf16(w_ref):
    scale = jnp.where(pl.program_id(1) == 0, LOG2E / math.sqrt(HEAD_DIM), 1.0)
    return (w_ref[...] * scale).astype(jnp.bfloat16)


def _norm_proj_kernel(x_ref, g_ref, w_ref, o_ref, h_ref):
    i, j = pl.program_id(0), pl.program_id(1)
    rows = x_ref.shape[0]

    def normalise_chunk():
        start = pl.multiple_of(j * rows, rows)
        h_ref[i % 2, pl.ds(start, rows), :] = _rmsnorm(x_ref[...], g_ref[...]).astype(h_ref.dtype)

    @pl.when(i == 0)
    def _():
        normalise_chunk()

    @pl.when(i > 0)
    def _():
        res = jnp.dot(h_ref[(i - 1) % 2], _q_scaled_bf16(w_ref),
                      preferred_element_type=jnp.float32).astype(o_ref.dtype)
        for k in range(o_ref.shape[0]):
            o_ref[k] = res[:, k * SLAB:(k + 1) * SLAB]
        normalise_chunk()


def _norm_proj(x, g, w, *, tm=2048, tn=1024):
    t, d = x.shape
    n = w.shape[1]
    n_i, n_j = t // tm, n // tn
    chunk = tm // n_j
    n_chunks = t // chunk
    return pl.pallas_call(
        _norm_proj_kernel,
        grid=(n_i + 1, n_j),
        in_specs=[
            pl.BlockSpec((chunk, d), lambda i, j: (jnp.minimum(i * n_j + j, n_chunks - 1), 0)),
            pl.BlockSpec((1, d), lambda i, j: (0, 0)),
            pl.BlockSpec((d, tn), lambda i, j: (0, j)),
        ],
        out_specs=pl.BlockSpec((tn // SLAB, tm, SLAB),
                               lambda i, j: (jnp.where(i == 0, 0, j), jnp.maximum(i - 1, 0), 0)),
        out_shape=jax.ShapeDtypeStruct((n // SLAB, t, SLAB), jnp.bfloat16),
        scratch_shapes=[pltpu.VMEM((2, tm, d), jnp.bfloat16)],
        compiler_params=pltpu.CompilerParams(
            dimension_semantics=("arbitrary", "arbitrary"),
            vmem_limit_bytes=VMEM_LIMIT),
        name="norm_proj",
    )(x, g.reshape(1, d), w)


def _proj_kernel(h_ref, w_ref, o_ref):
    o_ref[...] = jnp.dot(h_ref[...], _q_scaled_bf16(w_ref),
                         preferred_element_type=jnp.float32).astype(o_ref.dtype)


def _proj(h, w, col_blocks, *, tm=2048):
    t, d = h.shape
    tn = d
    n = len(col_blocks) * tn

    def w_block(j):
        blk = col_blocks[-1]
        for k in range(len(col_blocks) - 2, -1, -1):
            blk = jnp.where(j == k, col_blocks[k], blk)
        return blk

    return pl.pallas_call(
        _proj_kernel,
        grid=(t // tm, n // tn),
        in_specs=[
            pl.BlockSpec((tm, d), lambda i, j: (i, 0)),
            pl.BlockSpec((d, tn), lambda i, j: (0, w_block(j))),
        ],
        out_specs=pl.BlockSpec((tm, tn), lambda i, j: (i, j)),
        out_shape=jax.ShapeDtypeStruct((t, n), jnp.bfloat16),
        compiler_params=pltpu.CompilerParams(
            dimension_semantics=("parallel", "parallel"),
            vmem_limit_bytes=VMEM_LIMIT),
        name="proj",
    )(h, w)


def _na_kernel(q_ref, k_ref, v_ref, b_ref, o_ref):
    step = pl.program_id(2)
    n_tiles = k_ref.shape[1] // GRID_W
    lane = lax.broadcasted_iota(jnp.int32, (NA_Q_TILE, SLAB), 1) // HEAD_DIM

    for u in range(NA_TILES_PER_STEP):
        r = step * NA_TILES_PER_STEP + u
        first_row = jnp.clip(r - NA_ROWS // 2, 0, n_tiles - NA_ROWS)
        start = pl.multiple_of(first_row * GRID_W, GRID_W)
        tile_rows = slice(u * NA_Q_TILE, (u + 1) * NA_Q_TILE)
        q4 = q_ref[0, tile_rows, :]
        zero = jnp.zeros_like(q4)
        q_stack = jnp.concatenate(
            [jnp.where(lane == h, q4, zero) for h in range(HEADS_PER_STEP)], axis=0)
        s = lax.dot_general(q_stack, k_ref[0, pl.ds(start, NA_WIN), :], _NT_DIMS,
                            preferred_element_type=jnp.float32)
        first_off = first_row - r + NA_ROWS - 1
        s = s + jnp.concatenate(
            [jnp.concatenate([b_ref[h, first_off + 2 * j] for j in range(NA_ROWS // 2)], axis=1)
             for h in range(HEADS_PER_STEP)], axis=0)
        m = jnp.max(s, axis=-1, keepdims=True)
        e = jnp.exp2(s - m)
        inv_l = 1.0 / jnp.sum(e, axis=-1, keepdims=True)
        o_all = jnp.dot(e.astype(jnp.bfloat16), v_ref[0, pl.ds(start, NA_WIN), :],
                        preferred_element_type=jnp.float32) * inv_l
        out = o_all[:NA_Q_TILE]
        for h in range(1, HEADS_PER_STEP):
            out = jnp.where(lane == h, o_all[h * NA_Q_TILE:(h + 1) * NA_Q_TILE], out)
        o_ref[0, tile_rows, :] = out.astype(o_ref.dtype)


def _na_bias_table(rpb):
    n_col_off = 2 * NA_COLS - 1
    c = np.arange(GRID_W)[:, None]
    kc = np.arange(GRID_W)[None, :]
    cs = np.clip(c - NA_COLS // 2, 0, GRID_W - NA_COLS)
    col_ok = (kc >= cs) & (kc < cs + NA_COLS)
    col_off = kc - c + NA_COLS - 1
    assert np.all(((col_off >= 0) & (col_off < n_col_off))[col_ok])
    pick = (np.arange(n_col_off)[:, None, None] == col_off[None]) & col_ok[None]
    pick2 = np.zeros((2, n_col_off, GRID_W, 2, GRID_W), np.float32)
    for half in range(2):
        pick2[half, :, :, half, :] = pick
    pick2 = pick2.reshape(2 * n_col_off, GRID_W, 2 * GRID_W)
    rpb = rpb.astype(jnp.float32)
    row_pairs = jnp.concatenate([rpb[:, :-1], rpb[:, 1:]], axis=-1)
    table = jnp.einsum('hrm,mck->hrck', row_pairs, jnp.asarray(pick2),
                       precision=lax.Precision.HIGHEST)
    ok2 = np.concatenate([col_ok, col_ok], axis=1)
    return jnp.where(jnp.asarray(ok2), table * LOG2E, NEG_INF)


def _na_attention(proj, bias_table, b):
    s = proj.shape[1] // b
    tile = NA_Q_TILE * NA_TILES_PER_STEP
    assert s % tile == 0 and NA_Q_TILE == GRID_W
    n_blocks = s // tile
    return pl.pallas_call(
        _na_kernel,
        grid=(b, N_SLABS, n_blocks),
        in_specs=[
            pl.BlockSpec((1, tile, SLAB), lambda bi, g, j: (g, bi * n_blocks + j, 0)),
            pl.BlockSpec((1, s, SLAB), lambda bi, g, j: (N_SLABS + g, bi, 0)),
            pl.BlockSpec((1, s, SLAB), lambda bi, g, j: (2 * N_SLABS + g, bi, 0)),
            pl.BlockSpec((HEADS_PER_STEP,) + bias_table.shape[1:], lambda bi, g, j: (g, 0, 0, 0)),
        ],
        out_specs=pl.BlockSpec((1, tile, SLAB), lambda bi, g, j: (g, bi * n_blocks + j, 0)),
        out_shape=jax.ShapeDtypeStruct((N_SLABS, b * s, SLAB), jnp.bfloat16),
        compiler_params=pltpu.CompilerParams(
            dimension_semantics=("parallel", "parallel", "arbitrary"),
            vmem_limit_bytes=VMEM_LIMIT),
        name="na_attention",
    )(proj, proj, proj, bias_table)


def _alibi_slopes():
    return np.asarray(2.0 ** (-8.0 * (np.arange(N_HEADS) + 1) / N_HEADS), dtype=np.float32)


def _slab_attention(q4, k4, v4, bias):
    lane = lax.broadcasted_iota(jnp.int32, q4.shape, 1) // HEAD_DIM
    zero = jnp.zeros_like(q4)
    q_stack = jnp.concatenate(
        [jnp.where(lane == h, q4, zero) for h in range(HEADS_PER_STEP)], axis=0)
    s = lax.dot_general(q_stack, k4, _NT_DIMS, preferred_element_type=jnp.float32)
    s = s + bias
    m = jnp.max(s, axis=-1, keepdims=True)
    p = jnp.exp2(s - m)
    l = jnp.sum(p, axis=-1, keepdims=True)
    p = p.astype(jnp.bfloat16)
    o_all = jnp.dot(p, v4, preferred_element_type=jnp.float32) * (1.0 / l)
    out = o_all[:Q_TILE]
    for h in range(1, HEADS_PER_STEP):
        out = jnp.where(lane == h, o_all[h * Q_TILE:(h + 1) * Q_TILE], out)
    return out, m, l


def _dil_kernel(q_ref, ka_ref, kb_ref, kc_ref, va_ref, vb_ref, vc_ref, b_ref,
                o_ref, lse_ref):
    step = pl.program_id(2)
    n_steps = pl.num_programs(2)
    k_win = jnp.concatenate([ka_ref[0, 0], kb_ref[0, 0], kc_ref[0, 0]], axis=0)
    v_win = jnp.concatenate([va_ref[0, 0], vb_ref[0, 0], vc_ref[0, 0]], axis=0)
    lane = lax.broadcasted_iota(jnp.int32, (Q_TILE, LSE_LANES), 1)
    tiles = q_ref.shape[2] // Q_TILE
    for u in range(tiles):
        pattern = 0
        if u == 0:
            pattern = jnp.where(step == 0, 1, pattern)
        if u == tiles - 1:
            pattern = jnp.where(step == n_steps - 1, 2, pattern)
        tile = slice(u * Q_TILE, (u + 1) * Q_TILE)
        win = slice(u * Q_TILE, u * Q_TILE + DIL_WIN)
        lse_tile = jnp.zeros((Q_TILE, LSE_LANES), jnp.float32)
        for hg in range(N_SLABS):
            cols = slice(hg * SLAB, (hg + 1) * SLAB)
            bias = b_ref[pattern, hg * HEADS_PER_STEP:(hg + 1) * HEADS_PER_STEP]
            bias = bias.reshape(HEADS_PER_STEP * Q_TILE, DIL_WIN)
            out, m, l = _slab_attention(q_ref[0, 0, tile, cols], k_win[win, cols],
                                        v_win[win, cols], bias)
            o_ref[0, 0, tile, cols] = out.astype(o_ref.dtype)
            lse = m + jnp.log2(l)
            for h in range(HEADS_PER_STEP):
                lse_h = lse[h * Q_TILE:(h + 1) * Q_TILE]
                lse_tile = jnp.where(lane == hg * HEADS_PER_STEP + h, lse_h, lse_tile)
        lse_ref[0, 0, tile, :] = lse_tile


def _dil_bias_tables():
    qi = np.arange(Q_TILE)[:, None]
    kj = np.arange(DIL_WIN)[None, :]
    delta = kj - DIL_RADIUS - qi
    in_band = np.abs(delta) <= DIL_RADIUS
    valid = np.stack([in_band,
                      in_band & (kj >= DIL_RADIUS),
                      in_band & (kj < DIL_WIN - DIL_RADIUS)])
    dils = np.asarray([dil for _, dil in DIL_PAIRS])
    dist = jnp.asarray(np.abs(delta)[None] * dils[:, None, None], dtype=jnp.float32)
    bias = -jnp.asarray(_alibi_slopes())[None, :, None, None] * dist[:, None]
    return jnp.where(jnp.asarray(valid)[None, :, None], bias[:, None] * LOG2E, NEG_INF)


def _dil_attention(qkv, bias_tables, group):
    b, dil, l, _ = qkv.shape
    tiles = min(DIL_TILES_PER_STEP, l // Q_TILE)
    tile = Q_TILE * tiles
    n_steps = l // tile
    assert l % tile == 0 and tiles >= 2
    half = DIL_RADIUS
    per_tile = tile // half
    n_half = l // half

    def spec_mid(which):
        return pl.BlockSpec((1, 1, tile, D_MODEL), lambda bi, r, i: (bi, r, i, which))

    def spec_lo(which):
        return pl.BlockSpec((1, 1, half, D_MODEL),
                            lambda bi, r, i: (bi, r, jnp.maximum(per_tile * i - 1, 0), which))

    def spec_hi(which):
        return pl.BlockSpec((1, 1, half, D_MODEL),
                            lambda bi, r, i: (bi, r, jnp.minimum(per_tile * (i + 1), n_half - 1), which))

    return pl.pallas_call(
        _dil_kernel,
        grid=(b, dil, n_steps),
        in_specs=[
            spec_mid(0),
            spec_lo(1), spec_mid(1), spec_hi(1),
            spec_lo(2), spec_mid(2), spec_hi(2),
            pl.BlockSpec((None,) + bias_tables.shape[1:], lambda bi, r, i: (group, 0, 0, 0, 0)),
        ],
        out_specs=[
            pl.BlockSpec((1, 1, tile, D_MODEL), lambda bi, r, i: (bi, r, i, 0)),
            pl.BlockSpec((1, 1, tile, LSE_LANES), lambda bi, r, i: (bi, r, i, 0)),
        ],
        out_shape=[
            jax.ShapeDtypeStruct((b, dil, l, D_MODEL), jnp.bfloat16),
            jax.ShapeDtypeStruct((b, dil, l, LSE_LANES), jnp.float32),
        ],
        compiler_params=pltpu.CompilerParams(
            dimension_semantics=("parallel", "parallel", "arbitrary"),
            vmem_limit_bytes=VMEM_LIMIT),
        name=f"dilated_attention_{dil}",
    )(qkv, qkv, qkv, qkv, qkv, qkv, qkv, bias_tables)


def _silu(x):
    half = 0.5 * x
    return half + half * jnp.tanh(half)


def _out0_kernel(o_ref, gate_ref, x_ref, w_ref, g_ref, y_ref, *rest):
    h_refs, slab_refs = rest[:-2], rest[-2:]
    o = jnp.concatenate([o_ref[k] for k in range(N_SLABS)], axis=1).astype(jnp.float32)
    gate = jnp.concatenate([gate_ref[k] for k in range(N_SLABS)], axis=1).astype(jnp.float32)
    z = (o * _silu(gate)).astype(jnp.bfloat16)
    y = x_ref[0] + jnp.dot(z, w_ref[...], preferred_element_type=jnp.float32)
    y_ref[0] = y
    hn = _rmsnorm(y, g_ref[...])
    tm = hn.shape[0]
    dils = [dil for _, dil in DIL_PAIRS]
    for s in range(N_LANE_SLABS):
        slab_refs[0][s] = hn[:, s * LANES:(s + 1) * LANES]
    staged = {1: slab_refs[0]}
    for k, (h_ref, dil) in enumerate(zip(h_refs, dils)):
        if dil == 1:
            h_ref[0, 0] = hn.astype(h_ref.dtype)
            continue
        p = max(q for q in staged if dil % q == 0)
        f, n = dil // p, tm // dil
        keep = any(later % dil == 0 for later in dils[k + 1:])
        for r in range(dil):
            start = (r % p) * (tm // p) + r // p
            for s in range(N_LANE_SLABS):
                rows = staged[p][s, pl.ds(start, n, stride=f), :]
                h_ref[0, r, :, s * LANES:(s + 1) * LANES] = rows.astype(h_ref.dtype)
                if keep:
                    slab_refs[1][s, r * n:(r + 1) * n, :] = rows
        if keep:
            staged[dil] = slab_refs[1]


def _out_proj0(o, proj, x, w_out, g_next, *, tm=512):
    b, s, d = x.shape
    gate_block = proj.shape[0] // N_SLABS - 1
    tiles = s // tm
    tok = pl.BlockSpec((1, tm, d), lambda bi, i: (bi, i, 0))
    dils = [dil for _, dil in DIL_PAIRS]
    return pl.pallas_call(
        _out0_kernel,
        grid=(b, s // tm),
        in_specs=[
            pl.BlockSpec((N_SLABS, tm, SLAB), lambda bi, i: (0, bi * tiles + i, 0)),
            pl.BlockSpec((N_SLABS, tm, SLAB), lambda bi, i: (gate_block, bi * tiles + i, 0)),
            tok,
            pl.BlockSpec((d, d), lambda bi, i: (0, 0)),
            pl.BlockSpec((1, d), lambda bi, i: (0, 0)),
        ],
        out_specs=[tok] + [pl.BlockSpec((1, dil, tm // dil, d), lambda bi, i: (bi, 0, i, 0))
                           for dil in dils],
        out_shape=([jax.ShapeDtypeStruct((b, s, d), jnp.float32)]
                   + [jax.ShapeDtypeStruct((b, dil, s // dil, d), jnp.bfloat16) for dil in dils]),
        scratch_shapes=[pltpu.VMEM((N_LANE_SLABS, tm, LANES), jnp.float32)] * 2,
        compiler_params=pltpu.CompilerParams(
            dimension_semantics=("parallel", "parallel"), vmem_limit_bytes=VMEM_LIMIT),
        name="out_proj0",
    )(o, proj, x, w_out, g_next.reshape(1, d))


_MAX_ROW_STRIDE = 4


def _natural_order(ref, slab_refs, dil):
    _, n, c = ref.shape
    if dil == 1:
        return ref[0].astype(jnp.float32)
    out_ref, mid_ref = slab_refs
    tm = dil * n
    two_level = dil > _MAX_ROW_STRIDE
    if two_level:
        f = _MAX_ROW_STRIDE
        p = dil // f
        assert p <= _MAX_ROW_STRIDE
    pieces = []
    for s in range(c // LANES):
        lanes = slice(s * LANES, (s + 1) * LANES)
        if not two_level:
            for r in range(dil):
                out_ref[s, pl.ds(r, n, stride=dil), :] = ref[r, :, lanes].astype(jnp.float32)
        else:
            for r in range(dil):
                start = (r % p) * (tm // p) + r // p
                mid_ref[s, pl.ds(start, n, stride=f), :] = ref[r, :, lanes].astype(jnp.float32)
            for b_ in range(p):
                out_ref[s, pl.ds(b_, tm // p, stride=p), :] = (
                    mid_ref[s, b_ * (tm // p):(b_ + 1) * (tm // p), :])
        pieces.append(out_ref[s])
    return pieces[0] if len(pieces) == 1 else jnp.concatenate(pieces, axis=1)


def _out1_kernel(o0_ref, o1_ref, o2_ref, l0_ref, l1_ref, l2_ref, gate_ref, x_ref,
                 w_ref, e_ref, g_ref, y_ref, *slab_refs):
    dils = [dil for _, dil in DIL_PAIRS]
    lses = [_natural_order(ref.at[0], slab_refs, dil)
            for ref, dil in zip((l0_ref, l1_ref, l2_ref), dils)]
    m = jnp.maximum(jnp.maximum(lses[0], lses[1]), lses[2])
    es = [jnp.exp2(v - m) for v in lses]
    denom = es[0] + es[1] + es[2]
    o = None
    for e, o_ref, dil in zip(es, (o0_ref, o1_ref, o2_ref), dils):
        w = e / denom
        hi = w.astype(jnp.bfloat16)
        lo = (w - hi.astype(jnp.float32)).astype(jnp.bfloat16)
        w_full = jnp.dot(jnp.concatenate([hi, lo], axis=1), e_ref[...],
                         preferred_element_type=jnp.float32)
        term = w_full * _natural_order(o_ref.at[0], slab_refs, dil)
        o = term if o is None else o + term
    gate = gate_ref[0].astype(jnp.float32)
    z = (o * _silu(gate)).astype(jnp.bfloat16)
    x = x_ref[0] + jnp.dot(z, w_ref[...], preferred_element_type=jnp.float32)
    y_ref[0] = _rmsnorm(x, g_ref[...])


def _head_expansion():
    e = np.zeros((LSE_LANES, D_MODEL), np.float32)
    for h in range(N_HEADS):
        e[h, h * HEAD_DIM:(h + 1) * HEAD_DIM] = 1.0
    return jnp.asarray(np.concatenate([e, e], axis=0), dtype=jnp.bfloat16)


def _out_proj1(os_, lses, proj, x, w_out, norm_f, *, tm=512):
    b, s, d = x.shape
    gate_block = proj.shape[2] // d - 1
    dils = [dil for _, dil in DIL_PAIRS]

    def grouped(width, dil):
        return pl.BlockSpec((1, dil, tm // dil, width), lambda bi, i: (bi, 0, i, 0))

    tok = pl.BlockSpec((1, tm, d), lambda bi, i: (bi, i, 0))
    return pl.pallas_call(
        _out1_kernel,
        grid=(b, s // tm),
        in_specs=(
            [grouped(d, dil) for dil in dils]
            + [grouped(LSE_LANES, dil) for dil in dils]
            + [pl.BlockSpec((1, tm, d), lambda bi, i: (bi, i, gate_block)),
               tok,
               pl.BlockSpec((d, d), lambda bi, i: (0, 0)),
               pl.BlockSpec((2 * LSE_LANES, d), lambda bi, i: (0, 0)),
               pl.BlockSpec((1, d), lambda bi, i: (0, 0))]),
        out_specs=tok,
        out_shape=jax.ShapeDtypeStruct((b, s, d), jnp.float32),
        scratch_shapes=[pltpu.VMEM((N_LANE_SLABS, tm, LANES), jnp.float32)] * 2,
        compiler_params=pltpu.CompilerParams(
            dimension_semantics=("parallel", "parallel"), vmem_limit_bytes=VMEM_LIMIT),
        name="out_proj1",
    )(*os_, *lses, proj, x, w_out, _head_expansion(), norm_f.reshape(1, d))


def kernel(x, norm_0, w_in_0, rpb_0, w_out_0, norm_1, w_in_1, w_out_1, norm_f):
    b, s, d = x.shape
    t = b * s

    proj0 = _norm_proj(x.reshape(t, d), norm_0, w_in_0)
    o0 = _na_attention(proj0, _na_bias_table(rpb_0), b)
    x1, *h1 = _out_proj0(o0, proj0, x, w_out_0.astype(jnp.bfloat16), norm_1)

    gate_block = 3 * N_DIL_GROUPS
    alibi = _dil_bias_tables()
    outs, lses, proj_gate = [], [], None
    for g, (window, dil) in enumerate(DIL_PAIRS):
        assert window // (2 * dil) == DIL_RADIUS
        blocks = (3 * g, 3 * g + 1, 3 * g + 2) + ((gate_block,) if dil == 1 else ())
        qkv = _proj(h1[g].reshape(t, d), w_in_1, blocks).reshape(b, dil, s // dil, -1)
        if dil == 1:
            proj_gate = qkv.reshape(b, s, -1)
        o_g, lse_g = _dil_attention(qkv, alibi, g)
        outs.append(o_g)
        lses.append(lse_g)
    return _out_proj1(outs, lses, proj_gate, x1, w_out_1.astype(jnp.bfloat16), norm_f)
```

```python
import functools
import math

import numpy as np
import jax
import jax.numpy as jnp
from jax import lax
from jax.experimental import pallas as pl
from jax.experimental.pallas import tpu as pltpu

D_MODEL = 1024
HEAD_DIM = 64
N_HEADS = 16
GRID_W = 64
NA_ROWS = 8
NA_COLS = 16
DIL_PAIRS = ((128, 1), (512, 4), (2048, 16))
N_DIL_GROUPS = len(DIL_PAIRS)
RMS_EPS = 1e-6
NEG_INF = -1e30
LOG2E = math.log2(math.e)

LANES = 128
N_LANE_SLABS = D_MODEL // LANES
HEADS_PER_STEP = 4
SLAB = HEADS_PER_STEP * HEAD_DIM
N_SLABS = D_MODEL // SLAB
Q_TILE = 128
NA_Q_TILE = GRID_W
NA_WIN = NA_ROWS * GRID_W
NA_TILES_PER_STEP = 32
DIL_TILES_PER_STEP = 8
DIL_RADIUS = 64
DIL_WIN = Q_TILE + 2 * DIL_RADIUS
LSE_LANES = LANES
VMEM_LIMIT = 56 * 1024 * 1024

_NT_DIMS = (((1,), (1,)), ((), ()))


def _rmsnorm(x, g):
    ms = jnp.mean(x * x, axis=-1, keepdims=True)
    return x * lax.rsqrt(ms + RMS_EPS) * g


def _q_scaled_bf16(w_ref):
    scale = jnp.where(pl.program_id(1) == 0, LOG2E / math.sqrt(HEAD_DIM), 1.0)
    return (w_ref[...] * scale).astype(jnp.bfloat16)


def _norm_proj_kernel(x_ref, g_ref, w_ref, o_ref, h_ref):
    i, j = pl.program_id(0), pl.program_id(1)
    rows = x_ref.shape[0]

    def normalise_chunk():
        start = pl.multiple_of(j * rows, rows)
        h_ref[i % 2, pl.ds(start, rows), :] = _rmsnorm(x_ref[...], g_ref[...]).astype(h_ref.dtype)

    @pl.when(i == 0)
    def _():
        normalise_chunk()

    @pl.when(i > 0)
    def _():
        res = jnp.dot(h_ref[(i - 1) % 2], _q_scaled_bf16(w_ref),
                      preferred_element_type=jnp.float32).astype(o_ref.dtype)
        for k in range(o_ref.shape[0]):
            o_ref[k] = res[:, k * SLAB:(k + 1) * SLAB]
        normalise_chunk()


def _norm_proj(x, g, w, *, tm=2048, tn=1024):
    t, d = x.shape
    n = w.shape[1]
    n_i, n_j = t // tm, n // tn
    chunk = tm // n_j
    n_chunks = t // chunk
    return pl.pallas_call(
        _norm_proj_kernel,
        grid=(n_i + 1, n_j),
        in_specs=[
            pl.BlockSpec((chunk, d), lambda i, j: (jnp.minimum(i * n_j + j, n_chunks - 1), 0)),
            pl.BlockSpec((1, d), lambda i, j: (0, 0)),
            pl.BlockSpec((d, tn), lambda i, j: (0, j)),
        ],
        out_specs=pl.BlockSpec((tn // SLAB, tm, SLAB),
                               lambda i, j: (jnp.where(i == 0, 0, j), jnp.maximum(i - 1, 0), 0)),
        out_shape=jax.ShapeDtypeStruct((n // SLAB, t, SLAB), jnp.bfloat16),
        scratch_shapes=[pltpu.VMEM((2, tm, d), jnp.bfloat16)],
        compiler_params=pltpu.CompilerParams(
            dimension_semantics=("arbitrary", "arbitrary"),
            vmem_limit_bytes=VMEM_LIMIT),
        name="norm_proj",
    )(x, g.reshape(1, d), w)


def _proj_kernel(h_ref, w_ref, o_ref):
    o_ref[...] = jnp.dot(h_ref[...], _q_scaled_bf16(w_ref),
                         preferred_element_type=jnp.float32).astype(o_ref.dtype)


def _proj(h, w, col_blocks, *, tm=2048):
    t, d = h.shape
    tn = d
    n = len(col_blocks) * tn

    def w_block(j):
        blk = col_blocks[-1]
        for k in range(len(col_blocks) - 2, -1, -1):
            blk = jnp.where(j == k, col_blocks[k], blk)
        return blk

    return pl.pallas_call(
        _proj_kernel,
        grid=(t // tm, n // tn),
        in_specs=[
            pl.BlockSpec((tm, d), lambda i, j: (i, 0)),
            pl.BlockSpec((d, tn), lambda i, j: (0, w_block(j))),
        ],
        out_specs=pl.BlockSpec((tm, tn), lambda i, j: (i, j)),
        out_shape=jax.ShapeDtypeStruct((t, n), jnp.bfloat16),
        compiler_params=pltpu.CompilerParams(
            dimension_semantics=("parallel", "parallel"),
            vmem_limit_bytes=VMEM_LIMIT),
        name="proj",
    )(h, w)


def _na_kernel(q_ref, k_ref, v_ref, b_ref, o_ref):
    step = pl.program_id(2)
    n_tiles = k_ref.shape[1] // GRID_W
    lane = lax.broadcasted_iota(jnp.int32, (NA_Q_TILE, SLAB), 1) // HEAD_DIM

    for u in range(NA_TILES_PER_STEP):
        r = step * NA_TILES_PER_STEP + u
        first_row = jnp.clip(r - NA_ROWS // 2, 0, n_tiles - NA_ROWS)
        start = pl.multiple_of(first_row * GRID_W, GRID_W)
        tile_rows = slice(u * NA_Q_TILE, (u + 1) * NA_Q_TILE)
        q4 = q_ref[0, tile_rows, :]
        zero = jnp.zeros_like(q4)
        q_stack = jnp.concatenate(
            [jnp.where(lane == h, q4, zero) for h in range(HEADS_PER_STEP)], axis=0)
        s = lax.dot_general(q_stack, k_ref[0, pl.ds(start, NA_WIN), :], _NT_DIMS,
                            preferred_element_type=jnp.float32)
        first_off = first_row - r + NA_ROWS - 1
        s = s + jnp.concatenate(
            [jnp.concatenate([b_ref[h, first_off + 2 * j] for j in range(NA_ROWS // 2)], axis=1)
             for h in range(HEADS_PER_STEP)], axis=0)
        m = jnp.max(s, axis=-1, keepdims=True)
        e = jnp.exp2(s - m)
        inv_l = 1.0 / jnp.sum(e, axis=-1, keepdims=True)
        o_all = jnp.dot(e.astype(jnp.bfloat16), v_ref[0, pl.ds(start, NA_WIN), :],
                        preferred_element_type=jnp.float32) * inv_l
        out = o_all[:NA_Q_TILE]
        for h in range(1, HEADS_PER_STEP):
            out = jnp.where(lane == h, o_all[h * NA_Q_TILE:(h + 1) * NA_Q_TILE], out)
        o_ref[0, tile_rows, :] = out.astype(o_ref.dtype)


def _na_bias_table(rpb):
    n_col_off = 2 * NA_COLS - 1
    c = np.arange(GRID_W)[:, None]
    kc = np.arange(GRID_W)[None, :]
    cs = np.clip(c - NA_COLS // 2, 0, GRID_W - NA_COLS)
    col_ok = (kc >= cs) & (kc < cs + NA_COLS)
    col_off = kc - c + NA_COLS - 1
    assert np.all(((col_off >= 0) & (col_off < n_col_off))[col_ok])
    pick = (np.arange(n_col_off)[:, None, None] == col_off[None]) & col_ok[None]
    pick2 = np.zeros((2, n_col_off, GRID_W, 2, GRID_W), np.float32)
    for half in range(2):
        pick2[half, :, :, half, :] = pick
    pick2 = pick2.reshape(2 * n_col_off, GRID_W, 2 * GRID_W)
    rpb = rpb.astype(jnp.float32)
    row_pairs = jnp.concatenate([rpb[:, :-1], rpb[:, 1:]], axis=-1)
    table = jnp.einsum('hrm,mck->hrck', row_pairs, jnp.asarray(pick2),
                       precision=lax.Precision.HIGHEST)
    ok2 = np.concatenate([col_ok, col_ok], axis=1)
    return jnp.where(jnp.asarray(ok2), table * LOG2E, NEG_INF)


def _na_attention(proj, bias_table, b):
    s = proj.shape[1] // b
    tile = NA_Q_TILE * NA_TILES_PER_STEP
    assert s % tile == 0 and NA_Q_TILE == GRID_W
    n_blocks = s // tile
    return pl.pallas_call(
        _na_kernel,
        grid=(b, N_SLABS, n_blocks),
        in_specs=[
            pl.BlockSpec((1, tile, SLAB), lambda bi, g, j: (g, bi * n_blocks + j, 0)),
            pl.BlockSpec((1, s, SLAB), lambda bi, g, j: (N_SLABS + g, bi, 0)),
            pl.BlockSpec((1, s, SLAB), lambda bi, g, j: (2 * N_SLABS + g, bi, 0)),
            pl.BlockSpec((HEADS_PER_STEP,) + bias_table.shape[1:], lambda bi, g, j: (g, 0, 0, 0)),
        ],
        out_specs=pl.BlockSpec((1, tile, SLAB), lambda bi, g, j: (g, bi * n_blocks + j, 0)),
        out_shape=jax.ShapeDtypeStruct((N_SLABS, b * s, SLAB), jnp.bfloat16),
        compiler_params=pltpu.CompilerParams(
            dimension_semantics=("parallel", "parallel", "arbitrary"),
            vmem_limit_bytes=VMEM_LIMIT),
        name="na_attention",
    )(proj, proj, proj, bias_table)


def _alibi_slopes():
    return np.asarray(2.0 ** (-8.0 * (np.arange(N_HEADS) + 1) / N_HEADS), dtype=np.float32)


def _slab_attention(q4, k4, v4, bias):
    lane = lax.broadcasted_iota(jnp.int32, q4.shape, 1) // HEAD_DIM
    zero = jnp.zeros_like(q4)
    q_stack = jnp.concatenate(
        [jnp.where(lane == h, q4, zero) for h in range(HEADS_PER_STEP)], axis=0)
    s = lax.dot_general(q_stack, k4, _NT_DIMS, preferred_element_type=jnp.float32)
    s = s + bias
    m = jnp.max(s, axis=-1, keepdims=True)
    p = jnp.exp2(s - m)
    l = jnp.sum(p, axis=-1, keepdims=True)
    p = p.astype(jnp.bfloat16)
    o_all = jnp.dot(p, v4, preferred_element_type=jnp.float32)
    out = o_all[:Q_TILE]
    for h in range(1, HEADS_PER_STEP):
        out = jnp.where(lane == h, o_all[h * Q_TILE:(h + 1) * Q_TILE], out)
    return out, m, l


def _dil_kernel(q_ref, ka_ref, kb_ref, kc_ref, va_ref, vb_ref, vc_ref, b_ref,
                o_ref, lse_ref):
    step = pl.program_id(2)
    n_steps = pl.num_programs(2)
    k_win = jnp.concatenate([ka_ref[0, 0], kb_ref[0, 0], kc_ref[0, 0]], axis=0)
    v_win = jnp.concatenate([va_ref[0, 0], vb_ref[0, 0], vc_ref[0, 0]], axis=0)
    lane = lax.broadcasted_iota(jnp.int32, (Q_TILE, LSE_LANES), 1)
    tiles = q_ref.shape[2] // Q_TILE
    for u in range(tiles):
        pattern = 0
        if u == 0:
            pattern = jnp.where(step == 0, 1, pattern)
        if u == tiles - 1:
            pattern = jnp.where(step == n_steps - 1, 2, pattern)
        tile = slice(u * Q_TILE, (u + 1) * Q_TILE)
        win = slice(u * Q_TILE, u * Q_TILE + DIL_WIN)
        lse_tile = jnp.zeros((Q_TILE, LSE_LANES), jnp.float32)
        for hg in range(N_SLABS):
            cols = slice(hg * SLAB, (hg + 1) * SLAB)
            bias = b_ref[pattern, hg * HEADS_PER_STEP:(hg + 1) * HEADS_PER_STEP]
            bias = bias.reshape(HEADS_PER_STEP * Q_TILE, DIL_WIN)
            out, m, l = _slab_attention(q_ref[0, 0, tile, cols], k_win[win, cols],
                                        v_win[win, cols], bias)
            o_ref[0, 0, tile, cols] = out.astype(o_ref.dtype)
            for h in range(HEADS_PER_STEP):
                rows = slice(h * Q_TILE, (h + 1) * Q_TILE)
                head = hg * HEADS_PER_STEP + h
                lse_tile = jnp.where(lane == head, m[rows], lse_tile)
                lse_tile = jnp.where(lane == N_HEADS + head, l[rows], lse_tile)
        lse_ref[0, 0, tile, :] = lse_tile


def _dil_bias_tables():
    qi = np.arange(Q_TILE)[:, None]
    kj = np.arange(DIL_WIN)[None, :]
    delta = kj - DIL_RADIUS - qi
    in_band = np.abs(delta) <= DIL_RADIUS
    valid = np.stack([in_band,
                      in_band & (kj >= DIL_RADIUS),
                      in_band & (kj < DIL_WIN - DIL_RADIUS)])
    dils = np.asarray([dil for _, dil in DIL_PAIRS])
    dist = jnp.asarray(np.abs(delta)[None] * dils[:, None, None], dtype=jnp.float32)
    bias = -jnp.asarray(_alibi_slopes())[None, :, None, None] * dist[:, None]
    return jnp.where(jnp.asarray(valid)[None, :, None], bias[:, None] * LOG2E, NEG_INF)


def _dil_attention(qkv, bias_tables, group):
    b, dil, l, _ = qkv.shape
    tiles = min(DIL_TILES_PER_STEP, l // Q_TILE)
    tile = Q_TILE * tiles
    n_steps = l // tile
    assert l % tile == 0 and tiles >= 2
    half = DIL_RADIUS
    per_tile = tile // half
    n_half = l // half

    def spec_mid(which):
        return pl.BlockSpec((1, 1, tile, D_MODEL), lambda bi, r, i: (bi, r, i, which))

    def spec_lo(which):
        return pl.BlockSpec((1, 1, half, D_MODEL),
                            lambda bi, r, i: (bi, r, jnp.maximum(per_tile * i - 1, 0), which))

    def spec_hi(which):
        return pl.BlockSpec((1, 1, half, D_MODEL),
                            lambda bi, r, i: (bi, r, jnp.minimum(per_tile * (i + 1), n_half - 1), which))

    return pl.pallas_call(
        _dil_kernel,
        grid=(b, dil, n_steps),
        in_specs=[
            spec_mid(0),
            spec_lo(1), spec_mid(1), spec_hi(1),
            spec_lo(2), spec_mid(2), spec_hi(2),
            pl.BlockSpec((None,) + bias_tables.shape[1:], lambda bi, r, i: (group, 0, 0, 0, 0)),
        ],
        out_specs=[
            pl.BlockSpec((1, 1, tile, D_MODEL), lambda bi, r, i: (bi, r, i, 0)),
            pl.BlockSpec((1, 1, tile, LSE_LANES), lambda bi, r, i: (bi, r, i, 0)),
        ],
        out_shape=[
            jax.ShapeDtypeStruct((b, dil, l, D_MODEL), jnp.bfloat16),
            jax.ShapeDtypeStruct((b, dil, l, LSE_LANES), jnp.float32),
        ],
        compiler_params=pltpu.CompilerParams(
            dimension_semantics=("parallel", "parallel", "arbitrary"),
            vmem_limit_bytes=VMEM_LIMIT),
        name=f"dilated_attention_{dil}",
    )(qkv, qkv, qkv, qkv, qkv, qkv, qkv, bias_tables)


def _silu(x):
    half = 0.5 * x
    return half + half * jnp.tanh(half)


def _out0_kernel(o_ref, gate_ref, x_ref, w_ref, g_ref, y_ref, *rest):
    h_refs, slab_refs = rest[:-2], rest[-2:]
    o = jnp.concatenate([o_ref[k] for k in range(N_SLABS)], axis=1).astype(jnp.float32)
    gate = jnp.concatenate([gate_ref[k] for k in range(N_SLABS)], axis=1).astype(jnp.float32)
    z = (o * _silu(gate)).astype(jnp.bfloat16)
    y = x_ref[0] + jnp.dot(z, w_ref[...], preferred_element_type=jnp.float32)
    y_ref[0] = y
    hn = _rmsnorm(y, g_ref[...])
    tm = hn.shape[0]
    dils = [dil for _, dil in DIL_PAIRS]
    for s in range(N_LANE_SLABS):
        slab_refs[0][s] = hn[:, s * LANES:(s + 1) * LANES]
    staged = {1: slab_refs[0]}
    for k, (h_ref, dil) in enumerate(zip(h_refs, dils)):
        if dil == 1:
            h_ref[0, 0] = hn.astype(h_ref.dtype)
            continue
        p = max(q for q in staged if dil % q == 0)
        f, n = dil // p, tm // dil
        keep = any(later % dil == 0 for later in dils[k + 1:])
        for r in range(dil):
            start = (r % p) * (tm // p) + r // p
            for s in range(N_LANE_SLABS):
                rows = staged[p][s, pl.ds(start, n, stride=f), :]
                h_ref[0, r, :, s * LANES:(s + 1) * LANES] = rows.astype(h_ref.dtype)
                if keep:
                    slab_refs[1][s, r * n:(r + 1) * n, :] = rows
        if keep:
            staged[dil] = slab_refs[1]


def _out_proj0(o, proj, x, w_out, g_next, *, tm=512):
    b, s, d = x.shape
    gate_block = proj.shape[0] // N_SLABS - 1
    tiles = s // tm
    tok = pl.BlockSpec((1, tm, d), lambda bi, i: (bi, i, 0))
    dils = [dil for _, dil in DIL_PAIRS]
    return pl.pallas_call(
        _out0_kernel,
        grid=(b, s // tm),
        in_specs=[
            pl.BlockSpec((N_SLABS, tm, SLAB), lambda bi, i: (0, bi * tiles + i, 0)),
            pl.BlockSpec((N_SLABS, tm, SLAB), lambda bi, i: (gate_block, bi * tiles + i, 0)),
            tok,
            pl.BlockSpec((d, d), lambda bi, i: (0, 0)),
            pl.BlockSpec((1, d), lambda bi, i: (0, 0)),
        ],
        out_specs=[tok] + [pl.BlockSpec((1, dil, tm // dil, d), lambda bi, i: (bi, 0, i, 0))
                           for dil in dils],
        out_shape=([jax.ShapeDtypeStruct((b, s, d), jnp.float32)]
                   + [jax.ShapeDtypeStruct((b, dil, s // dil, d), jnp.bfloat16) for dil in dils]),
        scratch_shapes=[pltpu.VMEM((N_LANE_SLABS, tm, LANES), jnp.float32)] * 2,
        compiler_params=pltpu.CompilerParams(
            dimension_semantics=("parallel", "parallel"), vmem_limit_bytes=VMEM_LIMIT),
        name="out_proj0",
    )(o, proj, x, w_out, g_next.reshape(1, d))


_MAX_ROW_STRIDE = 4


def _natural_order(ref, slab_refs, dil):
    _, n, c = ref.shape
    if dil == 1:
        return ref[0].astype(jnp.float32)
    out_ref, mid_ref = slab_refs
    tm = dil * n
    two_level = dil > _MAX_ROW_STRIDE
    if two_level:
        f = _MAX_ROW_STRIDE
        p = dil // f
        assert p <= _MAX_ROW_STRIDE
    pieces = []
    for s in range(c // LANES):
        lanes = slice(s * LANES, (s + 1) * LANES)
        if not two_level:
            for r in range(dil):
                out_ref[s, pl.ds(r, n, stride=dil), :] = ref[r, :, lanes].astype(jnp.float32)
        else:
            for r in range(dil):
                start = (r % p) * (tm // p) + r // p
                mid_ref[s, pl.ds(start, n, stride=f), :] = ref[r, :, lanes].astype(jnp.float32)
            for b_ in range(p):
                out_ref[s, pl.ds(b_, tm // p, stride=p), :] = (
                    mid_ref[s, b_ * (tm // p):(b_ + 1) * (tm // p), :])
        pieces.append(out_ref[s])
    return pieces[0] if len(pieces) == 1 else jnp.concatenate(pieces, axis=1)


def _out1_kernel(o0_ref, o1_ref, o2_ref, l0_ref, l1_ref, l2_ref, gate_ref, x_ref,
                 w_ref, e_ref, g_ref, y_ref, *slab_refs):
    dils = [dil for _, dil in DIL_PAIRS]
    stats = [_natural_order(ref.at[0], slab_refs, dil)
             for ref, dil in zip((l0_ref, l1_ref, l2_ref), dils)]
    sums = [pltpu.roll(st, LSE_LANES - N_HEADS, axis=1) for st in stats]
    m = jnp.maximum(jnp.maximum(stats[0], stats[1]), stats[2])
    es = [jnp.exp2(st - m) for st in stats]
    denom = es[0] * sums[0] + es[1] * sums[1] + es[2] * sums[2]
    head_lane = lax.broadcasted_iota(jnp.int32, denom.shape, 1) < N_HEADS
    o = None
    for e, o_ref, dil in zip(es, (o0_ref, o1_ref, o2_ref), dils):
        w = jnp.where(head_lane, e / denom, 0.0)
        hi = w.astype(jnp.bfloat16)
        lo = (w - hi.astype(jnp.float32)).astype(jnp.bfloat16)
        w_full = jnp.dot(jnp.concatenate([hi, lo], axis=1), e_ref[...],
                         preferred_element_type=jnp.float32)
        term = w_full * _natural_order(o_ref.at[0], slab_refs, dil)
        o = term if o is None else o + term
    gate = gate_ref[0].astype(jnp.float32)
    z = (o * _silu(gate)).astype(jnp.bfloat16)
    x = x_ref[0] + jnp.dot(z, w_ref[...], preferred_element_type=jnp.float32)
    y_ref[0] = _rmsnorm(x, g_ref[...])


def _head_expansion():
    e = np.zeros((LSE_LANES, D_MODEL), np.float32)
    for h in range(N_HEADS):
        e[h, h * HEAD_DIM:(h + 1) * HEAD_DIM] = 1.0
    return jnp.asarray(np.concatenate([e, e], axis=0), dtype=jnp.bfloat16)


def _out_proj1(os_, lses, proj, x, w_out, norm_f, *, tm=512):
    b, s, d = x.shape
    gate_block = proj.shape[2] // d - 1
    dils = [dil for _, dil in DIL_PAIRS]

    def grouped(width, dil):
        return pl.BlockSpec((1, dil, tm // dil, width), lambda bi, i: (bi, 0, i, 0))

    tok = pl.BlockSpec((1, tm, d), lambda bi, i: (bi, i, 0))
    return pl.pallas_call(
        _out1_kernel,
        grid=(b, s // tm),
        in_specs=(
            [grouped(d, dil) for dil in dils]
            + [grouped(LSE_LANES, dil) for dil in dils]
            + [pl.BlockSpec((1, tm, d), lambda bi, i: (bi, i, gate_block)),
               tok,
               pl.BlockSpec((d, d), lambda bi, i: (0, 0)),
               pl.BlockSpec((2 * LSE_LANES, d), lambda bi, i: (0, 0)),
               pl.BlockSpec((1, d), lambda bi, i: (0, 0))]),
        out_specs=tok,
        out_shape=jax.ShapeDtypeStruct((b, s, d), jnp.float32),
        scratch_shapes=[pltpu.VMEM((N_LANE_SLABS, tm, LANES), jnp.float32)] * 2,
        compiler_params=pltpu.CompilerParams(
            dimension_semantics=("parallel", "parallel"), vmem_limit_bytes=VMEM_LIMIT),
        name="out_proj1",
    )(*os_, *lses, proj, x, w_out, _head_expansion(), norm_f.reshape(1, d))


def kernel(x, norm_0, w_in_0, rpb_0, w_out_0, norm_1, w_in_1, w_out_1, norm_f):
    b, s, d = x.shape
    t = b * s

    proj0 = _norm_proj(x.reshape(t, d), norm_0, w_in_0)
    o0 = _na_attention(proj0, _na_bias_table(rpb_0), b)
    x1, *h1 = _out_proj0(o0, proj0, x, w_out_0.astype(jnp.bfloat16), norm_1)

    gate_block = 3 * N_DIL_GROUPS
    alibi = _dil_bias_tables()
    outs, lses, proj_gate = [], [], None
    for g, (window, dil) in enumerate(DIL_PAIRS):
        assert window // (2 * dil) == DIL_RADIUS
        blocks = (3 * g, 3 * g + 1, 3 * g + 2) + ((gate_block,) if dil == 1 else ())
        qkv = _proj(h1[g].reshape(t, d), w_in_1, blocks).reshape(b, dil, s // dil, -1)
        if dil == 1:
            proj_gate = qkv.reshape(b, s, -1)
        o_g, lse_g = _dil_attention(qkv, alibi, g)
        outs.append(o_g)
        lses.append(lse_g)
    return _out_proj1(outs, lses, proj_gate, x1, w_out_1.astype(jnp.bfloat16), norm_f)
```

```python
import functools
import math

import numpy as np
import jax
import jax.numpy as jnp
from jax import lax
from jax.experimental import pallas as pl
from jax.experimental.pallas import tpu as pltpu

D_MODEL = 1024
HEAD_DIM = 64
N_HEADS = 16
GRID_W = 64
NA_ROWS = 8
NA_COLS = 16
DIL_PAIRS = ((128, 1), (512, 4), (2048, 16))
N_DIL_GROUPS = len(DIL_PAIRS)
RMS_EPS = 1e-6
NEG_INF = -1e30
LOG2E = math.log2(math.e)

LANES = 128
N_LANE_SLABS = D_MODEL // LANES
HEADS_PER_STEP = 4
SLAB = HEADS_PER_STEP * HEAD_DIM
N_SLABS = D_MODEL // SLAB
Q_TILE = 128
NA_Q_TILE = GRID_W
NA_WIN = NA_ROWS * GRID_W
NA_TILES_PER_STEP = 32
DIL_TILES_PER_STEP = 8
DIL_RADIUS = 64
DIL_WIN = Q_TILE + 2 * DIL_RADIUS
LSE_LANES = LANES
VMEM_LIMIT = 56 * 1024 * 1024

_NT_DIMS = (((1,), (1,)), ((), ()))


def _rmsnorm(x, g):
    ms = jnp.mean(x * x, axis=-1, keepdims=True)
    return x * lax.rsqrt(ms + RMS_EPS) * g


def _q_scaled_bf16(w_ref):
    q_steps = D_MODEL // w_ref.shape[1]
    scale = jnp.where(pl.program_id(1) < q_steps, LOG2E / math.sqrt(HEAD_DIM), 1.0)
    return (w_ref[...] * scale).astype(jnp.bfloat16)


def _norm_proj_kernel(x_ref, g_ref, w_ref, o_ref, h_ref):
    i, j = pl.program_id(0), pl.program_id(1)
    rows = x_ref.shape[0]

    def normalise_chunk():
        start = pl.multiple_of(j * rows, rows)
        h_ref[i % 2, pl.ds(start, rows), :] = _rmsnorm(x_ref[...], g_ref[...]).astype(h_ref.dtype)

    @pl.when(i == 0)
    def _():
        normalise_chunk()

    @pl.when(i > 0)
    def _():
        res = jnp.dot(h_ref[(i - 1) % 2], _q_scaled_bf16(w_ref),
                      preferred_element_type=jnp.float32).astype(o_ref.dtype)
        for k in range(o_ref.shape[0]):
            o_ref[k] = res[:, k * SLAB:(k + 1) * SLAB]
        normalise_chunk()


def _norm_proj(x, g, w, *, tm=2048, tn=1024):
    t, d = x.shape
    n = w.shape[1]
    n_i, n_j = t // tm, n // tn
    chunk = tm // n_j
    n_chunks = t // chunk
    return pl.pallas_call(
        _norm_proj_kernel,
        grid=(n_i + 1, n_j),
        in_specs=[
            pl.BlockSpec((chunk, d), lambda i, j: (jnp.minimum(i * n_j + j, n_chunks - 1), 0)),
            pl.BlockSpec((1, d), lambda i, j: (0, 0)),
            pl.BlockSpec((d, tn), lambda i, j: (0, j)),
        ],
        out_specs=pl.BlockSpec((tn // SLAB, tm, SLAB),
                               lambda i, j: (jnp.where(i == 0, 0, j), jnp.maximum(i - 1, 0), 0)),
        out_shape=jax.ShapeDtypeStruct((n // SLAB, t, SLAB), jnp.bfloat16),
        scratch_shapes=[pltpu.VMEM((2, tm, d), jnp.bfloat16)],
        compiler_params=pltpu.CompilerParams(
            dimension_semantics=("arbitrary", "arbitrary"),
            vmem_limit_bytes=VMEM_LIMIT),
        name="norm_proj",
    )(x, g.reshape(1, d), w)


def _proj_kernel(h_ref, w_ref, o_ref):
    o_ref[...] = jnp.dot(h_ref[...], _q_scaled_bf16(w_ref),
                         preferred_element_type=jnp.float32).astype(o_ref.dtype)


def _proj(h, w, col_blocks, *, tm=4096, tn=512):
    t, d = h.shape
    per_block = d // tn
    n = len(col_blocks) * d

    def w_block(j):
        k_of_j = j // per_block
        blk = col_blocks[-1]
        for k in range(len(col_blocks) - 2, -1, -1):
            blk = jnp.where(k_of_j == k, col_blocks[k], blk)
        return blk * per_block + j % per_block

    return pl.pallas_call(
        _proj_kernel,
        grid=(t // tm, n // tn),
        in_specs=[
            pl.BlockSpec((tm, d), lambda i, j: (i, 0)),
            pl.BlockSpec((d, tn), lambda i, j: (0, w_block(j))),
        ],
        out_specs=pl.BlockSpec((tm, tn), lambda i, j: (i, j)),
        out_shape=jax.ShapeDtypeStruct((t, n), jnp.bfloat16),
        compiler_params=pltpu.CompilerParams(
            dimension_semantics=("parallel", "parallel"),
            vmem_limit_bytes=VMEM_LIMIT),
        name="proj",
    )(h, w)


def _na_kernel(q_ref, k_ref, v_ref, b_ref, o_ref):
    step = pl.program_id(2)
    n_tiles = k_ref.shape[1] // GRID_W
    lane = lax.broadcasted_iota(jnp.int32, (NA_Q_TILE, SLAB), 1) // HEAD_DIM

    for u in range(NA_TILES_PER_STEP):
        r = step * NA_TILES_PER_STEP + u
        first_row = jnp.clip(r - NA_ROWS // 2, 0, n_tiles - NA_ROWS)
        start = pl.multiple_of(first_row * GRID_W, GRID_W)
        tile_rows = slice(u * NA_Q_TILE, (u + 1) * NA_Q_TILE)
        q4 = q_ref[0, tile_rows, :]
        zero = jnp.zeros_like(q4)
        q_stack = jnp.concatenate(
            [jnp.where(lane == h, q4, zero) for h in range(HEADS_PER_STEP)], axis=0)
        s = lax.dot_general(q_stack, k_ref[0, pl.ds(start, NA_WIN), :], _NT_DIMS,
                            preferred_element_type=jnp.float32)
        first_off = first_row - r + NA_ROWS - 1
        s = s + jnp.concatenate(
            [jnp.concatenate([b_ref[h, first_off + 2 * j] for j in range(NA_ROWS // 2)], axis=1)
             for h in range(HEADS_PER_STEP)], axis=0)
        m = jnp.max(s, axis=-1, keepdims=True)
        e = jnp.exp2(s - m)
        inv_l = 1.0 / jnp.sum(e, axis=-1, keepdims=True)
        o_all = jnp.dot(e.astype(jnp.bfloat16), v_ref[0, pl.ds(start, NA_WIN), :],
                        preferred_element_type=jnp.float32) * inv_l
        out = o_all[:NA_Q_TILE]
        for h in range(1, HEADS_PER_STEP):
            out = jnp.where(lane == h, o_all[h * NA_Q_TILE:(h + 1) * NA_Q_TILE], out)
        o_ref[0, tile_rows, :] = out.astype(o_ref.dtype)


def _na_bias_table(rpb):
    n_col_off = 2 * NA_COLS - 1
    c = np.arange(GRID_W)[:, None]
    kc = np.arange(GRID_W)[None, :]
    cs = np.clip(c - NA_COLS // 2, 0, GRID_W - NA_COLS)
    col_ok = (kc >= cs) & (kc < cs + NA_COLS)
    col_off = kc - c + NA_COLS - 1
    assert np.all(((col_off >= 0) & (col_off < n_col_off))[col_ok])
    pick = (np.arange(n_col_off)[:, None, None] == col_off[None]) & col_ok[None]
    pick2 = np.zeros((2, n_col_off, GRID_W, 2, GRID_W), np.float32)
    for half in range(2):
        pick2[half, :, :, half, :] = pick
    pick2 = pick2.reshape(2 * n_col_off, GRID_W, 2 * GRID_W)
    rpb = rpb.astype(jnp.float32)
    row_pairs = jnp.concatenate([rpb[:, :-1], rpb[:, 1:]], axis=-1)
    table = jnp.einsum('hrm,mck->hrck', row_pairs, jnp.asarray(pick2),
                       precision=lax.Precision.HIGHEST)
    ok2 = np.concatenate([col_ok, col_ok], axis=1)
    return jnp.where(jnp.asarray(ok2), table * LOG2E, NEG_INF)


def _na_attention(proj, bias_table, b):
    s = proj.shape[1] // b
    tile = NA_Q_TILE * NA_TILES_PER_STEP
    assert s % tile == 0 and NA_Q_TILE == GRID_W
    n_blocks = s // tile
    return pl.pallas_call(
        _na_kernel,
        grid=(b, N_SLABS, n_blocks),
        in_specs=[
            pl.BlockSpec((1, tile, SLAB), lambda bi, g, j: (g, bi * n_blocks + j, 0)),
            pl.BlockSpec((1, s, SLAB), lambda bi, g, j: (N_SLABS + g, bi, 0)),
            pl.BlockSpec((1, s, SLAB), lambda bi, g, j: (2 * N_SLABS + g, bi, 0)),
            pl.BlockSpec((HEADS_PER_STEP,) + bias_table.shape[1:], lambda bi, g, j: (g, 0, 0, 0)),
        ],
        out_specs=pl.BlockSpec((1, tile, SLAB), lambda bi, g, j: (g, bi * n_blocks + j, 0)),
        out_shape=jax.ShapeDtypeStruct((N_SLABS, b * s, SLAB), jnp.bfloat16),
        compiler_params=pltpu.CompilerParams(
            dimension_semantics=("parallel", "parallel", "arbitrary"),
            vmem_limit_bytes=VMEM_LIMIT),
        name="na_attention",
    )(proj, proj, proj, bias_table)


def _alibi_slopes():
    return np.asarray(2.0 ** (-8.0 * (np.arange(N_HEADS) + 1) / N_HEADS), dtype=np.float32)


def _slab_attention(q4, k4, v4, bias):
    lane = lax.broadcasted_iota(jnp.int32, q4.shape, 1) // HEAD_DIM
    zero = jnp.zeros_like(q4)
    q_stack = jnp.concatenate(
        [jnp.where(lane == h, q4, zero) for h in range(HEADS_PER_STEP)], axis=0)
    s = lax.dot_general(q_stack, k4, _NT_DIMS, preferred_element_type=jnp.float32)
    s = s + bias
    m = jnp.max(s, axis=-1, keepdims=True)
    p = jnp.exp2(s - m)
    l = jnp.sum(p, axis=-1, keepdims=True)
    p = p.astype(jnp.bfloat16)
    o_all = jnp.dot(p, v4, preferred_element_type=jnp.float32)
    out = o_all[:Q_TILE]
    for h in range(1, HEADS_PER_STEP):
        out = jnp.where(lane == h, o_all[h * Q_TILE:(h + 1) * Q_TILE], out)
    return out, m, l


def _dil_kernel(q_ref, ka_ref, kb_ref, kc_ref, va_ref, vb_ref, vc_ref, b_ref,
                o_ref, lse_ref):
    step = pl.program_id(2)
    n_steps = pl.num_programs(2)
    k_win = jnp.concatenate([ka_ref[0, 0], kb_ref[0, 0], kc_ref[0, 0]], axis=0)
    v_win = jnp.concatenate([va_ref[0, 0], vb_ref[0, 0], vc_ref[0, 0]], axis=0)
    lane = lax.broadcasted_iota(jnp.int32, (Q_TILE, LSE_LANES), 1)
    tiles = q_ref.shape[2] // Q_TILE
    for u in range(tiles):
        pattern = 0
        if u == 0:
            pattern = jnp.where(step == 0, 1, pattern)
        if u == tiles - 1:
            pattern = jnp.where(step == n_steps - 1, 2, pattern)
        tile = slice(u * Q_TILE, (u + 1) * Q_TILE)
        win = slice(u * Q_TILE, u * Q_TILE + DIL_WIN)
        lse_tile = jnp.zeros((Q_TILE, LSE_LANES), jnp.float32)
        for hg in range(N_SLABS):
            cols = slice(hg * SLAB, (hg + 1) * SLAB)
            bias = b_ref[pattern, hg * HEADS_PER_STEP:(hg + 1) * HEADS_PER_STEP]
            bias = bias.reshape(HEADS_PER_STEP * Q_TILE, DIL_WIN)
            out, m, l = _slab_attention(q_ref[0, 0, tile, cols], k_win[win, cols],
                                        v_win[win, cols], bias)
            o_ref[0, 0, tile, cols] = out.astype(o_ref.dtype)
            for h in range(HEADS_PER_STEP):
                rows = slice(h * Q_TILE, (h + 1) * Q_TILE)
                head = hg * HEADS_PER_STEP + h
                lse_tile = jnp.where(lane == head, m[rows], lse_tile)
                lse_tile = jnp.where(lane == N_HEADS + head, l[rows], lse_tile)
        lse_ref[0, 0, tile, :] = lse_tile


def _dil_bias_tables():
    qi = np.arange(Q_TILE)[:, None]
    kj = np.arange(DIL_WIN)[None, :]
    delta = kj - DIL_RADIUS - qi
    in_band = np.abs(delta) <= DIL_RADIUS
    valid = np.stack([in_band,
                      in_band & (kj >= DIL_RADIUS),
                      in_band & (kj < DIL_WIN - DIL_RADIUS)])
    dils = np.asarray([dil for _, dil in DIL_PAIRS])
    dist = jnp.asarray(np.abs(delta)[None] * dils[:, None, None], dtype=jnp.float32)
    bias = -jnp.asarray(_alibi_slopes())[None, :, None, None] * dist[:, None]
    return jnp.where(jnp.asarray(valid)[None, :, None], bias[:, None] * LOG2E, NEG_INF)


def _dil_attention(qkv, bias_tables, group):
    b, dil, l, _ = qkv.shape
    tiles = min(DIL_TILES_PER_STEP, l // Q_TILE)
    tile = Q_TILE * tiles
    n_steps = l // tile
    assert l % tile == 0 and tiles >= 2
    half = DIL_RADIUS
    per_tile = tile // half
    n_half = l // half

    def spec_mid(which):
        return pl.BlockSpec((1, 1, tile, D_MODEL), lambda bi, r, i: (bi, r, i, which))

    def spec_lo(which):
        return pl.BlockSpec((1, 1, half, D_MODEL),
                            lambda bi, r, i: (bi, r, jnp.maximum(per_tile * i - 1, 0), which))

    def spec_hi(which):
        return pl.BlockSpec((1, 1, half, D_MODEL),
                            lambda bi, r, i: (bi, r, jnp.minimum(per_tile * (i + 1), n_half - 1), which))

    return pl.pallas_call(
        _dil_kernel,
        grid=(b, dil, n_steps),
        in_specs=[
            spec_mid(0),
            spec_lo(1), spec_mid(1), spec_hi(1),
            spec_lo(2), spec_mid(2), spec_hi(2),
            pl.BlockSpec((None,) + bias_tables.shape[1:], lambda bi, r, i: (group, 0, 0, 0, 0)),
        ],
        out_specs=[
            pl.BlockSpec((1, 1, tile, D_MODEL), lambda bi, r, i: (bi, r, i, 0)),
            pl.BlockSpec((1, 1, tile, LSE_LANES), lambda bi, r, i: (bi, r, i, 0)),
        ],
        out_shape=[
            jax.ShapeDtypeStruct((b, dil, l, D_MODEL), jnp.bfloat16),
            jax.ShapeDtypeStruct((b, dil, l, LSE_LANES), jnp.float32),
        ],
        compiler_params=pltpu.CompilerParams(
            dimension_semantics=("parallel", "parallel", "arbitrary"),
            vmem_limit_bytes=VMEM_LIMIT),
        name=f"dilated_attention_{dil}",
    )(qkv, qkv, qkv, qkv, qkv, qkv, qkv, bias_tables)


def _silu(x):
    half = 0.5 * x
    return half + half * jnp.tanh(half)


def _out0_kernel(o_ref, gate_ref, x_ref, w_ref, g_ref, y_ref, *rest):
    h_refs, slab_refs = rest[:-2], rest[-2:]
    o = jnp.concatenate([o_ref[k] for k in range(N_SLABS)], axis=1).astype(jnp.float32)
    gate = jnp.concatenate([gate_ref[k] for k in range(N_SLABS)], axis=1).astype(jnp.float32)
    z = (o * _silu(gate)).astype(jnp.bfloat16)
    y = x_ref[0] + jnp.dot(z, w_ref[...], preferred_element_type=jnp.float32)
    y_ref[0] = y
    hn = _rmsnorm(y, g_ref[...])
    tm = hn.shape[0]
    dils = [dil for _, dil in DIL_PAIRS]
    for s in range(N_LANE_SLABS):
        slab_refs[0][s] = hn[:, s * LANES:(s + 1) * LANES]
    staged = {1: slab_refs[0]}
    for k, (h_ref, dil) in enumerate(zip(h_refs, dils)):
        if dil == 1:
            h_ref[0, 0] = hn.astype(h_ref.dtype)
            continue
        p = max(q for q in staged if dil % q == 0)
        f, n = dil // p, tm // dil
        keep = any(later % dil == 0 for later in dils[k + 1:])
        for r in range(dil):
            start = (r % p) * (tm // p) + r // p
            for s in range(N_LANE_SLABS):
                rows = staged[p][s, pl.ds(start, n, stride=f), :]
                h_ref[0, r, :, s * LANES:(s + 1) * LANES] = rows.astype(h_ref.dtype)
                if keep:
                    slab_refs[1][s, r * n:(r + 1) * n, :] = rows
        if keep:
            staged[dil] = slab_refs[1]


def _out_proj0(o, proj, x, w_out, g_next, *, tm=512):
    b, s, d = x.shape
    gate_block = proj.shape[0] // N_SLABS - 1
    tiles = s // tm
    tok = pl.BlockSpec((1, tm, d), lambda bi, i: (bi, i, 0))
    dils = [dil for _, dil in DIL_PAIRS]
    return pl.pallas_call(
        _out0_kernel,
        grid=(b, s // tm),
        in_specs=[
            pl.BlockSpec((N_SLABS, tm, SLAB), lambda bi, i: (0, bi * tiles + i, 0)),
            pl.BlockSpec((N_SLABS, tm, SLAB), lambda bi, i: (gate_block, bi * tiles + i, 0)),
            tok,
            pl.BlockSpec((d, d), lambda bi, i: (0, 0)),
            pl.BlockSpec((1, d), lambda bi, i: (0, 0)),
        ],
        out_specs=[tok] + [pl.BlockSpec((1, dil, tm // dil, d), lambda bi, i: (bi, 0, i, 0))
                           for dil in dils],
        out_shape=([jax.ShapeDtypeStruct((b, s, d), jnp.float32)]
                   + [jax.ShapeDtypeStruct((b, dil, s // dil, d), jnp.bfloat16) for dil in dils]),
        scratch_shapes=[pltpu.VMEM((N_LANE_SLABS, tm, LANES), jnp.float32)] * 2,
        compiler_params=pltpu.CompilerParams(
            dimension_semantics=("parallel", "parallel"), vmem_limit_bytes=VMEM_LIMIT),
        name="out_proj0",
    )(o, proj, x, w_out, g_next.reshape(1, d))


_MAX_ROW_STRIDE = 4


def _natural_order(ref, slab_refs, dil):
    _, n, c = ref.shape
    if dil == 1:
        return ref[0].astype(jnp.float32)
    out_ref, mid_ref = slab_refs
    tm = dil * n
    two_level = dil > _MAX_ROW_STRIDE
    if two_level:
        f = _MAX_ROW_STRIDE
        p = dil // f
        assert p <= _MAX_ROW_STRIDE
    pieces = []
    for s in range(c // LANES):
        lanes = slice(s * LANES, (s + 1) * LANES)
        if not two_level:
            for r in range(dil):
                out_ref[s, pl.ds(r, n, stride=dil), :] = ref[r, :, lanes].astype(jnp.float32)
        else:
            for r in range(dil):
                start = (r % p) * (tm // p) + r // p
                mid_ref[s, pl.ds(start, n, stride=f), :] = ref[r, :, lanes].astype(jnp.float32)
            for b_ in range(p):
                out_ref[s, pl.ds(b_, tm // p, stride=p), :] = (
                    mid_ref[s, b_ * (tm // p):(b_ + 1) * (tm // p), :])
        pieces.append(out_ref[s])
    return pieces[0] if len(pieces) == 1 else jnp.concatenate(pieces, axis=1)


def _out1_kernel(o0_ref, o1_ref, o2_ref, l0_ref, l1_ref, l2_ref, gate_ref, x_ref,
                 w_ref, e_ref, g_ref, y_ref, *slab_refs):
    dils = [dil for _, dil in DIL_PAIRS]
    stats = [_natural_order(ref.at[0], slab_refs, dil)
             for ref, dil in zip((l0_ref, l1_ref, l2_ref), dils)]
    sums = [pltpu.roll(st, LSE_LANES - N_HEADS, axis=1) for st in stats]
    m = jnp.maximum(jnp.maximum(stats[0], stats[1]), stats[2])
    es = [jnp.exp2(st - m) for st in stats]
    denom = es[0] * sums[0] + es[1] * sums[1] + es[2] * sums[2]
    head_lane = lax.broadcasted_iota(jnp.int32, denom.shape, 1) < N_HEADS
    o = None
    for e, o_ref, dil in zip(es, (o0_ref, o1_ref, o2_ref), dils):
        w = jnp.where(head_lane, e / denom, 0.0)
        hi = w.astype(jnp.bfloat16)
        lo = (w - hi.astype(jnp.float32)).astype(jnp.bfloat16)
        w_full = jnp.dot(jnp.concatenate([hi, lo], axis=1), e_ref[...],
                         preferred_element_type=jnp.float32)
        term = w_full * _natural_order(o_ref.at[0], slab_refs, dil)
        o = term if o is None else o + term
    gate = gate_ref[0].astype(jnp.float32)
    z = (o * _silu(gate)).astype(jnp.bfloat16)
    x = x_ref[0] + jnp.dot(z, w_ref[...], preferred_element_type=jnp.float32)
    y_ref[0] = _rmsnorm(x, g_ref[...])


def _head_expansion():
    e = np.zeros((LSE_LANES, D_MODEL), np.float32)
    for h in range(N_HEADS):
        e[h, h * HEAD_DIM:(h + 1) * HEAD_DIM] = 1.0
    return jnp.asarray(np.concatenate([e, e], axis=0), dtype=jnp.bfloat16)


def _out_proj1(os_, lses, proj, x, w_out, norm_f, *, tm=512):
    b, s, d = x.shape
    gate_block = proj.shape[2] // d - 1
    dils = [dil for _, dil in DIL_PAIRS]

    def grouped(width, dil):
        return pl.BlockSpec((1, dil, tm // dil, width), lambda bi, i: (bi, 0, i, 0))

    tok = pl.BlockSpec((1, tm, d), lambda bi, i: (bi, i, 0))
    return pl.pallas_call(
        _out1_kernel,
        grid=(b, s // tm),
        in_specs=(
            [grouped(d, dil) for dil in dils]
            + [grouped(LSE_LANES, dil) for dil in dils]
            + [pl.BlockSpec((1, tm, d), lambda bi, i: (bi, i, gate_block)),
               tok,
               pl.BlockSpec((d, d), lambda bi, i: (0, 0)),
               pl.BlockSpec((2 * LSE_LANES, d), lambda bi, i: (0, 0)),
               pl.BlockSpec((1, d), lambda bi, i: (0, 0))]),
        out_specs=tok,
        out_shape=jax.ShapeDtypeStruct((b, s, d), jnp.float32),
        scratch_shapes=[pltpu.VMEM((N_LANE_SLABS, tm, LANES), jnp.float32)] * 2,
        compiler_params=pltpu.CompilerParams(
            dimension_semantics=("parallel", "parallel"), vmem_limit_bytes=VMEM_LIMIT),
        name="out_proj1",
    )(*os_, *lses, proj, x, w_out, _head_expansion(), norm_f.reshape(1, d))


def kernel(x, norm_0, w_in_0, rpb_0, w_out_0, norm_1, w_in_1, w_out_1, norm_f):
    b, s, d = x.shape
    t = b * s

    proj0 = _norm_proj(x.reshape(t, d), norm_0, w_in_0)
    o0 = _na_attention(proj0, _na_bias_table(rpb_0), b)
    x1, *h1 = _out_proj0(o0, proj0, x, w_out_0.astype(jnp.bfloat16), norm_1)

    gate_block = 3 * N_DIL_GROUPS
    alibi = _dil_bias_tables()
    outs, lses, proj_gate = [], [], None
    for g, (window, dil) in enumerate(DIL_PAIRS):
        assert window // (2 * dil) == DIL_RADIUS
        blocks = (3 * g, 3 * g + 1, 3 * g + 2) + ((gate_block,) if dil == 1 else ())
        qkv = _proj(h1[g].reshape(t, d), w_in_1, blocks).reshape(b, dil, s // dil, -1)
        if dil == 1:
            proj_gate = qkv.reshape(b, s, -1)
        o_g, lse_g = _dil_attention(qkv, alibi, g)
        outs.append(o_g)
        lses.append(lse_g)
    return _out_proj1(outs, lses, proj_gate, x1, w_out_1.astype(jnp.bfloat16), norm_f)
```

```python
import functools
import math

import numpy as np
import jax
import jax.numpy as jnp
from jax import lax
from jax.experimental import pallas as pl
from jax.experimental.pallas import tpu as pltpu

D_MODEL = 1024
HEAD_DIM = 64
N_HEADS = 16
GRID_W = 64
NA_ROWS = 8
NA_COLS = 16
DIL_PAIRS = ((128, 1), (512, 4), (2048, 16))
N_DIL_GROUPS = len(DIL_PAIRS)
RMS_EPS = 1e-6
NEG_INF = -1e30
LOG2E = math.log2(math.e)

LANES = 128
N_LANE_SLABS = D_MODEL // LANES
HEADS_PER_STEP = 4
SLAB = HEADS_PER_STEP * HEAD_DIM
N_SLABS = D_MODEL // SLAB
Q_TILE = 128
NA_Q_TILE = GRID_W
NA_WIN = NA_ROWS * GRID_W
NA_TILES_PER_STEP = 32
DIL_TILES_PER_STEP = 8
DIL_RADIUS = 64
DIL_WIN = Q_TILE + 2 * DIL_RADIUS
LSE_LANES = LANES
VMEM_LIMIT = 56 * 1024 * 1024

_NT_DIMS = (((1,), (1,)), ((), ()))


def _rmsnorm(x, g):
    ms = jnp.mean(x * x, axis=-1, keepdims=True)
    return x * lax.rsqrt(ms + RMS_EPS) * g


def _q_scaled_bf16(w_ref):
    q_steps = D_MODEL // w_ref.shape[1]
    scale = jnp.where(pl.program_id(1) < q_steps, LOG2E / math.sqrt(HEAD_DIM), 1.0)
    return (w_ref[...] * scale).astype(jnp.bfloat16)


def _norm_proj_kernel(x_ref, g_ref, w_ref, o_ref, h_ref):
    i, j = pl.program_id(0), pl.program_id(1)
    rows = x_ref.shape[0]

    def normalise_chunk():
        start = pl.multiple_of(j * rows, rows)
        h_ref[i % 2, pl.ds(start, rows), :] = _rmsnorm(x_ref[...], g_ref[...]).astype(h_ref.dtype)

    @pl.when(i == 0)
    def _():
        normalise_chunk()

    @pl.when(i > 0)
    def _():
        res = jnp.dot(h_ref[(i - 1) % 2], _q_scaled_bf16(w_ref),
                      preferred_element_type=jnp.float32).astype(o_ref.dtype)
        for k in range(o_ref.shape[0]):
            o_ref[k] = res[:, k * SLAB:(k + 1) * SLAB]
        normalise_chunk()


def _norm_proj(x, g, w, *, tm=2048, tn=1024):
    t, d = x.shape
    n = w.shape[1]
    n_i, n_j = t // tm, n // tn
    chunk = tm // n_j
    n_chunks = t // chunk
    return pl.pallas_call(
        _norm_proj_kernel,
        grid=(n_i + 1, n_j),
        in_specs=[
            pl.BlockSpec((chunk, d), lambda i, j: (jnp.minimum(i * n_j + j, n_chunks - 1), 0)),
            pl.BlockSpec((1, d), lambda i, j: (0, 0)),
            pl.BlockSpec((d, tn), lambda i, j: (0, j)),
        ],
        out_specs=pl.BlockSpec((tn // SLAB, tm, SLAB),
                               lambda i, j: (jnp.where(i == 0, 0, j), jnp.maximum(i - 1, 0), 0)),
        out_shape=jax.ShapeDtypeStruct((n // SLAB, t, SLAB), jnp.bfloat16),
        scratch_shapes=[pltpu.VMEM((2, tm, d), jnp.bfloat16)],
        compiler_params=pltpu.CompilerParams(
            dimension_semantics=("arbitrary", "arbitrary"),
            vmem_limit_bytes=VMEM_LIMIT),
        name="norm_proj",
    )(x, g.reshape(1, d), w)


def _proj_kernel(h_ref, w_ref, o_ref):
    o_ref[...] = jnp.dot(h_ref[...], _q_scaled_bf16(w_ref),
                         preferred_element_type=jnp.float32).astype(o_ref.dtype)


def _proj(h, w, col_blocks, *, tm=2048, tn=1024):
    t, d = h.shape
    per_block = d // tn
    n = len(col_blocks) * d

    def w_block(j):
        k_of_j = j // per_block
        blk = col_blocks[-1]
        for k in range(len(col_blocks) - 2, -1, -1):
            blk = jnp.where(k_of_j == k, col_blocks[k], blk)
        return blk * per_block + j % per_block

    return pl.pallas_call(
        _proj_kernel,
        grid=(t // tm, n // tn),
        in_specs=[
            pl.BlockSpec((tm, d), lambda i, j: (i, 0)),
            pl.BlockSpec((d, tn), lambda i, j: (0, w_block(j))),
        ],
        out_specs=pl.BlockSpec((tm, tn), lambda i, j: (i, j)),
        out_shape=jax.ShapeDtypeStruct((t, n), jnp.bfloat16),
        compiler_params=pltpu.CompilerParams(
            dimension_semantics=("parallel", "parallel"),
            vmem_limit_bytes=VMEM_LIMIT),
        name="proj",
    )(h, w)


def _na_kernel(q_ref, k_ref, v_ref, b_ref, o_ref):
    step = pl.program_id(2)
    n_tiles = k_ref.shape[1] // GRID_W
    lane = lax.broadcasted_iota(jnp.int32, (NA_Q_TILE, SLAB), 1) // HEAD_DIM

    for u in range(NA_TILES_PER_STEP):
        r = step * NA_TILES_PER_STEP + u
        first_row = jnp.clip(r - NA_ROWS // 2, 0, n_tiles - NA_ROWS)
        start = pl.multiple_of(first_row * GRID_W, GRID_W)
        tile_rows = slice(u * NA_Q_TILE, (u + 1) * NA_Q_TILE)
        q4 = q_ref[0, tile_rows, :]
        zero = jnp.zeros_like(q4)
        q_stack = jnp.concatenate(
            [jnp.where(lane == h, q4, zero) for h in range(HEADS_PER_STEP)], axis=0)
        s = lax.dot_general(q_stack, k_ref[0, pl.ds(start, NA_WIN), :], _NT_DIMS,
                            preferred_element_type=jnp.float32)
        first_off = first_row - r + NA_ROWS - 1
        s = s + jnp.concatenate(
            [jnp.concatenate([b_ref[h, first_off + 2 * j] for j in range(NA_ROWS // 2)], axis=1)
             for h in range(HEADS_PER_STEP)], axis=0)
        m = jnp.max(s, axis=-1, keepdims=True)
        e = jnp.exp2(s - m)
        inv_l = 1.0 / jnp.sum(e, axis=-1, keepdims=True)
        o_all = jnp.dot(e.astype(jnp.bfloat16), v_ref[0, pl.ds(start, NA_WIN), :],
                        preferred_element_type=jnp.float32) * inv_l
        out = o_all[:NA_Q_TILE]
        for h in range(1, HEADS_PER_STEP):
            out = jnp.where(lane == h, o_all[h * NA_Q_TILE:(h + 1) * NA_Q_TILE], out)
        o_ref[0, tile_rows, :] = out.astype(o_ref.dtype)


def _na_bias_table(rpb):
    n_col_off = 2 * NA_COLS - 1
    c = np.arange(GRID_W)[:, None]
    kc = np.arange(GRID_W)[None, :]
    cs = np.clip(c - NA_COLS // 2, 0, GRID_W - NA_COLS)
    col_ok = (kc >= cs) & (kc < cs + NA_COLS)
    col_off = kc - c + NA_COLS - 1
    assert np.all(((col_off >= 0) & (col_off < n_col_off))[col_ok])
    pick = (np.arange(n_col_off)[:, None, None] == col_off[None]) & col_ok[None]
    pick2 = np.zeros((2, n_col_off, GRID_W, 2, GRID_W), np.float32)
    for half in range(2):
        pick2[half, :, :, half, :] = pick
    pick2 = pick2.reshape(2 * n_col_off, GRID_W, 2 * GRID_W)
    rpb = rpb.astype(jnp.float32)
    row_pairs = jnp.concatenate([rpb[:, :-1], rpb[:, 1:]], axis=-1)
    table = jnp.einsum('hrm,mck->hrck', row_pairs, jnp.asarray(pick2),
                       precision=lax.Precision.HIGHEST)
    ok2 = np.concatenate([col_ok, col_ok], axis=1)
    return jnp.where(jnp.asarray(ok2), table * LOG2E, NEG_INF)


def _na_attention(proj, bias_table, b):
    s = proj.shape[1] // b
    tile = NA_Q_TILE * NA_TILES_PER_STEP
    assert s % tile == 0 and NA_Q_TILE == GRID_W
    n_blocks = s // tile
    return pl.pallas_call(
        _na_kernel,
        grid=(b, N_SLABS, n_blocks),
        in_specs=[
            pl.BlockSpec((1, tile, SLAB), lambda bi, g, j: (g, bi * n_blocks + j, 0)),
            pl.BlockSpec((1, s, SLAB), lambda bi, g, j: (N_SLABS + g, bi, 0)),
            pl.BlockSpec((1, s, SLAB), lambda bi, g, j: (2 * N_SLABS + g, bi, 0)),
            pl.BlockSpec((HEADS_PER_STEP,) + bias_table.shape[1:], lambda bi, g, j: (g, 0, 0, 0)),
        ],
        out_specs=pl.BlockSpec((1, tile, SLAB), lambda bi, g, j: (g, bi * n_blocks + j, 0)),
        out_shape=jax.ShapeDtypeStruct((N_SLABS, b * s, SLAB), jnp.bfloat16),
        compiler_params=pltpu.CompilerParams(
            dimension_semantics=("parallel", "parallel", "arbitrary"),
            vmem_limit_bytes=VMEM_LIMIT),
        name="na_attention",
    )(proj, proj, proj, bias_table)


def _alibi_slopes():
    return np.asarray(2.0 ** (-8.0 * (np.arange(N_HEADS) + 1) / N_HEADS), dtype=np.float32)


def _slab_attention(q4, k4, v4, bias):
    lane = lax.broadcasted_iota(jnp.int32, q4.shape, 1) // HEAD_DIM
    zero = jnp.zeros_like(q4)
    q_stack = jnp.concatenate(
        [jnp.where(lane == h, q4, zero) for h in range(HEADS_PER_STEP)], axis=0)
    s = lax.dot_general(q_stack, k4, _NT_DIMS, preferred_element_type=jnp.float32)
    s = s + bias
    m = jnp.max(s, axis=-1, keepdims=True)
    p = jnp.exp2(s - m)
    l = jnp.sum(p, axis=-1, keepdims=True)
    p = p.astype(jnp.bfloat16)
    o_all = jnp.dot(p, v4, preferred_element_type=jnp.float32)
    out = o_all[:Q_TILE]
    for h in range(1, HEADS_PER_STEP):
        out = jnp.where(lane == h, o_all[h * Q_TILE:(h + 1) * Q_TILE], out)
    return out, m, l


def _dil_kernel(q_ref, ka_ref, kb_ref, kc_ref, va_ref, vb_ref, vc_ref, b_ref,
                o_ref, stat_ref):
    step = pl.program_id(2)
    n_steps = pl.num_programs(2)
    lane = lax.broadcasted_iota(jnp.int32, (Q_TILE, LSE_LANES), 1)
    tiles = q_ref.shape[2] // Q_TILE
    for r in range(q_ref.shape[1]):
        k_win = jnp.concatenate([ka_ref[0, r], kb_ref[0, r], kc_ref[0, r]], axis=0)
        v_win = jnp.concatenate([va_ref[0, r], vb_ref[0, r], vc_ref[0, r]], axis=0)
        for u in range(tiles):
            pattern = 0
            if u == 0:
                pattern = jnp.where(step == 0, 1, pattern)
            if u == tiles - 1:
                pattern = jnp.where(step == n_steps - 1, 2, pattern)
            tile = slice(u * Q_TILE, (u + 1) * Q_TILE)
            win = slice(u * Q_TILE, u * Q_TILE + DIL_WIN)
            stat_tile = jnp.zeros((Q_TILE, LSE_LANES), jnp.float32)
            for hg in range(N_SLABS):
                cols = slice(hg * SLAB, (hg + 1) * SLAB)
                bias = b_ref[pattern, hg * HEADS_PER_STEP:(hg + 1) * HEADS_PER_STEP]
                bias = bias.reshape(HEADS_PER_STEP * Q_TILE, DIL_WIN)
                out, m, l = _slab_attention(q_ref[0, r, tile, cols], k_win[win, cols],
                                            v_win[win, cols], bias)
                o_ref[0, r, tile, cols] = out.astype(o_ref.dtype)
                for h in range(HEADS_PER_STEP):
                    rows = slice(h * Q_TILE, (h + 1) * Q_TILE)
                    head = hg * HEADS_PER_STEP + h
                    stat_tile = jnp.where(lane == head, m[rows], stat_tile)
                    stat_tile = jnp.where(lane == N_HEADS + head, l[rows], stat_tile)
            stat_ref[0, r, tile, :] = stat_tile


def _dil_bias_tables():
    qi = np.arange(Q_TILE)[:, None]
    kj = np.arange(DIL_WIN)[None, :]
    delta = kj - DIL_RADIUS - qi
    in_band = np.abs(delta) <= DIL_RADIUS
    valid = np.stack([in_band,
                      in_band & (kj >= DIL_RADIUS),
                      in_band & (kj < DIL_WIN - DIL_RADIUS)])
    dils = np.asarray([dil for _, dil in DIL_PAIRS])
    dist = jnp.asarray(np.abs(delta)[None] * dils[:, None, None], dtype=jnp.float32)
    bias = -jnp.asarray(_alibi_slopes())[None, :, None, None] * dist[:, None]
    return jnp.where(jnp.asarray(valid)[None, :, None], bias[:, None] * LOG2E, NEG_INF)


def _dil_attention(qkv, bias_tables, group):
    b, dil, l, _ = qkv.shape
    tiles = min(DIL_TILES_PER_STEP, l // Q_TILE)
    seqs = min(DIL_TILES_PER_STEP // tiles, dil)
    tile = Q_TILE * tiles
    n_steps = l // tile
    assert l % tile == 0 and tiles >= 2 and dil % seqs == 0
    half = DIL_RADIUS
    per_tile = tile // half
    n_half = l // half

    def spec_mid(which):
        return pl.BlockSpec((1, seqs, tile, D_MODEL), lambda bi, r, i: (bi, r, i, which))

    def spec_lo(which):
        return pl.BlockSpec((1, seqs, half, D_MODEL),
                            lambda bi, r, i: (bi, r, jnp.maximum(per_tile * i - 1, 0), which))

    def spec_hi(which):
        return pl.BlockSpec((1, seqs, half, D_MODEL),
                            lambda bi, r, i: (bi, r, jnp.minimum(per_tile * (i + 1), n_half - 1), which))

    return pl.pallas_call(
        _dil_kernel,
        grid=(b, dil // seqs, n_steps),
        in_specs=[
            spec_mid(0),
            spec_lo(1), spec_mid(1), spec_hi(1),
            spec_lo(2), spec_mid(2), spec_hi(2),
            pl.BlockSpec((None,) + bias_tables.shape[1:], lambda bi, r, i: (group, 0, 0, 0, 0)),
        ],
        out_specs=[
            pl.BlockSpec((1, seqs, tile, D_MODEL), lambda bi, r, i: (bi, r, i, 0)),
            pl.BlockSpec((1, seqs, tile, LSE_LANES), lambda bi, r, i: (bi, r, i, 0)),
        ],
        out_shape=[
            jax.ShapeDtypeStruct((b, dil, l, D_MODEL), jnp.bfloat16),
            jax.ShapeDtypeStruct((b, dil, l, LSE_LANES), jnp.float32),
        ],
        compiler_params=pltpu.CompilerParams(
            dimension_semantics=("parallel", "parallel", "arbitrary"),
            vmem_limit_bytes=VMEM_LIMIT),
        name=f"dilated_attention_{dil}",
    )(qkv, qkv, qkv, qkv, qkv, qkv, qkv, bias_tables)


def _silu(x):
    half = 0.5 * x
    return half + half * jnp.tanh(half)


def _out0_kernel(o_ref, gate_ref, x_ref, w_ref, g_ref, y_ref, *rest):
    h_refs, slab_refs = rest[:-2], rest[-2:]
    o = jnp.concatenate([o_ref[k] for k in range(N_SLABS)], axis=1).astype(jnp.float32)
    gate = jnp.concatenate([gate_ref[k] for k in range(N_SLABS)], axis=1).astype(jnp.float32)
    z = (o * _silu(gate)).astype(jnp.bfloat16)
    y = x_ref[0] + jnp.dot(z, w_ref[...], preferred_element_type=jnp.float32)
    y_ref[0] = y
    hn = _rmsnorm(y, g_ref[...])
    tm = hn.shape[0]
    dils = [dil for _, dil in DIL_PAIRS]
    for s in range(N_LANE_SLABS):
        slab_refs[0][s] = hn[:, s * LANES:(s + 1) * LANES]
    staged = {1: slab_refs[0]}
    for k, (h_ref, dil) in enumerate(zip(h_refs, dils)):
        if dil == 1:
            h_ref[0, 0] = hn.astype(h_ref.dtype)
            continue
        p = max(q for q in staged if dil % q == 0)
        f, n = dil // p, tm // dil
        keep = any(later % dil == 0 for later in dils[k + 1:])
        for r in range(dil):
            start = (r % p) * (tm // p) + r // p
            for s in range(N_LANE_SLABS):
                rows = staged[p][s, pl.ds(start, n, stride=f), :]
                h_ref[0, r, :, s * LANES:(s + 1) * LANES] = rows.astype(h_ref.dtype)
                if keep:
                    slab_refs[1][s, r * n:(r + 1) * n, :] = rows
        if keep:
            staged[dil] = slab_refs[1]


def _out_proj0(o, proj, x, w_out, g_next, *, tm=512):
    b, s, d = x.shape
    gate_block = proj.shape[0] // N_SLABS - 1
    tiles = s // tm
    tok = pl.BlockSpec((1, tm, d), lambda bi, i: (bi, i, 0))
    dils = [dil for _, dil in DIL_PAIRS]
    return pl.pallas_call(
        _out0_kernel,
        grid=(b, s // tm),
        in_specs=[
            pl.BlockSpec((N_SLABS, tm, SLAB), lambda bi, i: (0, bi * tiles + i, 0)),
            pl.BlockSpec((N_SLABS, tm, SLAB), lambda bi, i: (gate_block, bi * tiles + i, 0)),
            tok,
            pl.BlockSpec((d, d), lambda bi, i: (0, 0)),
            pl.BlockSpec((1, d), lambda bi, i: (0, 0)),
        ],
        out_specs=[tok] + [pl.BlockSpec((1, dil, tm // dil, d), lambda bi, i: (bi, 0, i, 0))
                           for dil in dils],
        out_shape=([jax.ShapeDtypeStruct((b, s, d), jnp.float32)]
                   + [jax.ShapeDtypeStruct((b, dil, s // dil, d), jnp.bfloat16) for dil in dils]),
        scratch_shapes=[pltpu.VMEM((N_LANE_SLABS, tm, LANES), jnp.float32)] * 2,
        compiler_params=pltpu.CompilerParams(
            dimension_semantics=("parallel", "parallel"), vmem_limit_bytes=VMEM_LIMIT),
        name="out_proj0",
    )(o, proj, x, w_out, g_next.reshape(1, d))


_MAX_ROW_STRIDE = 4


def _natural_order(ref, slab_refs, dil):
    _, n, c = ref.shape
    if dil == 1:
        return ref[0].astype(jnp.float32)
    out_ref, mid_ref = slab_refs
    tm = dil * n
    two_level = dil > _MAX_ROW_STRIDE
    if two_level:
        f = _MAX_ROW_STRIDE
        p = dil // f
        assert p <= _MAX_ROW_STRIDE
    pieces = []
    for s in range(c // LANES):
        lanes = slice(s * LANES, (s + 1) * LANES)
        if not two_level:
            for r in range(dil):
                out_ref[s, pl.ds(r, n, stride=dil), :] = ref[r, :, lanes].astype(jnp.float32)
        else:
            for r in range(dil):
                start = (r % p) * (tm // p) + r // p
                mid_ref[s, pl.ds(start, n, stride=f), :] = ref[r, :, lanes].astype(jnp.float32)
            for b_ in range(p):
                out_ref[s, pl.ds(b_, tm // p, stride=p), :] = (
                    mid_ref[s, b_ * (tm // p):(b_ + 1) * (tm // p), :])
        pieces.append(out_ref[s])
    return pieces[0] if len(pieces) == 1 else jnp.concatenate(pieces, axis=1)


def _out1_kernel(o0_ref, o1_ref, o2_ref, l0_ref, l1_ref, l2_ref, gate_ref, x_ref,
                 w_ref, e_ref, g_ref, y_ref, *slab_refs):
    dils = [dil for _, dil in DIL_PAIRS]
    stats = [_natural_order(ref.at[0], slab_refs, dil)
             for ref, dil in zip((l0_ref, l1_ref, l2_ref), dils)]
    sums = [pltpu.roll(st, LSE_LANES - N_HEADS, axis=1) for st in stats]
    m = jnp.maximum(jnp.maximum(stats[0], stats[1]), stats[2])
    es = [jnp.exp2(st - m) for st in stats]
    denom = es[0] * sums[0] + es[1] * sums[1] + es[2] * sums[2]
    head_lane = lax.broadcasted_iota(jnp.int32, denom.shape, 1) < N_HEADS
    o = None
    for e, o_ref, dil in zip(es, (o0_ref, o1_ref, o2_ref), dils):
        w = jnp.where(head_lane, e / denom, 0.0)
        hi = w.astype(jnp.bfloat16)
        lo = (w - hi.astype(jnp.float32)).astype(jnp.bfloat16)
        w_full = jnp.dot(jnp.concatenate([hi, lo], axis=1), e_ref[...],
                         preferred_element_type=jnp.float32)
        term = w_full * _natural_order(o_ref.at[0], slab_refs, dil)
        o = term if o is None else o + term
    gate = gate_ref[0].astype(jnp.float32)
    z = (o * _silu(gate)).astype(jnp.bfloat16)
    x = x_ref[0] + jnp.dot(z, w_ref[...], preferred_element_type=jnp.float32)
    y_ref[0] = _rmsnorm(x, g_ref[...])


def _head_expansion():
    e = np.zeros((LSE_LANES, D_MODEL), np.float32)
    for h in range(N_HEADS):
        e[h, h * HEAD_DIM:(h + 1) * HEAD_DIM] = 1.0
    return jnp.asarray(np.concatenate([e, e], axis=0), dtype=jnp.bfloat16)


def _out_proj1(os_, lses, proj, x, w_out, norm_f, *, tm=512):
    b, s, d = x.shape
    gate_block = proj.shape[2] // d - 1
    dils = [dil for _, dil in DIL_PAIRS]

    def grouped(width, dil):
        return pl.BlockSpec((1, dil, tm // dil, width), lambda bi, i: (bi, 0, i, 0))

    tok = pl.BlockSpec((1, tm, d), lambda bi, i: (bi, i, 0))
    return pl.pallas_call(
        _out1_kernel,
        grid=(b, s // tm),
        in_specs=(
            [grouped(d, dil) for dil in dils]
            + [grouped(LSE_LANES, dil) for dil in dils]
            + [pl.BlockSpec((1, tm, d), lambda bi, i: (bi, i, gate_block)),
               tok,
               pl.BlockSpec((d, d), lambda bi, i: (0, 0)),
               pl.BlockSpec((2 * LSE_LANES, d), lambda bi, i: (0, 0)),
               pl.BlockSpec((1, d), lambda bi, i: (0, 0))]),
        out_specs=tok,
        out_shape=jax.ShapeDtypeStruct((b, s, d), jnp.float32),
        scratch_shapes=[pltpu.VMEM((N_LANE_SLABS, tm, LANES), jnp.float32)] * 2,
        compiler_params=pltpu.CompilerParams(
            dimension_semantics=("parallel", "parallel"), vmem_limit_bytes=VMEM_LIMIT),
        name="out_proj1",
    )(*os_, *lses, proj, x, w_out, _head_expansion(), norm_f.reshape(1, d))


def kernel(x, norm_0, w_in_0, rpb_0, w_out_0, norm_1, w_in_1, w_out_1, norm_f):
    b, s, d = x.shape
    t = b * s

    proj0 = _norm_proj(x.reshape(t, d), norm_0, w_in_0)
    o0 = _na_attention(proj0, _na_bias_table(rpb_0), b)
    x1, *h1 = _out_proj0(o0, proj0, x, w_out_0.astype(jnp.bfloat16), norm_1)

    gate_block = 3 * N_DIL_GROUPS
    alibi = _dil_bias_tables()
    outs, lses, proj_gate = [], [], None
    for g, (window, dil) in enumerate(DIL_PAIRS):
        assert window // (2 * dil) == DIL_RADIUS
        blocks = (3 * g, 3 * g + 1, 3 * g + 2) + ((gate_block,) if dil == 1 else ())
        qkv = _proj(h1[g].reshape(t, d), w_in_1, blocks).reshape(b, dil, s // dil, -1)
        if dil == 1:
            proj_gate = qkv.reshape(b, s, -1)
        o_g, lse_g = _dil_attention(qkv, alibi, g)
        outs.append(o_g)
        lses.append(lse_g)
    return _out_proj1(outs, lses, proj_gate, x1, w_out_1.astype(jnp.bfloat16), norm_f)
```

```python
import functools
import math

import numpy as np
import jax
import jax.numpy as jnp
from jax import lax
from jax.experimental import pallas as pl
from jax.experimental.pallas import tpu as pltpu

D_MODEL = 1024
HEAD_DIM = 64
N_HEADS = 16
GRID_W = 64
NA_ROWS = 8
NA_COLS = 16
DIL_PAIRS = ((128, 1), (512, 4), (2048, 16))
N_DIL_GROUPS = len(DIL_PAIRS)
RMS_EPS = 1e-6
NEG_INF = -1e30
LOG2E = math.log2(math.e)

LANES = 128
N_LANE_SLABS = D_MODEL // LANES
HEADS_PER_STEP = 4
SLAB = HEADS_PER_STEP * HEAD_DIM
N_SLABS = D_MODEL // SLAB
Q_TILE = 128
NA_Q_TILE = GRID_W
NA_WIN = NA_ROWS * GRID_W
NA_TILES_PER_STEP = 32
DIL_TILES_PER_STEP = 8
DIL_RADIUS = 64
DIL_WIN = Q_TILE + 2 * DIL_RADIUS
LSE_LANES = LANES
VMEM_LIMIT = 56 * 1024 * 1024

_NT_DIMS = (((1,), (1,)), ((), ()))


def _rmsnorm(x, g):
    ms = jnp.mean(x * x, axis=-1, keepdims=True)
    return x * lax.rsqrt(ms + RMS_EPS) * g


def _q_scaled_bf16(w_ref):
    q_steps = D_MODEL // w_ref.shape[1]
    scale = jnp.where(pl.program_id(1) < q_steps, LOG2E / math.sqrt(HEAD_DIM), 1.0)
    return (w_ref[...] * scale).astype(jnp.bfloat16)


def _norm_proj_kernel(x_ref, g_ref, w_ref, o_ref, h_ref):
    i, j = pl.program_id(0), pl.program_id(1)
    rows = x_ref.shape[0]

    def normalise_chunk():
        start = pl.multiple_of(j * rows, rows)
        h_ref[i % 2, pl.ds(start, rows), :] = _rmsnorm(x_ref[...], g_ref[...]).astype(h_ref.dtype)

    @pl.when(i == 0)
    def _():
        normalise_chunk()

    @pl.when(i > 0)
    def _():
        res = jnp.dot(h_ref[(i - 1) % 2], _q_scaled_bf16(w_ref),
                      preferred_element_type=jnp.float32).astype(o_ref.dtype)
        for k in range(o_ref.shape[0]):
            o_ref[k] = res[:, k * SLAB:(k + 1) * SLAB]
        normalise_chunk()


def _norm_proj(x, g, w, *, tm=2048, tn=1024):
    t, d = x.shape
    n = w.shape[1]
    n_i, n_j = t // tm, n // tn
    chunk = tm // n_j
    n_chunks = t // chunk
    return pl.pallas_call(
        _norm_proj_kernel,
        grid=(n_i + 1, n_j),
        in_specs=[
            pl.BlockSpec((chunk, d), lambda i, j: (jnp.minimum(i * n_j + j, n_chunks - 1), 0)),
            pl.BlockSpec((1, d), lambda i, j: (0, 0)),
            pl.BlockSpec((d, tn), lambda i, j: (0, j)),
        ],
        out_specs=pl.BlockSpec((tn // SLAB, tm, SLAB),
                               lambda i, j: (jnp.where(i == 0, 0, j), jnp.maximum(i - 1, 0), 0)),
        out_shape=jax.ShapeDtypeStruct((n // SLAB, t, SLAB), jnp.bfloat16),
        scratch_shapes=[pltpu.VMEM((2, tm, d), jnp.bfloat16)],
        compiler_params=pltpu.CompilerParams(
            dimension_semantics=("arbitrary", "arbitrary"),
            vmem_limit_bytes=VMEM_LIMIT),
        name="norm_proj",
    )(x, g.reshape(1, d), w)


def _proj_kernel(h_ref, w_ref, o_ref):
    o_ref[...] = jnp.dot(h_ref[...], _q_scaled_bf16(w_ref),
                         preferred_element_type=jnp.float32).astype(o_ref.dtype)


def _proj(h, w, col_blocks, *, tm=2048, tn=1024):
    t, d = h.shape
    per_block = d // tn
    n = len(col_blocks) * d

    def w_block(j):
        k_of_j = j // per_block
        blk = col_blocks[-1]
        for k in range(len(col_blocks) - 2, -1, -1):
            blk = jnp.where(k_of_j == k, col_blocks[k], blk)
        return blk * per_block + j % per_block

    return pl.pallas_call(
        _proj_kernel,
        grid=(t // tm, n // tn),
        in_specs=[
            pl.BlockSpec((tm, d), lambda i, j: (i, 0)),
            pl.BlockSpec((d, tn), lambda i, j: (0, w_block(j))),
        ],
        out_specs=pl.BlockSpec((tm, tn), lambda i, j: (i, j)),
        out_shape=jax.ShapeDtypeStruct((t, n), jnp.bfloat16),
        compiler_params=pltpu.CompilerParams(
            dimension_semantics=("parallel", "parallel"),
            vmem_limit_bytes=VMEM_LIMIT),
        name="proj",
    )(h, w)


def _na_kernel(q_ref, k_ref, v_ref, b_ref, o_ref):
    step = pl.program_id(2)
    n_tiles = k_ref.shape[1] // GRID_W
    lane = lax.broadcasted_iota(jnp.int32, (NA_Q_TILE, SLAB), 1) // HEAD_DIM

    for u in range(NA_TILES_PER_STEP):
        r = step * NA_TILES_PER_STEP + u
        first_row = jnp.clip(r - NA_ROWS // 2, 0, n_tiles - NA_ROWS)
        start = pl.multiple_of(first_row * GRID_W, GRID_W)
        tile_rows = slice(u * NA_Q_TILE, (u + 1) * NA_Q_TILE)
        q4 = q_ref[0, tile_rows, :]
        zero = jnp.zeros_like(q4)
        q_stack = jnp.concatenate(
            [jnp.where(lane == h, q4, zero) for h in range(HEADS_PER_STEP)], axis=0)
        s = lax.dot_general(q_stack, k_ref[0, pl.ds(start, NA_WIN), :], _NT_DIMS,
                            preferred_element_type=jnp.float32)
        first_off = first_row - r + NA_ROWS - 1
        s = s + jnp.concatenate(
            [jnp.concatenate([b_ref[h, first_off + 2 * j] for j in range(NA_ROWS // 2)], axis=1)
             for h in range(HEADS_PER_STEP)], axis=0)
        m = jnp.max(s, axis=-1, keepdims=True)
        e = jnp.exp2(s - m)
        inv_l = 1.0 / jnp.sum(e, axis=-1, keepdims=True)
        o_all = jnp.dot(e.astype(jnp.bfloat16), v_ref[0, pl.ds(start, NA_WIN), :],
                        preferred_element_type=jnp.float32) * inv_l
        out = o_all[:NA_Q_TILE]
        for h in range(1, HEADS_PER_STEP):
            out = jnp.where(lane == h, o_all[h * NA_Q_TILE:(h + 1) * NA_Q_TILE], out)
        o_ref[0, tile_rows, :] = out.astype(o_ref.dtype)


def _na_bias_table(rpb):
    n_col_off = 2 * NA_COLS - 1
    c = np.arange(GRID_W)[:, None]
    kc = np.arange(GRID_W)[None, :]
    cs = np.clip(c - NA_COLS // 2, 0, GRID_W - NA_COLS)
    col_ok = (kc >= cs) & (kc < cs + NA_COLS)
    col_off = kc - c + NA_COLS - 1
    assert np.all(((col_off >= 0) & (col_off < n_col_off))[col_ok])
    pick = (np.arange(n_col_off)[:, None, None] == col_off[None]) & col_ok[None]
    pick2 = np.zeros((2, n_col_off, GRID_W, 2, GRID_W), np.float32)
    for half in range(2):
        pick2[half, :, :, half, :] = pick
    pick2 = pick2.reshape(2 * n_col_off, GRID_W, 2 * GRID_W)
    rpb = rpb.astype(jnp.float32)
    row_pairs = jnp.concatenate([rpb[:, :-1], rpb[:, 1:]], axis=-1)
    table = jnp.einsum('hrm,mck->hrck', row_pairs, jnp.asarray(pick2),
                       precision=lax.Precision.HIGHEST)
    ok2 = np.concatenate([col_ok, col_ok], axis=1)
    return jnp.where(jnp.asarray(ok2), table * LOG2E, NEG_INF)


def _na_attention(proj, bias_table, b):
    s = proj.shape[1] // b
    tile = NA_Q_TILE * NA_TILES_PER_STEP
    assert s % tile == 0 and NA_Q_TILE == GRID_W
    n_blocks = s // tile
    return pl.pallas_call(
        _na_kernel,
        grid=(b, N_SLABS, n_blocks),
        in_specs=[
            pl.BlockSpec((1, tile, SLAB), lambda bi, g, j: (g, bi * n_blocks + j, 0)),
            pl.BlockSpec((1, s, SLAB), lambda bi, g, j: (N_SLABS + g, bi, 0)),
            pl.BlockSpec((1, s, SLAB), lambda bi, g, j: (2 * N_SLABS + g, bi, 0)),
            pl.BlockSpec((HEADS_PER_STEP,) + bias_table.shape[1:], lambda bi, g, j: (g, 0, 0, 0)),
        ],
        out_specs=pl.BlockSpec((1, tile, SLAB), lambda bi, g, j: (g, bi * n_blocks + j, 0)),
        out_shape=jax.ShapeDtypeStruct((N_SLABS, b * s, SLAB), jnp.bfloat16),
        compiler_params=pltpu.CompilerParams(
            dimension_semantics=("parallel", "parallel", "arbitrary"),
            vmem_limit_bytes=VMEM_LIMIT),
        name="na_attention",
    )(proj, proj, proj, bias_table)


def _alibi_slopes():
    return np.asarray(2.0 ** (-8.0 * (np.arange(N_HEADS) + 1) / N_HEADS), dtype=np.float32)


def _slab_attention(q4, k4, v4, bias):
    lane = lax.broadcasted_iota(jnp.int32, q4.shape, 1) // HEAD_DIM
    zero = jnp.zeros_like(q4)
    q_stack = jnp.concatenate(
        [jnp.where(lane == h, q4, zero) for h in range(HEADS_PER_STEP)], axis=0)
    s = lax.dot_general(q_stack, k4, _NT_DIMS, preferred_element_type=jnp.float32)
    s = s + bias
    m = jnp.max(s, axis=-1, keepdims=True)
    p = jnp.exp2(s - m)
    l = jnp.sum(p, axis=-1, keepdims=True)
    p = p.astype(jnp.bfloat16)
    o_all = jnp.dot(p, v4, preferred_element_type=jnp.float32)
    out = o_all[:Q_TILE]
    for h in range(1, HEADS_PER_STEP):
        out = jnp.where(lane == h, o_all[h * Q_TILE:(h + 1) * Q_TILE], out)
    return out, m, l


def _dil_kernel(q_ref, ka_ref, kb_ref, kc_ref, va_ref, vb_ref, vc_ref, b_ref,
                o_ref, stat_ref):
    step = pl.program_id(2)
    n_steps = pl.num_programs(2)
    lane = lax.broadcasted_iota(jnp.int32, (Q_TILE, LSE_LANES), 1)
    tiles = q_ref.shape[2] // Q_TILE
    for r in range(q_ref.shape[1]):
        k_win = jnp.concatenate([ka_ref[0, r], kb_ref[0, r], kc_ref[0, r]], axis=0)
        v_win = jnp.concatenate([va_ref[0, r], vb_ref[0, r], vc_ref[0, r]], axis=0)
        for u in range(tiles):
            pattern = 0
            if u == 0:
                pattern = jnp.where(step == 0, 1, pattern)
            if u == tiles - 1:
                pattern = jnp.where(step == n_steps - 1, 2, pattern)
            tile = slice(u * Q_TILE, (u + 1) * Q_TILE)
            win = slice(u * Q_TILE, u * Q_TILE + DIL_WIN)
            stat_tile = jnp.zeros((Q_TILE, LSE_LANES), jnp.float32)
            for hg in range(N_SLABS):
                cols = slice(hg * SLAB, (hg + 1) * SLAB)
                bias = b_ref[pattern, hg * HEADS_PER_STEP:(hg + 1) * HEADS_PER_STEP]
                bias = bias.reshape(HEADS_PER_STEP * Q_TILE, DIL_WIN)
                out, m, l = _slab_attention(q_ref[0, r, tile, cols], k_win[win, cols],
                                            v_win[win, cols], bias)
                o_ref[0, r, tile, cols] = out.astype(o_ref.dtype)
                for h in range(HEADS_PER_STEP):
                    rows = slice(h * Q_TILE, (h + 1) * Q_TILE)
                    head = hg * HEADS_PER_STEP + h
                    stat_tile = jnp.where(lane == head, m[rows], stat_tile)
                    stat_tile = jnp.where(lane == N_HEADS + head, l[rows], stat_tile)
            stat_ref[0, r, tile, :] = stat_tile


def _dil_bias_tables():
    qi = np.arange(Q_TILE)[:, None]
    kj = np.arange(DIL_WIN)[None, :]
    delta = kj - DIL_RADIUS - qi
    in_band = np.abs(delta) <= DIL_RADIUS
    valid = np.stack([in_band,
                      in_band & (kj >= DIL_RADIUS),
                      in_band & (kj < DIL_WIN - DIL_RADIUS)])
    dils = np.asarray([dil for _, dil in DIL_PAIRS])
    dist = jnp.asarray(np.abs(delta)[None] * dils[:, None, None], dtype=jnp.float32)
    bias = -jnp.asarray(_alibi_slopes())[None, :, None, None] * dist[:, None]
    return jnp.where(jnp.asarray(valid)[None, :, None], bias[:, None] * LOG2E, NEG_INF)


def _dil_attention(qkv, bias_tables, group):
    b, dil, l, _ = qkv.shape
    tiles = min(DIL_TILES_PER_STEP, l // Q_TILE)
    seqs = min(DIL_TILES_PER_STEP // tiles, dil)
    tile = Q_TILE * tiles
    n_steps = l // tile
    assert l % tile == 0 and tiles >= 2 and dil % seqs == 0
    half = DIL_RADIUS
    per_tile = tile // half
    n_half = l // half

    def spec_mid(which):
        return pl.BlockSpec((1, seqs, tile, D_MODEL), lambda bi, r, i: (bi, r, i, which))

    def spec_lo(which):
        return pl.BlockSpec((1, seqs, half, D_MODEL),
                            lambda bi, r, i: (bi, r, jnp.maximum(per_tile * i - 1, 0), which))

    def spec_hi(which):
        return pl.BlockSpec((1, seqs, half, D_MODEL),
                            lambda bi, r, i: (bi, r, jnp.minimum(per_tile * (i + 1), n_half - 1), which))

    return pl.pallas_call(
        _dil_kernel,
        grid=(b, dil // seqs, n_steps),
        in_specs=[
            spec_mid(0),
            spec_lo(1), spec_mid(1), spec_hi(1),
            spec_lo(2), spec_mid(2), spec_hi(2),
            pl.BlockSpec((None,) + bias_tables.shape[1:], lambda bi, r, i: (group, 0, 0, 0, 0)),
        ],
        out_specs=[
            pl.BlockSpec((1, seqs, tile, D_MODEL), lambda bi, r, i: (bi, r, i, 0)),
            pl.BlockSpec((1, seqs, tile, LSE_LANES), lambda bi, r, i: (bi, r, i, 0)),
        ],
        out_shape=[
            jax.ShapeDtypeStruct((b, dil, l, D_MODEL), jnp.bfloat16),
            jax.ShapeDtypeStruct((b, dil, l, LSE_LANES), jnp.float32),
        ],
        compiler_params=pltpu.CompilerParams(
            dimension_semantics=("parallel", "parallel", "arbitrary"),
            vmem_limit_bytes=VMEM_LIMIT),
        name=f"dilated_attention_{dil}",
    )(qkv, qkv, qkv, qkv, qkv, qkv, qkv, bias_tables)


def _silu(x):
    half = 0.5 * x
    return half + half * jnp.tanh(half)


def _out0_kernel(o_ref, gate_ref, x_ref, w_ref, g_ref, y_ref, *rest):
    h_refs, slab_refs = rest[:-2], rest[-2:]
    o = jnp.concatenate([o_ref[k] for k in range(N_SLABS)], axis=1).astype(jnp.float32)
    gate = jnp.concatenate([gate_ref[k] for k in range(N_SLABS)], axis=1).astype(jnp.float32)
    z = (o * _silu(gate)).astype(jnp.bfloat16)
    y = x_ref[0] + jnp.dot(z, w_ref[...], preferred_element_type=jnp.float32)
    y_ref[0] = y
    hn = _rmsnorm(y, g_ref[...])
    tm = hn.shape[0]
    dils = [dil for _, dil in DIL_PAIRS]
    for s in range(N_LANE_SLABS):
        slab_refs[0][s] = hn[:, s * LANES:(s + 1) * LANES]
    staged = {1: slab_refs[0]}
    for k, (h_ref, dil) in enumerate(zip(h_refs, dils)):
        if dil == 1:
            h_ref[0, 0] = hn.astype(h_ref.dtype)
            continue
        p = max(q for q in staged if dil % q == 0)
        f, n = dil // p, tm // dil
        keep = any(later % dil == 0 for later in dils[k + 1:])
        for r in range(dil):
            start = (r % p) * (tm // p) + r // p
            for s in range(N_LANE_SLABS):
                rows = staged[p][s, pl.ds(start, n, stride=f), :]
                h_ref[0, r, :, s * LANES:(s + 1) * LANES] = rows.astype(h_ref.dtype)
                if keep:
                    slab_refs[1][s, r * n:(r + 1) * n, :] = rows
        if keep:
            staged[dil] = slab_refs[1]


def _out_proj0(o, proj, x, w_out, g_next, *, tm=1024):
    b, s, d = x.shape
    gate_block = proj.shape[0] // N_SLABS - 1
    tiles = s // tm
    tok = pl.BlockSpec((1, tm, d), lambda bi, i: (bi, i, 0))
    dils = [dil for _, dil in DIL_PAIRS]
    return pl.pallas_call(
        _out0_kernel,
        grid=(b, s // tm),
        in_specs=[
            pl.BlockSpec((N_SLABS, tm, SLAB), lambda bi, i: (0, bi * tiles + i, 0)),
            pl.BlockSpec((N_SLABS, tm, SLAB), lambda bi, i: (gate_block, bi * tiles + i, 0)),
            tok,
            pl.BlockSpec((d, d), lambda bi, i: (0, 0)),
            pl.BlockSpec((1, d), lambda bi, i: (0, 0)),
        ],
        out_specs=[tok] + [pl.BlockSpec((1, dil, tm // dil, d), lambda bi, i: (bi, 0, i, 0))
                           for dil in dils],
        out_shape=([jax.ShapeDtypeStruct((b, s, d), jnp.float32)]
                   + [jax.ShapeDtypeStruct((b, dil, s // dil, d), jnp.bfloat16) for dil in dils]),
        scratch_shapes=[pltpu.VMEM((N_LANE_SLABS, tm, LANES), jnp.float32)] * 2,
        compiler_params=pltpu.CompilerParams(
            dimension_semantics=("parallel", "parallel"), vmem_limit_bytes=VMEM_LIMIT),
        name="out_proj0",
    )(o, proj, x, w_out, g_next.reshape(1, d))


_MAX_ROW_STRIDE = 4


def _natural_order(ref, slab_refs, dil):
    _, n, c = ref.shape
    if dil == 1:
        return ref[0].astype(jnp.float32)
    out_ref, mid_ref = slab_refs
    tm = dil * n
    two_level = dil > _MAX_ROW_STRIDE
    if two_level:
        f = _MAX_ROW_STRIDE
        p = dil // f
        assert p <= _MAX_ROW_STRIDE
    pieces = []
    for s in range(c // LANES):
        lanes = slice(s * LANES, (s + 1) * LANES)
        if not two_level:
            for r in range(dil):
                out_ref[s, pl.ds(r, n, stride=dil), :] = ref[r, :, lanes].astype(jnp.float32)
        else:
            for r in range(dil):
                start = (r % p) * (tm // p) + r // p
                mid_ref[s, pl.ds(start, n, stride=f), :] = ref[r, :, lanes].astype(jnp.float32)
            for b_ in range(p):
                out_ref[s, pl.ds(b_, tm // p, stride=p), :] = (
                    mid_ref[s, b_ * (tm // p):(b_ + 1) * (tm // p), :])
        pieces.append(out_ref[s])
    return pieces[0] if len(pieces) == 1 else jnp.concatenate(pieces, axis=1)


def _out1_kernel(o0_ref, o1_ref, o2_ref, l0_ref, l1_ref, l2_ref, gate_ref, x_ref,
                 w_ref, e_ref, g_ref, y_ref, *slab_refs):
    dils = [dil for _, dil in DIL_PAIRS]
    stats = [_natural_order(ref.at[0], slab_refs, dil)
             for ref, dil in zip((l0_ref, l1_ref, l2_ref), dils)]
    sums = [pltpu.roll(st, LSE_LANES - N_HEADS, axis=1) for st in stats]
    m = jnp.maximum(jnp.maximum(stats[0], stats[1]), stats[2])
    es = [jnp.exp2(st - m) for st in stats]
    denom = es[0] * sums[0] + es[1] * sums[1] + es[2] * sums[2]
    head_lane = lax.broadcasted_iota(jnp.int32, denom.shape, 1) < N_HEADS
    o = None
    for e, o_ref, dil in zip(es, (o0_ref, o1_ref, o2_ref), dils):
        w = jnp.where(head_lane, e / denom, 0.0)
        hi = w.astype(jnp.bfloat16)
        lo = (w - hi.astype(jnp.float32)).astype(jnp.bfloat16)
        w_full = jnp.dot(jnp.concatenate([hi, lo], axis=1), e_ref[...],
                         preferred_element_type=jnp.float32)
        term = w_full * _natural_order(o_ref.at[0], slab_refs, dil)
        o = term if o is None else o + term
    gate = gate_ref[0].astype(jnp.float32)
    z = (o * _silu(gate)).astype(jnp.bfloat16)
    x = x_ref[0] + jnp.dot(z, w_ref[...], preferred_element_type=jnp.float32)
    y_ref[0] = _rmsnorm(x, g_ref[...])


def _head_expansion():
    e = np.zeros((LSE_LANES, D_MODEL), np.float32)
    for h in range(N_HEADS):
        e[h, h * HEAD_DIM:(h + 1) * HEAD_DIM] = 1.0
    return jnp.asarray(np.concatenate([e, e], axis=0), dtype=jnp.bfloat16)


def _out_proj1(os_, lses, proj, x, w_out, norm_f, *, tm=1024):
    b, s, d = x.shape
    gate_block = proj.shape[2] // d - 1
    dils = [dil for _, dil in DIL_PAIRS]

    def grouped(width, dil):
        return pl.BlockSpec((1, dil, tm // dil, width), lambda bi, i: (bi, 0, i, 0))

    tok = pl.BlockSpec((1, tm, d), lambda bi, i: (bi, i, 0))
    return pl.pallas_call(
        _out1_kernel,
        grid=(b, s // tm),
        in_specs=(
            [grouped(d, dil) for dil in dils]
            + [grouped(LSE_LANES, dil) for dil in dils]
            + [pl.BlockSpec((1, tm, d), lambda bi, i: (bi, i, gate_block)),
               tok,
               pl.BlockSpec((d, d), lambda bi, i: (0, 0)),
               pl.BlockSpec((2 * LSE_LANES, d), lambda bi, i: (0, 0)),
               pl.BlockSpec((1, d), lambda bi, i: (0, 0))]),
        out_specs=tok,
        out_shape=jax.ShapeDtypeStruct((b, s, d), jnp.float32),
        scratch_shapes=[pltpu.VMEM((N_LANE_SLABS, tm, LANES), jnp.float32)] * 2,
        compiler_params=pltpu.CompilerParams(
            dimension_semantics=("parallel", "parallel"), vmem_limit_bytes=VMEM_LIMIT),
        name="out_proj1",
    )(*os_, *lses, proj, x, w_out, _head_expansion(), norm_f.reshape(1, d))


def kernel(x, norm_0, w_in_0, rpb_0, w_out_0, norm_1, w_in_1, w_out_1, norm_f):
    b, s, d = x.shape
    t = b * s

    proj0 = _norm_proj(x.reshape(t, d), norm_0, w_in_0)
    o0 = _na_attention(proj0, _na_bias_table(rpb_0), b)
    x1, *h1 = _out_proj0(o0, proj0, x, w_out_0.astype(jnp.bfloat16), norm_1)

    gate_block = 3 * N_DIL_GROUPS
    alibi = _dil_bias_tables()
    outs, lses, proj_gate = [], [], None
    for g, (window, dil) in enumerate(DIL_PAIRS):
        assert window // (2 * dil) == DIL_RADIUS
        blocks = (3 * g, 3 * g + 1, 3 * g + 2) + ((gate_block,) if dil == 1 else ())
        qkv = _proj(h1[g].reshape(t, d), w_in_1, blocks).reshape(b, dil, s // dil, -1)
        if dil == 1:
            proj_gate = qkv.reshape(b, s, -1)
        o_g, lse_g = _dil_attention(qkv, alibi, g)
        outs.append(o_g)
        lses.append(lse_g)
    return _out_proj1(outs, lses, proj_gate, x1, w_out_1.astype(jnp.bfloat16), norm_f)
```

```python
import functools
import math

import numpy as np
import jax
import jax.numpy as jnp
from jax import lax
from jax.experimental import pallas as pl
from jax.experimental.pallas import tpu as pltpu

D_MODEL = 1024
HEAD_DIM = 64
N_HEADS = 16
GRID_W = 64
NA_ROWS = 8
NA_COLS = 16
DIL_PAIRS = ((128, 1), (512, 4), (2048, 16))
N_DIL_GROUPS = len(DIL_PAIRS)
RMS_EPS = 1e-6
NEG_INF = -1e30
LOG2E = math.log2(math.e)

LANES = 128
N_LANE_SLABS = D_MODEL // LANES
HEADS_PER_STEP = 4
SLAB = HEADS_PER_STEP * HEAD_DIM
N_SLABS = D_MODEL // SLAB
Q_TILE = 128
NA_Q_TILE = GRID_W
NA_WIN = NA_ROWS * GRID_W
NA_TILES_PER_STEP = 32
DIL_TILES_PER_STEP = 8
DIL_RADIUS = 64
DIL_WIN = Q_TILE + 2 * DIL_RADIUS
LSE_LANES = LANES
VMEM_LIMIT = 56 * 1024 * 1024

_NT_DIMS = (((1,), (1,)), ((), ()))


def _rmsnorm(x, g):
    ms = jnp.mean(x * x, axis=-1, keepdims=True)
    return x * lax.rsqrt(ms + RMS_EPS) * g


def _q_scaled_bf16(w_ref):
    q_steps = D_MODEL // w_ref.shape[1]
    scale = jnp.where(pl.program_id(1) < q_steps, LOG2E / math.sqrt(HEAD_DIM), 1.0)
    return (w_ref[...] * scale).astype(jnp.bfloat16)


def _norm_proj_kernel(x_ref, g_ref, w_ref, o_ref, h_ref):
    i, j = pl.program_id(0), pl.program_id(1)
    rows = x_ref.shape[0]

    def normalise_chunk():
        start = pl.multiple_of(j * rows, rows)
        h_ref[i % 2, pl.ds(start, rows), :] = _rmsnorm(x_ref[...], g_ref[...]).astype(h_ref.dtype)

    @pl.when(i == 0)
    def _():
        normalise_chunk()

    @pl.when(i > 0)
    def _():
        res = jnp.dot(h_ref[(i - 1) % 2], _q_scaled_bf16(w_ref),
                      preferred_element_type=jnp.float32).astype(o_ref.dtype)
        for k in range(o_ref.shape[0]):
            o_ref[k] = res[:, k * SLAB:(k + 1) * SLAB]
        normalise_chunk()


def _norm_proj(x, g, w, *, tm=2048, tn=1024):
    t, d = x.shape
    n = w.shape[1]
    n_i, n_j = t // tm, n // tn
    chunk = tm // n_j
    n_chunks = t // chunk
    return pl.pallas_call(
        _norm_proj_kernel,
        grid=(n_i + 1, n_j),
        in_specs=[
            pl.BlockSpec((chunk, d), lambda i, j: (jnp.minimum(i * n_j + j, n_chunks - 1), 0)),
            pl.BlockSpec((1, d), lambda i, j: (0, 0)),
            pl.BlockSpec((d, tn), lambda i, j: (0, j)),
        ],
        out_specs=pl.BlockSpec((tn // SLAB, tm, SLAB),
                               lambda i, j: (jnp.where(i == 0, 0, j), jnp.maximum(i - 1, 0), 0)),
        out_shape=jax.ShapeDtypeStruct((n // SLAB, t, SLAB), jnp.bfloat16),
        scratch_shapes=[pltpu.VMEM((2, tm, d), jnp.bfloat16)],
        compiler_params=pltpu.CompilerParams(
            dimension_semantics=("arbitrary", "arbitrary"),
            vmem_limit_bytes=VMEM_LIMIT),
        name="norm_proj",
    )(x, g.reshape(1, d), w)


def _proj_kernel(h_ref, w_ref, o_ref):
    o_ref[...] = jnp.dot(h_ref[...], _q_scaled_bf16(w_ref),
                         preferred_element_type=jnp.float32).astype(o_ref.dtype)


def _proj(h, w, col_blocks, *, tm=2048, tn=1024):
    t, d = h.shape
    per_block = d // tn
    n = len(col_blocks) * d

    def w_block(j):
        k_of_j = j // per_block
        blk = col_blocks[-1]
        for k in range(len(col_blocks) - 2, -1, -1):
            blk = jnp.where(k_of_j == k, col_blocks[k], blk)
        return blk * per_block + j % per_block

    return pl.pallas_call(
        _proj_kernel,
        grid=(t // tm, n // tn),
        in_specs=[
            pl.BlockSpec((tm, d), lambda i, j: (i, 0)),
            pl.BlockSpec((d, tn), lambda i, j: (0, w_block(j))),
        ],
        out_specs=pl.BlockSpec((tm, tn), lambda i, j: (i, j)),
        out_shape=jax.ShapeDtypeStruct((t, n), jnp.bfloat16),
        compiler_params=pltpu.CompilerParams(
            dimension_semantics=("parallel", "parallel"),
            vmem_limit_bytes=VMEM_LIMIT),
        name="proj",
    )(h, w)


def _na_kernel(q_ref, k_ref, v_ref, b_ref, o_ref):
    step = pl.program_id(2)
    n_tiles = k_ref.shape[1] // GRID_W
    lane = lax.broadcasted_iota(jnp.int32, (NA_Q_TILE, SLAB), 1) // HEAD_DIM

    for u in range(NA_TILES_PER_STEP):
        r = step * NA_TILES_PER_STEP + u
        first_row = jnp.clip(r - NA_ROWS // 2, 0, n_tiles - NA_ROWS)
        start = pl.multiple_of(first_row * GRID_W, GRID_W)
        tile_rows = slice(u * NA_Q_TILE, (u + 1) * NA_Q_TILE)
        q4 = q_ref[0, tile_rows, :]
        zero = jnp.zeros_like(q4)
        q_stack = jnp.concatenate(
            [jnp.where(lane == h, q4, zero) for h in range(HEADS_PER_STEP)], axis=0)
        s = lax.dot_general(q_stack, k_ref[0, pl.ds(start, NA_WIN), :], _NT_DIMS,
                            preferred_element_type=jnp.float32)
        first_off = first_row - r + NA_ROWS - 1
        s = s + jnp.concatenate(
            [jnp.concatenate([b_ref[h, first_off + 2 * j] for j in range(NA_ROWS // 2)], axis=1)
             for h in range(HEADS_PER_STEP)], axis=0)
        m = jnp.max(s, axis=-1, keepdims=True)
        e = jnp.exp2(s - m)
        inv_l = 1.0 / jnp.sum(e, axis=-1, keepdims=True)
        o_all = jnp.dot(e.astype(jnp.bfloat16), v_ref[0, pl.ds(start, NA_WIN), :],
                        preferred_element_type=jnp.float32) * inv_l
        out = o_all[:NA_Q_TILE]
        for h in range(1, HEADS_PER_STEP):
            out = jnp.where(lane == h, o_all[h * NA_Q_TILE:(h + 1) * NA_Q_TILE], out)
        o_ref[0, tile_rows, :] = out.astype(o_ref.dtype)


def _na_bias_table(rpb):
    n_col_off = 2 * NA_COLS - 1
    c = np.arange(GRID_W)[:, None]
    kc = np.arange(GRID_W)[None, :]
    cs = np.clip(c - NA_COLS // 2, 0, GRID_W - NA_COLS)
    col_ok = (kc >= cs) & (kc < cs + NA_COLS)
    col_off = kc - c + NA_COLS - 1
    assert np.all(((col_off >= 0) & (col_off < n_col_off))[col_ok])
    pick = (np.arange(n_col_off)[:, None, None] == col_off[None]) & col_ok[None]
    pick2 = np.zeros((2, n_col_off, GRID_W, 2, GRID_W), np.float32)
    for half in range(2):
        pick2[half, :, :, half, :] = pick
    pick2 = pick2.reshape(2 * n_col_off, GRID_W, 2 * GRID_W)
    rpb = rpb.astype(jnp.float32)
    row_pairs = jnp.concatenate([rpb[:, :-1], rpb[:, 1:]], axis=-1)
    table = jnp.einsum('hrm,mck->hrck', row_pairs, jnp.asarray(pick2),
                       precision=lax.Precision.HIGHEST)
    ok2 = np.concatenate([col_ok, col_ok], axis=1)
    return jnp.where(jnp.asarray(ok2), table * LOG2E, NEG_INF)


def _na_attention(proj, bias_table, b):
    s = proj.shape[1] // b
    tile = NA_Q_TILE * NA_TILES_PER_STEP
    assert s % tile == 0 and NA_Q_TILE == GRID_W
    n_blocks = s // tile
    return pl.pallas_call(
        _na_kernel,
        grid=(b, N_SLABS, n_blocks),
        in_specs=[
            pl.BlockSpec((1, tile, SLAB), lambda bi, g, j: (g, bi * n_blocks + j, 0)),
            pl.BlockSpec((1, s, SLAB), lambda bi, g, j: (N_SLABS + g, bi, 0)),
            pl.BlockSpec((1, s, SLAB), lambda bi, g, j: (2 * N_SLABS + g, bi, 0)),
            pl.BlockSpec((HEADS_PER_STEP,) + bias_table.shape[1:], lambda bi, g, j: (g, 0, 0, 0)),
        ],
        out_specs=pl.BlockSpec((1, tile, SLAB), lambda bi, g, j: (g, bi * n_blocks + j, 0)),
        out_shape=jax.ShapeDtypeStruct((N_SLABS, b * s, SLAB), jnp.bfloat16),
        compiler_params=pltpu.CompilerParams(
            dimension_semantics=("parallel", "parallel", "arbitrary"),
            vmem_limit_bytes=VMEM_LIMIT),
        name="na_attention",
    )(proj, proj, proj, bias_table)


def _alibi_slopes():
    return np.asarray(2.0 ** (-8.0 * (np.arange(N_HEADS) + 1) / N_HEADS), dtype=np.float32)


def _slab_attention(q4, k4, v4, bias):
    lane = lax.broadcasted_iota(jnp.int32, q4.shape, 1) // HEAD_DIM
    zero = jnp.zeros_like(q4)
    q_stack = jnp.concatenate(
        [jnp.where(lane == h, q4, zero) for h in range(HEADS_PER_STEP)], axis=0)
    s = lax.dot_general(q_stack, k4, _NT_DIMS, preferred_element_type=jnp.float32)
    s = s + bias
    m = jnp.max(s, axis=-1, keepdims=True)
    p = jnp.exp2(s - m)
    l = jnp.sum(p, axis=-1, keepdims=True)
    p = p.astype(jnp.bfloat16)
    o_all = jnp.dot(p, v4, preferred_element_type=jnp.float32)
    out = o_all[:Q_TILE]
    for h in range(1, HEADS_PER_STEP):
        out = jnp.where(lane == h, o_all[h * Q_TILE:(h + 1) * Q_TILE], out)
    return out, m, l


def _dil_kernel(q_ref, ka_ref, kb_ref, kc_ref, va_ref, vb_ref, vc_ref, b_ref,
                o_ref, stat_ref):
    step = pl.program_id(2)
    n_steps = pl.num_programs(2)
    lane = lax.broadcasted_iota(jnp.int32, (Q_TILE, LSE_LANES), 1)
    tiles = q_ref.shape[2] // Q_TILE
    for r in range(q_ref.shape[1]):
        k_win = jnp.concatenate([ka_ref[0, r], kb_ref[0, r], kc_ref[0, r]], axis=0)
        v_win = jnp.concatenate([va_ref[0, r], vb_ref[0, r], vc_ref[0, r]], axis=0)
        for u in range(tiles):
            pattern = 0
            if u == 0:
                pattern = jnp.where(step == 0, 1, pattern)
            if u == tiles - 1:
                pattern = jnp.where(step == n_steps - 1, 2, pattern)
            tile = slice(u * Q_TILE, (u + 1) * Q_TILE)
            win = slice(u * Q_TILE, u * Q_TILE + DIL_WIN)
            stat_tile = jnp.zeros((Q_TILE, LSE_LANES), jnp.float32)
            for hg in range(N_SLABS):
                cols = slice(hg * SLAB, (hg + 1) * SLAB)
                bias = b_ref[pattern, hg * HEADS_PER_STEP:(hg + 1) * HEADS_PER_STEP]
                bias = bias.reshape(HEADS_PER_STEP * Q_TILE, DIL_WIN)
                out, m, l = _slab_attention(q_ref[0, r, tile, cols], k_win[win, cols],
                                            v_win[win, cols], bias)
                o_ref[0, r, tile, cols] = out.astype(o_ref.dtype)
                for h in range(HEADS_PER_STEP):
                    rows = slice(h * Q_TILE, (h + 1) * Q_TILE)
                    head = hg * HEADS_PER_STEP + h
                    stat_tile = jnp.where(lane == head, m[rows], stat_tile)
                    stat_tile = jnp.where(lane == N_HEADS + head, l[rows], stat_tile)
            stat_ref[0, r, tile, :] = stat_tile


def _dil_bias_tables():
    qi = np.arange(Q_TILE)[:, None]
    kj = np.arange(DIL_WIN)[None, :]
    delta = kj - DIL_RADIUS - qi
    in_band = np.abs(delta) <= DIL_RADIUS
    valid = np.stack([in_band,
                      in_band & (kj >= DIL_RADIUS),
                      in_band & (kj < DIL_WIN - DIL_RADIUS)])
    dils = np.asarray([dil for _, dil in DIL_PAIRS])
    dist = (np.abs(delta)[None] * dils[:, None, None]).astype(np.float32)
    bias = -_alibi_slopes()[None, :, None, None] * dist[:, None]
    table = np.where(valid[None, :, None], bias[:, None] * np.float32(LOG2E), np.float32(NEG_INF))
    assert table.dtype == np.float32
    return jnp.asarray(table)


def _dil_attention(qkv, bias_tables, group):
    b, dil, l, _ = qkv.shape
    tiles = min(DIL_TILES_PER_STEP, l // Q_TILE)
    seqs = min(DIL_TILES_PER_STEP // tiles, dil)
    tile = Q_TILE * tiles
    n_steps = l // tile
    assert l % tile == 0 and tiles >= 2 and dil % seqs == 0
    half = DIL_RADIUS
    per_tile = tile // half
    n_half = l // half

    def spec_mid(which):
        return pl.BlockSpec((1, seqs, tile, D_MODEL), lambda bi, r, i: (bi, r, i, which))

    def spec_lo(which):
        return pl.BlockSpec((1, seqs, half, D_MODEL),
                            lambda bi, r, i: (bi, r, jnp.maximum(per_tile * i - 1, 0), which))

    def spec_hi(which):
        return pl.BlockSpec((1, seqs, half, D_MODEL),
                            lambda bi, r, i: (bi, r, jnp.minimum(per_tile * (i + 1), n_half - 1), which))

    return pl.pallas_call(
        _dil_kernel,
        grid=(b, dil // seqs, n_steps),
        in_specs=[
            spec_mid(0),
            spec_lo(1), spec_mid(1), spec_hi(1),
            spec_lo(2), spec_mid(2), spec_hi(2),
            pl.BlockSpec((None,) + bias_tables.shape[1:], lambda bi, r, i: (group, 0, 0, 0, 0)),
        ],
        out_specs=[
            pl.BlockSpec((1, seqs, tile, D_MODEL), lambda bi, r, i: (bi, r, i, 0)),
            pl.BlockSpec((1, seqs, tile, LSE_LANES), lambda bi, r, i: (bi, r, i, 0)),
        ],
        out_shape=[
            jax.ShapeDtypeStruct((b, dil, l, D_MODEL), jnp.bfloat16),
            jax.ShapeDtypeStruct((b, dil, l, LSE_LANES), jnp.float32),
        ],
        compiler_params=pltpu.CompilerParams(
            dimension_semantics=("parallel", "parallel", "arbitrary"),
            vmem_limit_bytes=VMEM_LIMIT),
        name=f"dilated_attention_{dil}",
    )(qkv, qkv, qkv, qkv, qkv, qkv, qkv, bias_tables)


def _silu(x):
    half = 0.5 * x
    return half + half * jnp.tanh(half)


def _out0_kernel(o_ref, gate_ref, x_ref, w_ref, g_ref, y_ref, *rest):
    h_refs, slab_refs = rest[:-2], rest[-2:]
    o = jnp.concatenate([o_ref[k] for k in range(N_SLABS)], axis=1).astype(jnp.float32)
    gate = jnp.concatenate([gate_ref[k] for k in range(N_SLABS)], axis=1).astype(jnp.float32)
    z = (o * _silu(gate)).astype(jnp.bfloat16)
    y = x_ref[0] + jnp.dot(z, w_ref[...], preferred_element_type=jnp.float32)
    y_ref[0] = y
    hn = _rmsnorm(y, g_ref[...])
    tm = hn.shape[0]
    dils = [dil for _, dil in DIL_PAIRS]
    for s in range(N_LANE_SLABS):
        slab_refs[0][s] = hn[:, s * LANES:(s + 1) * LANES]
    staged = {1: slab_refs[0]}
    for k, (h_ref, dil) in enumerate(zip(h_refs, dils)):
        if dil == 1:
            h_ref[0, 0] = hn.astype(h_ref.dtype)
            continue
        p = max(q for q in staged if dil % q == 0)
        f, n = dil // p, tm // dil
        keep = any(later % dil == 0 for later in dils[k + 1:])
        for r in range(dil):
            start = (r % p) * (tm // p) + r // p
            for s in range(N_LANE_SLABS):
                rows = staged[p][s, pl.ds(start, n, stride=f), :]
                h_ref[0, r, :, s * LANES:(s + 1) * LANES] = rows.astype(h_ref.dtype)
                if keep:
                    slab_refs[1][s, r * n:(r + 1) * n, :] = rows
        if keep:
            staged[dil] = slab_refs[1]


def _out_proj0(o, proj, x, w_out, g_next, *, tm=1024):
    b, s, d = x.shape
    gate_block = proj.shape[0] // N_SLABS - 1
    tiles = s // tm
    tok = pl.BlockSpec((1, tm, d), lambda bi, i: (bi, i, 0))
    dils = [dil for _, dil in DIL_PAIRS]
    return pl.pallas_call(
        _out0_kernel,
        grid=(b, s // tm),
        in_specs=[
            pl.BlockSpec((N_SLABS, tm, SLAB), lambda bi, i: (0, bi * tiles + i, 0)),
            pl.BlockSpec((N_SLABS, tm, SLAB), lambda bi, i: (gate_block, bi * tiles + i, 0)),
            tok,
            pl.BlockSpec((d, d), lambda bi, i: (0, 0)),
            pl.BlockSpec((1, d), lambda bi, i: (0, 0)),
        ],
        out_specs=[tok] + [pl.BlockSpec((1, dil, tm // dil, d), lambda bi, i: (bi, 0, i, 0))
                           for dil in dils],
        out_shape=([jax.ShapeDtypeStruct((b, s, d), jnp.float32)]
                   + [jax.ShapeDtypeStruct((b, dil, s // dil, d), jnp.bfloat16) for dil in dils]),
        scratch_shapes=[pltpu.VMEM((N_LANE_SLABS, tm, LANES), jnp.float32)] * 2,
        compiler_params=pltpu.CompilerParams(
            dimension_semantics=("parallel", "parallel"), vmem_limit_bytes=VMEM_LIMIT),
        name="out_proj0",
    )(o, proj, x, w_out, g_next.reshape(1, d))


_MAX_ROW_STRIDE = 4


def _natural_order(ref, slab_refs, dil):
    _, n, c = ref.shape
    if dil == 1:
        return ref[0].astype(jnp.float32)
    out_ref, mid_ref = slab_refs
    tm = dil * n
    two_level = dil > _MAX_ROW_STRIDE
    if two_level:
        f = _MAX_ROW_STRIDE
        p = dil // f
        assert p <= _MAX_ROW_STRIDE
    pieces = []
    for s in range(c // LANES):
        lanes = slice(s * LANES, (s + 1) * LANES)
        if not two_level:
            for r in range(dil):
                out_ref[s, pl.ds(r, n, stride=dil), :] = ref[r, :, lanes].astype(jnp.float32)
        else:
            for r in range(dil):
                start = (r % p) * (tm // p) + r // p
                mid_ref[s, pl.ds(start, n, stride=f), :] = ref[r, :, lanes].astype(jnp.float32)
            for b_ in range(p):
                out_ref[s, pl.ds(b_, tm // p, stride=p), :] = (
                    mid_ref[s, b_ * (tm // p):(b_ + 1) * (tm // p), :])
        pieces.append(out_ref[s])
    return pieces[0] if len(pieces) == 1 else jnp.concatenate(pieces, axis=1)


def _out1_kernel(o0_ref, o1_ref, o2_ref, l0_ref, l1_ref, l2_ref, gate_ref, x_ref,
                 w_ref, e_ref, g_ref, y_ref, *slab_refs):
    dils = [dil for _, dil in DIL_PAIRS]
    stats = [_natural_order(ref.at[0], slab_refs, dil)
             for ref, dil in zip((l0_ref, l1_ref, l2_ref), dils)]
    sums = [pltpu.roll(st, LSE_LANES - N_HEADS, axis=1) for st in stats]
    m = jnp.maximum(jnp.maximum(stats[0], stats[1]), stats[2])
    es = [jnp.exp2(st - m) for st in stats]
    denom = es[0] * sums[0] + es[1] * sums[1] + es[2] * sums[2]
    head_lane = lax.broadcasted_iota(jnp.int32, denom.shape, 1) < N_HEADS
    o = None
    for e, o_ref, dil in zip(es, (o0_ref, o1_ref, o2_ref), dils):
        w = jnp.where(head_lane, e / denom, 0.0)
        hi = w.astype(jnp.bfloat16)
        lo = (w - hi.astype(jnp.float32)).astype(jnp.bfloat16)
        w_full = jnp.dot(jnp.concatenate([hi, lo], axis=1), e_ref[...],
                         preferred_element_type=jnp.float32)
        term = w_full * _natural_order(o_ref.at[0], slab_refs, dil)
        o = term if o is None else o + term
    gate = gate_ref[0].astype(jnp.float32)
    z = (o * _silu(gate)).astype(jnp.bfloat16)
    x = x_ref[0] + jnp.dot(z, w_ref[...], preferred_element_type=jnp.float32)
    y_ref[0] = _rmsnorm(x, g_ref[...])


def _head_expansion():
    e = np.zeros((LSE_LANES, D_MODEL), np.float32)
    for h in range(N_HEADS):
        e[h, h * HEAD_DIM:(h + 1) * HEAD_DIM] = 1.0
    return jnp.asarray(np.concatenate([e, e], axis=0), dtype=jnp.bfloat16)


def _out_proj1(os_, lses, proj, x, w_out, norm_f, *, tm=1024):
    b, s, d = x.shape
    gate_block = proj.shape[2] // d - 1
    dils = [dil for _, dil in DIL_PAIRS]

    def grouped(width, dil):
        return pl.BlockSpec((1, dil, tm // dil, width), lambda bi, i: (bi, 0, i, 0))

    tok = pl.BlockSpec((1, tm, d), lambda bi, i: (bi, i, 0))
    return pl.pallas_call(
        _out1_kernel,
        grid=(b, s // tm),
        in_specs=(
            [grouped(d, dil) for dil in dils]
            + [grouped(LSE_LANES, dil) for dil in dils]
            + [pl.BlockSpec((1, tm, d), lambda bi, i: (bi, i, gate_block)),
               tok,
               pl.BlockSpec((d, d), lambda bi, i: (0, 0)),
               pl.BlockSpec((2 * LSE_LANES, d), lambda bi, i: (0, 0)),
               pl.BlockSpec((1, d), lambda bi, i: (0, 0))]),
        out_specs=tok,
        out_shape=jax.ShapeDtypeStruct((b, s, d), jnp.float32),
        scratch_shapes=[pltpu.VMEM((N_LANE_SLABS, tm, LANES), jnp.float32)] * 2,
        compiler_params=pltpu.CompilerParams(
            dimension_semantics=("parallel", "parallel"), vmem_limit_bytes=VMEM_LIMIT),
        name="out_proj1",
    )(*os_, *lses, proj, x, w_out, _head_expansion(), norm_f.reshape(1, d))


def kernel(x, norm_0, w_in_0, rpb_0, w_out_0, norm_1, w_in_1, w_out_1, norm_f):
    b, s, d = x.shape
    t = b * s

    proj0 = _norm_proj(x.reshape(t, d), norm_0, w_in_0)
    o0 = _na_attention(proj0, _na_bias_table(rpb_0), b)
    x1, *h1 = _out_proj0(o0, proj0, x, w_out_0.astype(jnp.bfloat16), norm_1)

    gate_block = 3 * N_DIL_GROUPS
    alibi = _dil_bias_tables()
    outs, lses, proj_gate = [], [], None
    for g, (window, dil) in enumerate(DIL_PAIRS):
        assert window // (2 * dil) == DIL_RADIUS
        blocks = (3 * g, 3 * g + 1, 3 * g + 2) + ((gate_block,) if dil == 1 else ())
        qkv = _proj(h1[g].reshape(t, d), w_in_1, blocks).reshape(b, dil, s // dil, -1)
        if dil == 1:
            proj_gate = qkv.reshape(b, s, -1)
        o_g, lse_g = _dil_attention(qkv, alibi, g)
        outs.append(o_g)
        lses.append(lse_g)
    return _out_proj1(outs, lses, proj_gate, x1, w_out_1.astype(jnp.bfloat16), norm_f)
```

```python
import math

import numpy as np
import jax
import jax.numpy as jnp
from jax import lax
from jax.experimental import pallas as pl
from jax.experimental.pallas import tpu as pltpu

D_MODEL = 1024
HEAD_DIM = 64
N_HEADS = 16
GRID_W = 64
NA_ROWS = 8
NA_COLS = 16
DIL_PAIRS = ((128, 1), (512, 4), (2048, 16))
N_DIL_GROUPS = len(DIL_PAIRS)
RMS_EPS = 1e-6
NEG_INF = -1e30
LOG2E = math.log2(math.e)

LANES = 128
N_LANE_SLABS = D_MODEL // LANES
HEADS_PER_STEP = 4
SLAB = HEADS_PER_STEP * HEAD_DIM
N_SLABS = D_MODEL // SLAB
Q_TILE = 128
NA_Q_TILE = GRID_W
NA_WIN = NA_ROWS * GRID_W
NA_TILES_PER_STEP = 32
DIL_TILES_PER_STEP = 8
DIL_RADIUS = 64
DIL_WIN = Q_TILE + 2 * DIL_RADIUS
LSE_LANES = LANES
VMEM_LIMIT = 56 * 1024 * 1024

_NT_DIMS = (((1,), (1,)), ((), ()))


def _rmsnorm(x, g):
    ms = jnp.mean(x * x, axis=-1, keepdims=True)
    return x * lax.rsqrt(ms + RMS_EPS) * g


def _q_scaled_bf16(w_ref):
    q_steps = D_MODEL // w_ref.shape[1]
    scale = jnp.where(pl.program_id(1) < q_steps, LOG2E / math.sqrt(HEAD_DIM), 1.0)
    return (w_ref[...] * scale).astype(jnp.bfloat16)


def _norm_proj_kernel(x_ref, g_ref, w_ref, o_ref, h_ref):
    i, j = pl.program_id(0), pl.program_id(1)
    rows = x_ref.shape[0]

    def normalise_chunk():
        start = pl.multiple_of(j * rows, rows)
        h_ref[i % 2, pl.ds(start, rows), :] = _rmsnorm(x_ref[...], g_ref[...]).astype(h_ref.dtype)

    @pl.when(i == 0)
    def _():
        normalise_chunk()

    @pl.when(i > 0)
    def _():
        res = jnp.dot(h_ref[(i - 1) % 2], _q_scaled_bf16(w_ref),
                      preferred_element_type=jnp.float32).astype(o_ref.dtype)
        for k in range(o_ref.shape[0]):
            o_ref[k] = res[:, k * SLAB:(k + 1) * SLAB]
        normalise_chunk()


def _norm_proj(x, g, w, *, tm=2048, tn=1024):
    t, d = x.shape
    n = w.shape[1]
    n_i, n_j = t // tm, n // tn
    chunk = tm // n_j
    n_chunks = t // chunk
    return pl.pallas_call(
        _norm_proj_kernel,
        grid=(n_i + 1, n_j),
        in_specs=[
            pl.BlockSpec((chunk, d), lambda i, j: (jnp.minimum(i * n_j + j, n_chunks - 1), 0)),
            pl.BlockSpec((1, d), lambda i, j: (0, 0)),
            pl.BlockSpec((d, tn), lambda i, j: (0, j)),
        ],
        out_specs=pl.BlockSpec((tn // SLAB, tm, SLAB),
                               lambda i, j: (jnp.where(i == 0, 0, j), jnp.maximum(i - 1, 0), 0)),
        out_shape=jax.ShapeDtypeStruct((n // SLAB, t, SLAB), jnp.bfloat16),
        scratch_shapes=[pltpu.VMEM((2, tm, d), jnp.bfloat16)],
        compiler_params=pltpu.CompilerParams(
            dimension_semantics=("arbitrary", "arbitrary"),
            vmem_limit_bytes=VMEM_LIMIT),
        name="norm_proj",
    )(x, g.reshape(1, d), w)


def _proj_kernel(h_ref, w_ref, o_ref):
    o_ref[...] = jnp.dot(h_ref[...], _q_scaled_bf16(w_ref),
                         preferred_element_type=jnp.float32).astype(o_ref.dtype)


def _proj(h, w, col_blocks, *, tm=2048, tn=1024):
    t, d = h.shape
    per_block = d // tn
    n = len(col_blocks) * d

    def w_block(j):
        k_of_j = j // per_block
        blk = col_blocks[-1]
        for k in range(len(col_blocks) - 2, -1, -1):
            blk = jnp.where(k_of_j == k, col_blocks[k], blk)
        return blk * per_block + j % per_block

    return pl.pallas_call(
        _proj_kernel,
        grid=(t // tm, n // tn),
        in_specs=[
            pl.BlockSpec((tm, d), lambda i, j: (i, 0)),
            pl.BlockSpec((d, tn), lambda i, j: (0, w_block(j))),
        ],
        out_specs=pl.BlockSpec((tm, tn), lambda i, j: (i, j)),
        out_shape=jax.ShapeDtypeStruct((t, n), jnp.bfloat16),
        compiler_params=pltpu.CompilerParams(
            dimension_semantics=("parallel", "parallel"),
            vmem_limit_bytes=VMEM_LIMIT),
        name="proj",
    )(h, w)


def _na_kernel(q_ref, k_ref, v_ref, b_ref, o_ref):
    step = pl.program_id(2)
    n_tiles = k_ref.shape[1] // GRID_W
    lane = lax.broadcasted_iota(jnp.int32, (NA_Q_TILE, SLAB), 1) // HEAD_DIM

    for u in range(NA_TILES_PER_STEP):
        r = step * NA_TILES_PER_STEP + u
        first_row = jnp.clip(r - NA_ROWS // 2, 0, n_tiles - NA_ROWS)
        start = pl.multiple_of(first_row * GRID_W, GRID_W)
        tile_rows = slice(u * NA_Q_TILE, (u + 1) * NA_Q_TILE)
        q4 = q_ref[0, tile_rows, :]
        zero = jnp.zeros_like(q4)
        q_stack = jnp.concatenate(
            [jnp.where(lane == h, q4, zero) for h in range(HEADS_PER_STEP)], axis=0)
        s = lax.dot_general(q_stack, k_ref[0, pl.ds(start, NA_WIN), :], _NT_DIMS,
                            preferred_element_type=jnp.float32)
        first_off = first_row - r + NA_ROWS - 1
        s = s + jnp.concatenate(
            [jnp.concatenate([b_ref[h, first_off + 2 * j] for j in range(NA_ROWS // 2)], axis=1)
             for h in range(HEADS_PER_STEP)], axis=0)
        m = jnp.max(s, axis=-1, keepdims=True)
        e = jnp.exp2(s - m)
        inv_l = 1.0 / jnp.sum(e, axis=-1, keepdims=True)
        o_all = jnp.dot(e.astype(jnp.bfloat16), v_ref[0, pl.ds(start, NA_WIN), :],
                        preferred_element_type=jnp.float32) * inv_l
        out = o_all[:NA_Q_TILE]
        for h in range(1, HEADS_PER_STEP):
            out = jnp.where(lane == h, o_all[h * NA_Q_TILE:(h + 1) * NA_Q_TILE], out)
        o_ref[0, tile_rows, :] = out.astype(o_ref.dtype)


def _na_bias_table(rpb):
    n_col_off = 2 * NA_COLS - 1
    c = np.arange(GRID_W)[:, None]
    kc = np.arange(GRID_W)[None, :]
    cs = np.clip(c - NA_COLS // 2, 0, GRID_W - NA_COLS)
    col_ok = (kc >= cs) & (kc < cs + NA_COLS)
    col_off = kc - c + NA_COLS - 1
    assert np.all(((col_off >= 0) & (col_off < n_col_off))[col_ok])
    pick = (np.arange(n_col_off)[:, None, None] == col_off[None]) & col_ok[None]
    pick2 = np.zeros((2, n_col_off, GRID_W, 2, GRID_W), np.float32)
    for half in range(2):
        pick2[half, :, :, half, :] = pick
    pick2 = pick2.reshape(2 * n_col_off, GRID_W, 2 * GRID_W)
    rpb = rpb.astype(jnp.float32) * LOG2E
    row_pairs = jnp.concatenate([rpb[:, :-1], rpb[:, 1:]], axis=-1)
    table = jnp.einsum('hrm,mck->hrck', row_pairs, jnp.asarray(pick2),
                       precision=lax.Precision.HIGHEST)
    ok2 = np.concatenate([col_ok, col_ok], axis=1)
    return table + jnp.asarray(np.where(ok2, 0.0, NEG_INF).astype(np.float32))


def _na_attention(proj, bias_table, b):
    s = proj.shape[1] // b
    tile = NA_Q_TILE * NA_TILES_PER_STEP
    assert s % tile == 0 and NA_Q_TILE == GRID_W
    n_blocks = s // tile
    return pl.pallas_call(
        _na_kernel,
        grid=(b, N_SLABS, n_blocks),
        in_specs=[
            pl.BlockSpec((1, tile, SLAB), lambda bi, g, j: (g, bi * n_blocks + j, 0)),
            pl.BlockSpec((1, s, SLAB), lambda bi, g, j: (N_SLABS + g, bi, 0)),
            pl.BlockSpec((1, s, SLAB), lambda bi, g, j: (2 * N_SLABS + g, bi, 0)),
            pl.BlockSpec((HEADS_PER_STEP,) + bias_table.shape[1:], lambda bi, g, j: (g, 0, 0, 0)),
        ],
        out_specs=pl.BlockSpec((1, tile, SLAB), lambda bi, g, j: (g, bi * n_blocks + j, 0)),
        out_shape=jax.ShapeDtypeStruct((N_SLABS, b * s, SLAB), jnp.bfloat16),
        compiler_params=pltpu.CompilerParams(
            dimension_semantics=("parallel", "parallel", "arbitrary"),
            vmem_limit_bytes=VMEM_LIMIT),
        name="na_attention",
    )(proj, proj, proj, bias_table)


def _alibi_slopes():
    return np.asarray(2.0 ** (-8.0 * (np.arange(N_HEADS) + 1) / N_HEADS), dtype=np.float32)


def _slab_attention(q4, k4, v4, bias):
    lane = lax.broadcasted_iota(jnp.int32, q4.shape, 1) // HEAD_DIM
    zero = jnp.zeros_like(q4)
    q_stack = jnp.concatenate(
        [jnp.where(lane == h, q4, zero) for h in range(HEADS_PER_STEP)], axis=0)
    s = lax.dot_general(q_stack, k4, _NT_DIMS, preferred_element_type=jnp.float32)
    s = s + bias
    m = jnp.max(s, axis=-1, keepdims=True)
    p = jnp.exp2(s - m)
    l = jnp.sum(p, axis=-1, keepdims=True)
    p = p.astype(jnp.bfloat16)
    o_all = jnp.dot(p, v4, preferred_element_type=jnp.float32)
    out = o_all[:Q_TILE]
    for h in range(1, HEADS_PER_STEP):
        out = jnp.where(lane == h, o_all[h * Q_TILE:(h + 1) * Q_TILE], out)
    return out, m, l


def _dil_kernel(q_ref, ka_ref, kb_ref, kc_ref, va_ref, vb_ref, vc_ref, b_ref,
                o_ref, stat_ref):
    step = pl.program_id(2)
    n_steps = pl.num_programs(2)
    lane = lax.broadcasted_iota(jnp.int32, (Q_TILE, LSE_LANES), 1)
    tiles = q_ref.shape[2] // Q_TILE
    for r in range(q_ref.shape[1]):
        k_win = jnp.concatenate([ka_ref[0, r], kb_ref[0, r], kc_ref[0, r]], axis=0)
        v_win = jnp.concatenate([va_ref[0, r], vb_ref[0, r], vc_ref[0, r]], axis=0)
        for u in range(tiles):
            pattern = 0
            if u == 0:
                pattern = jnp.where(step == 0, 1, pattern)
            if u == tiles - 1:
                pattern = jnp.where(step == n_steps - 1, 2, pattern)
            tile = slice(u * Q_TILE, (u + 1) * Q_TILE)
            win = slice(u * Q_TILE, u * Q_TILE + DIL_WIN)
            stat_tile = jnp.zeros((Q_TILE, LSE_LANES), jnp.float32)
            for hg in range(N_SLABS):
                cols = slice(hg * SLAB, (hg + 1) * SLAB)
                bias = b_ref[pattern, hg * HEADS_PER_STEP:(hg + 1) * HEADS_PER_STEP]
                bias = bias.reshape(HEADS_PER_STEP * Q_TILE, DIL_WIN)
                out, m, l = _slab_attention(q_ref[0, r, tile, cols], k_win[win, cols],
                                            v_win[win, cols], bias)
                o_ref[0, r, tile, cols] = out.astype(o_ref.dtype)
                for h in range(HEADS_PER_STEP):
                    rows = slice(h * Q_TILE, (h + 1) * Q_TILE)
                    head = hg * HEADS_PER_STEP + h
                    stat_tile = jnp.where(lane == head, m[rows], stat_tile)
                    stat_tile = jnp.where(lane == N_HEADS + head, l[rows], stat_tile)
            stat_ref[0, r, tile, :] = stat_tile


def _dil_bias_tables():
    qi = np.arange(Q_TILE)[:, None]
    kj = np.arange(DIL_WIN)[None, :]
    delta = kj - DIL_RADIUS - qi
    in_band = np.abs(delta) <= DIL_RADIUS
    valid = np.stack([in_band,
                      in_band & (kj >= DIL_RADIUS),
                      in_band & (kj < DIL_WIN - DIL_RADIUS)])
    dils = np.asarray([dil for _, dil in DIL_PAIRS])
    dist = (np.abs(delta)[None] * dils[:, None, None]).astype(np.float32)
    bias = -_alibi_slopes()[None, :, None, None] * dist[:, None]
    table = np.where(valid[None, :, None], bias[:, None] * np.float32(LOG2E), np.float32(NEG_INF))
    assert table.dtype == np.float32
    return jnp.asarray(table)


def _dil_attention(qkv, bias_tables, group):
    b, dil, l, _ = qkv.shape
    tiles = min(DIL_TILES_PER_STEP, l // Q_TILE)
    seqs = min(DIL_TILES_PER_STEP // tiles, dil)
    tile = Q_TILE * tiles
    n_steps = l // tile
    assert l % tile == 0 and tiles >= 2 and dil % seqs == 0
    half = DIL_RADIUS
    per_tile = tile // half
    n_half = l // half

    def spec_mid(which):
        return pl.BlockSpec((1, seqs, tile, D_MODEL), lambda bi, r, i: (bi, r, i, which))

    def spec_lo(which):
        return pl.BlockSpec((1, seqs, half, D_MODEL),
                            lambda bi, r, i: (bi, r, jnp.maximum(per_tile * i - 1, 0), which))

    def spec_hi(which):
        return pl.BlockSpec((1, seqs, half, D_MODEL),
                            lambda bi, r, i: (bi, r, jnp.minimum(per_tile * (i + 1), n_half - 1), which))

    return pl.pallas_call(
        _dil_kernel,
        grid=(b, dil // seqs, n_steps),
        in_specs=[
            spec_mid(0),
            spec_lo(1), spec_mid(1), spec_hi(1),
            spec_lo(2), spec_mid(2), spec_hi(2),
            pl.BlockSpec((None,) + bias_tables.shape[1:], lambda bi, r, i: (group, 0, 0, 0, 0)),
        ],
        out_specs=[
            pl.BlockSpec((1, seqs, tile, D_MODEL), lambda bi, r, i: (bi, r, i, 0)),
            pl.BlockSpec((1, seqs, tile, LSE_LANES), lambda bi, r, i: (bi, r, i, 0)),
        ],
        out_shape=[
            jax.ShapeDtypeStruct((b, dil, l, D_MODEL), jnp.bfloat16),
            jax.ShapeDtypeStruct((b, dil, l, LSE_LANES), jnp.float32),
        ],
        compiler_params=pltpu.CompilerParams(
            dimension_semantics=("parallel", "parallel", "arbitrary"),
            vmem_limit_bytes=VMEM_LIMIT),
        name=f"dilated_attention_{dil}",
    )(qkv, qkv, qkv, qkv, qkv, qkv, qkv, bias_tables)


def _silu(x):
    half = 0.5 * x
    return half + half * jnp.tanh(half)


def _out0_kernel(o_ref, gate_ref, x_ref, w_ref, g_ref, y_ref, *rest):
    h_refs, slab_refs = rest[:-2], rest[-2:]
    o = jnp.concatenate([o_ref[k] for k in range(N_SLABS)], axis=1).astype(jnp.float32)
    gate = jnp.concatenate([gate_ref[k] for k in range(N_SLABS)], axis=1).astype(jnp.float32)
    z = (o * _silu(gate)).astype(jnp.bfloat16)
    y = x_ref[0] + jnp.dot(z, w_ref[...], preferred_element_type=jnp.float32)
    y_ref[0] = y
    hn = _rmsnorm(y, g_ref[...])
    tm = hn.shape[0]
    dils = [dil for _, dil in DIL_PAIRS]
    for s in range(N_LANE_SLABS):
        slab_refs[0][s] = hn[:, s * LANES:(s + 1) * LANES]
    staged = {1: slab_refs[0]}
    for k, (h_ref, dil) in enumerate(zip(h_refs, dils)):
        if dil == 1:
            h_ref[0, 0] = hn.astype(h_ref.dtype)
            continue
        p = max(q for q in staged if dil % q == 0)
        f, n = dil // p, tm // dil
        keep = any(later % dil == 0 for later in dils[k + 1:])
        for r in range(dil):
            start = (r % p) * (tm // p) + r // p
            for s in range(N_LANE_SLABS):
                rows = staged[p][s, pl.ds(start, n, stride=f), :]
                h_ref[0, r, :, s * LANES:(s + 1) * LANES] = rows.astype(h_ref.dtype)
                if keep:
                    slab_refs[1][s, r * n:(r + 1) * n, :] = rows
        if keep:
            staged[dil] = slab_refs[1]


def _out_proj0(o, proj, x, w_out, g_next, *, tm=1024):
    b, s, d = x.shape
    gate_block = proj.shape[0] // N_SLABS - 1
    tiles = s // tm
    tok = pl.BlockSpec((1, tm, d), lambda bi, i: (bi, i, 0))
    dils = [dil for _, dil in DIL_PAIRS]
    return pl.pallas_call(
        _out0_kernel,
        grid=(b, s // tm),
        in_specs=[
            pl.BlockSpec((N_SLABS, tm, SLAB), lambda bi, i: (0, bi * tiles + i, 0)),
            pl.BlockSpec((N_SLABS, tm, SLAB), lambda bi, i: (gate_block, bi * tiles + i, 0)),
            tok,
            pl.BlockSpec((d, d), lambda bi, i: (0, 0)),
            pl.BlockSpec((1, d), lambda bi, i: (0, 0)),
        ],
        out_specs=[tok] + [pl.BlockSpec((1, dil, tm // dil, d), lambda bi, i: (bi, 0, i, 0))
                           for dil in dils],
        out_shape=([jax.ShapeDtypeStruct((b, s, d), jnp.float32)]
                   + [jax.ShapeDtypeStruct((b, dil, s // dil, d), jnp.bfloat16) for dil in dils]),
        scratch_shapes=[pltpu.VMEM((N_LANE_SLABS, tm, LANES), jnp.float32)] * 2,
        compiler_params=pltpu.CompilerParams(
            dimension_semantics=("parallel", "parallel"), vmem_limit_bytes=VMEM_LIMIT),
        name="out_proj0",
    )(o, proj, x, w_out, g_next.reshape(1, d))


_MAX_ROW_STRIDE = 4


def _natural_order(ref, slab_refs, dil):
    _, n, c = ref.shape
    if dil == 1:
        return ref[0].astype(jnp.float32)
    out_ref, mid_ref = slab_refs
    tm = dil * n
    two_level = dil > _MAX_ROW_STRIDE
    if two_level:
        f = _MAX_ROW_STRIDE
        p = dil // f
        assert p <= _MAX_ROW_STRIDE
    pieces = []
    for s in range(c // LANES):
        lanes = slice(s * LANES, (s + 1) * LANES)
        if not two_level:
            for r in range(dil):
                out_ref[s, pl.ds(r, n, stride=dil), :] = ref[r, :, lanes].astype(jnp.float32)
        else:
            for r in range(dil):
                start = (r % p) * (tm // p) + r // p
                mid_ref[s, pl.ds(start, n, stride=f), :] = ref[r, :, lanes].astype(jnp.float32)
            for b_ in range(p):
                out_ref[s, pl.ds(b_, tm // p, stride=p), :] = (
                    mid_ref[s, b_ * (tm // p):(b_ + 1) * (tm // p), :])
        pieces.append(out_ref[s])
    return pieces[0] if len(pieces) == 1 else jnp.concatenate(pieces, axis=1)


def _out1_kernel(o0_ref, o1_ref, o2_ref, l0_ref, l1_ref, l2_ref, gate_ref, x_ref,
                 w_ref, e_ref, g_ref, y_ref, *slab_refs):
    dils = [dil for _, dil in DIL_PAIRS]
    stats = [_natural_order(ref.at[0], slab_refs, dil)
             for ref, dil in zip((l0_ref, l1_ref, l2_ref), dils)]
    sums = [pltpu.roll(st, LSE_LANES - N_HEADS, axis=1) for st in stats]
    m = jnp.maximum(jnp.maximum(stats[0], stats[1]), stats[2])
    es = [jnp.exp2(st - m) for st in stats]
    denom = es[0] * sums[0] + es[1] * sums[1] + es[2] * sums[2]
    head_lane = lax.broadcasted_iota(jnp.int32, denom.shape, 1) < N_HEADS
    o = None
    for e, o_ref, dil in zip(es, (o0_ref, o1_ref, o2_ref), dils):
        w = jnp.where(head_lane, e / denom, 0.0)
        hi = w.astype(jnp.bfloat16)
        lo = (w - hi.astype(jnp.float32)).astype(jnp.bfloat16)
        w_full = jnp.dot(jnp.concatenate([hi, lo], axis=1), e_ref[...],
                         preferred_element_type=jnp.float32)
        term = w_full * _natural_order(o_ref.at[0], slab_refs, dil)
        o = term if o is None else o + term
    gate = gate_ref[0].astype(jnp.float32)
    z = (o * _silu(gate)).astype(jnp.bfloat16)
    x = x_ref[0] + jnp.dot(z, w_ref[...], preferred_element_type=jnp.float32)
    y_ref[0] = _rmsnorm(x, g_ref[...])


def _head_expansion():
    e = np.zeros((LSE_LANES, D_MODEL), np.float32)
    for h in range(N_HEADS):
        e[h, h * HEAD_DIM:(h + 1) * HEAD_DIM] = 1.0
    return jnp.asarray(np.concatenate([e, e], axis=0), dtype=jnp.bfloat16)


def _out_proj1(os_, lses, proj, x, w_out, norm_f, *, tm=1024):
    b, s, d = x.shape
    gate_block = proj.shape[2] // d - 1
    dils = [dil for _, dil in DIL_PAIRS]

    def grouped(width, dil):
        return pl.BlockSpec((1, dil, tm // dil, width), lambda bi, i: (bi, 0, i, 0))

    tok = pl.BlockSpec((1, tm, d), lambda bi, i: (bi, i, 0))
    return pl.pallas_call(
        _out1_kernel,
        grid=(b, s // tm),
        in_specs=(
            [grouped(d, dil) for dil in dils]
            + [grouped(LSE_LANES, dil) for dil in dils]
            + [pl.BlockSpec((1, tm, d), lambda bi, i: (bi, i, gate_block)),
               tok,
               pl.BlockSpec((d, d), lambda bi, i: (0, 0)),
               pl.BlockSpec((2 * LSE_LANES, d), lambda bi, i: (0, 0)),
               pl.BlockSpec((1, d), lambda bi, i: (0, 0))]),
        out_specs=tok,
        out_shape=jax.ShapeDtypeStruct((b, s, d), jnp.float32),
        scratch_shapes=[pltpu.VMEM((N_LANE_SLABS, tm, LANES), jnp.float32)] * 2,
        compiler_params=pltpu.CompilerParams(
            dimension_semantics=("parallel", "parallel"), vmem_limit_bytes=VMEM_LIMIT),
        name="out_proj1",
    )(*os_, *lses, proj, x, w_out, _head_expansion(), norm_f.reshape(1, d))


def kernel(x, norm_0, w_in_0, rpb_0, w_out_0, norm_1, w_in_1, w_out_1, norm_f):
    b, s, d = x.shape
    t = b * s

    proj0 = _norm_proj(x.reshape(t, d), norm_0, w_in_0)
    o0 = _na_attention(proj0, _na_bias_table(rpb_0), b)
    x1, *h1 = _out_proj0(o0, proj0, x, w_out_0.astype(jnp.bfloat16), norm_1)

    gate_block = 3 * N_DIL_GROUPS
    alibi = _dil_bias_tables()
    outs, lses, proj_gate = [], [], None
    for g, (window, dil) in enumerate(DIL_PAIRS):
        assert window // (2 * dil) == DIL_RADIUS
        blocks = (3 * g, 3 * g + 1, 3 * g + 2) + ((gate_block,) if dil == 1 else ())
        qkv = _proj(h1[g].reshape(t, d), w_in_1, blocks).reshape(b, dil, s // dil, -1)
        if dil == 1:
            proj_gate = qkv.reshape(b, s, -1)
        o_g, lse_g = _dil_attention(qkv, alibi, g)
        outs.append(o_g)
        lses.append(lse_g)
    return _out_proj1(outs, lses, proj_gate, x1, w_out_1.astype(jnp.bfloat16), norm_f)
```

```python
import functools
import math

import numpy as np
import jax
import jax.numpy as jnp
from jax import lax
from jax.experimental import pallas as pl
from jax.experimental.pallas import tpu as pltpu

D_MODEL = 1024
HEAD_DIM = 64
N_HEADS = 16
GRID_W = 64
NA_ROWS = 8
NA_COLS = 16
DIL_PAIRS = ((128, 1), (512, 4), (2048, 16))
N_DIL_GROUPS = len(DIL_PAIRS)
RMS_EPS = 1e-6
NEG_INF = -1e30
LOG2E = math.log2(math.e)

LANES = 128
N_LANE_SLABS = D_MODEL // LANES
HEADS_PER_STEP = 4
SLAB = HEADS_PER_STEP * HEAD_DIM
N_SLABS = D_MODEL // SLAB
Q_TILE = 128
NA_Q_TILE = GRID_W
NA_WIN = NA_ROWS * GRID_W
NA_TILES_PER_STEP = 32
DIL_TILES_PER_STEP = 8
DIL_RADIUS = 64
DIL_WIN = Q_TILE + 2 * DIL_RADIUS
LSE_LANES = LANES
VMEM_LIMIT = 56 * 1024 * 1024

_NT_DIMS = (((1,), (1,)), ((), ()))


def _rmsnorm(x, g):
    ms = jnp.mean(x * x, axis=-1, keepdims=True)
    return x * lax.rsqrt(ms + RMS_EPS) * g


def _q_scaled_bf16(w_ref):
    q_steps = D_MODEL // w_ref.shape[1]
    scale = jnp.where(pl.program_id(1) < q_steps, LOG2E / math.sqrt(HEAD_DIM), 1.0)
    return (w_ref[...] * scale).astype(jnp.bfloat16)


def _norm_proj_kernel(x_ref, g_ref, w_ref, o_ref, h_ref):
    i, j = pl.program_id(0), pl.program_id(1)
    rows = x_ref.shape[0]

    def normalise_chunk():
        start = pl.multiple_of(j * rows, rows)
        h_ref[i % 2, pl.ds(start, rows), :] = _rmsnorm(x_ref[...], g_ref[...]).astype(h_ref.dtype)

    @pl.when(i == 0)
    def _():
        normalise_chunk()

    @pl.when(i > 0)
    def _():
        res = jnp.dot(h_ref[(i - 1) % 2], _q_scaled_bf16(w_ref),
                      preferred_element_type=jnp.float32).astype(o_ref.dtype)
        for k in range(o_ref.shape[0]):
            o_ref[k] = res[:, k * SLAB:(k + 1) * SLAB]
        normalise_chunk()


def _norm_proj(x, g, w, *, tm=2048, tn=1024):
    t, d = x.shape
    n = w.shape[1]
    n_i, n_j = t // tm, n // tn
    chunk = tm // n_j
    n_chunks = t // chunk
    return pl.pallas_call(
        _norm_proj_kernel,
        grid=(n_i + 1, n_j),
        in_specs=[
            pl.BlockSpec((chunk, d), lambda i, j: (jnp.minimum(i * n_j + j, n_chunks - 1), 0)),
            pl.BlockSpec((1, d), lambda i, j: (0, 0)),
            pl.BlockSpec((d, tn), lambda i, j: (0, j)),
        ],
        out_specs=pl.BlockSpec((tn // SLAB, tm, SLAB),
                               lambda i, j: (jnp.where(i == 0, 0, j), jnp.maximum(i - 1, 0), 0)),
        out_shape=jax.ShapeDtypeStruct((n // SLAB, t, SLAB), jnp.bfloat16),
        scratch_shapes=[pltpu.VMEM((2, tm, d), jnp.bfloat16)],
        compiler_params=pltpu.CompilerParams(
            dimension_semantics=("arbitrary", "arbitrary"),
            vmem_limit_bytes=VMEM_LIMIT),
        name="norm_proj",
    )(x, g.reshape(1, d), w)


def _group_proj_kernel(*refs, first_steps):
    h_refs, w_ref, o_ref = refs[:-2], refs[-2], refs[-1]
    c = pl.program_id(1)
    bounds = list(first_steps) + [pl.num_programs(1)]
    for g, h_ref in enumerate(h_refs):
        @pl.when((c >= bounds[g]) & (c < bounds[g + 1]))
        def _(h_ref=h_ref, g=g):
            scale = jnp.where(c == bounds[g], LOG2E / math.sqrt(HEAD_DIM), 1.0)
            w = (w_ref[...] * scale).astype(jnp.bfloat16)
            o_ref[...] = jnp.dot(h_ref[...], w,
                                 preferred_element_type=jnp.float32).astype(o_ref.dtype)


def _group_proj(hs, w, col_blocks, *, tm=2048):
    t, d = hs[0].shape
    order = [blk for blocks in col_blocks for blk in blocks]
    first_steps = tuple(int(v) for v in np.cumsum([0] + [len(blocks) for blocks in col_blocks[:-1]]))

    def w_block(c):
        blk = order[-1]
        for k in range(len(order) - 2, -1, -1):
            blk = jnp.where(c == k, order[k], blk)
        return blk

    return pl.pallas_call(
        functools.partial(_group_proj_kernel, first_steps=first_steps),
        grid=(t // tm, len(order)),
        in_specs=([pl.BlockSpec((tm, d), lambda i, c: (i, 0)) for _ in hs]
                  + [pl.BlockSpec((d, d), lambda i, c: (0, w_block(c)))]),
        out_specs=pl.BlockSpec((tm, d), lambda i, c: (i, c)),
        out_shape=jax.ShapeDtypeStruct((t, len(order) * d), jnp.bfloat16),
        compiler_params=pltpu.CompilerParams(
            dimension_semantics=("parallel", "arbitrary"),
            vmem_limit_bytes=VMEM_LIMIT),
        name="group_proj",
    )(*hs, w)


def _na_kernel(q_ref, k_ref, v_ref, b_ref, o_ref):
    step = pl.program_id(2)
    n_tiles = k_ref.shape[1] // GRID_W
    lane = lax.broadcasted_iota(jnp.int32, (NA_Q_TILE, SLAB), 1) // HEAD_DIM

    for u in range(NA_TILES_PER_STEP):
        r = step * NA_TILES_PER_STEP + u
        first_row = jnp.clip(r - NA_ROWS // 2, 0, n_tiles - NA_ROWS)
        start = pl.multiple_of(first_row * GRID_W, GRID_W)
        tile_rows = slice(u * NA_Q_TILE, (u + 1) * NA_Q_TILE)
        q4 = q_ref[0, tile_rows, :]
        zero = jnp.zeros_like(q4)
        q_stack = jnp.concatenate(
            [jnp.where(lane == h, q4, zero) for h in range(HEADS_PER_STEP)], axis=0)
        s = lax.dot_general(q_stack, k_ref[0, pl.ds(start, NA_WIN), :], _NT_DIMS,
                            preferred_element_type=jnp.float32)
        first_off = first_row - r + NA_ROWS - 1
        s = s + jnp.concatenate(
            [jnp.concatenate([b_ref[h, first_off + 2 * j] for j in range(NA_ROWS // 2)], axis=1)
             for h in range(HEADS_PER_STEP)], axis=0)
        m = jnp.max(s, axis=-1, keepdims=True)
        e = jnp.exp2(s - m)
        inv_l = 1.0 / jnp.sum(e, axis=-1, keepdims=True)
        o_all = jnp.dot(e.astype(jnp.bfloat16), v_ref[0, pl.ds(start, NA_WIN), :],
                        preferred_element_type=jnp.float32) * inv_l
        out = o_all[:NA_Q_TILE]
        for h in range(1, HEADS_PER_STEP):
            out = jnp.where(lane == h, o_all[h * NA_Q_TILE:(h + 1) * NA_Q_TILE], out)
        o_ref[0, tile_rows, :] = out.astype(o_ref.dtype)


def _na_bias_table(rpb):
    n_col_off = 2 * NA_COLS - 1
    c = np.arange(GRID_W)[:, None]
    kc = np.arange(GRID_W)[None, :]
    cs = np.clip(c - NA_COLS // 2, 0, GRID_W - NA_COLS)
    col_ok = (kc >= cs) & (kc < cs + NA_COLS)
    col_off = kc - c + NA_COLS - 1
    assert np.all(((col_off >= 0) & (col_off < n_col_off))[col_ok])
    pick = (np.arange(n_col_off)[:, None, None] == col_off[None]) & col_ok[None]
    pick2 = np.zeros((2, n_col_off, GRID_W, 2, GRID_W), np.float32)
    for half in range(2):
        pick2[half, :, :, half, :] = pick
    pick2 = pick2.reshape(2 * n_col_off, GRID_W, 2 * GRID_W)
    rpb = rpb.astype(jnp.float32) * LOG2E
    row_pairs = jnp.concatenate([rpb[:, :-1], rpb[:, 1:]], axis=-1)
    table = jnp.einsum('hrm,mck->hrck', row_pairs, jnp.asarray(pick2),
                       precision=lax.Precision.HIGHEST)
    ok2 = np.concatenate([col_ok, col_ok], axis=1)
    return table + jnp.asarray(np.where(ok2, 0.0, NEG_INF).astype(np.float32))


def _na_attention(proj, bias_table, b):
    s = proj.shape[1] // b
    tile = NA_Q_TILE * NA_TILES_PER_STEP
    assert s % tile == 0 and NA_Q_TILE == GRID_W
    n_blocks = s // tile
    return pl.pallas_call(
        _na_kernel,
        grid=(b, N_SLABS, n_blocks),
        in_specs=[
            pl.BlockSpec((1, tile, SLAB), lambda bi, g, j: (g, bi * n_blocks + j, 0)),
            pl.BlockSpec((1, s, SLAB), lambda bi, g, j: (N_SLABS + g, bi, 0)),
            pl.BlockSpec((1, s, SLAB), lambda bi, g, j: (2 * N_SLABS + g, bi, 0)),
            pl.BlockSpec((HEADS_PER_STEP,) + bias_table.shape[1:], lambda bi, g, j: (g, 0, 0, 0)),
        ],
        out_specs=pl.BlockSpec((1, tile, SLAB), lambda bi, g, j: (g, bi * n_blocks + j, 0)),
        out_shape=jax.ShapeDtypeStruct((N_SLABS, b * s, SLAB), jnp.bfloat16),
        compiler_params=pltpu.CompilerParams(
            dimension_semantics=("parallel", "parallel", "arbitrary"),
            vmem_limit_bytes=VMEM_LIMIT),
        name="na_attention",
    )(proj, proj, proj, bias_table)


def _alibi_slopes():
    return np.asarray(2.0 ** (-8.0 * (np.arange(N_HEADS) + 1) / N_HEADS), dtype=np.float32)


def _slab_attention(q4, k4, v4, bias):
    lane = lax.broadcasted_iota(jnp.int32, q4.shape, 1) // HEAD_DIM
    zero = jnp.zeros_like(q4)
    q_stack = jnp.concatenate(
        [jnp.where(lane == h, q4, zero) for h in range(HEADS_PER_STEP)], axis=0)
    s = lax.dot_general(q_stack, k4, _NT_DIMS, preferred_element_type=jnp.float32)
    s = s + bias
    m = jnp.max(s, axis=-1, keepdims=True)
    p = jnp.exp2(s - m)
    l = jnp.sum(p, axis=-1, keepdims=True)
    p = p.astype(jnp.bfloat16)
    o_all = jnp.dot(p, v4, preferred_element_type=jnp.float32)
    out = o_all[:Q_TILE]
    for h in range(1, HEADS_PER_STEP):
        out = jnp.where(lane == h, o_all[h * Q_TILE:(h + 1) * Q_TILE], out)
    return out, m, l


def _dil_kernel(q_ref, ka_ref, kb_ref, kc_ref, va_ref, vb_ref, vc_ref, b_ref,
                o_ref, stat_ref):
    step = pl.program_id(2)
    n_steps = pl.num_programs(2)
    lane = lax.broadcasted_iota(jnp.int32, (Q_TILE, LSE_LANES), 1)
    tiles = q_ref.shape[2] // Q_TILE
    for r in range(q_ref.shape[1]):
        k_win = jnp.concatenate([ka_ref[0, r], kb_ref[0, r], kc_ref[0, r]], axis=0)
        v_win = jnp.concatenate([va_ref[0, r], vb_ref[0, r], vc_ref[0, r]], axis=0)
        for u in range(tiles):
            pattern = 0
            if u == 0:
                pattern = jnp.where(step == 0, 1, pattern)
            if u == tiles - 1:
                pattern = jnp.where(step == n_steps - 1, 2, pattern)
            tile = slice(u * Q_TILE, (u + 1) * Q_TILE)
            win = slice(u * Q_TILE, u * Q_TILE + DIL_WIN)
            stat_tile = jnp.zeros((Q_TILE, LSE_LANES), jnp.float32)
            for hg in range(N_SLABS):
                cols = slice(hg * SLAB, (hg + 1) * SLAB)
                bias = b_ref[pattern, hg * HEADS_PER_STEP:(hg + 1) * HEADS_PER_STEP]
                bias = bias.reshape(HEADS_PER_STEP * Q_TILE, DIL_WIN)
                out, m, l = _slab_attention(q_ref[0, r, tile, cols], k_win[win, cols],
                                            v_win[win, cols], bias)
                o_ref[0, r, tile, cols] = out.astype(o_ref.dtype)
                for h in range(HEADS_PER_STEP):
                    rows = slice(h * Q_TILE, (h + 1) * Q_TILE)
                    head = hg * HEADS_PER_STEP + h
                    stat_tile = jnp.where(lane == head, m[rows], stat_tile)
                    stat_tile = jnp.where(lane == N_HEADS + head, l[rows], stat_tile)
            stat_ref[0, r, tile, :] = stat_tile


def _dil_bias_tables():
    qi = np.arange(Q_TILE)[:, None]
    kj = np.arange(DIL_WIN)[None, :]
    delta = kj - DIL_RADIUS - qi
    in_band = np.abs(delta) <= DIL_RADIUS
    valid = np.stack([in_band,
                      in_band & (kj >= DIL_RADIUS),
                      in_band & (kj < DIL_WIN - DIL_RADIUS)])
    dils = np.asarray([dil for _, dil in DIL_PAIRS])
    dist = (np.abs(delta)[None] * dils[:, None, None]).astype(np.float32)
    bias = -_alibi_slopes()[None, :, None, None] * dist[:, None]
    table = np.where(valid[None, :, None], bias[:, None] * np.float32(LOG2E), np.float32(NEG_INF))
    assert table.dtype == np.float32
    return jnp.asarray(table)


def _dil_attention(qkv, bias_tables, group, col_base):
    b, dil, l, _ = qkv.shape
    tiles = min(DIL_TILES_PER_STEP, l // Q_TILE)
    seqs = min(DIL_TILES_PER_STEP // tiles, dil)
    tile = Q_TILE * tiles
    n_steps = l // tile
    assert l % tile == 0 and tiles >= 2 and dil % seqs == 0
    half = DIL_RADIUS
    per_tile = tile // half
    n_half = l // half

    def spec_mid(which):
        return pl.BlockSpec((1, seqs, tile, D_MODEL),
                            lambda bi, r, i: (bi, r, i, col_base + which))

    def spec_lo(which):
        return pl.BlockSpec((1, seqs, half, D_MODEL),
                            lambda bi, r, i: (bi, r, jnp.maximum(per_tile * i - 1, 0),
                                              col_base + which))

    def spec_hi(which):
        return pl.BlockSpec((1, seqs, half, D_MODEL),
                            lambda bi, r, i: (bi, r, jnp.minimum(per_tile * (i + 1), n_half - 1),
                                              col_base + which))

    return pl.pallas_call(
        _dil_kernel,
        grid=(b, dil // seqs, n_steps),
        in_specs=[
            spec_mid(0),
            spec_lo(1), spec_mid(1), spec_hi(1),
            spec_lo(2), spec_mid(2), spec_hi(2),
            pl.BlockSpec((None,) + bias_tables.shape[1:], lambda bi, r, i: (group, 0, 0, 0, 0)),
        ],
        out_specs=[
            pl.BlockSpec((1, seqs, tile, D_MODEL), lambda bi, r, i: (bi, r, i, 0)),
            pl.BlockSpec((1, seqs, tile, LSE_LANES), lambda bi, r, i: (bi, r, i, 0)),
        ],
        out_shape=[
            jax.ShapeDtypeStruct((b, dil, l, D_MODEL), jnp.bfloat16),
            jax.ShapeDtypeStruct((b, dil, l, LSE_LANES), jnp.float32),
        ],
        compiler_params=pltpu.CompilerParams(
            dimension_semantics=("parallel", "parallel", "arbitrary"),
            vmem_limit_bytes=VMEM_LIMIT),
        name=f"dilated_attention_{dil}",
    )(qkv, qkv, qkv, qkv, qkv, qkv, qkv, bias_tables)


def _silu(x):
    half = 0.5 * x
    return half + half * jnp.tanh(half)


def _out0_kernel(o_ref, gate_ref, x_ref, w_ref, g_ref, y_ref, *rest):
    h_refs, slab_refs = rest[:-2], rest[-2:]
    o = jnp.concatenate([o_ref[k] for k in range(N_SLABS)], axis=1).astype(jnp.float32)
    gate = jnp.concatenate([gate_ref[k] for k in range(N_SLABS)], axis=1).astype(jnp.float32)
    z = (o * _silu(gate)).astype(jnp.bfloat16)
    y = x_ref[0] + jnp.dot(z, w_ref[...], preferred_element_type=jnp.float32)
    y_ref[0] = y
    hn = _rmsnorm(y, g_ref[...])
    tm = hn.shape[0]
    dils = [dil for _, dil in DIL_PAIRS]
    for s in range(N_LANE_SLABS):
        slab_refs[0][s] = hn[:, s * LANES:(s + 1) * LANES]
    staged = {1: slab_refs[0]}
    for k, (h_ref, dil) in enumerate(zip(h_refs, dils)):
        if dil == 1:
            h_ref[0, 0] = hn.astype(h_ref.dtype)
            continue
        p = max(q for q in staged if dil % q == 0)
        f, n = dil // p, tm // dil
        keep = any(later % dil == 0 for later in dils[k + 1:])
        for r in range(dil):
            start = (r % p) * (tm // p) + r // p
            for s in range(N_LANE_SLABS):
                rows = staged[p][s, pl.ds(start, n, stride=f), :]
                h_ref[0, r, :, s * LANES:(s + 1) * LANES] = rows.astype(h_ref.dtype)
                if keep:
                    slab_refs[1][s, r * n:(r + 1) * n, :] = rows
        if keep:
            staged[dil] = slab_refs[1]


def _out_proj0(o, proj, x, w_out, g_next, *, tm=1024):
    b, s, d = x.shape
    gate_block = proj.shape[0] // N_SLABS - 1
    tiles = s // tm
    tok = pl.BlockSpec((1, tm, d), lambda bi, i: (bi, i, 0))
    dils = [dil for _, dil in DIL_PAIRS]
    return pl.pallas_call(
        _out0_kernel,
        grid=(b, s // tm),
        in_specs=[
            pl.BlockSpec((N_SLABS, tm, SLAB), lambda bi, i: (0, bi * tiles + i, 0)),
            pl.BlockSpec((N_SLABS, tm, SLAB), lambda bi, i: (gate_block, bi * tiles + i, 0)),
            tok,
            pl.BlockSpec((d, d), lambda bi, i: (0, 0)),
            pl.BlockSpec((1, d), lambda bi, i: (0, 0)),
        ],
        out_specs=[tok] + [pl.BlockSpec((1, dil, tm // dil, d), lambda bi, i: (bi, 0, i, 0))
                           for dil in dils],
        out_shape=([jax.ShapeDtypeStruct((b, s, d), jnp.float32)]
                   + [jax.ShapeDtypeStruct((b, dil, s // dil, d), jnp.bfloat16) for dil in dils]),
        scratch_shapes=[pltpu.VMEM((N_LANE_SLABS, tm, LANES), jnp.float32)] * 2,
        compiler_params=pltpu.CompilerParams(
            dimension_semantics=("parallel", "parallel"), vmem_limit_bytes=VMEM_LIMIT),
        name="out_proj0",
    )(o, proj, x, w_out, g_next.reshape(1, d))


_MAX_ROW_STRIDE = 4


def _natural_order(ref, slab_refs, dil):
    _, n, c = ref.shape
    if dil == 1:
        return ref[0].astype(jnp.float32)
    out_ref, mid_ref = slab_refs
    tm = dil * n
    two_level = dil > _MAX_ROW_STRIDE
    if two_level:
        f = _MAX_ROW_STRIDE
        p = dil // f
        assert p <= _MAX_ROW_STRIDE
    pieces = []
    for s in range(c // LANES):
        lanes = slice(s * LANES, (s + 1) * LANES)
        if not two_level:
            for r in range(dil):
                out_ref[s, pl.ds(r, n, stride=dil), :] = ref[r, :, lanes].astype(jnp.float32)
        else:
            for r in range(dil):
                start = (r % p) * (tm // p) + r // p
                mid_ref[s, pl.ds(start, n, stride=f), :] = ref[r, :, lanes].astype(jnp.float32)
            for b_ in range(p):
                out_ref[s, pl.ds(b_, tm // p, stride=p), :] = (
                    mid_ref[s, b_ * (tm // p):(b_ + 1) * (tm // p), :])
        pieces.append(out_ref[s])
    return pieces[0] if len(pieces) == 1 else jnp.concatenate(pieces, axis=1)


def _out1_kernel(o0_ref, o1_ref, o2_ref, l0_ref, l1_ref, l2_ref, gate_ref, x_ref,
                 w_ref, e_ref, g_ref, y_ref, *slab_refs):
    dils = [dil for _, dil in DIL_PAIRS]
    stats = [_natural_order(ref.at[0], slab_refs, dil)
             for ref, dil in zip((l0_ref, l1_ref, l2_ref), dils)]
    sums = [pltpu.roll(st, LSE_LANES - N_HEADS, axis=1) for st in stats]
    m = jnp.maximum(jnp.maximum(stats[0], stats[1]), stats[2])
    es = [jnp.exp2(st - m) for st in stats]
    denom = es[0] * sums[0] + es[1] * sums[1] + es[2] * sums[2]
    head_lane = lax.broadcasted_iota(jnp.int32, denom.shape, 1) < N_HEADS
    o = None
    for e, o_ref, dil in zip(es, (o0_ref, o1_ref, o2_ref), dils):
        w = jnp.where(head_lane, e / denom, 0.0)
        hi = w.astype(jnp.bfloat16)
        lo = (w - hi.astype(jnp.float32)).astype(jnp.bfloat16)
        w_full = jnp.dot(jnp.concatenate([hi, lo], axis=1), e_ref[...],
                         preferred_element_type=jnp.float32)
        term = w_full * _natural_order(o_ref.at[0], slab_refs, dil)
        o = term if o is None else o + term
    gate = gate_ref[0].astype(jnp.float32)
    z = (o * _silu(gate)).astype(jnp.bfloat16)
    x = x_ref[0] + jnp.dot(z, w_ref[...], preferred_element_type=jnp.float32)
    y_ref[0] = _rmsnorm(x, g_ref[...])


def _head_expansion():
    e = np.zeros((LSE_LANES, D_MODEL), np.float32)
    for h in range(N_HEADS):
        e[h, h * HEAD_DIM:(h + 1) * HEAD_DIM] = 1.0
    return jnp.asarray(np.concatenate([e, e], axis=0), dtype=jnp.bfloat16)


def _out_proj1(os_, lses, proj, gate_block, x, w_out, norm_f, *, tm=1024):
    b, s, d = x.shape
    dils = [dil for _, dil in DIL_PAIRS]

    def grouped(width, dil):
        return pl.BlockSpec((1, dil, tm // dil, width), lambda bi, i: (bi, 0, i, 0))

    tok = pl.BlockSpec((1, tm, d), lambda bi, i: (bi, i, 0))
    return pl.pallas_call(
        _out1_kernel,
        grid=(b, s // tm),
        in_specs=(
            [grouped(d, dil) for dil in dils]
            + [grouped(LSE_LANES, dil) for dil in dils]
            + [pl.BlockSpec((1, tm, d), lambda bi, i: (bi, i, gate_block)),
               tok,
               pl.BlockSpec((d, d), lambda bi, i: (0, 0)),
               pl.BlockSpec((2 * LSE_LANES, d), lambda bi, i: (0, 0)),
               pl.BlockSpec((1, d), lambda bi, i: (0, 0))]),
        out_specs=tok,
        out_shape=jax.ShapeDtypeStruct((b, s, d), jnp.float32),
        scratch_shapes=[pltpu.VMEM((N_LANE_SLABS, tm, LANES), jnp.float32)] * 2,
        compiler_params=pltpu.CompilerParams(
            dimension_semantics=("parallel", "parallel"), vmem_limit_bytes=VMEM_LIMIT),
        name="out_proj1",
    )(*os_, *lses, proj, x, w_out, _head_expansion(), norm_f.reshape(1, d))


def kernel(x, norm_0, w_in_0, rpb_0, w_out_0, norm_1, w_in_1, w_out_1, norm_f):
    b, s, d = x.shape
    t = b * s

    proj0 = _norm_proj(x.reshape(t, d), norm_0, w_in_0)
    o0 = _na_attention(proj0, _na_bias_table(rpb_0), b)
    x1, *h1 = _out_proj0(o0, proj0, x, w_out_0.astype(jnp.bfloat16), norm_1)

    gate_block = 3 * N_DIL_GROUPS
    col_blocks = [(3 * g, 3 * g + 1, 3 * g + 2) + ((gate_block,) if dil == 1 else ())
                  for g, (_, dil) in enumerate(DIL_PAIRS)]
    qkv = _group_proj([h.reshape(t, d) for h in h1], w_in_1, col_blocks)
    alibi = _dil_bias_tables()
    outs, stats, col_base = [], [], 0
    for g, (window, dil) in enumerate(DIL_PAIRS):
        assert window // (2 * dil) == DIL_RADIUS
        o_g, stat_g = _dil_attention(qkv.reshape(b, dil, s // dil, -1), alibi, g, col_base)
        outs.append(o_g)
        stats.append(stat_g)
        col_base += len(col_blocks[g])
    assert DIL_PAIRS[0][1] == 1
    return _out_proj1(outs, stats, qkv.reshape(b, s, -1), len(col_blocks[0]) - 1, x1,
                      w_out_1.astype(jnp.bfloat16), norm_f)
```

```python
import math

import numpy as np
import jax
import jax.numpy as jnp
from jax import lax
from jax.experimental import pallas as pl
from jax.experimental.pallas import tpu as pltpu

D_MODEL = 1024
HEAD_DIM = 64
N_HEADS = 16
GRID_W = 64
NA_ROWS = 8
NA_COLS = 16
DIL_PAIRS = ((128, 1), (512, 4), (2048, 16))
N_DIL_GROUPS = len(DIL_PAIRS)
RMS_EPS = 1e-6
NEG_INF = -1e30
LOG2E = math.log2(math.e)

LANES = 128
N_LANE_SLABS = D_MODEL // LANES
HEADS_PER_STEP = 4
SLAB = HEADS_PER_STEP * HEAD_DIM
N_SLABS = D_MODEL // SLAB
Q_TILE = 128
NA_Q_TILE = GRID_W
NA_WIN = NA_ROWS * GRID_W
NA_TILES_PER_STEP = 32
DIL_TILES_PER_STEP = 8
DIL_RADIUS = 64
DIL_WIN = Q_TILE + 2 * DIL_RADIUS
LSE_LANES = LANES
VMEM_LIMIT = 56 * 1024 * 1024

_NT_DIMS = (((1,), (1,)), ((), ()))


def _rmsnorm(x, g):
    ms = jnp.mean(x * x, axis=-1, keepdims=True)
    return x * lax.rsqrt(ms + RMS_EPS) * g


def _q_scaled_bf16(w_ref):
    q_steps = D_MODEL // w_ref.shape[1]
    scale = jnp.where(pl.program_id(1) < q_steps, LOG2E / math.sqrt(HEAD_DIM), 1.0)
    return (w_ref[...] * scale).astype(jnp.bfloat16)


def _norm_proj_kernel(x_ref, g_ref, w_ref, o_ref, h_ref):
    i, j = pl.program_id(0), pl.program_id(1)
    rows = x_ref.shape[0]

    def normalise_chunk():
        start = pl.multiple_of(j * rows, rows)
        h_ref[i % 2, pl.ds(start, rows), :] = _rmsnorm(x_ref[...], g_ref[...]).astype(h_ref.dtype)

    @pl.when(i == 0)
    def _():
        normalise_chunk()

    @pl.when(i > 0)
    def _():
        res = jnp.dot(h_ref[(i - 1) % 2], _q_scaled_bf16(w_ref),
                      preferred_element_type=jnp.float32).astype(o_ref.dtype)
        for k in range(o_ref.shape[0]):
            o_ref[k] = res[:, k * SLAB:(k + 1) * SLAB]
        normalise_chunk()


def _norm_proj(x, g, w, *, tm=2048, tn=1024):
    t, d = x.shape
    n = w.shape[1]
    n_i, n_j = t // tm, n // tn
    chunk = tm // n_j
    n_chunks = t // chunk
    return pl.pallas_call(
        _norm_proj_kernel,
        grid=(n_i + 1, n_j),
        in_specs=[
            pl.BlockSpec((chunk, d), lambda i, j: (jnp.minimum(i * n_j + j, n_chunks - 1), 0)),
            pl.BlockSpec((1, d), lambda i, j: (0, 0)),
            pl.BlockSpec((d, tn), lambda i, j: (0, j)),
        ],
        out_specs=pl.BlockSpec((tn // SLAB, tm, SLAB),
                               lambda i, j: (jnp.where(i == 0, 0, j), jnp.maximum(i - 1, 0), 0)),
        out_shape=jax.ShapeDtypeStruct((n // SLAB, t, SLAB), jnp.bfloat16),
        scratch_shapes=[pltpu.VMEM((2, tm, d), jnp.bfloat16)],
        compiler_params=pltpu.CompilerParams(
            dimension_semantics=("arbitrary", "arbitrary"),
            vmem_limit_bytes=VMEM_LIMIT),
        name="norm_proj",
    )(x, g.reshape(1, d), w)


def _proj_kernel(h_ref, w_ref, scale_ref, o_ref):
    w = (w_ref[...] * scale_ref[...]).astype(jnp.bfloat16)
    o_ref[...] = jnp.dot(h_ref[...], w, preferred_element_type=jnp.float32).astype(o_ref.dtype)


PROJ_WEIGHT_BUFFERS = 3


def _proj(h, w, col_blocks, *, tm=2048, tn=1024):
    t, d = h.shape
    per_block = d // tn
    n = len(col_blocks) * d
    scale = np.ones((1, n), np.float32)
    scale[:, :d] = LOG2E / math.sqrt(HEAD_DIM)

    def w_block(j):
        k_of_j = j // per_block
        blk = col_blocks[-1]
        for k in range(len(col_blocks) - 2, -1, -1):
            blk = jnp.where(k_of_j == k, col_blocks[k], blk)
        return blk * per_block + j % per_block

    pipeline = pltpu.emit_pipeline(
        _proj_kernel,
        grid=(t // tm, n // tn),
        in_specs=[
            pl.BlockSpec((tm, d), lambda i, j: (i, 0)),
            pl.BlockSpec((d, tn), lambda i, j: (0, w_block(j)),
                         pipeline_mode=pl.Buffered(PROJ_WEIGHT_BUFFERS)),
            pl.BlockSpec((1, tn), lambda i, j: (0, j)),
        ],
        out_specs=[pl.BlockSpec((tm, tn), lambda i, j: (i, j))],
    )

    def outer(h_hbm, w_hbm, scale_hbm, o_hbm):
        pipeline(h_hbm, w_hbm, scale_hbm, o_hbm)

    return pl.pallas_call(
        outer,
        in_specs=[pl.BlockSpec(memory_space=pl.ANY)] * 3,
        out_specs=pl.BlockSpec(memory_space=pl.ANY),
        out_shape=jax.ShapeDtypeStruct((t, n), jnp.bfloat16),
        compiler_params=pltpu.CompilerParams(vmem_limit_bytes=VMEM_LIMIT),
        name="proj",
    )(h, w, jnp.asarray(scale))


def _na_kernel(q_ref, k_ref, v_ref, b_ref, o_ref):
    step = pl.program_id(2)
    n_tiles = k_ref.shape[1] // GRID_W
    lane = lax.broadcasted_iota(jnp.int32, (NA_Q_TILE, SLAB), 1) // HEAD_DIM

    for u in range(NA_TILES_PER_STEP):
        r = step * NA_TILES_PER_STEP + u
        first_row = jnp.clip(r - NA_ROWS // 2, 0, n_tiles - NA_ROWS)
        start = pl.multiple_of(first_row * GRID_W, GRID_W)
        tile_rows = slice(u * NA_Q_TILE, (u + 1) * NA_Q_TILE)
        q4 = q_ref[0, tile_rows, :]
        zero = jnp.zeros_like(q4)
        q_stack = jnp.concatenate(
            [jnp.where(lane == h, q4, zero) for h in range(HEADS_PER_STEP)], axis=0)
        s = lax.dot_general(q_stack, k_ref[0, pl.ds(start, NA_WIN), :], _NT_DIMS,
                            preferred_element_type=jnp.float32)
        first_off = first_row - r + NA_ROWS - 1
        s = s + jnp.concatenate(
            [jnp.concatenate([b_ref[h, first_off + 2 * j] for j in range(NA_ROWS // 2)], axis=1)
             for h in range(HEADS_PER_STEP)], axis=0)
        m = jnp.max(s, axis=-1, keepdims=True)
        e = jnp.exp2(s - m)
        inv_l = 1.0 / jnp.sum(e, axis=-1, keepdims=True)
        o_all = jnp.dot(e.astype(jnp.bfloat16), v_ref[0, pl.ds(start, NA_WIN), :],
                        preferred_element_type=jnp.float32) * inv_l
        out = o_all[:NA_Q_TILE]
        for h in range(1, HEADS_PER_STEP):
            out = jnp.where(lane == h, o_all[h * NA_Q_TILE:(h + 1) * NA_Q_TILE], out)
        o_ref[0, tile_rows, :] = out.astype(o_ref.dtype)


def _na_bias_table(rpb):
    n_col_off = 2 * NA_COLS - 1
    c = np.arange(GRID_W)[:, None]
    kc = np.arange(GRID_W)[None, :]
    cs = np.clip(c - NA_COLS // 2, 0, GRID_W - NA_COLS)
    col_ok = (kc >= cs) & (kc < cs + NA_COLS)
    col_off = kc - c + NA_COLS - 1
    assert np.all(((col_off >= 0) & (col_off < n_col_off))[col_ok])
    pick = (np.arange(n_col_off)[:, None, None] == col_off[None]) & col_ok[None]
    pick2 = np.zeros((2, n_col_off, GRID_W, 2, GRID_W), np.float32)
    for half in range(2):
        pick2[half, :, :, half, :] = pick
    pick2 = pick2.reshape(2 * n_col_off, GRID_W, 2 * GRID_W)
    rpb = rpb.astype(jnp.float32) * LOG2E
    row_pairs = jnp.concatenate([rpb[:, :-1], rpb[:, 1:]], axis=-1)
    table = jnp.einsum('hrm,mck->hrck', row_pairs, jnp.asarray(pick2),
                       precision=lax.Precision.HIGHEST)
    ok2 = np.concatenate([col_ok, col_ok], axis=1)
    return table + jnp.asarray(np.where(ok2, 0.0, NEG_INF).astype(np.float32))


def _na_attention(proj, bias_table, b):
    s = proj.shape[1] // b
    tile = NA_Q_TILE * NA_TILES_PER_STEP
    assert s % tile == 0 and NA_Q_TILE == GRID_W
    n_blocks = s // tile
    return pl.pallas_call(
        _na_kernel,
        grid=(b, N_SLABS, n_blocks),
        in_specs=[
            pl.BlockSpec((1, tile, SLAB), lambda bi, g, j: (g, bi * n_blocks + j, 0)),
            pl.BlockSpec((1, s, SLAB), lambda bi, g, j: (N_SLABS + g, bi, 0)),
            pl.BlockSpec((1, s, SLAB), lambda bi, g, j: (2 * N_SLABS + g, bi, 0)),
            pl.BlockSpec((HEADS_PER_STEP,) + bias_table.shape[1:], lambda bi, g, j: (g, 0, 0, 0)),
        ],
        out_specs=pl.BlockSpec((1, tile, SLAB), lambda bi, g, j: (g, bi * n_blocks + j, 0)),
        out_shape=jax.ShapeDtypeStruct((N_SLABS, b * s, SLAB), jnp.bfloat16),
        compiler_params=pltpu.CompilerParams(
            dimension_semantics=("parallel", "parallel", "arbitrary"),
            vmem_limit_bytes=VMEM_LIMIT),
        name="na_attention",
    )(proj, proj, proj, bias_table)


def _alibi_slopes():
    return np.asarray(2.0 ** (-8.0 * (np.arange(N_HEADS) + 1) / N_HEADS), dtype=np.float32)


def _slab_attention(q4, k4, v4, bias):
    lane = lax.broadcasted_iota(jnp.int32, q4.shape, 1) // HEAD_DIM
    zero = jnp.zeros_like(q4)
    q_stack = jnp.concatenate(
        [jnp.where(lane == h, q4, zero) for h in range(HEADS_PER_STEP)], axis=0)
    s = lax.dot_general(q_stack, k4, _NT_DIMS, preferred_element_type=jnp.float32)
    s = s + bias
    m = jnp.max(s, axis=-1, keepdims=True)
    p = jnp.exp2(s - m)
    l = jnp.sum(p, axis=-1, keepdims=True)
    p = p.astype(jnp.bfloat16)
    o_all = jnp.dot(p, v4, preferred_element_type=jnp.float32)
    out = o_all[:Q_TILE]
    for h in range(1, HEADS_PER_STEP):
        out = jnp.where(lane == h, o_all[h * Q_TILE:(h + 1) * Q_TILE], out)
    return out, m, l


def _dil_kernel(q_ref, ka_ref, kb_ref, kc_ref, va_ref, vb_ref, vc_ref, b_ref,
                o_ref, stat_ref):
    step = pl.program_id(2)
    n_steps = pl.num_programs(2)
    lane = lax.broadcasted_iota(jnp.int32, (Q_TILE, LSE_LANES), 1)
    tiles = q_ref.shape[2] // Q_TILE
    for r in range(q_ref.shape[1]):
        k_win = jnp.concatenate([ka_ref[0, r], kb_ref[0, r], kc_ref[0, r]], axis=0)
        v_win = jnp.concatenate([va_ref[0, r], vb_ref[0, r], vc_ref[0, r]], axis=0)
        for u in range(tiles):
            pattern = 0
            if u == 0:
                pattern = jnp.where(step == 0, 1, pattern)
            if u == tiles - 1:
                pattern = jnp.where(step == n_steps - 1, 2, pattern)
            tile = slice(u * Q_TILE, (u + 1) * Q_TILE)
            win = slice(u * Q_TILE, u * Q_TILE + DIL_WIN)
            stat_tile = jnp.zeros((Q_TILE, LSE_LANES), jnp.float32)
            for hg in range(N_SLABS):
                cols = slice(hg * SLAB, (hg + 1) * SLAB)
                bias = b_ref[pattern, hg * HEADS_PER_STEP:(hg + 1) * HEADS_PER_STEP]
                bias = bias.reshape(HEADS_PER_STEP * Q_TILE, DIL_WIN)
                out, m, l = _slab_attention(q_ref[0, r, tile, cols], k_win[win, cols],
                                            v_win[win, cols], bias)
                o_ref[0, r, tile, cols] = out.astype(o_ref.dtype)
                for h in range(HEADS_PER_STEP):
                    rows = slice(h * Q_TILE, (h + 1) * Q_TILE)
                    head = hg * HEADS_PER_STEP + h
                    stat_tile = jnp.where(lane == head, m[rows], stat_tile)
                    stat_tile = jnp.where(lane == N_HEADS + head, l[rows], stat_tile)
            stat_ref[0, r, tile, :] = stat_tile


def _dil_bias_tables():
    qi = np.arange(Q_TILE)[:, None]
    kj = np.arange(DIL_WIN)[None, :]
    delta = kj - DIL_RADIUS - qi
    in_band = np.abs(delta) <= DIL_RADIUS
    valid = np.stack([in_band,
                      in_band & (kj >= DIL_RADIUS),
                      in_band & (kj < DIL_WIN - DIL_RADIUS)])
    dils = np.asarray([dil for _, dil in DIL_PAIRS])
    dist = (np.abs(delta)[None] * dils[:, None, None]).astype(np.float32)
    bias = -_alibi_slopes()[None, :, None, None] * dist[:, None]
    table = np.where(valid[None, :, None], bias[:, None] * np.float32(LOG2E), np.float32(NEG_INF))
    assert table.dtype == np.float32
    return jnp.asarray(table)


def _dil_attention(qkv, bias_tables, group):
    b, dil, l, _ = qkv.shape
    tiles = min(DIL_TILES_PER_STEP, l // Q_TILE)
    seqs = min(DIL_TILES_PER_STEP // tiles, dil)
    tile = Q_TILE * tiles
    n_steps = l // tile
    assert l % tile == 0 and tiles >= 2 and dil % seqs == 0
    half = DIL_RADIUS
    per_tile = tile // half
    n_half = l // half

    def spec_mid(which):
        return pl.BlockSpec((1, seqs, tile, D_MODEL), lambda bi, r, i: (bi, r, i, which))

    def spec_lo(which):
        return pl.BlockSpec((1, seqs, half, D_MODEL),
                            lambda bi, r, i: (bi, r, jnp.maximum(per_tile * i - 1, 0), which))

    def spec_hi(which):
        return pl.BlockSpec((1, seqs, half, D_MODEL),
                            lambda bi, r, i: (bi, r, jnp.minimum(per_tile * (i + 1), n_half - 1), which))

    return pl.pallas_call(
        _dil_kernel,
        grid=(b, dil // seqs, n_steps),
        in_specs=[
            spec_mid(0),
            spec_lo(1), spec_mid(1), spec_hi(1),
            spec_lo(2), spec_mid(2), spec_hi(2),
            pl.BlockSpec((None,) + bias_tables.shape[1:], lambda bi, r, i: (group, 0, 0, 0, 0)),
        ],
        out_specs=[
            pl.BlockSpec((1, seqs, tile, D_MODEL), lambda bi, r, i: (bi, r, i, 0)),
            pl.BlockSpec((1, seqs, tile, LSE_LANES), lambda bi, r, i: (bi, r, i, 0)),
        ],
        out_shape=[
            jax.ShapeDtypeStruct((b, dil, l, D_MODEL), jnp.bfloat16),
            jax.ShapeDtypeStruct((b, dil, l, LSE_LANES), jnp.float32),
        ],
        compiler_params=pltpu.CompilerParams(
            dimension_semantics=("parallel", "parallel", "arbitrary"),
            vmem_limit_bytes=VMEM_LIMIT),
        name=f"dilated_attention_{dil}",
    )(qkv, qkv, qkv, qkv, qkv, qkv, qkv, bias_tables)


def _silu(x):
    half = 0.5 * x
    return half + half * jnp.tanh(half)


def _out0_kernel(o_ref, gate_ref, x_ref, w_ref, g_ref, y_ref, *rest):
    h_refs, slab_refs = rest[:-2], rest[-2:]
    o = jnp.concatenate([o_ref[k] for k in range(N_SLABS)], axis=1).astype(jnp.float32)
    gate = jnp.concatenate([gate_ref[k] for k in range(N_SLABS)], axis=1).astype(jnp.float32)
    z = (o * _silu(gate)).astype(jnp.bfloat16)
    y = x_ref[0] + jnp.dot(z, w_ref[...], preferred_element_type=jnp.float32)
    y_ref[0] = y
    hn = _rmsnorm(y, g_ref[...])
    tm = hn.shape[0]
    dils = [dil for _, dil in DIL_PAIRS]
    for s in range(N_LANE_SLABS):
        slab_refs[0][s] = hn[:, s * LANES:(s + 1) * LANES]
    staged = {1: slab_refs[0]}
    for k, (h_ref, dil) in enumerate(zip(h_refs, dils)):
        if dil == 1:
            h_ref[0, 0] = hn.astype(h_ref.dtype)
            continue
        p = max(q for q in staged if dil % q == 0)
        f, n = dil // p, tm // dil
        keep = any(later % dil == 0 for later in dils[k + 1:])
        for r in range(dil):
            start = (r % p) * (tm // p) + r // p
            for s in range(N_LANE_SLABS):
                rows = staged[p][s, pl.ds(start, n, stride=f), :]
                h_ref[0, r, :, s * LANES:(s + 1) * LANES] = rows.astype(h_ref.dtype)
                if keep:
                    slab_refs[1][s, r * n:(r + 1) * n, :] = rows
        if keep:
            staged[dil] = slab_refs[1]


def _out_proj0(o, proj, x, w_out, g_next, *, tm=1024):
    b, s, d = x.shape
    gate_block = proj.shape[0] // N_SLABS - 1
    tiles = s // tm
    tok = pl.BlockSpec((1, tm, d), lambda bi, i: (bi, i, 0))
    dils = [dil for _, dil in DIL_PAIRS]
    return pl.pallas_call(
        _out0_kernel,
        grid=(b, s // tm),
        in_specs=[
            pl.BlockSpec((N_SLABS, tm, SLAB), lambda bi, i: (0, bi * tiles + i, 0)),
            pl.BlockSpec((N_SLABS, tm, SLAB), lambda bi, i: (gate_block, bi * tiles + i, 0)),
            tok,
            pl.BlockSpec((d, d), lambda bi, i: (0, 0)),
            pl.BlockSpec((1, d), lambda bi, i: (0, 0)),
        ],
        out_specs=[tok] + [pl.BlockSpec((1, dil, tm // dil, d), lambda bi, i: (bi, 0, i, 0))
                           for dil in dils],
        out_shape=([jax.ShapeDtypeStruct((b, s, d), jnp.float32)]
                   + [jax.ShapeDtypeStruct((b, dil, s // dil, d), jnp.bfloat16) for dil in dils]),
        scratch_shapes=[pltpu.VMEM((N_LANE_SLABS, tm, LANES), jnp.float32)] * 2,
        compiler_params=pltpu.CompilerParams(
            dimension_semantics=("parallel", "parallel"), vmem_limit_bytes=VMEM_LIMIT),
        name="out_proj0",
    )(o, proj, x, w_out, g_next.reshape(1, d))


_MAX_ROW_STRIDE = 4


def _natural_order(ref, slab_refs, dil):
    _, n, c = ref.shape
    if dil == 1:
        return ref[0].astype(jnp.float32)
    out_ref, mid_ref = slab_refs
    tm = dil * n
    two_level = dil > _MAX_ROW_STRIDE
    if two_level:
        f = _MAX_ROW_STRIDE
        p = dil // f
        assert p <= _MAX_ROW_STRIDE
    pieces = []
    for s in range(c // LANES):
        lanes = slice(s * LANES, (s + 1) * LANES)
        if not two_level:
            for r in range(dil):
                out_ref[s, pl.ds(r, n, stride=dil), :] = ref[r, :, lanes].astype(jnp.float32)
        else:
            for r in range(dil):
                start = (r % p) * (tm // p) + r // p
                mid_ref[s, pl.ds(start, n, stride=f), :] = ref[r, :, lanes].astype(jnp.float32)
            for b_ in range(p):
                out_ref[s, pl.ds(b_, tm // p, stride=p), :] = (
                    mid_ref[s, b_ * (tm // p):(b_ + 1) * (tm // p), :])
        pieces.append(out_ref[s])
    return pieces[0] if len(pieces) == 1 else jnp.concatenate(pieces, axis=1)


def _out1_kernel(o0_ref, o1_ref, o2_ref, l0_ref, l1_ref, l2_ref, gate_ref, x_ref,
                 w_ref, e_ref, g_ref, y_ref, *slab_refs):
    dils = [dil for _, dil in DIL_PAIRS]
    stats = [_natural_order(ref.at[0], slab_refs, dil)
             for ref, dil in zip((l0_ref, l1_ref, l2_ref), dils)]
    sums = [pltpu.roll(st, LSE_LANES - N_HEADS, axis=1) for st in stats]
    m = jnp.maximum(jnp.maximum(stats[0], stats[1]), stats[2])
    es = [jnp.exp2(st - m) for st in stats]
    denom = es[0] * sums[0] + es[1] * sums[1] + es[2] * sums[2]
    head_lane = lax.broadcasted_iota(jnp.int32, denom.shape, 1) < N_HEADS
    o = None
    for e, o_ref, dil in zip(es, (o0_ref, o1_ref, o2_ref), dils):
        w = jnp.where(head_lane, e / denom, 0.0)
        hi = w.astype(jnp.bfloat16)
        lo = (w - hi.astype(jnp.float32)).astype(jnp.bfloat16)
        w_full = jnp.dot(jnp.concatenate([hi, lo], axis=1), e_ref[...],
                         preferred_element_type=jnp.float32)
        term = w_full * _natural_order(o_ref.at[0], slab_refs, dil)
        o = term if o is None else o + term
    gate = gate_ref[0].astype(jnp.float32)
    z = (o * _silu(gate)).astype(jnp.bfloat16)
    x = x_ref[0] + jnp.dot(z, w_ref[...], preferred_element_type=jnp.float32)
    y_ref[0] = _rmsnorm(x, g_ref[...])


def _head_expansion():
    e = np.zeros((LSE_LANES, D_MODEL), np.float32)
    for h in range(N_HEADS):
        e[h, h * HEAD_DIM:(h + 1) * HEAD_DIM] = 1.0
    return jnp.asarray(np.concatenate([e, e], axis=0), dtype=jnp.bfloat16)


def _out_proj1(os_, lses, proj, x, w_out, norm_f, *, tm=1024):
    b, s, d = x.shape
    gate_block = proj.shape[2] // d - 1
    dils = [dil for _, dil in DIL_PAIRS]

    def grouped(width, dil):
        return pl.BlockSpec((1, dil, tm // dil, width), lambda bi, i: (bi, 0, i, 0))

    tok = pl.BlockSpec((1, tm, d), lambda bi, i: (bi, i, 0))
    return pl.pallas_call(
        _out1_kernel,
        grid=(b, s // tm),
        in_specs=(
            [grouped(d, dil) for dil in dils]
            + [grouped(LSE_LANES, dil) for dil in dils]
            + [pl.BlockSpec((1, tm, d), lambda bi, i: (bi, i, gate_block)),
               tok,
               pl.BlockSpec((d, d), lambda bi, i: (0, 0)),
               pl.BlockSpec((2 * LSE_LANES, d), lambda bi, i: (0, 0)),
               pl.BlockSpec((1, d), lambda bi, i: (0, 0))]),
        out_specs=tok,
        out_shape=jax.ShapeDtypeStruct((b, s, d), jnp.float32),
        scratch_shapes=[pltpu.VMEM((N_LANE_SLABS, tm, LANES), jnp.float32)] * 2,
        compiler_params=pltpu.CompilerParams(
            dimension_semantics=("parallel", "parallel"), vmem_limit_bytes=VMEM_LIMIT),
        name="out_proj1",
    )(*os_, *lses, proj, x, w_out, _head_expansion(), norm_f.reshape(1, d))


def kernel(x, norm_0, w_in_0, rpb_0, w_out_0, norm_1, w_in_1, w_out_1, norm_f):
    b, s, d = x.shape
    t = b * s

    proj0 = _norm_proj(x.reshape(t, d), norm_0, w_in_0)
    o0 = _na_attention(proj0, _na_bias_table(rpb_0), b)
    x1, *h1 = _out_proj0(o0, proj0, x, w_out_0.astype(jnp.bfloat16), norm_1)

    gate_block = 3 * N_DIL_GROUPS
    alibi = _dil_bias_tables()
    outs, lses, proj_gate = [], [], None
    for g, (window, dil) in enumerate(DIL_PAIRS):
        assert window // (2 * dil) == DIL_RADIUS
        blocks = (3 * g, 3 * g + 1, 3 * g + 2) + ((gate_block,) if dil == 1 else ())
        qkv = _proj(h1[g].reshape(t, d), w_in_1, blocks).reshape(b, dil, s // dil, -1)
        if dil == 1:
            proj_gate = qkv.reshape(b, s, -1)
        o_g, lse_g = _dil_attention(qkv, alibi, g)
        outs.append(o_g)
        lses.append(lse_g)
    return _out_proj1(outs, lses, proj_gate, x1, w_out_1.astype(jnp.bfloat16), norm_f)
```

```python
import math

import numpy as np
import jax
import jax.numpy as jnp
from jax import lax
from jax.experimental import pallas as pl
from jax.experimental.pallas import tpu as pltpu

D_MODEL = 1024
HEAD_DIM = 64
N_HEADS = 16
GRID_W = 64
NA_ROWS = 8
NA_COLS = 16
DIL_PAIRS = ((128, 1), (512, 4), (2048, 16))
N_DIL_GROUPS = len(DIL_PAIRS)
RMS_EPS = 1e-6
NEG_INF = -1e30
LOG2E = math.log2(math.e)

LANES = 128
N_LANE_SLABS = D_MODEL // LANES
HEADS_PER_STEP = 4
SLAB = HEADS_PER_STEP * HEAD_DIM
N_SLABS = D_MODEL // SLAB
Q_TILE = 128
NA_Q_TILE = GRID_W
NA_WIN = NA_ROWS * GRID_W
NA_TILES_PER_STEP = 32
DIL_TILES_PER_STEP = 8
DIL_RADIUS = 64
DIL_WIN = Q_TILE + 2 * DIL_RADIUS
LSE_LANES = LANES
VMEM_LIMIT = 56 * 1024 * 1024

_NT_DIMS = (((1,), (1,)), ((), ()))


def _rmsnorm(x, g):
    ms = jnp.mean(x * x, axis=-1, keepdims=True)
    return x * lax.rsqrt(ms + RMS_EPS) * g


def _q_scaled_bf16(w_ref):
    q_steps = D_MODEL // w_ref.shape[1]
    scale = jnp.where(pl.program_id(1) < q_steps, LOG2E / math.sqrt(HEAD_DIM), 1.0)
    return (w_ref[...] * scale).astype(jnp.bfloat16)


def _norm_proj_kernel(x_ref, g_ref, w_ref, o_ref, h_ref):
    i, j = pl.program_id(0), pl.program_id(1)
    rows = x_ref.shape[0]

    def normalise_chunk():
        start = pl.multiple_of(j * rows, rows)
        h_ref[i % 2, pl.ds(start, rows), :] = _rmsnorm(x_ref[...], g_ref[...]).astype(h_ref.dtype)

    @pl.when(i == 0)
    def _():
        normalise_chunk()

    @pl.when(i > 0)
    def _():
        res = jnp.dot(h_ref[(i - 1) % 2], _q_scaled_bf16(w_ref),
                      preferred_element_type=jnp.float32).astype(o_ref.dtype)
        for k in range(o_ref.shape[0]):
            o_ref[k] = res[:, k * SLAB:(k + 1) * SLAB]
        normalise_chunk()


def _norm_proj(x, g, w, *, tm=2048, tn=1024):
    t, d = x.shape
    n = w.shape[1]
    n_i, n_j = t // tm, n // tn
    chunk = tm // n_j
    n_chunks = t // chunk
    return pl.pallas_call(
        _norm_proj_kernel,
        grid=(n_i + 1, n_j),
        in_specs=[
            pl.BlockSpec((chunk, d), lambda i, j: (jnp.minimum(i * n_j + j, n_chunks - 1), 0)),
            pl.BlockSpec((1, d), lambda i, j: (0, 0)),
            pl.BlockSpec((d, tn), lambda i, j: (0, j)),
        ],
        out_specs=pl.BlockSpec((tn // SLAB, tm, SLAB),
                               lambda i, j: (jnp.where(i == 0, 0, j), jnp.maximum(i - 1, 0), 0)),
        out_shape=jax.ShapeDtypeStruct((n // SLAB, t, SLAB), jnp.bfloat16),
        scratch_shapes=[pltpu.VMEM((2, tm, d), jnp.bfloat16)],
        compiler_params=pltpu.CompilerParams(
            dimension_semantics=("arbitrary", "arbitrary"),
            vmem_limit_bytes=VMEM_LIMIT),
        name="norm_proj",
    )(x, g.reshape(1, d), w)


def _proj_kernel(h_ref, w_ref, scale_ref, o_ref):
    w = (w_ref[...] * scale_ref[...]).astype(jnp.bfloat16)
    o_ref[...] = jnp.dot(h_ref[...], w, preferred_element_type=jnp.float32).astype(o_ref.dtype)


PROJ_WEIGHT_BUFFERS = 3


def _proj(h, w, col_blocks, *, tm=2048, tn=1024):
    t, d = h.shape
    per_block = d // tn
    n = len(col_blocks) * d
    scale = np.ones((1, n), np.float32)
    scale[:, :d] = LOG2E / math.sqrt(HEAD_DIM)

    def w_block(j):
        k_of_j = j // per_block
        blk = col_blocks[-1]
        for k in range(len(col_blocks) - 2, -1, -1):
            blk = jnp.where(k_of_j == k, col_blocks[k], blk)
        return blk * per_block + j % per_block

    pipeline = pltpu.emit_pipeline(
        _proj_kernel,
        grid=(t // tm, n // tn),
        in_specs=[
            pl.BlockSpec((tm, d), lambda i, j: (i, 0)),
            pl.BlockSpec((d, tn), lambda i, j: (0, w_block(j)),
                         pipeline_mode=pl.Buffered(PROJ_WEIGHT_BUFFERS)),
            pl.BlockSpec((1, tn), lambda i, j: (0, j)),
        ],
        out_specs=[pl.BlockSpec((tm, tn), lambda i, j: (i, j))],
    )

    def outer(h_hbm, w_hbm, scale_hbm, o_hbm):
        pipeline(h_hbm, w_hbm, scale_hbm, o_hbm)

    return pl.pallas_call(
        outer,
        in_specs=[pl.BlockSpec(memory_space=pl.ANY)] * 3,
        out_specs=pl.BlockSpec(memory_space=pl.ANY),
        out_shape=jax.ShapeDtypeStruct((t, n), jnp.bfloat16),
        compiler_params=pltpu.CompilerParams(vmem_limit_bytes=VMEM_LIMIT),
        name="proj",
    )(h, w, jnp.asarray(scale))


def _na_kernel(q_ref, k_ref, v_ref, b_ref, o_ref):
    step = pl.program_id(2)
    n_tiles = k_ref.shape[1] // GRID_W
    lane = lax.broadcasted_iota(jnp.int32, (NA_Q_TILE, SLAB), 1) // HEAD_DIM

    for u in range(NA_TILES_PER_STEP):
        r = step * NA_TILES_PER_STEP + u
        first_row = jnp.clip(r - NA_ROWS // 2, 0, n_tiles - NA_ROWS)
        start = pl.multiple_of(first_row * GRID_W, GRID_W)
        tile_rows = slice(u * NA_Q_TILE, (u + 1) * NA_Q_TILE)
        q4 = q_ref[0, tile_rows, :]
        zero = jnp.zeros_like(q4)
        q_stack = jnp.concatenate(
            [jnp.where(lane == h, q4, zero) for h in range(HEADS_PER_STEP)], axis=0)
        s = lax.dot_general(q_stack, k_ref[0, pl.ds(start, NA_WIN), :], _NT_DIMS,
                            preferred_element_type=jnp.float32)
        first_off = first_row - r + NA_ROWS - 1
        s = s + jnp.concatenate(
            [jnp.concatenate([b_ref[h, first_off + 2 * j] for j in range(NA_ROWS // 2)], axis=1)
             for h in range(HEADS_PER_STEP)], axis=0)
        m = jnp.max(s, axis=-1, keepdims=True)
        e = jnp.exp2(s - m)
        inv_l = 1.0 / jnp.sum(e, axis=-1, keepdims=True)
        o_all = jnp.dot(e.astype(jnp.bfloat16), v_ref[0, pl.ds(start, NA_WIN), :],
                        preferred_element_type=jnp.float32) * inv_l
        out = o_all[:NA_Q_TILE]
        for h in range(1, HEADS_PER_STEP):
            out = jnp.where(lane == h, o_all[h * NA_Q_TILE:(h + 1) * NA_Q_TILE], out)
        o_ref[0, tile_rows, :] = out.astype(o_ref.dtype)


def _na_bias_table(rpb):
    n_col_off = 2 * NA_COLS - 1
    c = np.arange(GRID_W)[:, None]
    kc = np.arange(GRID_W)[None, :]
    cs = np.clip(c - NA_COLS // 2, 0, GRID_W - NA_COLS)
    col_ok = (kc >= cs) & (kc < cs + NA_COLS)
    col_off = kc - c + NA_COLS - 1
    assert np.all(((col_off >= 0) & (col_off < n_col_off))[col_ok])
    pick = (np.arange(n_col_off)[:, None, None] == col_off[None]) & col_ok[None]
    pick2 = np.zeros((2, n_col_off, GRID_W, 2, GRID_W), np.float32)
    for half in range(2):
        pick2[half, :, :, half, :] = pick
    pick2 = pick2.reshape(2 * n_col_off, GRID_W, 2 * GRID_W)
    rpb = rpb.astype(jnp.float32) * LOG2E
    row_pairs = jnp.concatenate([rpb[:, :-1], rpb[:, 1:]], axis=-1)
    table = jnp.einsum('hrm,mck->hrck', row_pairs, jnp.asarray(pick2),
                       precision=lax.Precision.HIGHEST)
    ok2 = np.concatenate([col_ok, col_ok], axis=1)
    return table + jnp.asarray(np.where(ok2, 0.0, NEG_INF).astype(np.float32))


def _na_attention(proj, bias_table, b):
    s = proj.shape[1] // b
    tile = NA_Q_TILE * NA_TILES_PER_STEP
    assert s % tile == 0 and NA_Q_TILE == GRID_W
    n_blocks = s // tile
    return pl.pallas_call(
        _na_kernel,
        grid=(b, N_SLABS, n_blocks),
        in_specs=[
            pl.BlockSpec((1, tile, SLAB), lambda bi, g, j: (g, bi * n_blocks + j, 0)),
            pl.BlockSpec((1, s, SLAB), lambda bi, g, j: (N_SLABS + g, bi, 0)),
            pl.BlockSpec((1, s, SLAB), lambda bi, g, j: (2 * N_SLABS + g, bi, 0)),
            pl.BlockSpec((HEADS_PER_STEP,) + bias_table.shape[1:], lambda bi, g, j: (g, 0, 0, 0)),
        ],
        out_specs=pl.BlockSpec((1, tile, SLAB), lambda bi, g, j: (g, bi * n_blocks + j, 0)),
        out_shape=jax.ShapeDtypeStruct((N_SLABS, b * s, SLAB), jnp.bfloat16),
        compiler_params=pltpu.CompilerParams(
            dimension_semantics=("parallel", "parallel", "arbitrary"),
            vmem_limit_bytes=VMEM_LIMIT),
        name="na_attention",
    )(proj, proj, proj, bias_table)


def _alibi_slopes():
    return np.asarray(2.0 ** (-8.0 * (np.arange(N_HEADS) + 1) / N_HEADS), dtype=np.float32)


def _slab_attention(q4, k4, v4, bias):
    lane = lax.broadcasted_iota(jnp.int32, q4.shape, 1) // HEAD_DIM
    zero = jnp.zeros_like(q4)
    q_stack = jnp.concatenate(
        [jnp.where(lane == h, q4, zero) for h in range(HEADS_PER_STEP)], axis=0)
    s = lax.dot_general(q_stack, k4, _NT_DIMS, preferred_element_type=jnp.float32)
    s = s + bias
    m = jnp.max(s, axis=-1, keepdims=True)
    p = jnp.exp2(s - m)
    l = jnp.sum(p, axis=-1, keepdims=True)
    p = p.astype(jnp.bfloat16)
    o_all = jnp.dot(p, v4, preferred_element_type=jnp.float32)
    out = o_all[:Q_TILE]
    for h in range(1, HEADS_PER_STEP):
        out = jnp.where(lane == h, o_all[h * Q_TILE:(h + 1) * Q_TILE], out)
    return out, m, l


def _dil_kernel(q_ref, ka_ref, kb_ref, kc_ref, va_ref, vb_ref, vc_ref, b_ref,
                o_ref, stat_ref):
    step = pl.program_id(2)
    n_steps = pl.num_programs(2)
    lane = lax.broadcasted_iota(jnp.int32, (Q_TILE, LSE_LANES), 1)
    tiles = q_ref.shape[2] // Q_TILE
    for r in range(q_ref.shape[1]):
        k_win = jnp.concatenate([ka_ref[0, r], kb_ref[0, r], kc_ref[0, r]], axis=0)
        v_win = jnp.concatenate([va_ref[0, r], vb_ref[0, r], vc_ref[0, r]], axis=0)
        for u in range(tiles):
            pattern = 0
            if u == 0:
                pattern = jnp.where(step == 0, 1, pattern)
            if u == tiles - 1:
                pattern = jnp.where(step == n_steps - 1, 2, pattern)
            tile = slice(u * Q_TILE, (u + 1) * Q_TILE)
            win = slice(u * Q_TILE, u * Q_TILE + DIL_WIN)
            stat_tile = jnp.zeros((Q_TILE, LSE_LANES), jnp.float32)
            for hg in range(N_SLABS):
                cols = slice(hg * SLAB, (hg + 1) * SLAB)
                bias = b_ref[pattern, hg * HEADS_PER_STEP:(hg + 1) * HEADS_PER_STEP]
                bias = bias.reshape(HEADS_PER_STEP * Q_TILE, DIL_WIN)
                out, m, l = _slab_attention(q_ref[0, r, tile, cols], k_win[win, cols],
                                            v_win[win, cols], bias)
                o_ref[0, r, tile, cols] = out.astype(o_ref.dtype)
                for h in range(HEADS_PER_STEP):
                    rows = slice(h * Q_TILE, (h + 1) * Q_TILE)
                    head = hg * HEADS_PER_STEP + h
                    stat_tile = jnp.where(lane == head, m[rows], stat_tile)
                    stat_tile = jnp.where(lane == N_HEADS + head, l[rows], stat_tile)
            stat_ref[0, r, tile, :] = stat_tile


def _dil_bias_tables():
    qi = np.arange(Q_TILE)[:, None]
    kj = np.arange(DIL_WIN)[None, :]
    delta = kj - DIL_RADIUS - qi
    in_band = np.abs(delta) <= DIL_RADIUS
    valid = np.stack([in_band,
                      in_band & (kj >= DIL_RADIUS),
                      in_band & (kj < DIL_WIN - DIL_RADIUS)])
    dils = np.asarray([dil for _, dil in DIL_PAIRS])
    dist = (np.abs(delta)[None] * dils[:, None, None]).astype(np.float32)
    bias = -_alibi_slopes()[None, :, None, None] * dist[:, None]
    table = np.where(valid[None, :, None], bias[:, None] * np.float32(LOG2E), np.float32(NEG_INF))
    assert table.dtype == np.float32
    return jnp.asarray(table)


def _dil_attention(qkv, bias_tables, group):
    b, dil, l, _ = qkv.shape
    tiles = min(DIL_TILES_PER_STEP, l // Q_TILE)
    seqs = min(DIL_TILES_PER_STEP // tiles, dil)
    tile = Q_TILE * tiles
    n_steps = l // tile
    assert l % tile == 0 and tiles >= 2 and dil % seqs == 0
    half = DIL_RADIUS
    per_tile = tile // half
    n_half = l // half

    def spec_mid(which):
        return pl.BlockSpec((1, seqs, tile, D_MODEL), lambda bi, r, i: (bi, r, i, which))

    def spec_lo(which):
        return pl.BlockSpec((1, seqs, half, D_MODEL),
                            lambda bi, r, i: (bi, r, jnp.maximum(per_tile * i - 1, 0), which))

    def spec_hi(which):
        return pl.BlockSpec((1, seqs, half, D_MODEL),
                            lambda bi, r, i: (bi, r, jnp.minimum(per_tile * (i + 1), n_half - 1), which))

    return pl.pallas_call(
        _dil_kernel,
        grid=(b, dil // seqs, n_steps),
        in_specs=[
            spec_mid(0),
            spec_lo(1), spec_mid(1), spec_hi(1),
            spec_lo(2), spec_mid(2), spec_hi(2),
            pl.BlockSpec((None,) + bias_tables.shape[1:], lambda bi, r, i: (group, 0, 0, 0, 0)),
        ],
        out_specs=[
            pl.BlockSpec((1, seqs, tile, D_MODEL), lambda bi, r, i: (bi, r, i, 0)),
            pl.BlockSpec((1, seqs, tile, LSE_LANES), lambda bi, r, i: (bi, r, i, 0)),
        ],
        out_shape=[
            jax.ShapeDtypeStruct((b, dil, l, D_MODEL), jnp.bfloat16),
            jax.ShapeDtypeStruct((b, dil, l, LSE_LANES), jnp.float32),
        ],
        compiler_params=pltpu.CompilerParams(
            dimension_semantics=("parallel", "parallel", "arbitrary"),
            vmem_limit_bytes=VMEM_LIMIT),
        name=f"dilated_attention_{dil}",
    )(qkv, qkv, qkv, qkv, qkv, qkv, qkv, bias_tables)


def _silu(x):
    half = 0.5 * x
    return half + half * jnp.tanh(half)


def _out0_kernel(o_ref, gate_ref, x_ref, w_ref, g_ref, y_ref, *rest):
    h_refs, slab_refs = rest[:-2], rest[-2:]
    o = jnp.concatenate([o_ref[k] for k in range(N_SLABS)], axis=1).astype(jnp.float32)
    gate = jnp.concatenate([gate_ref[k] for k in range(N_SLABS)], axis=1).astype(jnp.float32)
    z = (o * _silu(gate)).astype(jnp.bfloat16)
    y = x_ref[0] + jnp.dot(z, w_ref[...], preferred_element_type=jnp.float32)
    y_ref[0] = y
    hn = _rmsnorm(y, g_ref[...])
    tm = hn.shape[0]
    dils = [dil for _, dil in DIL_PAIRS]
    for s in range(N_LANE_SLABS):
        slab_refs[0][s] = hn[:, s * LANES:(s + 1) * LANES]
    staged = {1: slab_refs[0]}
    for k, (h_ref, dil) in enumerate(zip(h_refs, dils)):
        if dil == 1:
            h_ref[0, 0] = hn.astype(h_ref.dtype)
            continue
        p = max(q for q in staged if dil % q == 0)
        f, n = dil // p, tm // dil
        keep = any(later % dil == 0 for later in dils[k + 1:])
        for r in range(dil):
            start = (r % p) * (tm // p) + r // p
            for s in range(N_LANE_SLABS):
                rows = staged[p][s, pl.ds(start, n, stride=f), :]
                h_ref[0, r, :, s * LANES:(s + 1) * LANES] = rows.astype(h_ref.dtype)
                if keep:
                    slab_refs[1][s, r * n:(r + 1) * n, :] = rows
        if keep:
            staged[dil] = slab_refs[1]


def _out_proj0(o, proj, x, w_out, g_next, *, tm=1024):
    b, s, d = x.shape
    gate_block = proj.shape[0] // N_SLABS - 1
    tiles = s // tm
    tok = pl.BlockSpec((1, tm, d), lambda bi, i: (bi, i, 0))
    dils = [dil for _, dil in DIL_PAIRS]
    return pl.pallas_call(
        _out0_kernel,
        grid=(b, s // tm),
        in_specs=[
            pl.BlockSpec((N_SLABS, tm, SLAB), lambda bi, i: (0, bi * tiles + i, 0)),
            pl.BlockSpec((N_SLABS, tm, SLAB), lambda bi, i: (gate_block, bi * tiles + i, 0)),
            tok,
            pl.BlockSpec((d, d), lambda bi, i: (0, 0)),
            pl.BlockSpec((1, d), lambda bi, i: (0, 0)),
        ],
        out_specs=[tok] + [pl.BlockSpec((1, dil, tm // dil, d), lambda bi, i: (bi, 0, i, 0))
                           for dil in dils],
        out_shape=([jax.ShapeDtypeStruct((b, s, d), jnp.float32)]
                   + [jax.ShapeDtypeStruct((b, dil, s // dil, d), jnp.bfloat16) for dil in dils]),
        scratch_shapes=[pltpu.VMEM((N_LANE_SLABS, tm, LANES), jnp.float32)] * 2,
        compiler_params=pltpu.CompilerParams(
            dimension_semantics=("parallel", "parallel"), vmem_limit_bytes=VMEM_LIMIT),
        name="out_proj0",
    )(o, proj, x, w_out, g_next.reshape(1, d))


_MAX_ROW_STRIDE = 4


def _natural_order(ref, slab_refs, dil):
    _, n, c = ref.shape
    if dil == 1:
        return ref[0].astype(jnp.float32)
    out_ref, mid_ref = slab_refs
    tm = dil * n
    two_level = dil > _MAX_ROW_STRIDE
    if two_level:
        f = _MAX_ROW_STRIDE
        p = dil // f
        assert p <= _MAX_ROW_STRIDE
    pieces = []
    for s in range(c // LANES):
        lanes = slice(s * LANES, (s + 1) * LANES)
        if not two_level:
            for r in range(dil):
                out_ref[s, pl.ds(r, n, stride=dil), :] = ref[r, :, lanes].astype(jnp.float32)
        else:
            for r in range(dil):
                start = (r % p) * (tm // p) + r // p
                mid_ref[s, pl.ds(start, n, stride=f), :] = ref[r, :, lanes].astype(jnp.float32)
            for b_ in range(p):
                out_ref[s, pl.ds(b_, tm // p, stride=p), :] = (
                    mid_ref[s, b_ * (tm // p):(b_ + 1) * (tm // p), :])
        pieces.append(out_ref[s])
    return pieces[0] if len(pieces) == 1 else jnp.concatenate(pieces, axis=1)


def _out1_kernel(o0_ref, o1_ref, o2_ref, l0_ref, l1_ref, l2_ref, gate_ref, x_ref,
                 w_ref, e_ref, g_ref, y_ref, *slab_refs):
    dils = [dil for _, dil in DIL_PAIRS]
    stats = [_natural_order(ref.at[0], slab_refs, dil)
             for ref, dil in zip((l0_ref, l1_ref, l2_ref), dils)]
    sums = [pltpu.roll(st, LSE_LANES - N_HEADS, axis=1) for st in stats]
    m = jnp.maximum(jnp.maximum(stats[0], stats[1]), stats[2])
    es = [jnp.exp2(st - m) for st in stats]
    denom = es[0] * sums[0] + es[1] * sums[1] + es[2] * sums[2]
    head_lane = lax.broadcasted_iota(jnp.int32, denom.shape, 1) < N_HEADS
    o = None
    for e, o_ref, dil in zip(es, (o0_ref, o1_ref, o2_ref), dils):
        w = jnp.where(head_lane, e / denom, 0.0)
        hi = w.astype(jnp.bfloat16)
        lo = (w - hi.astype(jnp.float32)).astype(jnp.bfloat16)
        w_full = jnp.dot(jnp.concatenate([hi, lo], axis=1), e_ref[...],
                         preferred_element_type=jnp.float32)
        term = w_full * _natural_order(o_ref.at[0], slab_refs, dil)
        o = term if o is None else o + term
    gate = gate_ref[0].astype(jnp.float32)
    z = (o * _silu(gate)).astype(jnp.bfloat16)
    x = x_ref[0] + jnp.dot(z, w_ref[...], preferred_element_type=jnp.float32)
    y_ref[0] = _rmsnorm(x, g_ref[...])


def _head_expansion():
    e = np.zeros((LSE_LANES, D_MODEL), np.float32)
    for h in range(N_HEADS):
        e[h, h * HEAD_DIM:(h + 1) * HEAD_DIM] = 1.0
    return jnp.asarray(np.concatenate([e, e], axis=0), dtype=jnp.bfloat16)


OUT1_INPUT_BUFFERS = 3


def _out_proj1(os_, lses, proj, x, w_out, norm_f, *, tm=512):
    b, s, d = x.shape
    gate_block = proj.shape[2] // d - 1
    dils = [dil for _, dil in DIL_PAIRS]
    deep = pl.Buffered(OUT1_INPUT_BUFFERS)

    def grouped(width, dil):
        return pl.BlockSpec((1, dil, tm // dil, width), lambda bi, i: (bi, 0, i, 0),
                            pipeline_mode=deep)

    tok_in = pl.BlockSpec((1, tm, d), lambda bi, i: (bi, i, 0), pipeline_mode=deep)
    n_in = 2 * len(dils) + 5

    def outer(*refs):
        ins, out, slabs = refs[:n_in], refs[n_in], refs[n_in + 1:]
        pltpu.emit_pipeline(
            lambda *tile_refs: _out1_kernel(*tile_refs, *slabs),
            grid=(b, s // tm),
            in_specs=(
                [grouped(d, dil) for dil in dils]
                + [grouped(LSE_LANES, dil) for dil in dils]
                + [pl.BlockSpec((1, tm, d), lambda bi, i: (bi, i, gate_block), pipeline_mode=deep),
                   tok_in,
                   pl.BlockSpec((d, d), lambda bi, i: (0, 0)),
                   pl.BlockSpec((2 * LSE_LANES, d), lambda bi, i: (0, 0)),
                   pl.BlockSpec((1, d), lambda bi, i: (0, 0))]),
            out_specs=[pl.BlockSpec((1, tm, d), lambda bi, i: (bi, i, 0))],
        )(*ins, out)

    return pl.pallas_call(
        outer,
        in_specs=[pl.BlockSpec(memory_space=pl.ANY)] * n_in,
        out_specs=pl.BlockSpec(memory_space=pl.ANY),
        out_shape=jax.ShapeDtypeStruct((b, s, d), jnp.float32),
        scratch_shapes=[pltpu.VMEM((N_LANE_SLABS, tm, LANES), jnp.float32)] * 2,
        compiler_params=pltpu.CompilerParams(vmem_limit_bytes=VMEM_LIMIT),
        name="out_proj1",
    )(*os_, *lses, proj, x, w_out, _head_expansion(), norm_f.reshape(1, d))


def kernel(x, norm_0, w_in_0, rpb_0, w_out_0, norm_1, w_in_1, w_out_1, norm_f):
    b, s, d = x.shape
    t = b * s

    proj0 = _norm_proj(x.reshape(t, d), norm_0, w_in_0)
    o0 = _na_attention(proj0, _na_bias_table(rpb_0), b)
    x1, *h1 = _out_proj0(o0, proj0, x, w_out_0.astype(jnp.bfloat16), norm_1)

    gate_block = 3 * N_DIL_GROUPS
    alibi = _dil_bias_tables()
    outs, lses, proj_gate = [], [], None
    for g, (window, dil) in enumerate(DIL_PAIRS):
        assert window // (2 * dil) == DIL_RADIUS
        blocks = (3 * g, 3 * g + 1, 3 * g + 2) + ((gate_block,) if dil == 1 else ())
        qkv = _proj(h1[g].reshape(t, d), w_in_1, blocks).reshape(b, dil, s // dil, -1)
        if dil == 1:
            proj_gate = qkv.reshape(b, s, -1)
        o_g, lse_g = _dil_attention(qkv, alibi, g)
        outs.append(o_g)
        lses.append(lse_g)
    return _out_proj1(outs, lses, proj_gate, x1, w_out_1.astype(jnp.bfloat16), norm_f)
```

```python
import math

import numpy as np
import jax
import jax.numpy as jnp
from jax import lax
from jax.experimental import pallas as pl
from jax.experimental.pallas import tpu as pltpu

D_MODEL = 1024
HEAD_DIM = 64
N_HEADS = 16
GRID_W = 64
NA_ROWS = 8
NA_COLS = 16
DIL_PAIRS = ((128, 1), (512, 4), (2048, 16))
N_DIL_GROUPS = len(DIL_PAIRS)
RMS_EPS = 1e-6
NEG_INF = -1e30
LOG2E = math.log2(math.e)

LANES = 128
N_LANE_SLABS = D_MODEL // LANES
HEADS_PER_STEP = 4
SLAB = HEADS_PER_STEP * HEAD_DIM
N_SLABS = D_MODEL // SLAB
Q_TILE = 128
NA_Q_TILE = GRID_W
NA_WIN = NA_ROWS * GRID_W
NA_TILES_PER_STEP = 32
DIL_TILES_PER_STEP = 8
DIL_RADIUS = 64
DIL_WIN = Q_TILE + 2 * DIL_RADIUS
LSE_LANES = LANES
VMEM_LIMIT = 56 * 1024 * 1024

_NT_DIMS = (((1,), (1,)), ((), ()))


def _rmsnorm(x, g):
    ms = jnp.mean(x * x, axis=-1, keepdims=True)
    return x * lax.rsqrt(ms + RMS_EPS) * g


def _q_scaled_bf16(w_ref):
    q_steps = D_MODEL // w_ref.shape[1]
    scale = jnp.where(pl.program_id(1) < q_steps, LOG2E / math.sqrt(HEAD_DIM), 1.0)
    return (w_ref[...] * scale).astype(jnp.bfloat16)


def _norm_proj_kernel(x_ref, g_ref, w_ref, o_ref, h_ref):
    i, j = pl.program_id(0), pl.program_id(1)
    rows = x_ref.shape[0]

    def normalise_chunk():
        start = pl.multiple_of(j * rows, rows)
        h_ref[i % 2, pl.ds(start, rows), :] = _rmsnorm(x_ref[...], g_ref[...]).astype(h_ref.dtype)

    @pl.when(i == 0)
    def _():
        normalise_chunk()

    @pl.when(i > 0)
    def _():
        res = jnp.dot(h_ref[(i - 1) % 2], _q_scaled_bf16(w_ref),
                      preferred_element_type=jnp.float32).astype(o_ref.dtype)
        for k in range(o_ref.shape[0]):
            o_ref[k] = res[:, k * SLAB:(k + 1) * SLAB]
        normalise_chunk()


def _norm_proj(x, g, w, *, tm=2048, tn=1024):
    t, d = x.shape
    n = w.shape[1]
    n_i, n_j = t // tm, n // tn
    chunk = tm // n_j
    n_chunks = t // chunk
    return pl.pallas_call(
        _norm_proj_kernel,
        grid=(n_i + 1, n_j),
        in_specs=[
            pl.BlockSpec((chunk, d), lambda i, j: (jnp.minimum(i * n_j + j, n_chunks - 1), 0)),
            pl.BlockSpec((1, d), lambda i, j: (0, 0)),
            pl.BlockSpec((d, tn), lambda i, j: (0, j)),
        ],
        out_specs=pl.BlockSpec((tn // SLAB, tm, SLAB),
                               lambda i, j: (jnp.where(i == 0, 0, j), jnp.maximum(i - 1, 0), 0)),
        out_shape=jax.ShapeDtypeStruct((n // SLAB, t, SLAB), jnp.bfloat16),
        scratch_shapes=[pltpu.VMEM((2, tm, d), jnp.bfloat16)],
        compiler_params=pltpu.CompilerParams(
            dimension_semantics=("arbitrary", "arbitrary"),
            vmem_limit_bytes=VMEM_LIMIT),
        name="norm_proj",
    )(x, g.reshape(1, d), w)


def _proj_kernel(h_ref, w_ref, scale_ref, o_ref):
    w = (w_ref[...] * scale_ref[...]).astype(jnp.bfloat16)
    o_ref[...] = jnp.dot(h_ref[...], w, preferred_element_type=jnp.float32).astype(o_ref.dtype)


PROJ_WEIGHT_BUFFERS = 3


def _proj(h, w, col_blocks, *, tm=2048, tn=1024):
    t, d = h.shape
    per_block = d // tn
    n = len(col_blocks) * d
    scale = np.ones((1, n), np.float32)
    scale[:, :d] = LOG2E / math.sqrt(HEAD_DIM)

    def w_block(j):
        k_of_j = j // per_block
        blk = col_blocks[-1]
        for k in range(len(col_blocks) - 2, -1, -1):
            blk = jnp.where(k_of_j == k, col_blocks[k], blk)
        return blk * per_block + j % per_block

    pipeline = pltpu.emit_pipeline(
        _proj_kernel,
        grid=(t // tm, n // tn),
        in_specs=[
            pl.BlockSpec((tm, d), lambda i, j: (i, 0)),
            pl.BlockSpec((d, tn), lambda i, j: (0, w_block(j)),
                         pipeline_mode=pl.Buffered(PROJ_WEIGHT_BUFFERS)),
            pl.BlockSpec((1, tn), lambda i, j: (0, j)),
        ],
        out_specs=[pl.BlockSpec((tm, tn), lambda i, j: (i, j))],
    )

    def outer(h_hbm, w_hbm, scale_hbm, o_hbm):
        pipeline(h_hbm, w_hbm, scale_hbm, o_hbm)

    return pl.pallas_call(
        outer,
        in_specs=[pl.BlockSpec(memory_space=pl.ANY)] * 3,
        out_specs=pl.BlockSpec(memory_space=pl.ANY),
        out_shape=jax.ShapeDtypeStruct((t, n), jnp.bfloat16),
        compiler_params=pltpu.CompilerParams(vmem_limit_bytes=VMEM_LIMIT),
        name="proj",
    )(h, w, jnp.asarray(scale))


def _na_kernel(q_ref, k_ref, v_ref, b_ref, o_ref):
    step = pl.program_id(2)
    n_tiles = k_ref.shape[1] // GRID_W
    lane = lax.broadcasted_iota(jnp.int32, (NA_Q_TILE, SLAB), 1) // HEAD_DIM

    for u in range(NA_TILES_PER_STEP):
        r = step * NA_TILES_PER_STEP + u
        first_row = jnp.clip(r - NA_ROWS // 2, 0, n_tiles - NA_ROWS)
        start = pl.multiple_of(first_row * GRID_W, GRID_W)
        tile_rows = slice(u * NA_Q_TILE, (u + 1) * NA_Q_TILE)
        q4 = q_ref[0, tile_rows, :]
        zero = jnp.zeros_like(q4)
        q_stack = jnp.concatenate(
            [jnp.where(lane == h, q4, zero) for h in range(HEADS_PER_STEP)], axis=0)
        s = lax.dot_general(q_stack, k_ref[0, pl.ds(start, NA_WIN), :], _NT_DIMS,
                            preferred_element_type=jnp.float32)
        first_off = first_row - r + NA_ROWS - 1
        s = s + jnp.concatenate(
            [jnp.concatenate([b_ref[h, first_off + 2 * j] for j in range(NA_ROWS // 2)], axis=1)
             for h in range(HEADS_PER_STEP)], axis=0)
        m = jnp.max(s, axis=-1, keepdims=True)
        e = jnp.exp2(s - m)
        inv_l = 1.0 / jnp.sum(e, axis=-1, keepdims=True)
        o_all = jnp.dot(e.astype(jnp.bfloat16), v_ref[0, pl.ds(start, NA_WIN), :],
                        preferred_element_type=jnp.float32) * inv_l
        out = o_all[:NA_Q_TILE]
        for h in range(1, HEADS_PER_STEP):
            out = jnp.where(lane == h, o_all[h * NA_Q_TILE:(h + 1) * NA_Q_TILE], out)
        o_ref[0, tile_rows, :] = out.astype(o_ref.dtype)


def _na_bias_table(rpb):
    n_col_off = 2 * NA_COLS - 1
    c = np.arange(GRID_W)[:, None]
    kc = np.arange(GRID_W)[None, :]
    cs = np.clip(c - NA_COLS // 2, 0, GRID_W - NA_COLS)
    col_ok = (kc >= cs) & (kc < cs + NA_COLS)
    col_off = kc - c + NA_COLS - 1
    assert np.all(((col_off >= 0) & (col_off < n_col_off))[col_ok])
    pick = (np.arange(n_col_off)[:, None, None] == col_off[None]) & col_ok[None]
    pick2 = np.zeros((2, n_col_off, GRID_W, 2, GRID_W), np.float32)
    for half in range(2):
        pick2[half, :, :, half, :] = pick
    pick2 = pick2.reshape(2 * n_col_off, GRID_W, 2 * GRID_W)
    rpb = rpb.astype(jnp.float32) * LOG2E
    row_pairs = jnp.concatenate([rpb[:, :-1], rpb[:, 1:]], axis=-1)
    table = jnp.einsum('hrm,mck->hrck', row_pairs, jnp.asarray(pick2),
                       precision=lax.Precision.HIGHEST)
    ok2 = np.concatenate([col_ok, col_ok], axis=1)
    return table + jnp.asarray(np.where(ok2, 0.0, NEG_INF).astype(np.float32))


def _na_attention(proj, bias_table, b):
    s = proj.shape[1] // b
    tile = NA_Q_TILE * NA_TILES_PER_STEP
    assert s % tile == 0 and NA_Q_TILE == GRID_W
    n_blocks = s // tile
    return pl.pallas_call(
        _na_kernel,
        grid=(b, N_SLABS, n_blocks),
        in_specs=[
            pl.BlockSpec((1, tile, SLAB), lambda bi, g, j: (g, bi * n_blocks + j, 0)),
            pl.BlockSpec((1, s, SLAB), lambda bi, g, j: (N_SLABS + g, bi, 0)),
            pl.BlockSpec((1, s, SLAB), lambda bi, g, j: (2 * N_SLABS + g, bi, 0)),
            pl.BlockSpec((HEADS_PER_STEP,) + bias_table.shape[1:], lambda bi, g, j: (g, 0, 0, 0)),
        ],
        out_specs=pl.BlockSpec((1, tile, SLAB), lambda bi, g, j: (g, bi * n_blocks + j, 0)),
        out_shape=jax.ShapeDtypeStruct((N_SLABS, b * s, SLAB), jnp.bfloat16),
        compiler_params=pltpu.CompilerParams(
            dimension_semantics=("parallel", "parallel", "arbitrary"),
            vmem_limit_bytes=VMEM_LIMIT),
        name="na_attention",
    )(proj, proj, proj, bias_table)


def _alibi_slopes():
    return np.asarray(2.0 ** (-8.0 * (np.arange(N_HEADS) + 1) / N_HEADS), dtype=np.float32)


def _slab_attention(q4, k4, v4, bias):
    lane = lax.broadcasted_iota(jnp.int32, q4.shape, 1) // HEAD_DIM
    zero = jnp.zeros_like(q4)
    q_stack = jnp.concatenate(
        [jnp.where(lane == h, q4, zero) for h in range(HEADS_PER_STEP)], axis=0)
    s = lax.dot_general(q_stack, k4, _NT_DIMS, preferred_element_type=jnp.float32)
    s = s + bias
    m = jnp.max(s, axis=-1, keepdims=True)
    p = jnp.exp2(s - m)
    l = jnp.sum(p, axis=-1, keepdims=True)
    p = p.astype(jnp.bfloat16)
    o_all = jnp.dot(p, v4, preferred_element_type=jnp.float32)
    out = o_all[:Q_TILE]
    for h in range(1, HEADS_PER_STEP):
        out = jnp.where(lane == h, o_all[h * Q_TILE:(h + 1) * Q_TILE], out)
    return out, m, l


def _dil_kernel(q_ref, ka_ref, kb_ref, kc_ref, va_ref, vb_ref, vc_ref, b_ref,
                o_ref, stat_ref):
    step = pl.program_id(2)
    n_steps = pl.num_programs(2)
    lane = lax.broadcasted_iota(jnp.int32, (Q_TILE, LSE_LANES), 1)
    tiles = q_ref.shape[2] // Q_TILE
    for r in range(q_ref.shape[1]):
        stat_tiles = [jnp.zeros((Q_TILE, LSE_LANES), jnp.float32) for _ in range(tiles)]
        for hg in range(N_SLABS):
            cols = slice(hg * SLAB, (hg + 1) * SLAB)
            k_win = jnp.concatenate(
                [ka_ref[0, r, :, cols], kb_ref[0, r, :, cols], kc_ref[0, r, :, cols]], axis=0)
            v_win = jnp.concatenate(
                [va_ref[0, r, :, cols], vb_ref[0, r, :, cols], vc_ref[0, r, :, cols]], axis=0)
            for u in range(tiles):
                pattern = 0
                if u == 0:
                    pattern = jnp.where(step == 0, 1, pattern)
                if u == tiles - 1:
                    pattern = jnp.where(step == n_steps - 1, 2, pattern)
                tile = slice(u * Q_TILE, (u + 1) * Q_TILE)
                win = slice(u * Q_TILE, u * Q_TILE + DIL_WIN)
                bias = b_ref[pattern, hg * HEADS_PER_STEP:(hg + 1) * HEADS_PER_STEP]
                bias = bias.reshape(HEADS_PER_STEP * Q_TILE, DIL_WIN)
                out, m, l = _slab_attention(q_ref[0, r, tile, cols], k_win[win], v_win[win], bias)
                o_ref[0, r, tile, cols] = out.astype(o_ref.dtype)
                for h in range(HEADS_PER_STEP):
                    rows = slice(h * Q_TILE, (h + 1) * Q_TILE)
                    head = hg * HEADS_PER_STEP + h
                    stat_tiles[u] = jnp.where(lane == head, m[rows], stat_tiles[u])
                    stat_tiles[u] = jnp.where(lane == N_HEADS + head, l[rows], stat_tiles[u])
        for u in range(tiles):
            stat_ref[0, r, u * Q_TILE:(u + 1) * Q_TILE, :] = stat_tiles[u]


def _dil_bias_tables():
    qi = np.arange(Q_TILE)[:, None]
    kj = np.arange(DIL_WIN)[None, :]
    delta = kj - DIL_RADIUS - qi
    in_band = np.abs(delta) <= DIL_RADIUS
    valid = np.stack([in_band,
                      in_band & (kj >= DIL_RADIUS),
                      in_band & (kj < DIL_WIN - DIL_RADIUS)])
    dils = np.asarray([dil for _, dil in DIL_PAIRS])
    dist = (np.abs(delta)[None] * dils[:, None, None]).astype(np.float32)
    bias = -_alibi_slopes()[None, :, None, None] * dist[:, None]
    table = np.where(valid[None, :, None], bias[:, None] * np.float32(LOG2E), np.float32(NEG_INF))
    assert table.dtype == np.float32
    return jnp.asarray(table)


def _dil_attention(qkv, bias_tables, group):
    b, dil, l, _ = qkv.shape
    tiles = min(DIL_TILES_PER_STEP, l // Q_TILE)
    seqs = min(DIL_TILES_PER_STEP // tiles, dil)
    tile = Q_TILE * tiles
    n_steps = l // tile
    assert l % tile == 0 and tiles >= 2 and dil % seqs == 0
    half = DIL_RADIUS
    per_tile = tile // half
    n_half = l // half

    def spec_mid(which):
        return pl.BlockSpec((1, seqs, tile, D_MODEL), lambda bi, r, i: (bi, r, i, which))

    def spec_lo(which):
        return pl.BlockSpec((1, seqs, half, D_MODEL),
                            lambda bi, r, i: (bi, r, jnp.maximum(per_tile * i - 1, 0), which))

    def spec_hi(which):
        return pl.BlockSpec((1, seqs, half, D_MODEL),
                            lambda bi, r, i: (bi, r, jnp.minimum(per_tile * (i + 1), n_half - 1), which))

    return pl.pallas_call(
        _dil_kernel,
        grid=(b, dil // seqs, n_steps),
        in_specs=[
            spec_mid(0),
            spec_lo(1), spec_mid(1), spec_hi(1),
            spec_lo(2), spec_mid(2), spec_hi(2),
            pl.BlockSpec((None,) + bias_tables.shape[1:], lambda bi, r, i: (group, 0, 0, 0, 0)),
        ],
        out_specs=[
            pl.BlockSpec((1, seqs, tile, D_MODEL), lambda bi, r, i: (bi, r, i, 0)),
            pl.BlockSpec((1, seqs, tile, LSE_LANES), lambda bi, r, i: (bi, r, i, 0)),
        ],
        out_shape=[
            jax.ShapeDtypeStruct((b, dil, l, D_MODEL), jnp.bfloat16),
            jax.ShapeDtypeStruct((b, dil, l, LSE_LANES), jnp.float32),
        ],
        compiler_params=pltpu.CompilerParams(
            dimension_semantics=("parallel", "parallel", "arbitrary"),
            vmem_limit_bytes=VMEM_LIMIT),
        name=f"dilated_attention_{dil}",
    )(qkv, qkv, qkv, qkv, qkv, qkv, qkv, bias_tables)


def _silu(x):
    half = 0.5 * x
    return half + half * jnp.tanh(half)


def _out0_kernel(o_ref, gate_ref, x_ref, w_ref, g_ref, y_ref, *rest):
    h_refs, slab_refs = rest[:-2], rest[-2:]
    o = jnp.concatenate([o_ref[k] for k in range(N_SLABS)], axis=1).astype(jnp.float32)
    gate = jnp.concatenate([gate_ref[k] for k in range(N_SLABS)], axis=1).astype(jnp.float32)
    z = (o * _silu(gate)).astype(jnp.bfloat16)
    y = x_ref[0] + jnp.dot(z, w_ref[...], preferred_element_type=jnp.float32)
    y_ref[0] = y
    hn = _rmsnorm(y, g_ref[...])
    tm = hn.shape[0]
    dils = [dil for _, dil in DIL_PAIRS]
    for s in range(N_LANE_SLABS):
        slab_refs[0][s] = hn[:, s * LANES:(s + 1) * LANES]
    staged = {1: slab_refs[0]}
    for k, (h_ref, dil) in enumerate(zip(h_refs, dils)):
        if dil == 1:
            h_ref[0, 0] = hn.astype(h_ref.dtype)
            continue
        p = max(q for q in staged if dil % q == 0)
        f, n = dil // p, tm // dil
        keep = any(later % dil == 0 for later in dils[k + 1:])
        for r in range(dil):
            start = (r % p) * (tm // p) + r // p
            for s in range(N_LANE_SLABS):
                rows = staged[p][s, pl.ds(start, n, stride=f), :]
                h_ref[0, r, :, s * LANES:(s + 1) * LANES] = rows.astype(h_ref.dtype)
                if keep:
                    slab_refs[1][s, r * n:(r + 1) * n, :] = rows
        if keep:
            staged[dil] = slab_refs[1]


def _out_proj0(o, proj, x, w_out, g_next, *, tm=1024):
    b, s, d = x.shape
    gate_block = proj.shape[0] // N_SLABS - 1
    tiles = s // tm
    tok = pl.BlockSpec((1, tm, d), lambda bi, i: (bi, i, 0))
    dils = [dil for _, dil in DIL_PAIRS]
    return pl.pallas_call(
        _out0_kernel,
        grid=(b, s // tm),
        in_specs=[
            pl.BlockSpec((N_SLABS, tm, SLAB), lambda bi, i: (0, bi * tiles + i, 0)),
            pl.BlockSpec((N_SLABS, tm, SLAB), lambda bi, i: (gate_block, bi * tiles + i, 0)),
            tok,
            pl.BlockSpec((d, d), lambda bi, i: (0, 0)),
            pl.BlockSpec((1, d), lambda bi, i: (0, 0)),
        ],
        out_specs=[tok] + [pl.BlockSpec((1, dil, tm // dil, d), lambda bi, i: (bi, 0, i, 0))
                           for dil in dils],
        out_shape=([jax.ShapeDtypeStruct((b, s, d), jnp.float32)]
                   + [jax.ShapeDtypeStruct((b, dil, s // dil, d), jnp.bfloat16) for dil in dils]),
        scratch_shapes=[pltpu.VMEM((N_LANE_SLABS, tm, LANES), jnp.float32)] * 2,
        compiler_params=pltpu.CompilerParams(
            dimension_semantics=("parallel", "parallel"), vmem_limit_bytes=VMEM_LIMIT),
        name="out_proj0",
    )(o, proj, x, w_out, g_next.reshape(1, d))


_MAX_ROW_STRIDE = 4


def _natural_order(ref, slab_refs, dil):
    _, n, c = ref.shape
    if dil == 1:
        return ref[0].astype(jnp.float32)
    out_ref, mid_ref = slab_refs
    tm = dil * n
    two_level = dil > _MAX_ROW_STRIDE
    if two_level:
        f = _MAX_ROW_STRIDE
        p = dil // f
        assert p <= _MAX_ROW_STRIDE
    pieces = []
    for s in range(c // LANES):
        lanes = slice(s * LANES, (s + 1) * LANES)
        if not two_level:
            for r in range(dil):
                out_ref[s, pl.ds(r, n, stride=dil), :] = ref[r, :, lanes].astype(jnp.float32)
        else:
            for r in range(dil):
                start = (r % p) * (tm // p) + r // p
                mid_ref[s, pl.ds(start, n, stride=f), :] = ref[r, :, lanes].astype(jnp.float32)
            for b_ in range(p):
                out_ref[s, pl.ds(b_, tm // p, stride=p), :] = (
                    mid_ref[s, b_ * (tm // p):(b_ + 1) * (tm // p), :])
        pieces.append(out_ref[s])
    return pieces[0] if len(pieces) == 1 else jnp.concatenate(pieces, axis=1)


def _out1_kernel(o0_ref, o1_ref, o2_ref, l0_ref, l1_ref, l2_ref, gate_ref, x_ref,
                 w_ref, e_ref, g_ref, y_ref, *slab_refs):
    dils = [dil for _, dil in DIL_PAIRS]
    stats = [_natural_order(ref.at[0], slab_refs, dil)
             for ref, dil in zip((l0_ref, l1_ref, l2_ref), dils)]
    sums = [pltpu.roll(st, LSE_LANES - N_HEADS, axis=1) for st in stats]
    m = jnp.maximum(jnp.maximum(stats[0], stats[1]), stats[2])
    es = [jnp.exp2(st - m) for st in stats]
    denom = es[0] * sums[0] + es[1] * sums[1] + es[2] * sums[2]
    head_lane = lax.broadcasted_iota(jnp.int32, denom.shape, 1) < N_HEADS
    o = None
    for e, o_ref, dil in zip(es, (o0_ref, o1_ref, o2_ref), dils):
        w = jnp.where(head_lane, e / denom, 0.0)
        hi = w.astype(jnp.bfloat16)
        lo = (w - hi.astype(jnp.float32)).astype(jnp.bfloat16)
        w_full = jnp.dot(jnp.concatenate([hi, lo], axis=1), e_ref[...],
                         preferred_element_type=jnp.float32)
        term = w_full * _natural_order(o_ref.at[0], slab_refs, dil)
        o = term if o is None else o + term
    gate = gate_ref[0].astype(jnp.float32)
    z = (o * _silu(gate)).astype(jnp.bfloat16)
    x = x_ref[0] + jnp.dot(z, w_ref[...], preferred_element_type=jnp.float32)
    y_ref[0] = _rmsnorm(x, g_ref[...])


def _head_expansion():
    e = np.zeros((LSE_LANES, D_MODEL), np.float32)
    for h in range(N_HEADS):
        e[h, h * HEAD_DIM:(h + 1) * HEAD_DIM] = 1.0
    return jnp.asarray(np.concatenate([e, e], axis=0), dtype=jnp.bfloat16)


OUT1_INPUT_BUFFERS = 3


def _out_proj1(os_, lses, proj, x, w_out, norm_f, *, tm=512):
    b, s, d = x.shape
    gate_block = proj.shape[2] // d - 1
    dils = [dil for _, dil in DIL_PAIRS]
    deep = pl.Buffered(OUT1_INPUT_BUFFERS)

    def grouped(width, dil):
        return pl.BlockSpec((1, dil, tm // dil, width), lambda bi, i: (bi, 0, i, 0),
                            pipeline_mode=deep)

    tok_in = pl.BlockSpec((1, tm, d), lambda bi, i: (bi, i, 0), pipeline_mode=deep)
    n_in = 2 * len(dils) + 5

    def outer(*refs):
        ins, out, slabs = refs[:n_in], refs[n_in], refs[n_in + 1:]
        pltpu.emit_pipeline(
            lambda *tile_refs: _out1_kernel(*tile_refs, *slabs),
            grid=(b, s // tm),
            in_specs=(
                [grouped(d, dil) for dil in dils]
                + [grouped(LSE_LANES, dil) for dil in dils]
                + [pl.BlockSpec((1, tm, d), lambda bi, i: (bi, i, gate_block), pipeline_mode=deep),
                   tok_in,
                   pl.BlockSpec((d, d), lambda bi, i: (0, 0)),
                   pl.BlockSpec((2 * LSE_LANES, d), lambda bi, i: (0, 0)),
                   pl.BlockSpec((1, d), lambda bi, i: (0, 0))]),
            out_specs=[pl.BlockSpec((1, tm, d), lambda bi, i: (bi, i, 0))],
        )(*ins, out)

    return pl.pallas_call(
        outer,
        in_specs=[pl.BlockSpec(memory_space=pl.ANY)] * n_in,
        out_specs=pl.BlockSpec(memory_space=pl.ANY),
        out_shape=jax.ShapeDtypeStruct((b, s, d), jnp.float32),
        scratch_shapes=[pltpu.VMEM((N_LANE_SLABS, tm, LANES), jnp.float32)] * 2,
        compiler_params=pltpu.CompilerParams(vmem_limit_bytes=VMEM_LIMIT),
        name="out_proj1",
    )(*os_, *lses, proj, x, w_out, _head_expansion(), norm_f.reshape(1, d))


def kernel(x, norm_0, w_in_0, rpb_0, w_out_0, norm_1, w_in_1, w_out_1, norm_f):
    b, s, d = x.shape
    t = b * s

    proj0 = _norm_proj(x.reshape(t, d), norm_0, w_in_0)
    o0 = _na_attention(proj0, _na_bias_table(rpb_0), b)
    x1, *h1 = _out_proj0(o0, proj0, x, w_out_0.astype(jnp.bfloat16), norm_1)

    gate_block = 3 * N_DIL_GROUPS
    alibi = _dil_bias_tables()
    outs, lses, proj_gate = [], [], None
    for g, (window, dil) in enumerate(DIL_PAIRS):
        assert window // (2 * dil) == DIL_RADIUS
        blocks = (3 * g, 3 * g + 1, 3 * g + 2) + ((gate_block,) if dil == 1 else ())
        qkv = _proj(h1[g].reshape(t, d), w_in_1, blocks).reshape(b, dil, s // dil, -1)
        if dil == 1:
            proj_gate = qkv.reshape(b, s, -1)
        o_g, lse_g = _dil_attention(qkv, alibi, g)
        outs.append(o_g)
        lses.append(lse_g)
    return _out_proj1(outs, lses, proj_gate, x1, w_out_1.astype(jnp.bfloat16), norm_f)
```
